```python
import math
import jax
import jax.numpy as jnp
from jax import lax
import numpy as np

D_MODEL = 1024
BATCH = 2
SEQ = 16384
DEPTH = 4

N_NSA_HEADS = 8
N_KV_HEADS = 2
GQA_GROUP = N_NSA_HEADS // N_KV_HEADS
HEAD_DIM = 64
NSA_WIDTH = N_NSA_HEADS * HEAD_DIM
KV_WIDTH = N_KV_HEADS * HEAD_DIM
N_BRANCHES = 3
N_GATES = N_BRANCHES * N_NSA_HEADS
CMP_BLOCK = 32
CMP_STRIDE = 16
CMP_HIDDEN = 4 * HEAD_DIM
SLC_BLOCK = 64
SLC_TOPK = 16
N_LOCAL_BLOCKS = 2
WINDOW = 512
Q_BLOCK = 128
GMLP_WIDTH = D_MODEL - NSA_WIDTH
N_GMLP_GROUPS = 8
GMLP_GROUP_DIM = GMLP_WIDTH // N_GMLP_GROUPS
GMLP_CHUNK = 128
IN_WIDTH = NSA_WIDTH + 6 * KV_WIDTH + N_GATES + 2 * GMLP_WIDTH
D_FF = ((8 * D_MODEL + 3 * 256 - 1) // (3 * 256)) * 256
N_BUCKETS = 32
REL_MAX_DISTANCE = 128
RMS_EPS = 1e-6
LN_EPS = 1e-5

kernel_name = "nsa_gmlp_parallel_hybrid"


def rms_norm(x, g):
    xf = x.astype(jnp.float32)
    y = xf * lax.rsqrt(jnp.mean(xf * xf, axis=-1, keepdims=True) + RMS_EPS)
    return (y * g.astype(jnp.float32)).astype(x.dtype)


def layer_norm(x, g, b):
    xf = x.astype(jnp.float32)
    mu = jnp.mean(xf, axis=-1, keepdims=True)
    var = jnp.mean(jnp.square(xf - mu), axis=-1, keepdims=True)
    y = (xf - mu) * lax.rsqrt(var + LN_EPS)
    return (y * g.astype(jnp.float32) + b.astype(jnp.float32)).astype(x.dtype)


def t5_bucket(dist):
    n = jnp.maximum(dist, 0)
    max_exact = N_BUCKETS // 2
    nf = jnp.maximum(n, max_exact).astype(jnp.float32)
    large = max_exact + (jnp.log(nf / max_exact) / math.log(REL_MAX_DISTANCE / max_exact)
                         * (N_BUCKETS - max_exact)).astype(jnp.int32)
    return jnp.where(n < max_exact, n, jnp.minimum(large, N_BUCKETS - 1))


def bias_shared(table, dist):
    b = table.astype(jnp.float32)[t5_bucket(dist)]
    return b.transpose(2, 0, 1).reshape(N_KV_HEADS, GQA_GROUP, *dist.shape)


def bias_grouped(table, dist):
    tab = table.astype(jnp.float32).reshape(N_BUCKETS, N_KV_HEADS, GQA_GROUP)
    b = jax.vmap(lambda t, d: t[d], in_axes=(1, 1), out_axes=1)(tab, t5_bucket(dist))
    return b.transpose(0, 1, 4, 2, 3)


def masked_softmax(logits, mask):
    lg = jnp.where(mask, logits.astype(jnp.float32), -jnp.inf)
    m = jnp.max(lg, axis=-1, keepdims=True)
    m = jnp.where(jnp.isfinite(m), m, 0.0)
    e = jnp.where(mask, jnp.exp(lg - m), 0.0)
    return e / jnp.maximum(jnp.sum(e, axis=-1, keepdims=True), 1e-30)


def compress_tokens(t, pos_emb, w1, w2):
    B, G, S, dh = t.shape
    r = CMP_BLOCK // CMP_STRIDE
    n_chunks = S // CMP_STRIDE
    n_cmp = n_chunks - r + 1
    chunks = t.reshape(B, G, n_chunks, CMP_STRIDE, dh)
    blocks = jnp.concatenate([chunks[:, :, i:i + n_cmp] for i in range(r)], axis=3)
    blocks = (blocks + pos_emb).reshape(B, G, n_cmp, CMP_BLOCK * dh)
    return jax.nn.gelu(blocks @ w1) @ w2


gather_blocks = jax.vmap(jax.vmap(lambda blocks, ix: blocks[ix]))


def nsa_gmlp_mixer(h, w_in, cmp_pos_k, cmp_w1_k, cmp_w2_k, cmp_pos_v, cmp_w1_v, cmp_w2_v,
                   gmlp_ln_g, gmlp_ln_b, gmlp_w_s, gmlp_b_s, w_out, rel_bias):
    B, S, _ = h.shape
    G, R, dh = N_KV_HEADS, GQA_GROUP, HEAD_DIM
    proj = jnp.einsum("bsd,de->bse", h, w_in)
    sizes = [NSA_WIDTH] + [KV_WIDTH] * 6 + [N_GATES, GMLP_WIDTH]
    points = [sum(sizes[:i + 1]) for i in range(len(sizes))]
    q, k_c, v_c, k_s, v_s, k_w, v_w, gate_logits, u, v = jnp.split(proj, points, axis=-1)

    def heads_kv(t):
        return t.reshape(B, S, G, dh).transpose(0, 2, 1, 3)

    q = q.reshape(B, S, G, R, dh).transpose(0, 2, 3, 1, 4) * (HEAD_DIM ** -0.5)
    gates = jax.nn.sigmoid(gate_logits.astype(jnp.float32)).reshape(B, S, G, R, N_BRANCHES)
    gates = gates.transpose(0, 2, 3, 1, 4)

    kc = compress_tokens(heads_kv(k_c), cmp_pos_k, cmp_w1_k, cmp_w2_k)
    vc = compress_tokens(heads_kv(v_c), cmp_pos_v, cmp_w1_v, cmp_w2_v)
    n_cmp = kc.shape[2]
    cmp_start = jnp.arange(n_cmp, dtype=jnp.int32) * CMP_STRIDE
    cmp_end = cmp_start + (CMP_BLOCK - 1)

    n_slc = S // SLC_BLOCK
    ks_blocks = heads_kv(k_s).reshape(B, G, n_slc, SLC_BLOCK, dh)
    vs_blocks = heads_kv(v_s).reshape(B, G, n_slc, SLC_BLOCK, dh)
    slc_start = jnp.arange(n_slc, dtype=jnp.int32) * SLC_BLOCK
    ov = (jnp.minimum(cmp_start[:, None] + CMP_BLOCK, slc_start[None, :] + SLC_BLOCK)
          - jnp.maximum(cmp_start[:, None], slc_start[None, :]))
    overlap = jnp.maximum(ov, 0).astype(jnp.float32) / CMP_BLOCK
    k_sel = min(SLC_TOPK, n_slc)

    pad = ((0, 0), (0, 0), (WINDOW, 0), (0, 0))
    kw_pad = jnp.pad(heads_kv(k_w), pad)
    vw_pad = jnp.pad(heads_kv(v_w), pad)

    def query_block(qi):
        s0 = qi * Q_BLOCK
        tq = s0 + jnp.arange(Q_BLOCK, dtype=jnp.int32)
        qb = lax.dynamic_slice_in_dim(q, s0, Q_BLOCK, axis=3)
        gb = lax.dynamic_slice_in_dim(gates, s0, Q_BLOCK, axis=3)

        d_c = tq[:, None] - cmp_end[None, :]
        lg_c = jnp.einsum("bgrqd,bgkd->bgrqk", qb, kc).astype(jnp.float32) + bias_shared(rel_bias, d_c)
        p_c = masked_softmax(lg_c, d_c >= 0)
        o_c = jnp.einsum("bgrqk,bgkd->bgrqd", p_c.astype(vc.dtype), vc)

        imp = jnp.einsum("bgrqk,kn->bgqn", p_c, overlap)
        jq = (tq // SLC_BLOCK)[:, None]
        blk = jnp.arange(n_slc, dtype=jnp.int32)[None, :]
        forced = (blk == 0) | ((blk <= jq) & (blk > jq - N_LOCAL_BLOCKS))
        score = jnp.where(forced, jnp.inf, jnp.where(blk <= jq, imp, -jnp.inf))
        _, idx = lax.top_k(score, k_sel)
        kb = gather_blocks(ks_blocks, idx).reshape(B, G, Q_BLOCK, k_sel * SLC_BLOCK, dh)
        vb = gather_blocks(vs_blocks, idx).reshape(B, G, Q_BLOCK, k_sel * SLC_BLOCK, dh)
        key_pos = (idx[..., None] * SLC_BLOCK + jnp.arange(SLC_BLOCK, dtype=jnp.int32)
                   ).reshape(B, G, Q_BLOCK, k_sel * SLC_BLOCK)
        d_s = tq[None, None, :, None] - key_pos
        lg_s = jnp.einsum("bgrqd,bgqkd->bgrqk", qb, kb).astype(jnp.float32) + bias_grouped(rel_bias, d_s)
        p_s = masked_softmax(lg_s, (d_s >= 0)[:, :, None])
        o_s = jnp.einsum("bgrqk,bgqkd->bgrqd", p_s.astype(vb.dtype), vb)

        kwb = lax.dynamic_slice_in_dim(kw_pad, s0, WINDOW + Q_BLOCK, axis=2)
        vwb = lax.dynamic_slice_in_dim(vw_pad, s0, WINDOW + Q_BLOCK, axis=2)
        kpos = s0 - WINDOW + jnp.arange(WINDOW + Q_BLOCK, dtype=jnp.int32)
        d_w = tq[:, None] - kpos[None, :]
        mask_w = (d_w >= 0) & (d_w < WINDOW) & (kpos[None, :] >= 0)
        lg_w = jnp.einsum("bgrqd,bgkd->bgrqk", qb, kwb).astype(jnp.float32) + bias_shared(rel_bias, d_w)
        p_w = masked_softmax(lg_w, mask_w)
        o_w = jnp.einsum("bgrqk,bgkd->bgrqd", p_w.astype(vwb.dtype), vwb)

        o = gb[..., 0:1] * o_c + gb[..., 1:2] * o_s + gb[..., 2:3] * o_w
        return o.astype(h.dtype)

    o = lax.map(query_block, jnp.arange(S // Q_BLOCK, dtype=jnp.int32))
    nsa_out = o.transpose(1, 0, 4, 2, 3, 5).reshape(B, S, NSA_WIDTH)

    z_u = jax.nn.gelu(u)
    z_v = layer_norm(jax.nn.gelu(v), gmlp_ln_g, gmlp_ln_b)
    n_chunks = S // GMLP_CHUNK
    zv = z_v.reshape(B, n_chunks, GMLP_CHUNK, N_GMLP_GROUPS, GMLP_GROUP_DIM)
    causal = jnp.tril(jnp.ones((GMLP_CHUNK, GMLP_CHUNK), dtype=bool))
    w_s = jnp.where(causal, gmlp_w_s, 0.0).astype(zv.dtype)
    sv = jnp.einsum("gts,bnsgd->bntgd", w_s, zv) + gmlp_b_s.T[None, None, :, :, None]
    gm = (z_u.reshape(B, n_chunks, GMLP_CHUNK, N_GMLP_GROUPS, GMLP_GROUP_DIM) * sv).reshape(B, S, GMLP_WIDTH)

    mixed = jnp.concatenate([nsa_out, gm.astype(h.dtype)], axis=-1)
    return jnp.einsum("bse,ed->bsd", mixed, w_out)


def swiglu_ffn(h, w_gate_up, w_down):
    gate, up = jnp.split(jnp.einsum("bsd,df->bsf", h, w_gate_up), 2, axis=-1)
    return jnp.einsum("bsf,fd->bsd", jax.nn.silu(gate) * up, w_down)


def setup_inputs(seed: int = 0) -> dict:
    key = jax.random.key(seed)
    ks = jax.random.split(key, 20)
    f32 = jnp.float32
    L = DEPTH

    def nrm(k, shape, scale):
        return jax.random.normal(k, shape, f32) * scale

    def gain(k, shape):
        return 1.0 + 0.05 * jax.random.normal(k, shape, f32)

    return {
        "x": nrm(ks[0], (BATCH, SEQ, D_MODEL), 1.0),
        "rel_bias": nrm(ks[1], (N_BUCKETS, N_NSA_HEADS), 0.5),
        "norm_mix_pre": gain(ks[2], (L, D_MODEL)),
        "norm_mix_post": gain(ks[3], (L, D_MODEL)),
        "norm_ffn_pre": gain(ks[4], (L, D_MODEL)),
        "norm_ffn_post": gain(ks[5], (L, D_MODEL)),
        "w_in": nrm(ks[6], (L, D_MODEL, IN_WIDTH), D_MODEL ** -0.5),
        "cmp_pos_k": nrm(ks[7], (L, CMP_BLOCK, HEAD_DIM), 0.1),
        "cmp_w1_k": nrm(ks[8], (L, CMP_BLOCK * HEAD_DIM, CMP_HIDDEN), (CMP_BLOCK * HEAD_DIM) ** -0.5),
        "cmp_w2_k": nrm(ks[9], (L, CMP_HIDDEN, HEAD_DIM), CMP_HIDDEN ** -0.5),
        "cmp_pos_v": nrm(ks[10], (L, CMP_BLOCK, HEAD_DIM), 0.1),
        "cmp_w1_v": nrm(ks[11], (L, CMP_BLOCK * HEAD_DIM, CMP_HIDDEN), (CMP_BLOCK * HEAD_DIM) ** -0.5),
        "cmp_w2_v": nrm(ks[12], (L, CMP_HIDDEN, HEAD_DIM), CMP_HIDDEN ** -0.5),
        "gmlp_ln_g": gain(ks[13], (L, GMLP_WIDTH)),
        "gmlp_ln_b": nrm(ks[14], (L, GMLP_WIDTH), 0.02),
        "gmlp_w_s": nrm(ks[15], (L, N_GMLP_GROUPS, GMLP_CHUNK, GMLP_CHUNK), GMLP_CHUNK ** -0.5),
        "gmlp_b_s": gain(ks[16], (L, N_GMLP_GROUPS, GMLP_CHUNK)),
        "w_out": nrm(ks[17], (L, D_MODEL, D_MODEL), D_MODEL ** -0.5),
        "w_gate_up": nrm(ks[18], (L, D_MODEL, 2 * D_FF), D_MODEL ** -0.5),
        "w_down": nrm(ks[19], (L, D_FF, D_MODEL), D_FF ** -0.5),
    }


def reference(x, rel_bias, norm_mix_pre, norm_mix_post, norm_ffn_pre, norm_ffn_post, w_in,
              cmp_pos_k, cmp_w1_k, cmp_w2_k, cmp_pos_v, cmp_w1_v, cmp_w2_v,
              gmlp_ln_g, gmlp_ln_b, gmlp_w_s, gmlp_b_s, w_out, w_gate_up, w_down):
    h = x
    for l in range(DEPTH):
        mix = nsa_gmlp_mixer(rms_norm(h, norm_mix_pre[l]), w_in[l],
                             cmp_pos_k[l], cmp_w1_k[l], cmp_w2_k[l],
                             cmp_pos_v[l], cmp_w1_v[l], cmp_w2_v[l],
                             gmlp_ln_g[l], gmlp_ln_b[l], gmlp_w_s[l], gmlp_b_s[l],
                             w_out[l], rel_bias)
        h = h + rms_norm(mix, norm_mix_post[l])
        ffn = swiglu_ffn(rms_norm(h, norm_ffn_pre[l]), w_gate_up[l], w_down[l])
        h = h + rms_norm(ffn, norm_ffn_post[l])
    return h
```

```python
import functools
import math

import jax
import jax.numpy as jnp
import numpy as np
from jax import lax
from jax.experimental import pallas as pl
from jax.experimental.pallas import tpu as pltpu

F32 = jnp.float32
BF16 = jnp.bfloat16

N_NSA_HEADS = 8
N_KV_HEADS = 2
GQA_GROUP = N_NSA_HEADS // N_KV_HEADS
HEAD_DIM = 64
NSA_WIDTH = N_NSA_HEADS * HEAD_DIM
KV_WIDTH = N_KV_HEADS * HEAD_DIM
N_BRANCHES = 3
N_GATES = N_BRANCHES * N_NSA_HEADS
CMP_BLOCK = 32
CMP_STRIDE = 16
SLC_BLOCK = 64
SLC_TOPK = 16
N_LOCAL_BLOCKS = 2
WINDOW = 512
Q_BLOCK = 128
N_GMLP_GROUPS = 8
GMLP_CHUNK = 128
N_BUCKETS = 32
REL_MAX_DISTANCE = 128
RMS_EPS = 1e-6
LN_EPS = 1e-5

ROWS = GQA_GROUP * Q_BLOCK
KEY_CHUNK = 128
CMP_FRONT_PAD = 24
CMP_WINDOW = 32
NEG = -1e30
LANE = 128
GATE_PAD = 128
VMEM_LIMIT = 48 * 1024 * 1024


def _dot(a, b):
    return jnp.dot(a, b, preferred_element_type=F32)


def _gelu(x):
    c = math.sqrt(2.0 / math.pi)
    return 0.5 * x * (1.0 + jnp.tanh(c * (x + 0.044715 * (x * x * x))))


def _sigmoid(x):
    return 1.0 / (1.0 + jnp.exp(-x))


def _rms(x, g):
    ms = jnp.mean(x * x, axis=-1, keepdims=True)
    return (x * lax.rsqrt(ms + RMS_EPS)) * g


def _resident(a):
    return pl.BlockSpec(a.shape, lambda i: (0,) * a.ndim, pipeline_mode=pl.Buffered(1))


def _params(n_axes):
    return pltpu.CompilerParams(dimension_semantics=("arbitrary",) * n_axes,
                                vmem_limit_bytes=VMEM_LIMIT)


def _inproj_body(x_ref, gpre_ref, wq_ref, wkv_ref, wg_ref, wu_ref, wv_ref, lng_ref, lnb_ref,
                 ws_ref, bs_ref, q_ref, cv_ref, sw_ref, gate_ref, gm_ref):
    tm = x_ref.shape[0]
    xb = _rms(x_ref[...], gpre_ref[...]).astype(BF16)
    q_ref[...] = (_dot(xb, wq_ref[...]) * (HEAD_DIM ** -0.5)).astype(BF16)
    kv = _dot(xb, wkv_ref[...])
    cv_ref[...] = kv[:, :2 * KV_WIDTH]
    sw_ref[...] = kv[:, 2 * KV_WIDTH:].astype(BF16)
    gate_ref[...] = _sigmoid(_dot(xb, wg_ref[...]))
    zu = _gelu(_dot(xb, wu_ref[...]))
    zv = _gelu(_dot(xb, wv_ref[...]))
    mu = jnp.mean(zv, axis=-1, keepdims=True)
    zc = zv - mu
    var = jnp.mean(zc * zc, axis=-1, keepdims=True)
    zv = ((zc * lax.rsqrt(var + LN_EPS)) * lng_ref[...] + lnb_ref[...]).astype(BF16)
    gdim = zv.shape[1] // N_GMLP_GROUPS
    left = lax.broadcasted_iota(jnp.int32, (GMLP_CHUNK, LANE), 1) < gdim
    for c in range(tm // GMLP_CHUNK):
        rows = slice(c * GMLP_CHUNK, (c + 1) * GMLP_CHUNK)
        for j in range(zv.shape[1] // LANE):
            cols = slice(j * LANE, (j + 1) * LANE)
            z = zv[rows, cols]
            sv = jnp.where(left, _dot(ws_ref[2 * j], z), _dot(ws_ref[2 * j + 1], z)) + bs_ref[:, cols]
            gm_ref[rows, cols] = (zu[rows, cols] * sv).astype(BF16)


def _inproj(x, gpre, wq, wkv, wg, wu, wv, lng, lnb, ws, bs, tm):
    n, d = x.shape
    full = _resident
    row = lambda w: pl.BlockSpec((tm, w), lambda i: (i, 0))
    weights = (gpre, wq, wkv, wg, wu, wv, lng, lnb, ws, bs)
    return pl.pallas_call(
        _inproj_body,
        grid=(n // tm,),
        in_specs=[row(d)] + [full(a) for a in weights],
        out_specs=[row(NSA_WIDTH), row(2 * KV_WIDTH), row(4 * KV_WIDTH), row(GATE_PAD), row(wu.shape[1])],
        out_shape=[jax.ShapeDtypeStruct((n, NSA_WIDTH), BF16),
                   jax.ShapeDtypeStruct((n, 2 * KV_WIDTH), F32),
                   jax.ShapeDtypeStruct((n, 4 * KV_WIDTH), BF16),
                   jax.ShapeDtypeStruct((n, GATE_PAD), F32),
                   jax.ShapeDtypeStruct((n, wu.shape[1]), BF16)],
        compiler_params=_params(1),
        name="inproj_gmlp",
    )(x, *weights)


def _compress_body(x_ref, pos_ref, w1_ref, w2_ref, o_ref):
    x = x_ref[...]
    half = x.shape[1]
    a = _dot((x + pos_ref[0:1, :]).astype(BF16), w1_ref[:half, :])
    b = _dot((x + pos_ref[1:2, :]).astype(BF16), w1_ref[half:, :])
    pre = a + pltpu.roll(b, x.shape[0] - 1, 0)
    o_ref[...] = _dot(_gelu(pre).astype(BF16), w2_ref[...])


def _compress(xc, pos, w1, w2):
    b, two, g, nch, width = xc.shape
    hid = w1.shape[2]
    dh = w2.shape[2]
    return pl.pallas_call(
        _compress_body,
        grid=(b, two, g),
        in_specs=[pl.BlockSpec((None, None, None, nch, width), lambda i, t, j: (i, t, j, 0, 0)),
                  pl.BlockSpec((None, 2, width), lambda i, t, j: (t, 0, 0)),
                  pl.BlockSpec((None, 2 * width, hid), lambda i, t, j: (t, 0, 0)),
                  pl.BlockSpec((None, hid, dh), lambda i, t, j: (t, 0, 0))],
        out_specs=pl.BlockSpec((None, None, None, nch, dh), lambda i, t, j: (i, t, j, 0, 0)),
        out_shape=jax.ShapeDtypeStruct((b, two, g, nch, dh), F32),
        compiler_params=_params(3),
        name="compress",
    )(xc, pos, w1, w2)


def _cmp_select_body(q_ref, k_ref, vt_ref, crow_ref, cb_ref, oc_ref, sel_ref, sc_ref, ps_ref, *, n_other):
    qi = pl.program_id(2)
    ncp = sc_ref.shape[0]
    n_slc = sel_ref.shape[0]
    qt = q_ref[...]
    crow = crow_ref[...]
    w0 = pl.multiple_of(qi * (Q_BLOCK // CMP_STRIDE), 8)
    n_chunks = (w0 + CMP_WINDOW + KEY_CHUNK - 1) // KEY_CHUNK
    row_iota = lax.broadcasted_iota(jnp.int32, (KEY_CHUNK, ROWS), 0)

    @pl.when(qi == 0)
    def _():
        ps_ref[...] = jnp.zeros_like(ps_ref)

    def rows_of(c):
        return pl.ds(pl.multiple_of(c * KEY_CHUNK, KEY_CHUNK), KEY_CHUNK)

    def logits(c, m):
        s = _dot(k_ref[rows_of(c), :].astype(BF16), qt) + crow
        r = row_iota + c * KEY_CHUNK
        s = jnp.where((r >= CMP_FRONT_PAD) & (r < w0), s, NEG)
        sc_ref[rows_of(c), :] = s
        return jnp.maximum(m, jnp.max(s, axis=0, keepdims=True))

    m = lax.fori_loop(0, n_chunks, logits, jnp.full((1, ROWS), NEG, F32))
    win = pl.ds(w0, CMP_WINDOW)
    s_win = _dot(k_ref[win, :].astype(BF16), qt) + cb_ref[...]
    win_iota = lax.broadcasted_iota(jnp.int32, (CMP_WINDOW, ROWS), 0)
    s_win = jnp.where(win_iota + w0 >= CMP_FRONT_PAD, s_win, NEG)
    m = jnp.maximum(m, jnp.max(s_win, axis=0, keepdims=True))

    def expsum(c, l):
        p = jnp.exp(sc_ref[rows_of(c), :] - m)
        sc_ref[rows_of(c), :] = p
        return l + jnp.sum(p, axis=0, keepdims=True)

    l = lax.fori_loop(0, n_chunks, expsum, jnp.zeros((1, ROWS), F32))
    p_win = jnp.exp(s_win - m)
    sc_ref[win, :] = p_win
    l = l + jnp.sum(p_win, axis=0, keepdims=True)
    scale = jnp.where(m > 0.5 * NEG, 1.0 / jnp.maximum(l, 1e-30), 0.0)

    def weighted(c, acc):
        pn = sc_ref[rows_of(c), :] * scale
        ps = pn[:, 0:Q_BLOCK]
        for r in range(1, GQA_GROUP):
            ps = ps + pn[:, r * Q_BLOCK:(r + 1) * Q_BLOCK]
        ps_ref[rows_of(c), :] = ps
        return acc + _dot(vt_ref[:, rows_of(c)], pn.astype(BF16))

    oc_ref[...] = lax.fori_loop(0, n_chunks, weighted, jnp.zeros(oc_ref.shape, F32))

    per = SLC_BLOCK // CMP_STRIDE
    part = lambda off: ps_ref[pl.ds(CMP_FRONT_PAD + off, n_slc, stride=per), :]
    imp = part(0)
    for k in range(1, per - 1):
        imp = imp + part(k)
    imp = imp + 0.5 * (part(per - 1) + part(-1))

    blk = lax.broadcasted_iota(jnp.int32, (n_slc, Q_BLOCK), 0)
    pos = qi * Q_BLOCK + lax.broadcasted_iota(jnp.int32, (n_slc, Q_BLOCK), 1)
    jq = pos // SLC_BLOCK
    valid = blk <= jq
    forced = (blk == 0) | (valid & (blk > jq - N_LOCAL_BLOCKS))
    blk_f = blk.astype(F32)
    work = jnp.where(valid & jnp.logical_not(forced), imp, -1.0)
    sel = jnp.where(forced, 1.0, 0.0)
    for _ in range(n_other):
        mx = jnp.max(work, axis=0, keepdims=True)
        first = jnp.min(jnp.where(work == mx, blk_f, float(n_slc)), axis=0, keepdims=True)
        pick = (blk_f == first) & (mx >= 0.0)
        sel = jnp.where(pick, 1.0, sel)
        work = jnp.where(pick, -1.0, work)
    sel_ref[...] = sel


def _cmp_select(qt, kcp, vct, crow, cb, n_slc, n_other):
    b, g, dh, total = qt.shape
    nqb = total // ROWS
    ncp = kcp.shape[2]
    blk_q = pl.BlockSpec((None, None, dh, ROWS), lambda i, j, q: (i, j, 0, q))
    return pl.pallas_call(
        functools.partial(_cmp_select_body, n_other=n_other),
        grid=(b, g, nqb),
        in_specs=[blk_q,
                  pl.BlockSpec((None, None, ncp, dh), lambda i, j, q: (i, j, 0, 0)),
                  pl.BlockSpec((None, None, dh, ncp), lambda i, j, q: (i, j, 0, 0)),
                  pl.BlockSpec((None, 1, ROWS), lambda i, j, q: (j, 0, 0)),
                  pl.BlockSpec((None, CMP_WINDOW, ROWS), lambda i, j, q: (j, 0, 0))],
        out_specs=[blk_q,
                   pl.BlockSpec((None, None, None, n_slc, Q_BLOCK), lambda i, j, q: (i, j, q, 0, 0))],
        out_shape=[jax.ShapeDtypeStruct((b, g, dh, total), F32),
                   jax.ShapeDtypeStruct((b, g, nqb, n_slc, Q_BLOCK), F32)],
        scratch_shapes=[pltpu.VMEM((ncp, ROWS), F32), pltpu.VMEM((ncp, Q_BLOCK), F32)],
        compiler_params=_params(3),
        name="cmp_select",
    )(qt, kcp, vct, crow, cb)


SLC_UNROLL = 4


def _slc_body(q_ref, k_ref, vt_ref, sel_ref, sb_ref, os_ref):
    qi = pl.program_id(2)
    qt = q_ref[...]
    dh = qt.shape[0]
    per = KEY_CHUNK // SLC_BLOCK

    def step(u, carry):
        m, l, acc = carry
        tiles = []
        for t in range(SLC_UNROLL):
            c = u * SLC_UNROLL + t
            rows = pl.ds(pl.multiple_of(c * KEY_CHUNK, KEY_CHUNK), KEY_CHUNK)
            kind = jnp.where(c > qi, 3, jnp.minimum(qi - c, 2))
            s = _dot(k_ref[rows, :], qt) + sb_ref[kind]
            parts = []
            for j in range(per):
                srow = sel_ref[pl.ds(c * per + j, 1), :]
                srow = jnp.concatenate([srow] * GQA_GROUP, axis=1)
                parts.append(jnp.where(srow > 0.5, s[j * SLC_BLOCK:(j + 1) * SLC_BLOCK, :], NEG))
            tiles.append(jnp.concatenate(parts, axis=0))
        s = jnp.concatenate(tiles, axis=0)
        m_new = jnp.maximum(m, jnp.max(s, axis=0, keepdims=True))
        alpha = jnp.exp(m - m_new)
        p = jnp.exp(s - m_new)
        l = alpha * l + jnp.sum(p, axis=0, keepdims=True)
        cols = pl.ds(pl.multiple_of(u * (SLC_UNROLL * KEY_CHUNK), SLC_UNROLL * KEY_CHUNK), SLC_UNROLL * KEY_CHUNK)
        acc = alpha * acc + _dot(vt_ref[:, cols], p.astype(BF16))
        return m_new, l, acc

    n_steps = qi // SLC_UNROLL + 1
    init = (jnp.full((1, ROWS), NEG, F32), jnp.zeros((1, ROWS), F32), jnp.zeros((dh, ROWS), F32))
    m, l, acc = lax.fori_loop(0, n_steps, step, init)
    os_ref[...] = acc / jnp.maximum(l, 1e-30)


def _slc(qt, ks, vst, sel, sb):
    b, g, dh, total = qt.shape
    nqb = total // ROWS
    s_len = ks.shape[2]
    n_slc = sel.shape[3]
    blk_q = pl.BlockSpec((None, None, dh, ROWS), lambda i, j, q: (i, j, 0, q))
    return pl.pallas_call(
        _slc_body,
        grid=(b, g, nqb),
        in_specs=[blk_q,
                  pl.BlockSpec((None, None, s_len, dh), lambda i, j, q: (i, j, 0, 0)),
                  pl.BlockSpec((None, None, dh, s_len), lambda i, j, q: (i, j, 0, 0)),
                  pl.BlockSpec((None, None, None, n_slc, Q_BLOCK), lambda i, j, q: (i, j, q, 0, 0)),
                  pl.BlockSpec((None, 4, KEY_CHUNK, ROWS), lambda i, j, q: (j, 0, 0, 0))],
        out_specs=blk_q,
        out_shape=jax.ShapeDtypeStruct((b, g, dh, total), F32),
        compiler_params=_params(3),
        name="slc_attention",
    )(qt, ks, vst, sel, sb)


WIN_CHUNKS = WINDOW // KEY_CHUNK + 1


def _win_body(q_ref, k_ref, vt_ref, wb_ref, oc_ref, os_ref, gate_ref, o_ref):
    qi = pl.program_id(2)
    qt = q_ref[...]
    tiles, vts = [], []
    for delta in range(WIN_CHUNKS - 1, -1, -1):
        c = jnp.maximum(qi - delta, 0)
        rows = pl.ds(pl.multiple_of(c * KEY_CHUNK, KEY_CHUNK), KEY_CHUNK)
        kind = jnp.where(qi >= delta, delta, WIN_CHUNKS)
        tiles.append(_dot(k_ref[rows, :], qt) + wb_ref[kind])
        vts.append(vt_ref[:, rows])
    s = jnp.concatenate(tiles, axis=0)
    m = jnp.max(s, axis=0, keepdims=True)
    p = jnp.exp(s - m)
    l = jnp.sum(p, axis=0, keepdims=True)
    o_w = _dot(jnp.concatenate(vts, axis=1), p.astype(BF16)) / jnp.maximum(l, 1e-30)
    o = gate_ref[0:1, :] * oc_ref[...] + gate_ref[1:2, :] * os_ref[...] + gate_ref[2:3, :] * o_w
    o_ref[...] = o.astype(o_ref.dtype)


def _win_mix(qt, kw, vwt, wb, oc, osl, gates):
    b, g, dh, total = qt.shape
    nqb = total // ROWS
    s_len = kw.shape[2]
    blk_q = pl.BlockSpec((None, None, dh, ROWS), lambda i, j, q: (i, j, 0, q))
    return pl.pallas_call(
        _win_body,
        grid=(b, g, nqb),
        in_specs=[blk_q,
                  pl.BlockSpec((None, None, s_len, dh), lambda i, j, q: (i, j, 0, 0)),
                  pl.BlockSpec((None, None, dh, s_len), lambda i, j, q: (i, j, 0, 0)),
                  pl.BlockSpec((None, WIN_CHUNKS + 1, KEY_CHUNK, ROWS), lambda i, j, q: (j, 0, 0, 0)),
                  blk_q, blk_q,
                  pl.BlockSpec((None, None, N_BRANCHES, ROWS), lambda i, j, q: (i, j, 0, q))],
        out_specs=blk_q,
        out_shape=jax.ShapeDtypeStruct((b, g, dh, total), BF16),
        compiler_params=_params(3),
        name="window_mix",
    )(qt, kw, vwt, wb, oc, osl, gates)


def _outproj_body(a_ref, gm_ref, h_ref, wa_ref, wb_ref, g_ref, o_ref):
    y = _dot(a_ref[...], wa_ref[...]) + _dot(gm_ref[...], wb_ref[...])
    o_ref[...] = h_ref[...] + _rms(y, g_ref[...])


def _outproj(a, gm, h, wa, wb, gpost, tm):
    n, d = h.shape
    full = _resident
    row = lambda w: pl.BlockSpec((tm, w), lambda i: (i, 0))
    return pl.pallas_call(
        _outproj_body,
        grid=(n // tm,),
        in_specs=[row(a.shape[1]), row(gm.shape[1]), row(d), full(wa), full(wb), full(gpost)],
        out_specs=row(d),
        out_shape=jax.ShapeDtypeStruct((n, d), F32),
        compiler_params=_params(1),
        name="outproj",
    )(a, gm, h, wa, wb, gpost)


FFN_TILE = 256


def _ffn_body(h_ref, gpre_ref, wg_ref, wu_ref, wd_ref, gpost_ref, o_ref):
    h = h_ref[...]
    xb = _rms(h, gpre_ref[...]).astype(BF16)
    acc = jnp.zeros(h.shape, F32)
    for j in range(wg_ref.shape[1] // FFN_TILE):
        cols = slice(j * FFN_TILE, (j + 1) * FFN_TILE)
        gate = _dot(xb, wg_ref[:, cols])
        up = _dot(xb, wu_ref[:, cols])
        act = (gate * _sigmoid(gate) * up).astype(BF16)
        acc = acc + _dot(act, wd_ref[cols, :])
    o_ref[...] = h + _rms(acc, gpost_ref[...])


def _ffn(h, gpre, wg, wu, wd, gpost, tm):
    n, d = h.shape
    full = _resident
    row = pl.BlockSpec((tm, d), lambda i: (i, 0))
    return pl.pallas_call(
        _ffn_body,
        grid=(n // tm,),
        in_specs=[row, full(gpre), full(wg), full(wu), full(wd), full(gpost)],
        out_specs=row,
        out_shape=jax.ShapeDtypeStruct((n, d), F32),
        compiler_params=_params(1),
        name="ffn",
    )(h, gpre, wg, wu, wd, gpost)


def _t5_bucket(dist):
    n = jnp.maximum(dist, 0)
    max_exact = N_BUCKETS // 2
    nf = jnp.maximum(n, max_exact).astype(F32)
    large = max_exact + (jnp.log(nf / max_exact) / math.log(REL_MAX_DISTANCE / max_exact)
                         * (N_BUCKETS - max_exact)).astype(jnp.int32)
    return jnp.where(n < max_exact, n, jnp.minimum(large, N_BUCKETS - 1))


def _bias_tile(table, dist, mask):
    b = table.astype(F32)[_t5_bucket(dist)]
    b = jnp.where(mask[..., None], b, NEG)
    k = dist.shape[0]
    return b.reshape(k, Q_BLOCK, N_KV_HEADS, GQA_GROUP).transpose(2, 0, 3, 1).reshape(N_KV_HEADS, k, ROWS)


def _bias_tiles(rel_bias):
    ql = jnp.arange(Q_BLOCK, dtype=jnp.int32)[None, :]
    kl = jnp.arange(KEY_CHUNK, dtype=jnp.int32)[:, None]
    chunk_dist = lambda delta: delta * KEY_CHUNK + ql - kl
    everything = jnp.ones((KEY_CHUNK, Q_BLOCK), bool)
    nothing = jnp.zeros((KEY_CHUNK, Q_BLOCK), bool)
    far = jnp.full((KEY_CHUNK, Q_BLOCK), REL_MAX_DISTANCE, jnp.int32)
    d0 = chunk_dist(0)
    sb = jnp.stack([_bias_tile(rel_bias, d0, d0 >= 0),
                    _bias_tile(rel_bias, chunk_dist(1), everything),
                    _bias_tile(rel_bias, far, everything),
                    _bias_tile(rel_bias, far, nothing)], axis=1)
    wtiles = []
    for delta in range(WIN_CHUNKS):
        d = chunk_dist(delta)
        wtiles.append(_bias_tile(rel_bias, d, (d >= 0) & (d < WINDOW)))
    wtiles.append(_bias_tile(rel_bias, far, nothing))
    wb = jnp.stack(wtiles, axis=1)
    rel = jnp.arange(CMP_WINDOW, dtype=jnp.int32)[:, None] - CMP_FRONT_PAD
    dc = ql - CMP_STRIDE * rel - (CMP_BLOCK - 1)
    cb = _bias_tile(rel_bias, dc, dc >= 0)
    crow = _bias_tile(rel_bias, far[:1], everything[:1])
    return sb, wb, cb, crow


def _to_rows(t, b, s):
    x = t.shape[1] // (N_KV_HEADS * GQA_GROUP)
    t = t.reshape(b, s // Q_BLOCK, Q_BLOCK, N_KV_HEADS, GQA_GROUP, x)
    return t.transpose(0, 3, 5, 1, 4, 2).reshape(b, N_KV_HEADS, x, (s // Q_BLOCK) * ROWS)


def _mixer(h, b, s, lw, tiles, tm):
    sb, wb, cb, crow = tiles
    q, cv, sw, gates, gm = _inproj(h, lw["gpre"], lw["wq"], lw["wkv"], lw["wg"], lw["wu"], lw["wv"],
                                   lw["lng"], lw["lnb"], lw["ws"], lw["bs"], tm)
    g, dh = N_KV_HEADS, HEAD_DIM
    n_chunks = s // CMP_STRIDE
    xc = cv.reshape(b, n_chunks, CMP_STRIDE, 2, g, dh).transpose(0, 3, 4, 1, 2, 5)
    xc = xc.reshape(b, 2, g, n_chunks, CMP_STRIDE * dh)
    comp = _compress(xc, lw["cpos"], lw["cw1"], lw["cw2"])
    ncp = -(-(CMP_FRONT_PAD + n_chunks + CMP_WINDOW - CMP_FRONT_PAD) // KEY_CHUNK) * KEY_CHUNK
    back = ncp - CMP_FRONT_PAD - n_chunks
    comp = jnp.pad(comp, ((0, 0), (0, 0), (0, 0), (CMP_FRONT_PAD, back), (0, 0)))
    kcp = comp[:, 0]
    vct = comp[:, 1].transpose(0, 1, 3, 2).astype(BF16)

    qt = _to_rows(q, b, s)
    n_slc = s // SLC_BLOCK
    n_other = min(SLC_TOPK, n_slc) - (N_LOCAL_BLOCKS + 1)
    oc, sel = _cmp_select(qt, kcp, vct, crow, cb, n_slc, n_other)

    heads = lambda t: t.reshape(b, s, g, dh).transpose(0, 2, 1, 3)
    heads_t = lambda t: t.reshape(b, s, g, dh).transpose(0, 2, 3, 1)
    kvw = KV_WIDTH
    osl = _slc(qt, heads(sw[:, 0:kvw]), heads_t(sw[:, kvw:2 * kvw]), sel, sb)
    gates_t = _to_rows(gates[:, :N_GATES], b, s)
    o = _win_mix(qt, heads(sw[:, 2 * kvw:3 * kvw]), heads_t(sw[:, 3 * kvw:]), wb, oc, osl, gates_t)
    nqb = s // Q_BLOCK
    nsa = o.reshape(b, g, dh, nqb, GQA_GROUP, Q_BLOCK).transpose(0, 3, 5, 1, 4, 2).reshape(b * s, NSA_WIDTH)
    return _outproj(nsa, gm, h, lw["wo_a"], lw["wo_b"], lw["gpost"], tm)


def _layer_weights(l, p):
    d = p["w_in"].shape[1]
    w_in = p["w_in"][l]
    o = NSA_WIDTH
    wq = w_in[:, :o]
    wkv = w_in[:, o:o + 6 * KV_WIDTH]
    o += 6 * KV_WIDTH
    wg = jnp.pad(w_in[:, o:o + N_GATES], ((0, 0), (0, GATE_PAD - N_GATES)))
    o += N_GATES
    gw = (w_in.shape[1] - o) // 2
    wu, wv = w_in[:, o:o + gw], w_in[:, o + gw:]
    causal = jnp.tril(jnp.ones((GMLP_CHUNK, GMLP_CHUNK), bool))
    ws = jnp.where(causal, p["gmlp_w_s"][l], 0.0)
    bs = jnp.repeat(p["gmlp_b_s"][l].T, gw // N_GMLP_GROUPS, axis=1)
    half = CMP_STRIDE * HEAD_DIM
    cpos = jnp.stack([p["cmp_pos_k"][l].reshape(2, half), p["cmp_pos_v"][l].reshape(2, half)])
    dff = p["w_down"].shape[1]
    row = lambda v: v[l].reshape(1, -1)
    bf = lambda w: w.astype(BF16)
    return dict(gpre=row(p["norm_mix_pre"]), wq=bf(wq), wkv=bf(wkv), wg=bf(wg), wu=bf(wu), wv=bf(wv),
                lng=row(p["gmlp_ln_g"]), lnb=row(p["gmlp_ln_b"]), ws=bf(ws), bs=bs,
                cpos=cpos, cw1=bf(jnp.stack([p["cmp_w1_k"][l], p["cmp_w1_v"][l]])),
                cw2=bf(jnp.stack([p["cmp_w2_k"][l], p["cmp_w2_v"][l]])),
                wo_a=bf(p["w_out"][l][:NSA_WIDTH]), wo_b=bf(p["w_out"][l][NSA_WIDTH:]),
                gpost=row(p["norm_mix_post"]), fpre=row(p["norm_ffn_pre"]), fpost=row(p["norm_ffn_post"]),
                fwg=bf(p["w_gate_up"][l][:, :dff]), fwu=bf(p["w_gate_up"][l][:, dff:]), fwd=bf(p["w_down"][l]))


def _trunk(p, tm):
    x = p["x"]
    b, s, d = x.shape
    h = x.reshape(b * s, d)
    tiles = _bias_tiles(p["rel_bias"])
    for l in range(p["w_in"].shape[0]):
        lw = _layer_weights(l, p)
        h = _mixer(h, b, s, lw, tiles, tm)
        h = _ffn(h, lw["fpre"], lw["fwg"], lw["fwu"], lw["fwd"], lw["fpost"], tm)
    return h.reshape(b, s, d)


def kernel(x, rel_bias, norm_mix_pre, norm_mix_post, norm_ffn_pre, norm_ffn_post, w_in, cmp_pos_k, cmp_w1_k, cmp_w2_k, cmp_pos_v, cmp_w1_v, cmp_w2_v, gmlp_ln_g, gmlp_ln_b, gmlp_w_s, gmlp_b_s, w_out, w_gate_up, w_down):
    p = dict(x=x, rel_bias=rel_bias, norm_mix_pre=norm_mix_pre, norm_mix_post=norm_mix_post,
             norm_ffn_pre=norm_ffn_pre, norm_ffn_post=norm_ffn_post, w_in=w_in,
             cmp_pos_k=cmp_pos_k, cmp_w1_k=cmp_w1_k, cmp_w2_k=cmp_w2_k,
             cmp_pos_v=cmp_pos_v, cmp_w1_v=cmp_w1_v, cmp_w2_v=cmp_w2_v,
             gmlp_ln_g=gmlp_ln_g, gmlp_ln_b=gmlp_ln_b, gmlp_w_s=gmlp_w_s, gmlp_b_s=gmlp_b_s,
             w_out=w_out, w_gate_up=w_gate_up, w_down=w_down)
    return _trunk(p, tm=512)
```

```python
import functools
import math

import jax
import jax.numpy as jnp
import numpy as np
from jax import lax
from jax.experimental import pallas as pl
from jax.experimental.pallas import tpu as pltpu

F32 = jnp.float32
BF16 = jnp.bfloat16

N_NSA_HEADS = 8
N_KV_HEADS = 2
GQA_GROUP = N_NSA_HEADS // N_KV_HEADS
HEAD_DIM = 64
NSA_WIDTH = N_NSA_HEADS * HEAD_DIM
KV_WIDTH = N_KV_HEADS * HEAD_DIM
N_BRANCHES = 3
N_GATES = N_BRANCHES * N_NSA_HEADS
CMP_BLOCK = 32
CMP_STRIDE = 16
SLC_BLOCK = 64
SLC_TOPK = 16
N_LOCAL_BLOCKS = 2
WINDOW = 512
Q_BLOCK = 128
N_GMLP_GROUPS = 8
GMLP_CHUNK = 128
N_BUCKETS = 32
REL_MAX_DISTANCE = 128
RMS_EPS = 1e-6
LN_EPS = 1e-5

ROWS = GQA_GROUP * Q_BLOCK
KEY_CHUNK = 128
CMP_FRONT_PAD = 24
CMP_WINDOW = 32
NEG = -1e30
LANE = 128
GATE_PAD = 128
VMEM_LIMIT = 48 * 1024 * 1024
LOG2E = math.log2(math.e)
Q_SCALE = HEAD_DIM ** -0.5 * LOG2E


def _dot(a, b):
    return jnp.dot(a, b, preferred_element_type=F32)


def _gelu(x):
    c = math.sqrt(2.0 / math.pi)
    return 0.5 * x * (1.0 + jnp.tanh(c * (x + 0.044715 * (x * x * x))))


def _sigmoid(x):
    return 1.0 / (1.0 + jnp.exp(-x))


def _rms(x, g):
    ms = jnp.mean(x * x, axis=-1, keepdims=True)
    return (x * lax.rsqrt(ms + RMS_EPS)) * g


def _resident(a):
    return pl.BlockSpec(a.shape, lambda i: (0,) * a.ndim, pipeline_mode=pl.Buffered(1))


def _params(n_axes):
    return pltpu.CompilerParams(dimension_semantics=("arbitrary",) * n_axes,
                                vmem_limit_bytes=VMEM_LIMIT)


def _inproj_body(x_ref, gpre_ref, wq_ref, wkv_ref, wg_ref, wu_ref, wv_ref, lng_ref, lnb_ref,
                 ws_ref, bs_ref, q_ref, cv_ref, sw_ref, gate_ref, gm_ref):
    tm = x_ref.shape[0]
    xb = _rms(x_ref[...], gpre_ref[...]).astype(BF16)
    q_ref[...] = (_dot(xb, wq_ref[...]) * Q_SCALE).astype(BF16)
    kv = _dot(xb, wkv_ref[...])
    cv_ref[...] = kv[:, :2 * KV_WIDTH]
    sw_ref[...] = kv[:, 2 * KV_WIDTH:].astype(BF16)
    gate_ref[...] = _sigmoid(_dot(xb, wg_ref[...]))
    zu = _gelu(_dot(xb, wu_ref[...]))
    zv = _gelu(_dot(xb, wv_ref[...]))
    mu = jnp.mean(zv, axis=-1, keepdims=True)
    zc = zv - mu
    var = jnp.mean(zc * zc, axis=-1, keepdims=True)
    zv = ((zc * lax.rsqrt(var + LN_EPS)) * lng_ref[...] + lnb_ref[...]).astype(BF16)
    gdim = zv.shape[1] // N_GMLP_GROUPS
    left = lax.broadcasted_iota(jnp.int32, (GMLP_CHUNK, LANE), 1) < gdim
    for c in range(tm // GMLP_CHUNK):
        rows = slice(c * GMLP_CHUNK, (c + 1) * GMLP_CHUNK)
        for j in range(zv.shape[1] // LANE):
            cols = slice(j * LANE, (j + 1) * LANE)
            z = zv[rows, cols]
            sv = jnp.where(left, _dot(ws_ref[2 * j], z), _dot(ws_ref[2 * j + 1], z)) + bs_ref[:, cols]
            gm_ref[rows, cols] = (zu[rows, cols] * sv).astype(BF16)


def _inproj(x, gpre, wq, wkv, wg, wu, wv, lng, lnb, ws, bs, tm):
    n, d = x.shape
    full = _resident
    row = lambda w: pl.BlockSpec((tm, w), lambda i: (i, 0))
    weights = (gpre, wq, wkv, wg, wu, wv, lng, lnb, ws, bs)
    return pl.pallas_call(
        _inproj_body,
        grid=(n // tm,),
        in_specs=[row(d)] + [full(a) for a in weights],
        out_specs=[row(NSA_WIDTH), row(2 * KV_WIDTH), row(4 * KV_WIDTH), row(GATE_PAD), row(wu.shape[1])],
        out_shape=[jax.ShapeDtypeStruct((n, NSA_WIDTH), BF16),
                   jax.ShapeDtypeStruct((n, 2 * KV_WIDTH), F32),
                   jax.ShapeDtypeStruct((n, 4 * KV_WIDTH), BF16),
                   jax.ShapeDtypeStruct((n, GATE_PAD), F32),
                   jax.ShapeDtypeStruct((n, wu.shape[1]), BF16)],
        compiler_params=_params(1),
        name="inproj_gmlp",
    )(x, *weights)


def _compress_body(x_ref, pos_ref, w1_ref, w2_ref, o_ref):
    x = x_ref[...]
    half = x.shape[1]
    a = _dot((x + pos_ref[0:1, :]).astype(BF16), w1_ref[:half, :])
    b = _dot((x + pos_ref[1:2, :]).astype(BF16), w1_ref[half:, :])
    pre = a + pltpu.roll(b, x.shape[0] - 1, 0)
    o_ref[...] = _dot(_gelu(pre).astype(BF16), w2_ref[...])


def _compress(xc, pos, w1, w2):
    b, two, g, nch, width = xc.shape
    hid = w1.shape[2]
    dh = w2.shape[2]
    return pl.pallas_call(
        _compress_body,
        grid=(b, two, g),
        in_specs=[pl.BlockSpec((None, None, None, nch, width), lambda i, t, j: (i, t, j, 0, 0)),
                  pl.BlockSpec((None, 2, width), lambda i, t, j: (t, 0, 0)),
                  pl.BlockSpec((None, 2 * width, hid), lambda i, t, j: (t, 0, 0)),
                  pl.BlockSpec((None, hid, dh), lambda i, t, j: (t, 0, 0))],
        out_specs=pl.BlockSpec((None, None, None, nch, dh), lambda i, t, j: (i, t, j, 0, 0)),
        out_shape=jax.ShapeDtypeStruct((b, two, g, nch, dh), F32),
        compiler_params=_params(3),
        name="compress",
    )(xc, pos, w1, w2)


def _cmp_select_body(q_ref, k_ref, vt_ref, crow_ref, cb_ref, oc_ref, sel_ref, sc_ref, ps_ref, *, n_other):
    qi = pl.program_id(2)
    ncp = sc_ref.shape[0]
    n_slc = sel_ref.shape[0]
    qt = q_ref[...]
    crow = crow_ref[...]
    w0 = pl.multiple_of(qi * (Q_BLOCK // CMP_STRIDE), 8)
    n_chunks = (w0 + CMP_WINDOW + KEY_CHUNK - 1) // KEY_CHUNK
    row_iota = lax.broadcasted_iota(jnp.int32, (KEY_CHUNK, ROWS), 0)

    @pl.when(qi == 0)
    def _():
        ps_ref[...] = jnp.zeros_like(ps_ref)

    def rows_of(c):
        return pl.ds(pl.multiple_of(c * KEY_CHUNK, KEY_CHUNK), KEY_CHUNK)

    def logits(c, m):
        s = _dot(k_ref[rows_of(c), :].astype(BF16), qt) + crow
        r = row_iota + c * KEY_CHUNK
        s = jnp.where((r >= CMP_FRONT_PAD) & (r < w0), s, NEG)
        sc_ref[rows_of(c), :] = s
        return jnp.maximum(m, jnp.max(s, axis=0, keepdims=True))

    m = lax.fori_loop(0, n_chunks, logits, jnp.full((1, ROWS), NEG, F32))
    win = pl.ds(w0, CMP_WINDOW)
    s_win = _dot(k_ref[win, :].astype(BF16), qt) + cb_ref[...]
    win_iota = lax.broadcasted_iota(jnp.int32, (CMP_WINDOW, ROWS), 0)
    s_win = jnp.where(win_iota + w0 >= CMP_FRONT_PAD, s_win, NEG)
    m = jnp.maximum(m, jnp.max(s_win, axis=0, keepdims=True))

    def expsum(c, l):
        p = jnp.exp2(sc_ref[rows_of(c), :] - m)
        sc_ref[rows_of(c), :] = p
        return l + jnp.sum(p, axis=0, keepdims=True)

    l = lax.fori_loop(0, n_chunks, expsum, jnp.zeros((1, ROWS), F32))
    p_win = jnp.exp2(s_win - m)
    sc_ref[win, :] = p_win
    l = l + jnp.sum(p_win, axis=0, keepdims=True)
    scale = jnp.where(m > 0.5 * NEG, 1.0 / jnp.maximum(l, 1e-30), 0.0)

    def weighted(c, acc):
        pn = sc_ref[rows_of(c), :] * scale
        ps = pn[:, 0:Q_BLOCK]
        for r in range(1, GQA_GROUP):
            ps = ps + pn[:, r * Q_BLOCK:(r + 1) * Q_BLOCK]
        ps_ref[rows_of(c), :] = ps
        return acc + _dot(vt_ref[:, rows_of(c)], pn.astype(BF16))

    oc_ref[...] = lax.fori_loop(0, n_chunks, weighted, jnp.zeros(oc_ref.shape, F32))

    per = SLC_BLOCK // CMP_STRIDE
    part = lambda off: ps_ref[pl.ds(CMP_FRONT_PAD + off, n_slc, stride=per), :]
    imp = part(0)
    for k in range(1, per - 1):
        imp = imp + part(k)
    imp = imp + 0.5 * (part(per - 1) + part(-1))

    blk = lax.broadcasted_iota(jnp.int32, (n_slc, Q_BLOCK), 0)
    pos = qi * Q_BLOCK + lax.broadcasted_iota(jnp.int32, (n_slc, Q_BLOCK), 1)
    jq = pos // SLC_BLOCK
    valid = blk <= jq
    forced = (blk == 0) | (valid & (blk > jq - N_LOCAL_BLOCKS))
    blk_f = blk.astype(F32)
    work = jnp.where(valid & jnp.logical_not(forced), imp, -1.0)
    sel = jnp.where(forced, 1.0, 0.0)
    for _ in range(n_other):
        mx = jnp.max(work, axis=0, keepdims=True)
        first = jnp.min(jnp.where(work == mx, blk_f, float(n_slc)), axis=0, keepdims=True)
        pick = (blk_f == first) & (mx >= 0.0)
        sel = jnp.where(pick, 1.0, sel)
        work = jnp.where(pick, -1.0, work)
    sel_ref[...] = sel


def _cmp_select(qt, kcp, vct, crow, cb, n_slc, n_other):
    b, g, dh, total = qt.shape
    nqb = total // ROWS
    ncp = kcp.shape[2]
    blk_q = pl.BlockSpec((None, None, dh, ROWS), lambda i, j, q: (i, j, 0, q))
    return pl.pallas_call(
        functools.partial(_cmp_select_body, n_other=n_other),
        grid=(b, g, nqb),
        in_specs=[blk_q,
                  pl.BlockSpec((None, None, ncp, dh), lambda i, j, q: (i, j, 0, 0)),
                  pl.BlockSpec((None, None, dh, ncp), lambda i, j, q: (i, j, 0, 0)),
                  pl.BlockSpec((None, 1, ROWS), lambda i, j, q: (j, 0, 0)),
                  pl.BlockSpec((None, CMP_WINDOW, ROWS), lambda i, j, q: (j, 0, 0))],
        out_specs=[blk_q,
                   pl.BlockSpec((None, None, None, n_slc, Q_BLOCK), lambda i, j, q: (i, j, q, 0, 0))],
        out_shape=[jax.ShapeDtypeStruct((b, g, dh, total), F32),
                   jax.ShapeDtypeStruct((b, g, nqb, n_slc, Q_BLOCK), F32)],
        scratch_shapes=[pltpu.VMEM((ncp, ROWS), F32), pltpu.VMEM((ncp, Q_BLOCK), F32)],
        compiler_params=_params(3),
        name="cmp_select",
    )(qt, kcp, vct, crow, cb)


FAR_KEYS = 512
FAR_BLOCKS = FAR_KEYS // SLC_BLOCK
FAR_CHUNKS = FAR_KEYS // KEY_CHUNK
AUG_CONST = 2
AUG_ROWS = 16
K_AUG_WIDTH = 2 * HEAD_DIM


def _augment_keys(k):
    s_len = k.shape[-2]
    pos = jnp.arange(s_len, dtype=jnp.int32)
    onehot = ((pos[:, None] // SLC_BLOCK) % FAR_BLOCKS == jnp.arange(FAR_BLOCKS, dtype=jnp.int32)[None, :])
    aug = jnp.concatenate([jnp.ones((s_len, AUG_CONST), k.dtype),
                           jnp.zeros((s_len, AUG_ROWS - FAR_BLOCKS - AUG_CONST), k.dtype),
                           onehot.astype(k.dtype),
                           jnp.zeros((s_len, K_AUG_WIDTH - HEAD_DIM - AUG_ROWS), k.dtype)], axis=1)
    return jnp.concatenate([k, jnp.broadcast_to(aug, k.shape[:-2] + aug.shape)], axis=-1)


def _slc_body(q_ref, k_ref, vt_ref, sel_ref, sb_ref, crow_ref, os_ref, sbuf):
    qi = pl.program_id(2)
    qt = q_ref[...]
    dh = qt.shape[0]
    per = KEY_CHUNK // SLC_BLOCK
    n_far_chunks = jnp.maximum(qi - 1, 0)
    far_limit = n_far_chunks * per
    n_steps = (n_far_chunks + FAR_CHUNKS - 1) // FAR_CHUNKS
    last_step = k_ref.shape[0] // FAR_KEYS - 1

    c = crow_ref[...]
    c_hi = c.astype(BF16).astype(F32)
    const_rows = jnp.concatenate([c_hi, c - c_hi, jnp.zeros((AUG_ROWS - FAR_BLOCKS - AUG_CONST, ROWS), F32)], axis=0)
    pad_rows = jnp.zeros((k_ref.shape[1] - dh - AUG_ROWS, ROWS), BF16)
    blk_iota = lax.broadcasted_iota(jnp.int32, (FAR_BLOCKS, Q_BLOCK), 0)

    def far_logits(u):
        ua = jnp.minimum(u, last_step)
        grp = sel_ref[pl.ds(pl.multiple_of(ua * FAR_BLOCKS, FAR_BLOCKS), FAR_BLOCKS), :]
        pen = jnp.where((grp > 0.5) & (blk_iota + u * FAR_BLOCKS < far_limit), 0.0, NEG)
        pen = jnp.concatenate([pen] * GQA_GROUP, axis=1)
        qa = jnp.concatenate([qt, jnp.concatenate([const_rows, pen], axis=0).astype(BF16), pad_rows], axis=0)
        return _dot(k_ref[pl.ds(pl.multiple_of(ua * FAR_KEYS, FAR_KEYS), FAR_KEYS), :], qa)

    def update(m, l, acc, s, mx, vt):
        m_new = jnp.maximum(m, mx)
        alpha = jnp.exp2(m - m_new)
        p = jnp.exp2(s - m_new)
        l = alpha * l + jnp.sum(p, axis=0, keepdims=True)
        acc = alpha * acc + _dot(vt, p.astype(BF16))
        return m_new, l, acc

    def far_pair(v, carry):
        m, l, acc, mx = carry
        for half in range(2):
            u = 2 * v + half
            s_next = far_logits(u + 1)
            sbuf[1 - half] = s_next
            mx_next = jnp.max(s_next, axis=0, keepdims=True)
            ua = jnp.minimum(u, last_step)
            cols = pl.ds(pl.multiple_of(ua * FAR_KEYS, FAR_KEYS), FAR_KEYS)
            m, l, acc = update(m, l, acc, sbuf[half], mx, vt_ref[:, cols])
            mx = mx_next
        return m, l, acc, mx

    s0 = far_logits(0)
    sbuf[0] = s0
    init = (jnp.full((1, ROWS), NEG, F32), jnp.zeros((1, ROWS), F32), jnp.zeros((dh, ROWS), F32),
            jnp.max(s0, axis=0, keepdims=True))
    m, l, acc, _ = lax.fori_loop(0, (n_steps + 1) // 2, far_pair, init)

    qd = jnp.concatenate([qt, jnp.zeros((k_ref.shape[1] - dh, ROWS), BF16)], axis=0)
    prev = jnp.maximum(qi - 1, 0)
    tiles, vts = [], []
    for chunk, kind in ((prev, jnp.where(qi >= 1, 1, 3)), (qi, 0)):
        rows = pl.ds(pl.multiple_of(chunk * KEY_CHUNK, KEY_CHUNK), KEY_CHUNK)
        s = _dot(k_ref[rows, :], qd) + sb_ref[kind]
        for j in range(per):
            srow = sel_ref[pl.ds(chunk * per + j, 1), :]
            srow = jnp.concatenate([srow] * GQA_GROUP, axis=1)
            tiles.append(jnp.where(srow > 0.5, s[j * SLC_BLOCK:(j + 1) * SLC_BLOCK, :], NEG))
        vts.append(vt_ref[:, rows])
    s = jnp.concatenate(tiles, axis=0)
    m, l, acc = update(m, l, acc, s, jnp.max(s, axis=0, keepdims=True), jnp.concatenate(vts, axis=1))
    os_ref[...] = acc / jnp.maximum(l, 1e-30)


def _slc(qt, ks, vst, sel, sb, crow):
    b, g, dh, total = qt.shape
    nqb = total // ROWS
    s_len, kw = ks.shape[2], ks.shape[3]
    n_slc = sel.shape[3]
    blk_q = pl.BlockSpec((None, None, dh, ROWS), lambda i, j, q: (i, j, 0, q))
    return pl.pallas_call(
        _slc_body,
        grid=(b, g, nqb),
        in_specs=[blk_q,
                  pl.BlockSpec((None, None, s_len, kw), lambda i, j, q: (i, j, 0, 0)),
                  pl.BlockSpec((None, None, dh, s_len), lambda i, j, q: (i, j, 0, 0)),
                  pl.BlockSpec((None, None, None, n_slc, Q_BLOCK), lambda i, j, q: (i, j, q, 0, 0)),
                  pl.BlockSpec((None, 4, KEY_CHUNK, ROWS), lambda i, j, q: (j, 0, 0, 0)),
                  pl.BlockSpec((None, 1, ROWS), lambda i, j, q: (j, 0, 0))],
        out_specs=blk_q,
        out_shape=jax.ShapeDtypeStruct((b, g, dh, total), F32),
        scratch_shapes=[pltpu.VMEM((2, FAR_KEYS, ROWS), F32)],
        compiler_params=_params(3),
        name="slc_attention",
    )(qt, ks, vst, sel, sb, crow)


WIN_CHUNKS = WINDOW // KEY_CHUNK + 1


def _win_body(q_ref, k_ref, vt_ref, wb_ref, oc_ref, os_ref, gate_ref, o_ref):
    qi = pl.program_id(2)
    qt = q_ref[...]
    tiles, vts = [], []
    for delta in range(WIN_CHUNKS - 1, -1, -1):
        c = jnp.maximum(qi - delta, 0)
        rows = pl.ds(pl.multiple_of(c * KEY_CHUNK, KEY_CHUNK), KEY_CHUNK)
        kind = jnp.where(qi >= delta, delta, WIN_CHUNKS)
        tiles.append(_dot(k_ref[rows, :], qt) + wb_ref[kind])
        vts.append(vt_ref[:, rows])
    s = jnp.concatenate(tiles, axis=0)
    m = jnp.max(s, axis=0, keepdims=True)
    p = jnp.exp2(s - m)
    l = jnp.sum(p, axis=0, keepdims=True)
    o_w = _dot(jnp.concatenate(vts, axis=1), p.astype(BF16)) / jnp.maximum(l, 1e-30)
    o = gate_ref[0:1, :] * oc_ref[...] + gate_ref[1:2, :] * os_ref[...] + gate_ref[2:3, :] * o_w
    o_ref[...] = o.astype(o_ref.dtype)


def _win_mix(qt, kw, vwt, wb, oc, osl, gates):
    b, g, dh, total = qt.shape
    nqb = total // ROWS
    s_len = kw.shape[2]
    blk_q = pl.BlockSpec((None, None, dh, ROWS), lambda i, j, q: (i, j, 0, q))
    return pl.pallas_call(
        _win_body,
        grid=(b, g, nqb),
        in_specs=[blk_q,
                  pl.BlockSpec((None, None, s_len, dh), lambda i, j, q: (i, j, 0, 0)),
                  pl.BlockSpec((None, None, dh, s_len), lambda i, j, q: (i, j, 0, 0)),
                  pl.BlockSpec((None, WIN_CHUNKS + 1, KEY_CHUNK, ROWS), lambda i, j, q: (j, 0, 0, 0)),
                  blk_q, blk_q,
                  pl.BlockSpec((None, None, N_BRANCHES, ROWS), lambda i, j, q: (i, j, 0, q))],
        out_specs=blk_q,
        out_shape=jax.ShapeDtypeStruct((b, g, dh, total), BF16),
        compiler_params=_params(3),
        name="window_mix",
    )(qt, kw, vwt, wb, oc, osl, gates)


def _outproj_body(a_ref, gm_ref, h_ref, wa_ref, wb_ref, g_ref, o_ref):
    y = _dot(a_ref[...], wa_ref[...]) + _dot(gm_ref[...], wb_ref[...])
    o_ref[...] = h_ref[...] + _rms(y, g_ref[...])


def _outproj(a, gm, h, wa, wb, gpost, tm):
    n, d = h.shape
    full = _resident
    row = lambda w: pl.BlockSpec((tm, w), lambda i: (i, 0))
    return pl.pallas_call(
        _outproj_body,
        grid=(n // tm,),
        in_specs=[row(a.shape[1]), row(gm.shape[1]), row(d), full(wa), full(wb), full(gpost)],
        out_specs=row(d),
        out_shape=jax.ShapeDtypeStruct((n, d), F32),
        compiler_params=_params(1),
        name="outproj",
    )(a, gm, h, wa, wb, gpost)


FFN_TILE = 256


def _ffn_body(h_ref, gpre_ref, wg_ref, wu_ref, wd_ref, gpost_ref, o_ref):
    h = h_ref[...]
    xb = _rms(h, gpre_ref[...]).astype(BF16)
    acc = jnp.zeros(h.shape, F32)
    for j in range(wg_ref.shape[1] // FFN_TILE):
        cols = slice(j * FFN_TILE, (j + 1) * FFN_TILE)
        gate = _dot(xb, wg_ref[:, cols])
        up = _dot(xb, wu_ref[:, cols])
        act = (gate * _sigmoid(gate) * up).astype(BF16)
        acc = acc + _dot(act, wd_ref[cols, :])
    o_ref[...] = h + _rms(acc, gpost_ref[...])


def _ffn(h, gpre, wg, wu, wd, gpost, tm):
    n, d = h.shape
    full = _resident
    row = pl.BlockSpec((tm, d), lambda i: (i, 0))
    return pl.pallas_call(
        _ffn_body,
        grid=(n // tm,),
        in_specs=[row, full(gpre), full(wg), full(wu), full(wd), full(gpost)],
        out_specs=row,
        out_shape=jax.ShapeDtypeStruct((n, d), F32),
        compiler_params=_params(1),
        name="ffn",
    )(h, gpre, wg, wu, wd, gpost)


def _t5_bucket(dist):
    n = jnp.maximum(dist, 0)
    max_exact = N_BUCKETS // 2
    nf = jnp.maximum(n, max_exact).astype(F32)
    large = max_exact + (jnp.log(nf / max_exact) / math.log(REL_MAX_DISTANCE / max_exact)
                         * (N_BUCKETS - max_exact)).astype(jnp.int32)
    return jnp.where(n < max_exact, n, jnp.minimum(large, N_BUCKETS - 1))


def _bias_tile(table, dist, mask):
    b = table.astype(F32)[_t5_bucket(dist)]
    b = jnp.where(mask[..., None], b * LOG2E, NEG)
    k = dist.shape[0]
    return b.reshape(k, Q_BLOCK, N_KV_HEADS, GQA_GROUP).transpose(2, 0, 3, 1).reshape(N_KV_HEADS, k, ROWS)


def _bias_tiles(rel_bias):
    ql = jnp.arange(Q_BLOCK, dtype=jnp.int32)[None, :]
    kl = jnp.arange(KEY_CHUNK, dtype=jnp.int32)[:, None]
    chunk_dist = lambda delta: delta * KEY_CHUNK + ql - kl
    everything = jnp.ones((KEY_CHUNK, Q_BLOCK), bool)
    nothing = jnp.zeros((KEY_CHUNK, Q_BLOCK), bool)
    far = jnp.full((KEY_CHUNK, Q_BLOCK), REL_MAX_DISTANCE, jnp.int32)
    d0 = chunk_dist(0)
    sb = jnp.stack([_bias_tile(rel_bias, d0, d0 >= 0),
                    _bias_tile(rel_bias, chunk_dist(1), everything),
                    _bias_tile(rel_bias, far, everything),
                    _bias_tile(rel_bias, far, nothing)], axis=1)
    wtiles = []
    for delta in range(WIN_CHUNKS):
        d = chunk_dist(delta)
        wtiles.append(_bias_tile(rel_bias, d, (d >= 0) & (d < WINDOW)))
    wtiles.append(_bias_tile(rel_bias, far, nothing))
    wb = jnp.stack(wtiles, axis=1)
    rel = jnp.arange(CMP_WINDOW, dtype=jnp.int32)[:, None] - CMP_FRONT_PAD
    dc = ql - CMP_STRIDE * rel - (CMP_BLOCK - 1)
    cb = _bias_tile(rel_bias, dc, dc >= 0)
    crow = _bias_tile(rel_bias, far[:1], everything[:1])
    return sb, wb, cb, crow


def _to_rows(t, b, s):
    x = t.shape[1] // (N_KV_HEADS * GQA_GROUP)
    t = t.reshape(b, s // Q_BLOCK, Q_BLOCK, N_KV_HEADS, GQA_GROUP, x)
    return t.transpose(0, 3, 5, 1, 4, 2).reshape(b, N_KV_HEADS, x, (s // Q_BLOCK) * ROWS)


def _mixer(h, b, s, lw, tiles, tm):
    sb, wb, cb, crow = tiles
    q, cv, sw, gates, gm = _inproj(h, lw["gpre"], lw["wq"], lw["wkv"], lw["wg"], lw["wu"], lw["wv"],
                                   lw["lng"], lw["lnb"], lw["ws"], lw["bs"], tm)
    g, dh = N_KV_HEADS, HEAD_DIM
    n_chunks = s // CMP_STRIDE
    xc = cv.reshape(b, n_chunks, CMP_STRIDE, 2, g, dh).transpose(0, 3, 4, 1, 2, 5)
    xc = xc.reshape(b, 2, g, n_chunks, CMP_STRIDE * dh)
    comp = _compress(xc, lw["cpos"], lw["cw1"], lw["cw2"])
    ncp = -(-(CMP_FRONT_PAD + n_chunks + CMP_WINDOW - CMP_FRONT_PAD) // KEY_CHUNK) * KEY_CHUNK
    back = ncp - CMP_FRONT_PAD - n_chunks
    comp = jnp.pad(comp, ((0, 0), (0, 0), (0, 0), (CMP_FRONT_PAD, back), (0, 0)))
    kcp = comp[:, 0]
    vct = comp[:, 1].transpose(0, 1, 3, 2).astype(BF16)

    qt = _to_rows(q, b, s)
    n_slc = s // SLC_BLOCK
    n_other = min(SLC_TOPK, n_slc) - (N_LOCAL_BLOCKS + 1)
    oc, sel = _cmp_select(qt, kcp, vct, crow, cb, n_slc, n_other)

    heads = lambda t: t.reshape(b, s, g, dh).transpose(0, 2, 1, 3)
    heads_t = lambda t: t.reshape(b, s, g, dh).transpose(0, 2, 3, 1)
    kvw = KV_WIDTH
    osl = _slc(qt, _augment_keys(heads(sw[:, 0:kvw])), heads_t(sw[:, kvw:2 * kvw]), sel, sb, crow)
    gates_t = _to_rows(gates[:, :N_GATES], b, s)
    o = _win_mix(qt, heads(sw[:, 2 * kvw:3 * kvw]), heads_t(sw[:, 3 * kvw:]), wb, oc, osl, gates_t)
    nqb = s // Q_BLOCK
    nsa = o.reshape(b, g, dh, nqb, GQA_GROUP, Q_BLOCK).transpose(0, 3, 5, 1, 4, 2).reshape(b * s, NSA_WIDTH)
    return _outproj(nsa, gm, h, lw["wo_a"], lw["wo_b"], lw["gpost"], tm)


def _layer_weights(l, p):
    d = p["w_in"].shape[1]
    w_in = p["w_in"][l]
    o = NSA_WIDTH
    wq = w_in[:, :o]
    wkv = w_in[:, o:o + 6 * KV_WIDTH]
    o += 6 * KV_WIDTH
    wg = jnp.pad(w_in[:, o:o + N_GATES], ((0, 0), (0, GATE_PAD - N_GATES)))
    o += N_GATES
    gw = (w_in.shape[1] - o) // 2
    wu, wv = w_in[:, o:o + gw], w_in[:, o + gw:]
    causal = jnp.tril(jnp.ones((GMLP_CHUNK, GMLP_CHUNK), bool))
    ws = jnp.where(causal, p["gmlp_w_s"][l], 0.0)
    bs = jnp.repeat(p["gmlp_b_s"][l].T, gw // N_GMLP_GROUPS, axis=1)
    half = CMP_STRIDE * HEAD_DIM
    cpos = jnp.stack([p["cmp_pos_k"][l].reshape(2, half), p["cmp_pos_v"][l].reshape(2, half)])
    dff = p["w_down"].shape[1]
    row = lambda v: v[l].reshape(1, -1)
    bf = lambda w: w.astype(BF16)
    return dict(gpre=row(p["norm_mix_pre"]), wq=bf(wq), wkv=bf(wkv), wg=bf(wg), wu=bf(wu), wv=bf(wv),
                lng=row(p["gmlp_ln_g"]), lnb=row(p["gmlp_ln_b"]), ws=bf(ws), bs=bs,
                cpos=cpos, cw1=bf(jnp.stack([p["cmp_w1_k"][l], p["cmp_w1_v"][l]])),
                cw2=bf(jnp.stack([p["cmp_w2_k"][l], p["cmp_w2_v"][l]])),
                wo_a=bf(p["w_out"][l][:NSA_WIDTH]), wo_b=bf(p["w_out"][l][NSA_WIDTH:]),
                gpost=row(p["norm_mix_post"]), fpre=row(p["norm_ffn_pre"]), fpost=row(p["norm_ffn_post"]),
                fwg=bf(p["w_gate_up"][l][:, :dff]), fwu=bf(p["w_gate_up"][l][:, dff:]), fwd=bf(p["w_down"][l]))


def _trunk(p, tm):
    x = p["x"]
    b, s, d = x.shape
    h = x.reshape(b * s, d)
    tiles = _bias_tiles(p["rel_bias"])
    for l in range(p["w_in"].shape[0]):
        lw = _layer_weights(l, p)
        h = _mixer(h, b, s, lw, tiles, tm)
        h = _ffn(h, lw["fpre"], lw["fwg"], lw["fwu"], lw["fwd"], lw["fpost"], tm)
    return h.reshape(b, s, d)


def kernel(x, rel_bias, norm_mix_pre, norm_mix_post, norm_ffn_pre, norm_ffn_post, w_in, cmp_pos_k, cmp_w1_k, cmp_w2_k, cmp_pos_v, cmp_w1_v, cmp_w2_v, gmlp_ln_g, gmlp_ln_b, gmlp_w_s, gmlp_b_s, w_out, w_gate_up, w_down):
    p = dict(x=x, rel_bias=rel_bias, norm_mix_pre=norm_mix_pre, norm_mix_post=norm_mix_post,
             norm_ffn_pre=norm_ffn_pre, norm_ffn_post=norm_ffn_post, w_in=w_in,
             cmp_pos_k=cmp_pos_k, cmp_w1_k=cmp_w1_k, cmp_w2_k=cmp_w2_k,
             cmp_pos_v=cmp_pos_v, cmp_w1_v=cmp_w1_v, cmp_w2_v=cmp_w2_v,
             gmlp_ln_g=gmlp_ln_g, gmlp_ln_b=gmlp_ln_b, gmlp_w_s=gmlp_w_s, gmlp_b_s=gmlp_b_s,
             w_out=w_out, w_gate_up=w_gate_up, w_down=w_down)
    return _trunk(p, tm=512)
```

```python
import functools
import math

import jax
import jax.numpy as jnp
import numpy as np
from jax import lax
from jax.experimental import pallas as pl
from jax.experimental.pallas import tpu as pltpu

F32 = jnp.float32
BF16 = jnp.bfloat16

N_NSA_HEADS = 8
N_KV_HEADS = 2
GQA_GROUP = N_NSA_HEADS // N_KV_HEADS
HEAD_DIM = 64
NSA_WIDTH = N_NSA_HEADS * HEAD_DIM
KV_WIDTH = N_KV_HEADS * HEAD_DIM
N_BRANCHES = 3
N_GATES = N_BRANCHES * N_NSA_HEADS
CMP_BLOCK = 32
CMP_STRIDE = 16
SLC_BLOCK = 64
SLC_TOPK = 16
N_LOCAL_BLOCKS = 2
WINDOW = 512
Q_BLOCK = 128
N_GMLP_GROUPS = 8
GMLP_CHUNK = 128
N_BUCKETS = 32
REL_MAX_DISTANCE = 128
RMS_EPS = 1e-6
LN_EPS = 1e-5

ROWS = GQA_GROUP * Q_BLOCK
KEY_CHUNK = 128
CMP_FRONT_PAD = 24
CMP_WINDOW = 32
NEG = -1e30
LANE = 128
GATE_PAD = 128
MXU_TILE = 256
BF16_ROWS = 16
VMEM_LIMIT = 48 * 1024 * 1024
LOG2E = math.log2(math.e)
Q_SCALE = HEAD_DIM ** -0.5 * LOG2E


def _dot(a, b):
    return jnp.dot(a, b, preferred_element_type=F32)


def _gelu(x):
    c = math.sqrt(2.0 / math.pi)
    return 0.5 * x * (1.0 + jnp.tanh(c * (x + 0.044715 * (x * x * x))))


def _sigmoid(x):
    return 1.0 / (1.0 + jnp.exp(-x))


def _rms(x, g):
    ms = jnp.mean(x * x, axis=-1, keepdims=True)
    return (x * lax.rsqrt(ms + RMS_EPS)) * g


def _resident(a):
    return pl.BlockSpec(a.shape, lambda i: (0,) * a.ndim, pipeline_mode=pl.Buffered(1))


def _params(n_axes):
    return pltpu.CompilerParams(dimension_semantics=("arbitrary",) * n_axes,
                                vmem_limit_bytes=VMEM_LIMIT)


def _inproj_body(x_ref, gpre_ref, wq_ref, wkv_ref, wg_ref, wu_ref, wv_ref, lng_ref, lnb_ref,
                 ws_ref, bs_ref, q_ref, cv_ref, sw_ref, gate_ref, gm_ref):
    tm = x_ref.shape[0]
    xb = _rms(x_ref[...], gpre_ref[...]).astype(BF16)
    q_ref[...] = (_dot(xb, wq_ref[...]) * Q_SCALE).astype(BF16)
    kv = _dot(xb, wkv_ref[...])
    cv_ref[...] = kv[:, :2 * KV_WIDTH]
    sw_ref[...] = kv[:, 2 * KV_WIDTH:].astype(BF16)
    gate_ref[...] = _sigmoid(_dot(xb, wg_ref[...]))
    zu = _gelu(_dot(xb, wu_ref[...]))
    zv = _gelu(_dot(xb, wv_ref[...]))
    mu = jnp.mean(zv, axis=-1, keepdims=True)
    zc = zv - mu
    var = jnp.mean(zc * zc, axis=-1, keepdims=True)
    zv = ((zc * lax.rsqrt(var + LN_EPS)) * lng_ref[...] + lnb_ref[...]).astype(BF16)
    gdim = zv.shape[1] // N_GMLP_GROUPS
    left = lax.broadcasted_iota(jnp.int32, (GMLP_CHUNK, LANE), 1) < gdim
    for c in range(tm // GMLP_CHUNK):
        rows = slice(c * GMLP_CHUNK, (c + 1) * GMLP_CHUNK)
        for j in range(zv.shape[1] // LANE):
            cols = slice(j * LANE, (j + 1) * LANE)
            z = zv[rows, cols]
            sv = jnp.where(left, _dot(ws_ref[2 * j], z), _dot(ws_ref[2 * j + 1], z)) + bs_ref[:, cols]
            gm_ref[rows, cols] = (zu[rows, cols] * sv).astype(BF16)


def _inproj(x, gpre, wq, wkv, wg, wu, wv, lng, lnb, ws, bs, tm):
    n, d = x.shape
    full = _resident
    row = lambda w: pl.BlockSpec((tm, w), lambda i: (i, 0))
    weights = (gpre, wq, wkv, wg, wu, wv, lng, lnb, ws, bs)
    return pl.pallas_call(
        _inproj_body,
        grid=(n // tm,),
        in_specs=[row(d)] + [full(a) for a in weights],
        out_specs=[row(NSA_WIDTH), row(2 * KV_WIDTH), row(4 * KV_WIDTH), row(GATE_PAD), row(wu.shape[1])],
        out_shape=[jax.ShapeDtypeStruct((n, NSA_WIDTH), BF16),
                   jax.ShapeDtypeStruct((n, 2 * KV_WIDTH), F32),
                   jax.ShapeDtypeStruct((n, 4 * KV_WIDTH), BF16),
                   jax.ShapeDtypeStruct((n, GATE_PAD), F32),
                   jax.ShapeDtypeStruct((n, wu.shape[1]), BF16)],
        compiler_params=_params(1),
        name="inproj_gmlp",
    )(x, *weights)


def _compress_body(x_ref, pos_ref, w1_ref, w2_ref, o_ref):
    x = x_ref[...]
    half = x.shape[1]
    a = _dot((x + pos_ref[0:1, :]).astype(BF16), w1_ref[:half, :])
    b = _dot((x + pos_ref[1:2, :]).astype(BF16), w1_ref[half:, :])
    pre = a + pltpu.roll(b, x.shape[0] - 1, 0)
    o_ref[...] = _dot(_gelu(pre).astype(BF16), w2_ref[...])


def _compress(xc, pos, w1, w2):
    b, two, g, nch, width = xc.shape
    hid = w1.shape[2]
    dh = w2.shape[2]
    return pl.pallas_call(
        _compress_body,
        grid=(b, two, g),
        in_specs=[pl.BlockSpec((None, None, None, nch, width), lambda i, t, j: (i, t, j, 0, 0)),
                  pl.BlockSpec((None, 2, width), lambda i, t, j: (t, 0, 0)),
                  pl.BlockSpec((None, 2 * width, hid), lambda i, t, j: (t, 0, 0)),
                  pl.BlockSpec((None, hid, dh), lambda i, t, j: (t, 0, 0))],
        out_specs=pl.BlockSpec((None, None, None, nch, dh), lambda i, t, j: (i, t, j, 0, 0)),
        out_shape=jax.ShapeDtypeStruct((b, two, g, nch, dh), F32),
        compiler_params=_params(3),
        name="compress",
    )(xc, pos, w1, w2)


CMP_CHUNK = 256
CMP_AUG_WIDTH = 2 * HEAD_DIM


def _cmp_select_body(q_ref, k_ref, vt_ref, crow_ref, cb_ref, oc_ref, sel_ref, sc_ref, *, n_other):
    qi = pl.program_id(2)
    n_slc = sel_ref.shape[0]
    qt = q_ref[...]
    dh = qt.shape[0]
    w0 = pl.multiple_of(qi * (Q_BLOCK // CMP_STRIDE), 8)
    n_chunks = (w0 + CMP_WINDOW + CMP_CHUNK - 1) // CMP_CHUNK
    row_iota = lax.broadcasted_iota(jnp.int32, (CMP_CHUNK, ROWS), 0)

    @pl.when(qi == 0)
    def _():
        sc_ref[...] = jnp.zeros_like(sc_ref)

    c = crow_ref[...]
    c_hi = c.astype(BF16).astype(F32)
    neg_row = jnp.full((1, ROWS), NEG, F32)
    zeros = lambda n, dt: jnp.zeros((n, ROWS), dt)
    tail = zeros(k_ref.shape[1] - dh - BF16_ROWS, BF16)
    qa = jnp.concatenate([qt, jnp.concatenate([c_hi, c - c_hi, neg_row, zeros(BF16_ROWS - 3, F32)]).astype(BF16), tail])
    qw = jnp.concatenate([qt, jnp.concatenate([zeros(2, F32), neg_row, zeros(BF16_ROWS - 3, F32)]).astype(BF16), tail])

    def rows_of(ch):
        return pl.ds(pl.multiple_of(ch * CMP_CHUNK, CMP_CHUNK), CMP_CHUNK)

    head = lambda r: slice(r * Q_BLOCK, (r + 1) * Q_BLOCK)

    def put(rows, val):
        for r in range(GQA_GROUP):
            sc_ref[r, rows, :] = val[:, head(r)]

    def get(rows):
        return jnp.concatenate([sc_ref[r, rows, :] for r in range(GQA_GROUP)], axis=1)

    def logits(ch, m):
        s = _dot(k_ref[rows_of(ch), :].astype(BF16), qa)
        s = jnp.where(row_iota + ch * CMP_CHUNK < w0, s, NEG)
        put(rows_of(ch), s)
        return jnp.maximum(m, jnp.max(s, axis=0, keepdims=True))

    m = lax.fori_loop(0, n_chunks, logits, jnp.full((1, ROWS), NEG, F32))
    win = pl.ds(w0, CMP_WINDOW)
    s_win = _dot(k_ref[win, :].astype(BF16), qw) + cb_ref[...]
    put(win, s_win)
    m = jnp.maximum(m, jnp.max(s_win, axis=0, keepdims=True))

    def weigh(ch, carry):
        l, acc = carry
        p = jnp.exp2(get(rows_of(ch)) - m)
        put(rows_of(ch), p)
        return l + jnp.sum(p, axis=0, keepdims=True), acc + _dot(vt_ref[:, rows_of(ch)], p.astype(BF16))

    l, acc = lax.fori_loop(0, n_chunks, weigh, (jnp.zeros((1, ROWS), F32), jnp.zeros(oc_ref.shape, F32)))
    scale = jnp.where(m > 0.5 * NEG, 1.0 / jnp.maximum(l, 1e-30), 0.0)
    oc_ref[...] = acc * scale

    per = SLC_BLOCK // CMP_STRIDE
    imp = jnp.zeros((n_slc, Q_BLOCK), F32)
    for r in range(GQA_GROUP):
        part = lambda off: sc_ref[r, pl.ds(CMP_FRONT_PAD + off, n_slc, stride=per), :]
        tot = part(0)
        for k in range(1, per - 1):
            tot = tot + part(k)
        imp = imp + (tot + 0.5 * (part(per - 1) + part(-1))) * scale[:, head(r)]

    blk = lax.broadcasted_iota(jnp.int32, (n_slc, Q_BLOCK), 0)
    pos = qi * Q_BLOCK + lax.broadcasted_iota(jnp.int32, (n_slc, Q_BLOCK), 1)
    jq = pos // SLC_BLOCK
    valid = blk <= jq
    forced = (blk == 0) | (valid & (blk > jq - N_LOCAL_BLOCKS))
    blk_f = blk.astype(F32)
    free = valid & jnp.logical_not(forced)
    work = jnp.where(free, imp, -1.0)
    for _ in range(n_other):
        mx = jnp.max(work, axis=0, keepdims=True)
        first = jnp.min(jnp.where(work == mx, blk_f, float(n_slc)), axis=0, keepdims=True)
        work = jnp.where((blk_f == first) & (mx >= 0.0), -1.0, work)
    sel_ref[...] = jnp.where(forced | (free & (work < 0.0)), 1.0, 0.0)


def _cmp_select(qt, kcp, vct, crow, cb, n_slc, n_other):
    b, g, dh, total = qt.shape
    nqb = total // ROWS
    ncp = kcp.shape[2]
    blk_q = pl.BlockSpec((None, None, dh, ROWS), lambda i, j, q: (i, j, 0, q))
    return pl.pallas_call(
        functools.partial(_cmp_select_body, n_other=n_other),
        grid=(b, g, nqb),
        in_specs=[blk_q,
                  pl.BlockSpec((None, None, ncp, kcp.shape[3]), lambda i, j, q: (i, j, 0, 0)),
                  pl.BlockSpec((None, None, dh, ncp), lambda i, j, q: (i, j, 0, 0)),
                  pl.BlockSpec((None, 1, ROWS), lambda i, j, q: (j, 0, 0)),
                  pl.BlockSpec((None, CMP_WINDOW, ROWS), lambda i, j, q: (j, 0, 0))],
        out_specs=[blk_q,
                   pl.BlockSpec((None, None, None, n_slc, Q_BLOCK), lambda i, j, q: (i, j, q, 0, 0))],
        out_shape=[jax.ShapeDtypeStruct((b, g, dh, total), F32),
                   jax.ShapeDtypeStruct((b, g, nqb, n_slc, Q_BLOCK), F32)],
        scratch_shapes=[pltpu.VMEM((GQA_GROUP, ncp, Q_BLOCK), F32)],
        compiler_params=_params(3),
        name="cmp_select",
    )(qt, kcp, vct, crow, cb)


SLC_QBLOCKS = 2
FAR_KEYS = 256
FAR_BLOCKS = FAR_KEYS // SLC_BLOCK
FAR_CHUNKS = FAR_KEYS // KEY_CHUNK
FAR_BUFFERS = 4
FAR_AHEAD = 2
PEN_BLOCKS = 8
AUG_CONST = 2
AUG_ROWS = 16
K_AUG_WIDTH = 2 * HEAD_DIM


def _augment_keys(k):
    s_len = k.shape[-2]
    pos = jnp.arange(s_len, dtype=jnp.int32)
    onehot = ((pos[:, None] // SLC_BLOCK) % PEN_BLOCKS == jnp.arange(PEN_BLOCKS, dtype=jnp.int32)[None, :])
    aug = jnp.concatenate([jnp.ones((s_len, AUG_CONST), k.dtype),
                           jnp.zeros((s_len, AUG_ROWS - PEN_BLOCKS - AUG_CONST), k.dtype),
                           onehot.astype(k.dtype),
                           jnp.zeros((s_len, K_AUG_WIDTH - HEAD_DIM - AUG_ROWS), k.dtype)], axis=1)
    return jnp.concatenate([k, jnp.broadcast_to(aug, k.shape[:-2] + aug.shape)], axis=-1)


def _augment_values_t(vt):
    extra = jnp.concatenate([jnp.ones((1, vt.shape[-1]), vt.dtype), jnp.zeros((BF16_ROWS - 1, vt.shape[-1]), vt.dtype)])
    return jnp.concatenate([vt, jnp.broadcast_to(extra, vt.shape[:-2] + extra.shape)], axis=-2)


def _slc_body(q_ref, k_ref, vt_ref, sel_ref, sb_ref, crow_ref, os_ref, sbuf):
    qis = [pl.program_id(2) * SLC_QBLOCKS + h for h in range(SLC_QBLOCKS)]
    qt = q_ref[...]
    dh, lanes = qt.shape
    per = KEY_CHUNK // SLC_BLOCK
    far_limits = [jnp.maximum(qi - 1, 0) * per for qi in qis]
    n_steps = (jnp.maximum(qis[-1] - 1, 0) + FAR_CHUNKS - 1) // FAR_CHUNKS
    last_step = k_ref.shape[0] // FAR_KEYS - 1
    steps_per_group = PEN_BLOCKS // FAR_BLOCKS

    c = jnp.concatenate([crow_ref[...]] * SLC_QBLOCKS, axis=1)
    c_hi = c.astype(BF16).astype(F32)
    const_rows = jnp.concatenate([c_hi, c - c_hi, jnp.zeros((AUG_ROWS - PEN_BLOCKS - AUG_CONST, lanes), F32)], axis=0)
    pad_rows = jnp.zeros((k_ref.shape[1] - dh - AUG_ROWS, lanes), BF16)
    blk_iota = lax.broadcasted_iota(jnp.int32, (PEN_BLOCKS, Q_BLOCK), 0)

    def far_logits(u):
        ua = jnp.minimum(u, last_step)
        grp0 = pl.multiple_of((ua // steps_per_group) * PEN_BLOCKS, PEN_BLOCKS)
        blk = blk_iota + (u // steps_per_group) * PEN_BLOCKS
        pens = []
        for h in range(SLC_QBLOCKS):
            pen = jnp.where((sel_ref[h, pl.ds(grp0, PEN_BLOCKS), :] > 0.5) & (blk < far_limits[h]), 0.0, NEG)
            pens += [pen] * GQA_GROUP
        qa = jnp.concatenate([qt, jnp.concatenate([const_rows, jnp.concatenate(pens, axis=1)], axis=0).astype(BF16),
                              pad_rows], axis=0)
        return _dot(k_ref[pl.ds(pl.multiple_of(ua * FAR_KEYS, FAR_KEYS), FAR_KEYS), :], qa)

    def stage(slot, u):
        s_new = far_logits(u).astype(BF16)
        sbuf[slot] = s_new
        groups = s_new.reshape(FAR_KEYS // BF16_ROWS, BF16_ROWS, lanes)
        return jnp.max(jnp.max(groups, axis=0).astype(F32), axis=0, keepdims=True)

    def far_update(m, acc, s_ref, mx, key0):
        m_new = jnp.maximum(m, mx)
        p = jnp.exp2(s_ref[...] - m_new.astype(BF16))
        cols = pl.ds(pl.multiple_of(key0, FAR_KEYS), FAR_KEYS)
        return m_new, jnp.exp2(m - m_new) * acc + _dot(vt_ref[:, cols], p)

    def far_round(v, carry):
        m, acc = carry[:2]
        ahead = list(carry[2:])
        for slot in range(FAR_BUFFERS):
            u = FAR_BUFFERS * v + slot
            ahead.append(stage((slot + FAR_AHEAD) % FAR_BUFFERS, u + FAR_AHEAD))
            ua = jnp.minimum(u, last_step)
            m, acc = far_update(m, acc, sbuf.at[slot], ahead.pop(0), ua * FAR_KEYS)
        return (m, acc, *ahead)

    ahead = [stage(u, u) for u in range(FAR_AHEAD)]
    init = (jnp.full((1, lanes), NEG, F32), jnp.zeros((vt_ref.shape[0], lanes), F32), *ahead)
    m, acc = lax.fori_loop(0, (n_steps + FAR_BUFFERS - 1) // FAR_BUFFERS, far_round, init)[:2]

    for h, qi in enumerate(qis):
        cols = slice(h * ROWS, (h + 1) * ROWS)
        qd = jnp.concatenate([qt[:, cols], jnp.zeros((k_ref.shape[1] - dh, ROWS), BF16)], axis=0)
        prev = jnp.maximum(qi - 1, 0)
        tiles, vts = [], []
        for chunk, kind in ((prev, jnp.where(qi >= 1, 1, 3)), (qi, 0)):
            rows = pl.ds(pl.multiple_of(chunk * KEY_CHUNK, KEY_CHUNK), KEY_CHUNK)
            s = _dot(k_ref[rows, :], qd) + sb_ref[kind]
            for j in range(per):
                srow = sel_ref[h, pl.ds(chunk * per + j, 1), :]
                srow = jnp.concatenate([srow] * GQA_GROUP, axis=1)
                tiles.append(jnp.where(srow > 0.5, s[j * SLC_BLOCK:(j + 1) * SLC_BLOCK, :], NEG))
            vts.append(vt_ref[:, rows])
        s = jnp.concatenate(tiles, axis=0)
        m_new = jnp.maximum(m[:, cols], jnp.max(s, axis=0, keepdims=True))
        p = jnp.exp2(s - m_new).astype(BF16)
        acc_h = jnp.exp2(m[:, cols] - m_new) * acc[:, cols] + _dot(jnp.concatenate(vts, axis=1), p)
        os_ref[:, cols] = acc_h[:dh] / jnp.maximum(acc_h[dh:dh + 1], 1e-30)


def _slc(qt, ks, vst, sel, sb, crow):
    b, g, dh, total = qt.shape
    nqb = total // ROWS
    s_len, kw = ks.shape[2], ks.shape[3]
    n_slc = sel.shape[3]
    blk_q = pl.BlockSpec((None, None, dh, SLC_QBLOCKS * ROWS), lambda i, j, q: (i, j, 0, q))
    return pl.pallas_call(
        _slc_body,
        grid=(b, g, nqb // SLC_QBLOCKS),
        in_specs=[blk_q,
                  pl.BlockSpec((None, None, s_len, kw), lambda i, j, q: (i, j, 0, 0)),
                  pl.BlockSpec((None, None, vst.shape[2], s_len), lambda i, j, q: (i, j, 0, 0)),
                  pl.BlockSpec((None, None, SLC_QBLOCKS, n_slc, Q_BLOCK), lambda i, j, q: (i, j, q, 0, 0)),
                  pl.BlockSpec((None, 4, KEY_CHUNK, ROWS), lambda i, j, q: (j, 0, 0, 0)),
                  pl.BlockSpec((None, 1, ROWS), lambda i, j, q: (j, 0, 0))],
        out_specs=blk_q,
        out_shape=jax.ShapeDtypeStruct((b, g, dh, total), F32),
        scratch_shapes=[pltpu.VMEM((FAR_BUFFERS, FAR_KEYS, SLC_QBLOCKS * ROWS), BF16)],
        compiler_params=_params(3),
        name="slc_attention",
    )(qt, ks, vst, sel, sb, crow)


WIN_CHUNKS = WINDOW // KEY_CHUNK + 1


def _win_body(q_ref, k_ref, vt_ref, wb_ref, oc_ref, os_ref, gate_ref, o_ref):
    qi = pl.program_id(2)
    qt = q_ref[...]
    tiles, vts = [], []
    for delta in range(WIN_CHUNKS - 1, -1, -1):
        c = jnp.maximum(qi - delta, 0)
        rows = pl.ds(pl.multiple_of(c * KEY_CHUNK, KEY_CHUNK), KEY_CHUNK)
        kind = jnp.where(qi >= delta, delta, WIN_CHUNKS)
        tiles.append(_dot(k_ref[rows, :], qt) + wb_ref[kind])
        vts.append(vt_ref[:, rows])
    s = jnp.concatenate(tiles, axis=0)
    m = jnp.max(s, axis=0, keepdims=True)
    p = jnp.exp2(s - m)
    l = jnp.sum(p, axis=0, keepdims=True)
    o_w = _dot(jnp.concatenate(vts, axis=1), p.astype(BF16)) / jnp.maximum(l, 1e-30)
    o = gate_ref[0:1, :] * oc_ref[...] + gate_ref[1:2, :] * os_ref[...] + gate_ref[2:3, :] * o_w
    o_ref[...] = o.astype(o_ref.dtype)


def _win_mix(qt, kw, vwt, wb, oc, osl, gates):
    b, g, dh, total = qt.shape
    nqb = total // ROWS
    s_len = kw.shape[2]
    blk_q = pl.BlockSpec((None, None, dh, ROWS), lambda i, j, q: (i, j, 0, q))
    return pl.pallas_call(
        _win_body,
        grid=(b, g, nqb),
        in_specs=[blk_q,
                  pl.BlockSpec((None, None, s_len, dh), lambda i, j, q: (i, j, 0, 0)),
                  pl.BlockSpec((None, None, dh, s_len), lambda i, j, q: (i, j, 0, 0)),
                  pl.BlockSpec((None, WIN_CHUNKS + 1, KEY_CHUNK, ROWS), lambda i, j, q: (j, 0, 0, 0)),
                  blk_q, blk_q,
                  pl.BlockSpec((None, None, N_BRANCHES, ROWS), lambda i, j, q: (i, j, 0, q))],
        out_specs=blk_q,
        out_shape=jax.ShapeDtypeStruct((b, g, dh, total), BF16),
        compiler_params=_params(3),
        name="window_mix",
    )(qt, kw, vwt, wb, oc, osl, gates)


def _outproj_body(a_ref, gm_ref, h_ref, wa_ref, wb_ref, g_ref, o_ref):
    y = _dot(a_ref[...], wa_ref[...]) + _dot(gm_ref[...], wb_ref[...])
    o_ref[...] = h_ref[...] + _rms(y, g_ref[...])


def _outproj(a, gm, h, wa, wb, gpost, tm):
    n, d = h.shape
    full = _resident
    row = lambda w: pl.BlockSpec((tm, w), lambda i: (i, 0))
    return pl.pallas_call(
        _outproj_body,
        grid=(n // tm,),
        in_specs=[row(a.shape[1]), row(gm.shape[1]), row(d), full(wa), full(wb), full(gpost)],
        out_specs=row(d),
        out_shape=jax.ShapeDtypeStruct((n, d), F32),
        compiler_params=_params(1),
        name="outproj",
    )(a, gm, h, wa, wb, gpost)


FFN_TILE = 256


def _ffn_body(h_ref, gpre_ref, wg_ref, wu_ref, wd_ref, gpost_ref, o_ref):
    h = h_ref[...]
    xb = _rms(h, gpre_ref[...]).astype(BF16)
    acc = jnp.zeros(h.shape, F32)
    for j in range(wg_ref.shape[1] // FFN_TILE):
        cols = slice(j * FFN_TILE, (j + 1) * FFN_TILE)
        gate = _dot(xb, wg_ref[:, cols])
        up = _dot(xb, wu_ref[:, cols])
        act = (gate * _sigmoid(gate) * up).astype(BF16)
        acc = acc + _dot(act, wd_ref[cols, :])
    o_ref[...] = h + _rms(acc, gpost_ref[...])


def _ffn(h, gpre, wg, wu, wd, gpost, tm):
    n, d = h.shape
    full = _resident
    row = pl.BlockSpec((tm, d), lambda i: (i, 0))
    return pl.pallas_call(
        _ffn_body,
        grid=(n // tm,),
        in_specs=[row, full(gpre), full(wg), full(wu), full(wd), full(gpost)],
        out_specs=row,
        out_shape=jax.ShapeDtypeStruct((n, d), F32),
        compiler_params=_params(1),
        name="ffn",
    )(h, gpre, wg, wu, wd, gpost)


def _t5_bucket(dist):
    n = jnp.maximum(dist, 0)
    max_exact = N_BUCKETS // 2
    nf = jnp.maximum(n, max_exact).astype(F32)
    large = max_exact + (jnp.log(nf / max_exact) / math.log(REL_MAX_DISTANCE / max_exact)
                         * (N_BUCKETS - max_exact)).astype(jnp.int32)
    return jnp.where(n < max_exact, n, jnp.minimum(large, N_BUCKETS - 1))


def _bias_tile(table, dist, mask):
    b = table.astype(F32)[_t5_bucket(dist)]
    b = jnp.where(mask[..., None], b * LOG2E, NEG)
    k = dist.shape[0]
    return b.reshape(k, Q_BLOCK, N_KV_HEADS, GQA_GROUP).transpose(2, 0, 3, 1).reshape(N_KV_HEADS, k, ROWS)


def _bias_tiles(rel_bias):
    ql = jnp.arange(Q_BLOCK, dtype=jnp.int32)[None, :]
    kl = jnp.arange(KEY_CHUNK, dtype=jnp.int32)[:, None]
    chunk_dist = lambda delta: delta * KEY_CHUNK + ql - kl
    everything = jnp.ones((KEY_CHUNK, Q_BLOCK), bool)
    nothing = jnp.zeros((KEY_CHUNK, Q_BLOCK), bool)
    far = jnp.full((KEY_CHUNK, Q_BLOCK), REL_MAX_DISTANCE, jnp.int32)
    d0 = chunk_dist(0)
    sb = jnp.stack([_bias_tile(rel_bias, d0, d0 >= 0),
                    _bias_tile(rel_bias, chunk_dist(1), everything),
                    _bias_tile(rel_bias, far, everything),
                    _bias_tile(rel_bias, far, nothing)], axis=1)
    wtiles = []
    for delta in range(WIN_CHUNKS):
        d = chunk_dist(delta)
        wtiles.append(_bias_tile(rel_bias, d, (d >= 0) & (d < WINDOW)))
    wtiles.append(_bias_tile(rel_bias, far, nothing))
    wb = jnp.stack(wtiles, axis=1)
    rel = jnp.arange(CMP_WINDOW, dtype=jnp.int32)[:, None] - CMP_FRONT_PAD
    dc = ql - CMP_STRIDE * rel - (CMP_BLOCK - 1)
    cb = _bias_tile(rel_bias, dc, dc >= 0)
    crow = _bias_tile(rel_bias, far[:1], everything[:1])
    return sb, wb, cb, crow


def _to_rows(t, b, s):
    x = t.shape[1] // (N_KV_HEADS * GQA_GROUP)
    t = t.reshape(b, s // Q_BLOCK, Q_BLOCK, N_KV_HEADS, GQA_GROUP, x)
    return t.transpose(0, 3, 5, 1, 4, 2).reshape(b, N_KV_HEADS, x, (s // Q_BLOCK) * ROWS)


def _mixer(h, b, s, lw, tiles, tm):
    sb, wb, cb, crow = tiles
    q, cv, sw, gates, gm = _inproj(h, lw["gpre"], lw["wq"], lw["wkv"], lw["wg"], lw["wu"], lw["wv"],
                                   lw["lng"], lw["lnb"], lw["ws"], lw["bs"], tm)
    g, dh = N_KV_HEADS, HEAD_DIM
    n_chunks = s // CMP_STRIDE
    xc = cv.reshape(b, n_chunks, CMP_STRIDE, 2, g, dh).transpose(0, 3, 4, 1, 2, 5)
    xc = xc.reshape(b, 2, g, n_chunks, CMP_STRIDE * dh)
    comp = _compress(xc, lw["cpos"], lw["cw1"], lw["cw2"])
    ncp = -(-(n_chunks + CMP_WINDOW) // CMP_CHUNK) * CMP_CHUNK
    back = ncp - CMP_FRONT_PAD - n_chunks
    comp = jnp.pad(comp, ((0, 0), (0, 0), (0, 0), (CMP_FRONT_PAD, back), (0, 0)))
    front = (jnp.arange(ncp) < CMP_FRONT_PAD).astype(F32)[:, None]
    aug = jnp.concatenate([jnp.ones((ncp, 2), F32), front, jnp.zeros((ncp, CMP_AUG_WIDTH - dh - 3), F32)], axis=1)
    kcp = jnp.concatenate([comp[:, 0], jnp.broadcast_to(aug, (b, g) + aug.shape)], axis=-1)
    vct = comp[:, 1].transpose(0, 1, 3, 2).astype(BF16)

    qt = _to_rows(q, b, s)
    n_slc = s // SLC_BLOCK
    n_other = min(SLC_TOPK, n_slc) - (N_LOCAL_BLOCKS + 1)
    oc, sel = _cmp_select(qt, kcp, vct, crow, cb, n_slc, n_other)

    heads = lambda t: t.reshape(b, s, g, dh).transpose(0, 2, 1, 3)
    heads_t = lambda t: t.reshape(b, s, g, dh).transpose(0, 2, 3, 1)
    kvw = KV_WIDTH
    osl = _slc(qt, _augment_keys(heads(sw[:, 0:kvw])), _augment_values_t(heads_t(sw[:, kvw:2 * kvw])), sel, sb, crow)
    gates_t = _to_rows(gates[:, :N_GATES], b, s)
    o = _win_mix(qt, heads(sw[:, 2 * kvw:3 * kvw]), heads_t(sw[:, 3 * kvw:]), wb, oc, osl, gates_t)
    nqb = s // Q_BLOCK
    nsa = o.reshape(b, g, dh, nqb, GQA_GROUP, Q_BLOCK).transpose(0, 3, 5, 1, 4, 2).reshape(b * s, NSA_WIDTH)
    return _outproj(nsa, gm, h, lw["wo_a"], lw["wo_b"], lw["gpost"], tm)


def _layer_weights(l, p):
    d = p["w_in"].shape[1]
    w_in = p["w_in"][l]
    o = NSA_WIDTH
    wq = w_in[:, :o]
    wkv = w_in[:, o:o + 6 * KV_WIDTH]
    o += 6 * KV_WIDTH
    wg = jnp.pad(w_in[:, o:o + N_GATES], ((0, 0), (0, GATE_PAD - N_GATES)))
    o += N_GATES
    gw = (w_in.shape[1] - o) // 2
    wu, wv = w_in[:, o:o + gw], w_in[:, o + gw:]
    causal = jnp.tril(jnp.ones((GMLP_CHUNK, GMLP_CHUNK), bool))
    ws = jnp.where(causal, p["gmlp_w_s"][l], 0.0)
    bs = jnp.repeat(p["gmlp_b_s"][l].T, gw // N_GMLP_GROUPS, axis=1)
    half = CMP_STRIDE * HEAD_DIM
    cpos = jnp.stack([p["cmp_pos_k"][l].reshape(2, half), p["cmp_pos_v"][l].reshape(2, half)])
    dff = p["w_down"].shape[1]
    row = lambda v: v[l].reshape(1, -1)
    bf = lambda w: w.astype(BF16)
    return dict(gpre=row(p["norm_mix_pre"]), wq=bf(wq), wkv=bf(wkv), wg=bf(wg), wu=bf(wu), wv=bf(wv),
                lng=row(p["gmlp_ln_g"]), lnb=row(p["gmlp_ln_b"]), ws=bf(ws), bs=bs,
                cpos=cpos, cw1=bf(jnp.stack([p["cmp_w1_k"][l], p["cmp_w1_v"][l]])),
                cw2=bf(jnp.stack([p["cmp_w2_k"][l], p["cmp_w2_v"][l]])),
                wo_a=bf(p["w_out"][l][:NSA_WIDTH]), wo_b=bf(p["w_out"][l][NSA_WIDTH:]),
                gpost=row(p["norm_mix_post"]), fpre=row(p["norm_ffn_pre"]), fpost=row(p["norm_ffn_post"]),
                fwg=bf(p["w_gate_up"][l][:, :dff]), fwu=bf(p["w_gate_up"][l][:, dff:]), fwd=bf(p["w_down"][l]))


def _trunk(p, tm):
    x = p["x"]
    b, s, d = x.shape
    h = x.reshape(b * s, d)
    tiles = _bias_tiles(p["rel_bias"])
    for l in range(p["w_in"].shape[0]):
        lw = _layer_weights(l, p)
        h = _mixer(h, b, s, lw, tiles, tm)
        h = _ffn(h, lw["fpre"], lw["fwg"], lw["fwu"], lw["fwd"], lw["fpost"], tm)
    return h.reshape(b, s, d)


def kernel(x, rel_bias, norm_mix_pre, norm_mix_post, norm_ffn_pre, norm_ffn_post, w_in, cmp_pos_k, cmp_w1_k, cmp_w2_k, cmp_pos_v, cmp_w1_v, cmp_w2_v, gmlp_ln_g, gmlp_ln_b, gmlp_w_s, gmlp_b_s, w_out, w_gate_up, w_down):
    p = dict(x=x, rel_bias=rel_bias, norm_mix_pre=norm_mix_pre, norm_mix_post=norm_mix_post,
             norm_ffn_pre=norm_ffn_pre, norm_ffn_post=norm_ffn_post, w_in=w_in,
             cmp_pos_k=cmp_pos_k, cmp_w1_k=cmp_w1_k, cmp_w2_k=cmp_w2_k,
             cmp_pos_v=cmp_pos_v, cmp_w1_v=cmp_w1_v, cmp_w2_v=cmp_w2_v,
             gmlp_ln_g=gmlp_ln_g, gmlp_ln_b=gmlp_ln_b, gmlp_w_s=gmlp_w_s, gmlp_b_s=gmlp_b_s,
             w_out=w_out, w_gate_up=w_gate_up, w_down=w_down)
    return _trunk(p, tm=512)
```

```python
import functools
import math

import jax
import jax.numpy as jnp
import numpy as np
from jax import lax
from jax.experimental import pallas as pl
from jax.experimental.pallas import tpu as pltpu

F32 = jnp.float32
BF16 = jnp.bfloat16

N_NSA_HEADS = 8
N_KV_HEADS = 2
GQA_GROUP = N_NSA_HEADS // N_KV_HEADS
HEAD_DIM = 64
NSA_WIDTH = N_NSA_HEADS * HEAD_DIM
KV_WIDTH = N_KV_HEADS * HEAD_DIM
N_BRANCHES = 3
N_GATES = N_BRANCHES * N_NSA_HEADS
CMP_BLOCK = 32
CMP_STRIDE = 16
SLC_BLOCK = 64
SLC_TOPK = 16
N_LOCAL_BLOCKS = 2
WINDOW = 512
Q_BLOCK = 128
N_GMLP_GROUPS = 8
GMLP_CHUNK = 128
N_BUCKETS = 32
REL_MAX_DISTANCE = 128
RMS_EPS = 1e-6
LN_EPS = 1e-5

ROWS = GQA_GROUP * Q_BLOCK
KEY_CHUNK = 128
CMP_FRONT_PAD = 24
CMP_WINDOW = 32
NEG = -1e30
LANE = 128
GATE_PAD = 128
MXU_TILE = 256
BF16_ROWS = 16
VMEM_LIMIT = 48 * 1024 * 1024
LOG2E = math.log2(math.e)
Q_SCALE = HEAD_DIM ** -0.5 * LOG2E


def _dot(a, b):
    return jnp.dot(a, b, preferred_element_type=F32)


def _gelu(x):
    c = math.sqrt(2.0 / math.pi)
    return 0.5 * x * (1.0 + jnp.tanh(c * (x + 0.044715 * (x * x * x))))


def _sigmoid(x):
    return 1.0 / (1.0 + jnp.exp(-x))


def _rms(x, g):
    ms = jnp.mean(x * x, axis=-1, keepdims=True)
    return (x * lax.rsqrt(ms + RMS_EPS)) * g


def _resident(a):
    return pl.BlockSpec(a.shape, lambda i: (0,) * a.ndim, pipeline_mode=pl.Buffered(1))


def _params(n_axes):
    return pltpu.CompilerParams(dimension_semantics=("arbitrary",) * n_axes,
                                vmem_limit_bytes=VMEM_LIMIT)


def _dot_nt(a, b):
    return lax.dot_general(a, b, (((1,), (1,)), ((), ())), preferred_element_type=F32)


def _dot_tn(a, b):
    return lax.dot_general(a, b, (((0,), (0,)), ((), ())), preferred_element_type=F32)


def _inproj_body(x_ref, gpre_ref, wqt_ref, wnat_ref, wvt_ref, wgt_ref, wu_ref, wv_ref, lng_ref, lnb_ref,
                 ws_ref, bs_ref, aug_ref, qt_ref, cv_ref, ksa_ref, kw_ref, vst_ref, vwt_ref, gate_ref, gm_ref):
    tm = x_ref.shape[0]
    xb = _rms(x_ref[...], gpre_ref[...]).astype(BF16)
    qt_ref[...] = (_dot_nt(wqt_ref[...], xb) * Q_SCALE).astype(BF16)
    nat = _dot(xb, wnat_ref[...])
    cv_ref[...] = nat[:, :2 * KV_WIDTH]
    ksa_ref[...] = jnp.concatenate([nat[:, 2 * KV_WIDTH:3 * KV_WIDTH].astype(BF16), aug_ref[...]], axis=1)
    kw_ref[...] = nat[:, 3 * KV_WIDTH:].astype(BF16)
    vt = _dot_nt(wvt_ref[...], xb).astype(BF16)
    denom = jnp.concatenate([jnp.ones((1, tm), BF16), jnp.zeros((BF16_ROWS - 1, tm), BF16)], axis=0)
    vst_ref[...] = jnp.concatenate(
        [piece for g in range(N_KV_HEADS) for piece in (vt[g * HEAD_DIM:(g + 1) * HEAD_DIM], denom)], axis=0)
    vwt_ref[...] = vt[KV_WIDTH:]
    gate_ref[...] = _sigmoid(_dot_nt(wgt_ref[...], xb))
    zu = _gelu(_dot(xb, wu_ref[...]))
    zv = _gelu(_dot(xb, wv_ref[...]))
    mu = jnp.mean(zv, axis=-1, keepdims=True)
    zc = zv - mu
    var = jnp.mean(zc * zc, axis=-1, keepdims=True)
    zv = ((zc * lax.rsqrt(var + LN_EPS)) * lng_ref[...] + lnb_ref[...]).astype(BF16)
    gdim = zv.shape[1] // N_GMLP_GROUPS
    left = lax.broadcasted_iota(jnp.int32, (GMLP_CHUNK, LANE), 1) < gdim
    for c in range(tm // GMLP_CHUNK):
        rows = slice(c * GMLP_CHUNK, (c + 1) * GMLP_CHUNK)
        for j in range(zv.shape[1] // LANE):
            cols = slice(j * LANE, (j + 1) * LANE)
            z = zv[rows, cols]
            sv = jnp.where(left, _dot(ws_ref[2 * j], z), _dot(ws_ref[2 * j + 1], z)) + bs_ref[:, cols]
            gm_ref[rows, cols] = (zu[rows, cols] * sv).astype(BF16)


VT_ROWS = HEAD_DIM + BF16_ROWS
GATE_ROWS = 16


def _inproj(x, b, weights, tm):
    n, d = x.shape
    s = n // b
    per_b = s // tm
    row = lambda w: pl.BlockSpec((tm, w), lambda i: (i, 0))
    col = lambda h: pl.BlockSpec((None, h, tm), lambda i: (i // per_b, 0, i % per_b))
    gw = weights[5].shape[1]
    return pl.pallas_call(
        _inproj_body,
        grid=(n // tm,),
        in_specs=[row(d)] + [_resident(a) for a in weights],
        out_specs=[col(NSA_WIDTH), row(2 * KV_WIDTH), row(K_AUG_WIDTH), row(KV_WIDTH),
                   col(N_KV_HEADS * VT_ROWS), col(KV_WIDTH), col(N_KV_HEADS * GATE_ROWS), row(gw)],
        out_shape=[jax.ShapeDtypeStruct((b, NSA_WIDTH, s), BF16),
                   jax.ShapeDtypeStruct((n, 2 * KV_WIDTH), F32),
                   jax.ShapeDtypeStruct((n, K_AUG_WIDTH), BF16),
                   jax.ShapeDtypeStruct((n, KV_WIDTH), BF16),
                   jax.ShapeDtypeStruct((b, N_KV_HEADS * VT_ROWS, s), BF16),
                   jax.ShapeDtypeStruct((b, KV_WIDTH, s), BF16),
                   jax.ShapeDtypeStruct((b, N_KV_HEADS * GATE_ROWS, s), F32),
                   jax.ShapeDtypeStruct((n, gw), BF16)],
        compiler_params=_params(1),
        name="inproj_gmlp",
    )(x, *weights)


def _compress_body(x_ref, pos_ref, w1_ref, w2_ref, o_ref):
    x = x_ref[...]
    half = x.shape[1]
    a = _dot((x + pos_ref[0:1, :]).astype(BF16), w1_ref[:half, :])
    b = _dot((x + pos_ref[1:2, :]).astype(BF16), w1_ref[half:, :])
    pre = a + pltpu.roll(b, x.shape[0] - 1, 0)
    o_ref[...] = _dot(_gelu(pre).astype(BF16), w2_ref[...])


def _compress(xc, pos, w1, w2):
    b, two, g, nch, width = xc.shape
    hid = w1.shape[2]
    dh = w2.shape[2]
    return pl.pallas_call(
        _compress_body,
        grid=(b, two, g),
        in_specs=[pl.BlockSpec((None, None, None, nch, width), lambda i, t, j: (i, t, j, 0, 0)),
                  pl.BlockSpec((None, 2, width), lambda i, t, j: (t, 0, 0)),
                  pl.BlockSpec((None, 2 * width, hid), lambda i, t, j: (t, 0, 0)),
                  pl.BlockSpec((None, hid, dh), lambda i, t, j: (t, 0, 0))],
        out_specs=pl.BlockSpec((None, None, None, nch, dh), lambda i, t, j: (i, t, j, 0, 0)),
        out_shape=jax.ShapeDtypeStruct((b, two, g, nch, dh), F32),
        compiler_params=_params(3),
        name="compress",
    )(xc, pos, w1, w2)


CMP_CHUNK = 256
CMP_AUG_WIDTH = 2 * HEAD_DIM


def _heads_to_lanes(blk):
    n = blk.shape[1] // Q_BLOCK
    return jnp.concatenate([blk[r * HEAD_DIM:(r + 1) * HEAD_DIM, h * Q_BLOCK:(h + 1) * Q_BLOCK]
                            for h in range(n) for r in range(GQA_GROUP)], axis=1)


def _group_slots(qt, g):
    return jnp.concatenate([jnp.where(g == j, qt, jnp.zeros_like(qt)) for j in range(N_KV_HEADS)], axis=0)


def _q_spec(n_blocks):
    return pl.BlockSpec((None, GQA_GROUP * HEAD_DIM, n_blocks * Q_BLOCK), lambda i, j, q: (i, j, q))


def _cmp_select_body(q_ref, k_ref, vt_ref, crow_ref, cb_ref, oc_ref, sel_ref, sc_ref, *, n_other):
    qi = pl.program_id(2)
    n_slc = sel_ref.shape[0]
    qt = _heads_to_lanes(q_ref[...])
    dh = qt.shape[0]
    w0 = pl.multiple_of(qi * (Q_BLOCK // CMP_STRIDE), 8)
    n_chunks = (w0 + CMP_WINDOW + CMP_CHUNK - 1) // CMP_CHUNK
    row_iota = lax.broadcasted_iota(jnp.int32, (CMP_CHUNK, ROWS), 0)

    @pl.when(qi == 0)
    def _():
        sc_ref[...] = jnp.zeros_like(sc_ref)

    c = crow_ref[...]
    c_hi = c.astype(BF16).astype(F32)
    neg_row = jnp.full((1, ROWS), NEG, F32)
    zeros = lambda n, dt: jnp.zeros((n, ROWS), dt)
    tail = zeros(k_ref.shape[1] - dh - BF16_ROWS, BF16)
    qa = jnp.concatenate([qt, jnp.concatenate([c_hi, c - c_hi, neg_row, zeros(BF16_ROWS - 3, F32)]).astype(BF16), tail])
    qw = jnp.concatenate([qt, jnp.concatenate([zeros(2, F32), neg_row, zeros(BF16_ROWS - 3, F32)]).astype(BF16), tail])

    def rows_of(ch):
        return pl.ds(pl.multiple_of(ch * CMP_CHUNK, CMP_CHUNK), CMP_CHUNK)

    head = lambda r: slice(r * Q_BLOCK, (r + 1) * Q_BLOCK)

    def put(rows, val):
        for r in range(GQA_GROUP):
            sc_ref[r, rows, :] = val[:, head(r)]

    def get(rows):
        return jnp.concatenate([sc_ref[r, rows, :] for r in range(GQA_GROUP)], axis=1)

    def logits(ch, m):
        s = _dot(k_ref[rows_of(ch), :].astype(BF16), qa)
        s = jnp.where(row_iota + ch * CMP_CHUNK < w0, s, NEG)
        put(rows_of(ch), s)
        return jnp.maximum(m, jnp.max(s, axis=0, keepdims=True))

    m = lax.fori_loop(0, n_chunks, logits, jnp.full((1, ROWS), NEG, F32))
    win = pl.ds(w0, CMP_WINDOW)
    s_win = _dot(k_ref[win, :].astype(BF16), qw) + cb_ref[...]
    put(win, s_win)
    m = jnp.maximum(m, jnp.max(s_win, axis=0, keepdims=True))

    def weigh(ch, carry):
        l, acc = carry
        p = jnp.exp2(get(rows_of(ch)) - m)
        put(rows_of(ch), p)
        return l + jnp.sum(p, axis=0, keepdims=True), acc + _dot(vt_ref[:, rows_of(ch)], p.astype(BF16))

    l, acc = lax.fori_loop(0, n_chunks, weigh, (jnp.zeros((1, ROWS), F32), jnp.zeros(oc_ref.shape, F32)))
    scale = jnp.where(m > 0.5 * NEG, 1.0 / jnp.maximum(l, 1e-30), 0.0)
    oc_ref[...] = acc * scale

    per = SLC_BLOCK // CMP_STRIDE
    imp = jnp.zeros((n_slc, Q_BLOCK), F32)
    for r in range(GQA_GROUP):
        part = lambda off: sc_ref[r, pl.ds(CMP_FRONT_PAD + off, n_slc, stride=per), :]
        tot = part(0)
        for k in range(1, per - 1):
            tot = tot + part(k)
        imp = imp + (tot + 0.5 * (part(per - 1) + part(-1))) * scale[:, head(r)]

    blk = lax.broadcasted_iota(jnp.int32, (n_slc, Q_BLOCK), 0)
    pos = qi * Q_BLOCK + lax.broadcasted_iota(jnp.int32, (n_slc, Q_BLOCK), 1)
    jq = pos // SLC_BLOCK
    valid = blk <= jq
    forced = (blk == 0) | (valid & (blk > jq - N_LOCAL_BLOCKS))
    blk_f = blk.astype(F32)
    free = valid & jnp.logical_not(forced)
    work = jnp.where(free, imp, -1.0)
    for _ in range(n_other):
        mx = jnp.max(work, axis=0, keepdims=True)
        first = jnp.min(jnp.where(work == mx, blk_f, float(n_slc)), axis=0, keepdims=True)
        work = jnp.where((blk_f == first) & (mx >= 0.0), -1.0, work)
    sel_ref[...] = jnp.where(forced | (free & (work < 0.0)), 1.0, 0.0)


def _cmp_select(qt, kcp, vct, crow, cb, n_slc, n_other):
    b, g, dh = qt.shape[0], N_KV_HEADS, HEAD_DIM
    nqb = qt.shape[2] // Q_BLOCK
    total = nqb * ROWS
    ncp = kcp.shape[2]
    blk_q = pl.BlockSpec((None, None, dh, ROWS), lambda i, j, q: (i, j, 0, q))
    return pl.pallas_call(
        functools.partial(_cmp_select_body, n_other=n_other),
        grid=(b, g, nqb),
        in_specs=[_q_spec(1),
                  pl.BlockSpec((None, None, ncp, kcp.shape[3]), lambda i, j, q: (i, j, 0, 0)),
                  pl.BlockSpec((None, None, dh, ncp), lambda i, j, q: (i, j, 0, 0)),
                  pl.BlockSpec((None, 1, ROWS), lambda i, j, q: (j, 0, 0)),
                  pl.BlockSpec((None, CMP_WINDOW, ROWS), lambda i, j, q: (j, 0, 0))],
        out_specs=[blk_q,
                   pl.BlockSpec((None, None, None, n_slc, Q_BLOCK), lambda i, j, q: (i, j, q, 0, 0))],
        out_shape=[jax.ShapeDtypeStruct((b, g, dh, total), F32),
                   jax.ShapeDtypeStruct((b, g, nqb, n_slc, Q_BLOCK), F32)],
        scratch_shapes=[pltpu.VMEM((GQA_GROUP, ncp, Q_BLOCK), F32)],
        compiler_params=_params(3),
        name="cmp_select",
    )(qt, kcp, vct, crow, cb)


SLC_QBLOCKS = 2
FAR_KEYS = 256
FAR_BLOCKS = FAR_KEYS // SLC_BLOCK
FAR_CHUNKS = FAR_KEYS // KEY_CHUNK
FAR_BUFFERS = 4
FAR_AHEAD = 2
PEN_BLOCKS = 8
AUG_CONST = 2
AUG_ROWS = 16
K_AUG_WIDTH = KV_WIDTH + LANE


def _key_augmentation(tm):
    assert tm % (PEN_BLOCKS * SLC_BLOCK) == 0
    pos = jnp.arange(tm, dtype=jnp.int32)
    onehot = ((pos[:, None] // SLC_BLOCK) % PEN_BLOCKS == jnp.arange(PEN_BLOCKS, dtype=jnp.int32)[None, :])
    return jnp.concatenate([jnp.ones((tm, AUG_CONST), BF16),
                            jnp.zeros((tm, AUG_ROWS - PEN_BLOCKS - AUG_CONST), BF16),
                            onehot.astype(BF16),
                            jnp.zeros((tm, LANE - AUG_ROWS), BF16)], axis=1)


def _slc_body(q_ref, k_ref, vt_ref, sel_ref, sb_ref, crow_ref, os_ref, sbuf):
    qis = [pl.program_id(2) * SLC_QBLOCKS + h for h in range(SLC_QBLOCKS)]
    group = pl.program_id(1)
    qt = _heads_to_lanes(q_ref[...])
    dh, lanes = qt.shape
    qslots = _group_slots(qt, group)
    per = KEY_CHUNK // SLC_BLOCK
    far_limits = [jnp.maximum(qi - 1, 0) * per for qi in qis]
    n_steps = (jnp.maximum(qis[-1] - 1, 0) + FAR_CHUNKS - 1) // FAR_CHUNKS
    last_step = k_ref.shape[0] // FAR_KEYS - 1
    steps_per_group = PEN_BLOCKS // FAR_BLOCKS

    c = jnp.concatenate([crow_ref[...]] * SLC_QBLOCKS, axis=1)
    c_hi = c.astype(BF16).astype(F32)
    const_rows = jnp.concatenate([c_hi, c - c_hi, jnp.zeros((AUG_ROWS - PEN_BLOCKS - AUG_CONST, lanes), F32)], axis=0)
    pad_rows = jnp.zeros((k_ref.shape[1] - KV_WIDTH - AUG_ROWS, lanes), BF16)
    blk_iota = lax.broadcasted_iota(jnp.int32, (PEN_BLOCKS, Q_BLOCK), 0)

    def far_logits(u):
        ua = jnp.minimum(u, last_step)
        grp0 = pl.multiple_of((ua // steps_per_group) * PEN_BLOCKS, PEN_BLOCKS)
        blk = blk_iota + (u // steps_per_group) * PEN_BLOCKS
        pens = []
        for h in range(SLC_QBLOCKS):
            pen = jnp.where((sel_ref[h, pl.ds(grp0, PEN_BLOCKS), :] > 0.5) & (blk < far_limits[h]), 0.0, NEG)
            pens += [pen] * GQA_GROUP
        qa = jnp.concatenate([qslots, jnp.concatenate([const_rows, jnp.concatenate(pens, axis=1)], axis=0).astype(BF16),
                              pad_rows], axis=0)
        return _dot(k_ref[pl.ds(pl.multiple_of(ua * FAR_KEYS, FAR_KEYS), FAR_KEYS), :], qa)

    def stage(slot, u):
        s_new = far_logits(u).astype(BF16)
        sbuf[slot] = s_new
        groups = s_new.reshape(FAR_KEYS // BF16_ROWS, BF16_ROWS, lanes)
        return jnp.max(jnp.max(groups, axis=0).astype(F32), axis=0, keepdims=True)

    def far_update(m, acc, s_ref, mx, key0):
        m_new = jnp.maximum(m, mx)
        p = jnp.exp2(s_ref[...] - m_new.astype(BF16))
        cols = pl.ds(pl.multiple_of(key0, FAR_KEYS), FAR_KEYS)
        return m_new, jnp.exp2(m - m_new) * acc + _dot(vt_ref[:, cols], p)

    def far_round(v, carry):
        m, acc = carry[:2]
        ahead = list(carry[2:])
        for slot in range(FAR_BUFFERS):
            u = FAR_BUFFERS * v + slot
            ahead.append(stage((slot + FAR_AHEAD) % FAR_BUFFERS, u + FAR_AHEAD))
            ua = jnp.minimum(u, last_step)
            m, acc = far_update(m, acc, sbuf.at[slot], ahead.pop(0), ua * FAR_KEYS)
        return (m, acc, *ahead)

    ahead = [stage(u, u) for u in range(FAR_AHEAD)]
    init = (jnp.full((1, lanes), NEG, F32), jnp.zeros((vt_ref.shape[0], lanes), F32), *ahead)
    m, acc = lax.fori_loop(0, (n_steps + FAR_BUFFERS - 1) // FAR_BUFFERS, far_round, init)[:2]

    for h, qi in enumerate(qis):
        cols = slice(h * ROWS, (h + 1) * ROWS)
        qd = jnp.concatenate([qslots[:, cols], jnp.zeros((k_ref.shape[1] - KV_WIDTH, ROWS), BF16)], axis=0)
        prev = jnp.maximum(qi - 1, 0)
        tiles, vts = [], []
        for chunk, kind in ((prev, jnp.where(qi >= 1, 1, 3)), (qi, 0)):
            rows = pl.ds(pl.multiple_of(chunk * KEY_CHUNK, KEY_CHUNK), KEY_CHUNK)
            s = _dot(k_ref[rows, :], qd) + sb_ref[kind]
            for j in range(per):
                srow = sel_ref[h, pl.ds(chunk * per + j, 1), :]
                srow = jnp.concatenate([srow] * GQA_GROUP, axis=1)
                tiles.append(jnp.where(srow > 0.5, s[j * SLC_BLOCK:(j + 1) * SLC_BLOCK, :], NEG))
            vts.append(vt_ref[:, rows])
        s = jnp.concatenate(tiles, axis=0)
        m_new = jnp.maximum(m[:, cols], jnp.max(s, axis=0, keepdims=True))
        p = jnp.exp2(s - m_new).astype(BF16)
        acc_h = jnp.exp2(m[:, cols] - m_new) * acc[:, cols] + _dot(jnp.concatenate(vts, axis=1), p)
        os_ref[:, cols] = acc_h[:dh] / jnp.maximum(acc_h[dh:dh + 1], 1e-30)


def _slc(qt, ks, vst, sel, sb, crow):
    b, g, dh = qt.shape[0], N_KV_HEADS, HEAD_DIM
    s_len, kw = ks.shape[1], ks.shape[2]
    nqb = s_len // Q_BLOCK
    total = nqb * ROWS
    n_slc = sel.shape[3]
    blk_q = pl.BlockSpec((None, None, dh, SLC_QBLOCKS * ROWS), lambda i, j, q: (i, j, 0, q))
    return pl.pallas_call(
        _slc_body,
        grid=(b, g, nqb // SLC_QBLOCKS),
        in_specs=[_q_spec(SLC_QBLOCKS),
                  pl.BlockSpec((None, s_len, kw), lambda i, j, q: (i, 0, 0)),
                  pl.BlockSpec((None, VT_ROWS, s_len), lambda i, j, q: (i, j, 0)),
                  pl.BlockSpec((None, None, SLC_QBLOCKS, n_slc, Q_BLOCK), lambda i, j, q: (i, j, q, 0, 0)),
                  pl.BlockSpec((None, 4, KEY_CHUNK, ROWS), lambda i, j, q: (j, 0, 0, 0)),
                  pl.BlockSpec((None, 1, ROWS), lambda i, j, q: (j, 0, 0))],
        out_specs=blk_q,
        out_shape=jax.ShapeDtypeStruct((b, g, dh, total), F32),
        scratch_shapes=[pltpu.VMEM((FAR_BUFFERS, FAR_KEYS, SLC_QBLOCKS * ROWS), BF16)],
        compiler_params=_params(3),
        name="slc_attention",
    )(qt, ks, vst, sel, sb, crow)


WIN_CHUNKS = WINDOW // KEY_CHUNK + 1


def _win_body(q_ref, k_ref, vt_ref, wb_ref, oc_ref, os_ref, gate_ref, o_ref):
    qi = pl.program_id(2)
    qt = _group_slots(_heads_to_lanes(q_ref[...]), pl.program_id(1))
    tiles, vts = [], []
    for delta in range(WIN_CHUNKS - 1, -1, -1):
        c = jnp.maximum(qi - delta, 0)
        rows = pl.ds(pl.multiple_of(c * KEY_CHUNK, KEY_CHUNK), KEY_CHUNK)
        kind = jnp.where(qi >= delta, delta, WIN_CHUNKS)
        tiles.append(_dot(k_ref[rows, :], qt) + wb_ref[kind])
        vts.append(vt_ref[:, rows])
    s = jnp.concatenate(tiles, axis=0)
    m = jnp.max(s, axis=0, keepdims=True)
    p = jnp.exp2(s - m)
    l = jnp.sum(p, axis=0, keepdims=True)
    o_w = _dot(jnp.concatenate(vts, axis=1), p.astype(BF16)) / jnp.maximum(l, 1e-30)
    gate = lambda br: jnp.concatenate([gate_ref[br * GQA_GROUP + r:br * GQA_GROUP + r + 1, :]
                                       for r in range(GQA_GROUP)], axis=1)
    o = gate(0) * oc_ref[...] + gate(1) * os_ref[...] + gate(2) * o_w
    for r in range(GQA_GROUP):
        o_ref[r * HEAD_DIM:(r + 1) * HEAD_DIM, :] = o[:, r * Q_BLOCK:(r + 1) * Q_BLOCK].astype(o_ref.dtype)


def _win_mix(qt, kw, vwt, wb, oc, osl, gates):
    b, g, dh = qt.shape[0], N_KV_HEADS, HEAD_DIM
    s_len = kw.shape[1]
    blk_q = pl.BlockSpec((None, None, dh, ROWS), lambda i, j, q: (i, j, 0, q))
    return pl.pallas_call(
        _win_body,
        grid=(b, g, s_len // Q_BLOCK),
        in_specs=[_q_spec(1),
                  pl.BlockSpec((None, s_len, kw.shape[2]), lambda i, j, q: (i, 0, 0)),
                  pl.BlockSpec((None, dh, s_len), lambda i, j, q: (i, j, 0)),
                  pl.BlockSpec((None, WIN_CHUNKS + 1, KEY_CHUNK, ROWS), lambda i, j, q: (j, 0, 0, 0)),
                  blk_q, blk_q,
                  pl.BlockSpec((None, GATE_ROWS, Q_BLOCK), lambda i, j, q: (i, j, q))],
        out_specs=_q_spec(1),
        out_shape=jax.ShapeDtypeStruct(qt.shape, BF16),
        compiler_params=_params(3),
        name="window_mix",
    )(qt, kw, vwt, wb, oc, osl, gates)


def _outproj_body(a_ref, gm_ref, h_ref, wa_ref, wb_ref, g_ref, o_ref):
    y = _dot_tn(a_ref[...], wa_ref[...]) + _dot(gm_ref[...], wb_ref[...])
    o_ref[...] = h_ref[...] + _rms(y, g_ref[...])


def _outproj(a, gm, h, wa, wb, gpost, tm):
    n, d = h.shape
    per_b = a.shape[2] // tm
    full = _resident
    row = lambda w: pl.BlockSpec((tm, w), lambda i: (i, 0))
    return pl.pallas_call(
        _outproj_body,
        grid=(n // tm,),
        in_specs=[pl.BlockSpec((None, a.shape[1], tm), lambda i: (i // per_b, 0, i % per_b)),
                  row(gm.shape[1]), row(d), full(wa), full(wb), full(gpost)],
        out_specs=row(d),
        out_shape=jax.ShapeDtypeStruct((n, d), F32),
        compiler_params=_params(1),
        name="outproj",
    )(a, gm, h, wa, wb, gpost)


FFN_TILE = 256


def _ffn_body(h_ref, gpre_ref, wg_ref, wu_ref, wd_ref, gpost_ref, o_ref):
    h = h_ref[...]
    xb = _rms(h, gpre_ref[...]).astype(BF16)
    acc = jnp.zeros(h.shape, F32)
    for j in range(wg_ref.shape[1] // FFN_TILE):
        cols = slice(j * FFN_TILE, (j + 1) * FFN_TILE)
        gate = _dot(xb, wg_ref[:, cols])
        up = _dot(xb, wu_ref[:, cols])
        act = (gate * _sigmoid(gate) * up).astype(BF16)
        acc = acc + _dot(act, wd_ref[cols, :])
    o_ref[...] = h + _rms(acc, gpost_ref[...])


def _ffn(h, gpre, wg, wu, wd, gpost, tm):
    n, d = h.shape
    full = _resident
    row = pl.BlockSpec((tm, d), lambda i: (i, 0))
    return pl.pallas_call(
        _ffn_body,
        grid=(n // tm,),
        in_specs=[row, full(gpre), full(wg), full(wu), full(wd), full(gpost)],
        out_specs=row,
        out_shape=jax.ShapeDtypeStruct((n, d), F32),
        compiler_params=_params(1),
        name="ffn",
    )(h, gpre, wg, wu, wd, gpost)


def _t5_bucket(dist):
    n = jnp.maximum(dist, 0)
    max_exact = N_BUCKETS // 2
    nf = jnp.maximum(n, max_exact).astype(F32)
    large = max_exact + (jnp.log(nf / max_exact) / math.log(REL_MAX_DISTANCE / max_exact)
                         * (N_BUCKETS - max_exact)).astype(jnp.int32)
    return jnp.where(n < max_exact, n, jnp.minimum(large, N_BUCKETS - 1))


def _bias_tile(table, dist, mask):
    b = table.astype(F32)[_t5_bucket(dist)]
    b = jnp.where(mask[..., None], b * LOG2E, NEG)
    k = dist.shape[0]
    return b.reshape(k, Q_BLOCK, N_KV_HEADS, GQA_GROUP).transpose(2, 0, 3, 1).reshape(N_KV_HEADS, k, ROWS)


def _bias_tiles(rel_bias):
    ql = jnp.arange(Q_BLOCK, dtype=jnp.int32)[None, :]
    kl = jnp.arange(KEY_CHUNK, dtype=jnp.int32)[:, None]
    chunk_dist = lambda delta: delta * KEY_CHUNK + ql - kl
    everything = jnp.ones((KEY_CHUNK, Q_BLOCK), bool)
    nothing = jnp.zeros((KEY_CHUNK, Q_BLOCK), bool)
    far = jnp.full((KEY_CHUNK, Q_BLOCK), REL_MAX_DISTANCE, jnp.int32)
    d0 = chunk_dist(0)
    sb = jnp.stack([_bias_tile(rel_bias, d0, d0 >= 0),
                    _bias_tile(rel_bias, chunk_dist(1), everything),
                    _bias_tile(rel_bias, far, everything),
                    _bias_tile(rel_bias, far, nothing)], axis=1)
    wtiles = []
    for delta in range(WIN_CHUNKS):
        d = chunk_dist(delta)
        wtiles.append(_bias_tile(rel_bias, d, (d >= 0) & (d < WINDOW)))
    wtiles.append(_bias_tile(rel_bias, far, nothing))
    wb = jnp.stack(wtiles, axis=1)
    rel = jnp.arange(CMP_WINDOW, dtype=jnp.int32)[:, None] - CMP_FRONT_PAD
    dc = ql - CMP_STRIDE * rel - (CMP_BLOCK - 1)
    cb = _bias_tile(rel_bias, dc, dc >= 0)
    crow = _bias_tile(rel_bias, far[:1], everything[:1])
    return sb, wb, cb, crow


def _mixer(h, b, s, lw, tiles, tm):
    sb, wb, cb, crow = tiles
    qt, cv, ksa, kw, vst, vwt, gates, gm = _inproj(
        h, b, (lw["gpre"], lw["wqt"], lw["wnat"], lw["wvt"], lw["wgt"], lw["wu"], lw["wv"],
               lw["lng"], lw["lnb"], lw["ws"], lw["bs"], _key_augmentation(tm)), tm)
    g, dh = N_KV_HEADS, HEAD_DIM
    n_chunks = s // CMP_STRIDE
    xc = cv.reshape(b, n_chunks, CMP_STRIDE, 2, g, dh).transpose(0, 3, 4, 1, 2, 5)
    xc = xc.reshape(b, 2, g, n_chunks, CMP_STRIDE * dh)
    comp = _compress(xc, lw["cpos"], lw["cw1"], lw["cw2"])
    ncp = -(-(n_chunks + CMP_WINDOW) // CMP_CHUNK) * CMP_CHUNK
    back = ncp - CMP_FRONT_PAD - n_chunks
    comp = jnp.pad(comp, ((0, 0), (0, 0), (0, 0), (CMP_FRONT_PAD, back), (0, 0)))
    front = (jnp.arange(ncp) < CMP_FRONT_PAD).astype(F32)[:, None]
    aug = jnp.concatenate([jnp.ones((ncp, 2), F32), front, jnp.zeros((ncp, CMP_AUG_WIDTH - dh - 3), F32)], axis=1)
    kcp = jnp.concatenate([comp[:, 0], jnp.broadcast_to(aug, (b, g) + aug.shape)], axis=-1)
    vct = comp[:, 1].transpose(0, 1, 3, 2).astype(BF16)

    n_slc = s // SLC_BLOCK
    n_other = min(SLC_TOPK, n_slc) - (N_LOCAL_BLOCKS + 1)
    oc, sel = _cmp_select(qt, kcp, vct, crow, cb, n_slc, n_other)
    osl = _slc(qt, ksa.reshape(b, s, ksa.shape[1]), vst, sel, sb, crow)
    ot = _win_mix(qt, kw.reshape(b, s, kw.shape[1]), vwt, wb, oc, osl, gates)
    return _outproj(ot, gm, h, lw["wo_a"], lw["wo_b"], lw["gpost"], tm)


def _layer_weights(l, p):
    d = p["w_in"].shape[1]
    w_in = p["w_in"][l]
    o = NSA_WIDTH
    wqt = w_in[:, :o].T
    k_c, v_c, k_s, v_s, k_w, v_w = [w_in[:, o + i * KV_WIDTH:o + (i + 1) * KV_WIDTH] for i in range(6)]
    wnat = jnp.concatenate([k_c, v_c, k_s, k_w], axis=1)
    wvt = jnp.concatenate([v_s, v_w], axis=1).T
    o += 6 * KV_WIDTH
    wg = w_in[:, o:o + N_GATES].reshape(d, N_KV_HEADS, GQA_GROUP, N_BRANCHES).transpose(1, 3, 2, 0)
    wg = wg.reshape(N_KV_HEADS, N_BRANCHES * GQA_GROUP, d)
    wgt = jnp.pad(wg, ((0, 0), (0, GATE_ROWS - N_BRANCHES * GQA_GROUP), (0, 0))).reshape(N_KV_HEADS * GATE_ROWS, d)
    o += N_GATES
    gw = (w_in.shape[1] - o) // 2
    wu, wv = w_in[:, o:o + gw], w_in[:, o + gw:]
    causal = jnp.tril(jnp.ones((GMLP_CHUNK, GMLP_CHUNK), bool))
    ws = jnp.where(causal, p["gmlp_w_s"][l], 0.0)
    bs = jnp.repeat(p["gmlp_b_s"][l].T, gw // N_GMLP_GROUPS, axis=1)
    half = CMP_STRIDE * HEAD_DIM
    cpos = jnp.stack([p["cmp_pos_k"][l].reshape(2, half), p["cmp_pos_v"][l].reshape(2, half)])
    dff = p["w_down"].shape[1]
    row = lambda v: v[l].reshape(1, -1)
    bf = lambda w: w.astype(BF16)
    return dict(gpre=row(p["norm_mix_pre"]), wqt=bf(wqt), wnat=bf(wnat), wvt=bf(wvt), wgt=bf(wgt), wu=bf(wu), wv=bf(wv),
                lng=row(p["gmlp_ln_g"]), lnb=row(p["gmlp_ln_b"]), ws=bf(ws), bs=bs,
                cpos=cpos, cw1=bf(jnp.stack([p["cmp_w1_k"][l], p["cmp_w1_v"][l]])),
                cw2=bf(jnp.stack([p["cmp_w2_k"][l], p["cmp_w2_v"][l]])),
                wo_a=bf(p["w_out"][l][:NSA_WIDTH]), wo_b=bf(p["w_out"][l][NSA_WIDTH:]),
                gpost=row(p["norm_mix_post"]), fpre=row(p["norm_ffn_pre"]), fpost=row(p["norm_ffn_post"]),
                fwg=bf(p["w_gate_up"][l][:, :dff]), fwu=bf(p["w_gate_up"][l][:, dff:]), fwd=bf(p["w_down"][l]))


def _trunk(p, tm):
    x = p["x"]
    b, s, d = x.shape
    h = x.reshape(b * s, d)
    tiles = _bias_tiles(p["rel_bias"])
    for l in range(p["w_in"].shape[0]):
        lw = _layer_weights(l, p)
        h = _mixer(h, b, s, lw, tiles, tm)
        h = _ffn(h, lw["fpre"], lw["fwg"], lw["fwu"], lw["fwd"], lw["fpost"], tm)
    return h.reshape(b, s, d)


def kernel(x, rel_bias, norm_mix_pre, norm_mix_post, norm_ffn_pre, norm_ffn_post, w_in, cmp_pos_k, cmp_w1_k, cmp_w2_k, cmp_pos_v, cmp_w1_v, cmp_w2_v, gmlp_ln_g, gmlp_ln_b, gmlp_w_s, gmlp_b_s, w_out, w_gate_up, w_down):
    p = dict(x=x, rel_bias=rel_bias, norm_mix_pre=norm_mix_pre, norm_mix_post=norm_mix_post,
             norm_ffn_pre=norm_ffn_pre, norm_ffn_post=norm_ffn_post, w_in=w_in,
             cmp_pos_k=cmp_pos_k, cmp_w1_k=cmp_w1_k, cmp_w2_k=cmp_w2_k,
             cmp_pos_v=cmp_pos_v, cmp_w1_v=cmp_w1_v, cmp_w2_v=cmp_w2_v,
             gmlp_ln_g=gmlp_ln_g, gmlp_ln_b=gmlp_ln_b, gmlp_w_s=gmlp_w_s, gmlp_b_s=gmlp_b_s,
             w_out=w_out, w_gate_up=w_gate_up, w_down=w_down)
    return _trunk(p, tm=512)
```

```python
import functools
import math

import jax
import jax.numpy as jnp
import numpy as np
from jax import lax
from jax.experimental import pallas as pl
from jax.experimental.pallas import tpu as pltpu

F32 = jnp.float32
BF16 = jnp.bfloat16

N_NSA_HEADS = 8
N_KV_HEADS = 2
GQA_GROUP = N_NSA_HEADS // N_KV_HEADS
HEAD_DIM = 64
NSA_WIDTH = N_NSA_HEADS * HEAD_DIM
KV_WIDTH = N_KV_HEADS * HEAD_DIM
N_BRANCHES = 3
N_GATES = N_BRANCHES * N_NSA_HEADS
CMP_BLOCK = 32
CMP_STRIDE = 16
SLC_BLOCK = 64
SLC_TOPK = 16
N_LOCAL_BLOCKS = 2
WINDOW = 512
Q_BLOCK = 128
N_GMLP_GROUPS = 8
GMLP_CHUNK = 128
N_BUCKETS = 32
REL_MAX_DISTANCE = 128
RMS_EPS = 1e-6
LN_EPS = 1e-5

ROWS = GQA_GROUP * Q_BLOCK
KEY_CHUNK = 128
CMP_FRONT_PAD = 24
CMP_WINDOW = 32
NEG = -1e30
LANE = 128
GATE_PAD = 128
MXU_TILE = 256
BF16_ROWS = 16
VMEM_LIMIT = 48 * 1024 * 1024
LOG2E = math.log2(math.e)
Q_SCALE = HEAD_DIM ** -0.5 * LOG2E


def _dot(a, b):
    return jnp.dot(a, b, preferred_element_type=F32)


def _gelu(x):
    c = math.sqrt(2.0 / math.pi)
    return 0.5 * x * (1.0 + jnp.tanh(c * (x + 0.044715 * (x * x * x))))


def _sigmoid(x):
    return 1.0 / (1.0 + jnp.exp(-x))


def _rms(x, g):
    ms = jnp.mean(x * x, axis=-1, keepdims=True)
    return (x * lax.rsqrt(ms + RMS_EPS)) * g


def _resident(a):
    return pl.BlockSpec(a.shape, lambda i: (0,) * a.ndim, pipeline_mode=pl.Buffered(1))


def _params(n_axes):
    return pltpu.CompilerParams(dimension_semantics=("arbitrary",) * n_axes,
                                vmem_limit_bytes=VMEM_LIMIT)


def _dot_nt(a, b):
    return lax.dot_general(a, b, (((1,), (1,)), ((), ())), preferred_element_type=F32)


def _dot_tn(a, b):
    return lax.dot_general(a, b, (((0,), (0,)), ((), ())), preferred_element_type=F32)


def _inproj_body(x_ref, gpre_ref, wqt_ref, wnat_ref, wvt_ref, wgt_ref, wu_ref, wv_ref, lng_ref, lnb_ref,
                 ws_ref, bs_ref, aug_ref, qt_ref, cv_ref, ksa_ref, kw_ref, vst_ref, vwt_ref, gate_ref, gm_ref):
    tm = x_ref.shape[0]
    xb = _rms(x_ref[...], gpre_ref[...]).astype(BF16)
    qt_ref[...] = (_dot_nt(wqt_ref[...], xb) * Q_SCALE).astype(BF16)
    nat = _dot(xb, wnat_ref[...])
    cv_ref[...] = nat[:, :2 * KV_WIDTH]
    k_slc = nat[:, 2 * KV_WIDTH:3 * KV_WIDTH].astype(BF16)
    ksa_ref[...] = jnp.concatenate(
        [piece for g in range(N_KV_HEADS) for piece in (k_slc[:, g * HEAD_DIM:(g + 1) * HEAD_DIM], aug_ref[...])], axis=1)
    kw_ref[...] = nat[:, 3 * KV_WIDTH:].astype(BF16)
    vt = _dot_nt(wvt_ref[...], xb).astype(BF16)
    denom = jnp.concatenate([jnp.ones((1, tm), BF16), jnp.zeros((BF16_ROWS - 1, tm), BF16)], axis=0)
    vst_ref[...] = jnp.concatenate(
        [piece for g in range(N_KV_HEADS) for piece in (vt[g * HEAD_DIM:(g + 1) * HEAD_DIM], denom)], axis=0)
    vwt_ref[...] = vt[KV_WIDTH:]
    gate_ref[...] = _sigmoid(_dot_nt(wgt_ref[...], xb))
    zu = _gelu(_dot(xb, wu_ref[...]))
    zv = _gelu(_dot(xb, wv_ref[...]))
    mu = jnp.mean(zv, axis=-1, keepdims=True)
    zc = zv - mu
    var = jnp.mean(zc * zc, axis=-1, keepdims=True)
    zv = ((zc * lax.rsqrt(var + LN_EPS)) * lng_ref[...] + lnb_ref[...]).astype(BF16)
    gdim = zv.shape[1] // N_GMLP_GROUPS
    left = lax.broadcasted_iota(jnp.int32, (GMLP_CHUNK, LANE), 1) < gdim
    for c in range(tm // GMLP_CHUNK):
        rows = slice(c * GMLP_CHUNK, (c + 1) * GMLP_CHUNK)
        for j in range(zv.shape[1] // LANE):
            cols = slice(j * LANE, (j + 1) * LANE)
            z = zv[rows, cols]
            sv = jnp.where(left, _dot(ws_ref[2 * j], z), _dot(ws_ref[2 * j + 1], z)) + bs_ref[:, cols]
            gm_ref[rows, cols] = (zu[rows, cols] * sv).astype(BF16)


VT_ROWS = HEAD_DIM + BF16_ROWS
GATE_ROWS = 16


def _inproj(x, b, weights, tm):
    n, d = x.shape
    s = n // b
    per_b = s // tm
    row = lambda w: pl.BlockSpec((tm, w), lambda i: (i, 0))
    col = lambda h: pl.BlockSpec((None, h, tm), lambda i: (i // per_b, 0, i % per_b))
    gw = weights[5].shape[1]
    return pl.pallas_call(
        _inproj_body,
        grid=(n // tm,),
        in_specs=[row(d)] + [_resident(a) for a in weights],
        out_specs=[col(NSA_WIDTH), row(2 * KV_WIDTH), row(K_AUG_WIDTH), row(KV_WIDTH),
                   col(N_KV_HEADS * VT_ROWS), col(KV_WIDTH), col(N_KV_HEADS * GATE_ROWS), row(gw)],
        out_shape=[jax.ShapeDtypeStruct((b, NSA_WIDTH, s), BF16),
                   jax.ShapeDtypeStruct((n, 2 * KV_WIDTH), F32),
                   jax.ShapeDtypeStruct((n, K_AUG_WIDTH), BF16),
                   jax.ShapeDtypeStruct((n, KV_WIDTH), BF16),
                   jax.ShapeDtypeStruct((b, N_KV_HEADS * VT_ROWS, s), BF16),
                   jax.ShapeDtypeStruct((b, KV_WIDTH, s), BF16),
                   jax.ShapeDtypeStruct((b, N_KV_HEADS * GATE_ROWS, s), F32),
                   jax.ShapeDtypeStruct((n, gw), BF16)],
        compiler_params=_params(1),
        name="inproj_gmlp",
    )(x, *weights)


def _compress_body(x_ref, pos_ref, w1_ref, w2_ref, o_ref):
    x = x_ref[...]
    half = x.shape[1]
    a = _dot((x + pos_ref[0:1, :]).astype(BF16), w1_ref[:half, :])
    b = _dot((x + pos_ref[1:2, :]).astype(BF16), w1_ref[half:, :])
    pre = a + pltpu.roll(b, x.shape[0] - 1, 0)
    o_ref[...] = _dot(_gelu(pre).astype(BF16), w2_ref[...])


def _compress(xc, pos, w1, w2):
    b, two, g, nch, width = xc.shape
    hid = w1.shape[2]
    dh = w2.shape[2]
    return pl.pallas_call(
        _compress_body,
        grid=(b, two, g),
        in_specs=[pl.BlockSpec((None, None, None, nch, width), lambda i, t, j: (i, t, j, 0, 0)),
                  pl.BlockSpec((None, 2, width), lambda i, t, j: (t, 0, 0)),
                  pl.BlockSpec((None, 2 * width, hid), lambda i, t, j: (t, 0, 0)),
                  pl.BlockSpec((None, hid, dh), lambda i, t, j: (t, 0, 0))],
        out_specs=pl.BlockSpec((None, None, None, nch, dh), lambda i, t, j: (i, t, j, 0, 0)),
        out_shape=jax.ShapeDtypeStruct((b, two, g, nch, dh), F32),
        compiler_params=_params(3),
        name="compress",
    )(xc, pos, w1, w2)


CMP_CHUNK = 256
CMP_AUG_WIDTH = 2 * HEAD_DIM


def _heads_to_lanes(blk):
    n = blk.shape[1] // Q_BLOCK
    return jnp.concatenate([blk[r * HEAD_DIM:(r + 1) * HEAD_DIM, h * Q_BLOCK:(h + 1) * Q_BLOCK]
                            for h in range(n) for r in range(GQA_GROUP)], axis=1)


def _group_slots(qt, g):
    return jnp.concatenate([jnp.where(g == j, qt, jnp.zeros_like(qt)) for j in range(N_KV_HEADS)], axis=0)


def _q_spec(n_blocks):
    return pl.BlockSpec((None, GQA_GROUP * HEAD_DIM, n_blocks * Q_BLOCK), lambda i, j, q: (i, j, q))


CMP_QBLOCKS = 2


def _cmp_select_body(q_ref, k_ref, vt_ref, crow_ref, cb_ref, oc_ref, sel_ref, sc_ref, *, n_other):
    qis = [pl.program_id(2) * CMP_QBLOCKS + h for h in range(CMP_QBLOCKS)]
    n_slc = sel_ref.shape[1]
    qt = _heads_to_lanes(q_ref[...])
    dh, lanes = qt.shape
    w0s = [pl.multiple_of(qi * (Q_BLOCK // CMP_STRIDE), 8) for qi in qis]
    n_chunks = (w0s[-1] + CMP_WINDOW + CMP_CHUNK - 1) // CMP_CHUNK
    row_iota = lax.broadcasted_iota(jnp.int32, (CMP_CHUNK, lanes), 0)
    lane_blk = lax.broadcasted_iota(jnp.int32, (1, lanes), 1) // ROWS
    w0_lane = w0s[0]
    for h in range(1, CMP_QBLOCKS):
        w0_lane = jnp.where(lane_blk >= h, w0s[h], w0_lane)

    @pl.when(qis[0] == 0)
    def _():
        sc_ref[...] = jnp.zeros_like(sc_ref)

    c = jnp.concatenate([crow_ref[...]] * CMP_QBLOCKS, axis=1)
    c_hi = c.astype(BF16).astype(F32)
    neg_row = jnp.full((1, lanes), NEG, F32)
    zeros = lambda n, dt: jnp.zeros((n, lanes), dt)
    tail = zeros(k_ref.shape[1] - dh - BF16_ROWS, BF16)
    qa = jnp.concatenate([qt, jnp.concatenate([c_hi, c - c_hi, neg_row, zeros(BF16_ROWS - 3, F32)]).astype(BF16), tail])
    qw = jnp.concatenate([qt, jnp.concatenate([zeros(2, F32), neg_row, zeros(BF16_ROWS - 3, F32)]).astype(BF16), tail])

    def rows_of(ch):
        return pl.ds(pl.multiple_of(ch * CMP_CHUNK, CMP_CHUNK), CMP_CHUNK)

    n_slabs = lanes // Q_BLOCK
    slab = lambda i: slice(i * Q_BLOCK, (i + 1) * Q_BLOCK)

    def put(rows, val, first=0):
        for i in range(val.shape[1] // Q_BLOCK):
            sc_ref[first + i, rows, :] = val[:, slab(i)]

    def get(rows):
        return jnp.concatenate([sc_ref[i, rows, :] for i in range(n_slabs)], axis=1)

    def logits(ch, m):
        s = _dot(k_ref[rows_of(ch), :].astype(BF16), qa)
        s = jnp.where(row_iota + ch * CMP_CHUNK < w0_lane, s, NEG)
        put(rows_of(ch), s)
        return jnp.maximum(m, jnp.max(s, axis=0, keepdims=True))

    m = lax.fori_loop(0, n_chunks, logits, jnp.full((1, lanes), NEG, F32))
    win_max = []
    for h, w0 in enumerate(w0s):
        win = pl.ds(w0, CMP_WINDOW)
        s_win = _dot(k_ref[win, :].astype(BF16), qw[:, h * ROWS:(h + 1) * ROWS]) + cb_ref[...]
        put(win, s_win, first=h * GQA_GROUP)
        win_max.append(jnp.max(s_win, axis=0, keepdims=True))
    m = jnp.maximum(m, jnp.concatenate(win_max, axis=1))

    def weigh(ch, carry):
        l, acc = carry
        p = jnp.exp2(get(rows_of(ch)) - m)
        put(rows_of(ch), p)
        return l + jnp.sum(p, axis=0, keepdims=True), acc + _dot(vt_ref[:, rows_of(ch)], p.astype(BF16))

    l, acc = lax.fori_loop(0, n_chunks, weigh, (jnp.zeros((1, lanes), F32), jnp.zeros(oc_ref.shape, F32)))
    scale = jnp.where(m > 0.5 * NEG, 1.0 / jnp.maximum(l, 1e-30), 0.0)
    oc_ref[...] = acc * scale

    per = SLC_BLOCK // CMP_STRIDE
    blk = lax.broadcasted_iota(jnp.int32, (n_slc, Q_BLOCK), 0)
    blk_f = blk.astype(F32)
    lane_pos = lax.broadcasted_iota(jnp.int32, (n_slc, Q_BLOCK), 1)
    for h, qi in enumerate(qis):
        imp = jnp.zeros((n_slc, Q_BLOCK), F32)
        for r in range(GQA_GROUP):
            i = h * GQA_GROUP + r
            part = lambda off: sc_ref[i, pl.ds(CMP_FRONT_PAD + off, n_slc, stride=per), :]
            tot = part(0)
            for k in range(1, per - 1):
                tot = tot + part(k)
            imp = imp + (tot + 0.5 * (part(per - 1) + part(-1))) * scale[:, slab(i)]

        jq = (qi * Q_BLOCK + lane_pos) // SLC_BLOCK
        valid = blk <= jq
        forced = (blk == 0) | (valid & (blk > jq - N_LOCAL_BLOCKS))
        free = valid & jnp.logical_not(forced)
        work = jnp.where(free, imp, -1.0)
        for _ in range(n_other):
            mx = jnp.max(work, axis=0, keepdims=True)
            first = jnp.min(jnp.where(work == mx, blk_f, float(n_slc)), axis=0, keepdims=True)
            work = jnp.where((blk_f == first) & (mx >= 0.0), -1.0, work)
        sel_ref[h] = jnp.where(forced | (free & (work < 0.0)), 1.0, 0.0)


def _cmp_select(qt, kcp, vct, crow, cb, n_slc, n_other):
    b, g, dh = qt.shape[0], N_KV_HEADS, HEAD_DIM
    nqb = qt.shape[2] // Q_BLOCK
    total = nqb * ROWS
    ncp = kcp.shape[2]
    blk_q = pl.BlockSpec((None, None, dh, CMP_QBLOCKS * ROWS), lambda i, j, q: (i, j, 0, q))
    return pl.pallas_call(
        functools.partial(_cmp_select_body, n_other=n_other),
        grid=(b, g, nqb // CMP_QBLOCKS),
        in_specs=[_q_spec(CMP_QBLOCKS),
                  pl.BlockSpec((None, None, ncp, kcp.shape[3]), lambda i, j, q: (i, j, 0, 0)),
                  pl.BlockSpec((None, None, dh, ncp), lambda i, j, q: (i, j, 0, 0)),
                  pl.BlockSpec((None, 1, ROWS), lambda i, j, q: (j, 0, 0)),
                  pl.BlockSpec((None, CMP_WINDOW, ROWS), lambda i, j, q: (j, 0, 0))],
        out_specs=[blk_q,
                   pl.BlockSpec((None, None, CMP_QBLOCKS, n_slc, Q_BLOCK), lambda i, j, q: (i, j, q, 0, 0))],
        out_shape=[jax.ShapeDtypeStruct((b, g, dh, total), F32),
                   jax.ShapeDtypeStruct((b, g, nqb, n_slc, Q_BLOCK), F32)],
        scratch_shapes=[pltpu.VMEM((CMP_QBLOCKS * GQA_GROUP, ncp, Q_BLOCK), F32)],
        compiler_params=_params(3),
        name="cmp_select",
    )(qt, kcp, vct, crow, cb)


SLC_QBLOCKS = 2
FAR_KEYS = 256
FAR_BLOCKS = FAR_KEYS // SLC_BLOCK
FAR_CHUNKS = FAR_KEYS // KEY_CHUNK
FAR_BUFFERS = 4
FAR_AHEAD = 2
PEN_BLOCKS = 8
AUG_CONST = 2
AUG_ROWS = 16
K_SLAB = LANE
K_AUG_WIDTH = N_KV_HEADS * K_SLAB


def _key_augmentation(tm):
    assert tm % (PEN_BLOCKS * SLC_BLOCK) == 0
    pos = jnp.arange(tm, dtype=jnp.int32)
    onehot = ((pos[:, None] // SLC_BLOCK) % PEN_BLOCKS == jnp.arange(PEN_BLOCKS, dtype=jnp.int32)[None, :])
    return jnp.concatenate([jnp.ones((tm, AUG_CONST), BF16),
                            jnp.zeros((tm, AUG_ROWS - PEN_BLOCKS - AUG_CONST), BF16),
                            onehot.astype(BF16),
                            jnp.zeros((tm, K_SLAB - HEAD_DIM - AUG_ROWS), BF16)], axis=1)


def _slc_body(q_ref, k_ref, vt_ref, sel_ref, sb_ref, crow_ref, os_ref, sbuf):
    qis = [pl.program_id(2) * SLC_QBLOCKS + h for h in range(SLC_QBLOCKS)]
    qt = _heads_to_lanes(q_ref[...])
    dh, lanes = qt.shape
    per = KEY_CHUNK // SLC_BLOCK
    far_limits = [jnp.maximum(qi - 1, 0) * per for qi in qis]
    n_steps = (jnp.maximum(qis[-1] - 1, 0) + FAR_CHUNKS - 1) // FAR_CHUNKS
    last_step = k_ref.shape[0] // FAR_KEYS - 1
    steps_per_group = PEN_BLOCKS // FAR_BLOCKS

    c = jnp.concatenate([crow_ref[...]] * SLC_QBLOCKS, axis=1)
    c_hi = c.astype(BF16).astype(F32)
    const_rows = jnp.concatenate([c_hi, c - c_hi, jnp.zeros((AUG_ROWS - PEN_BLOCKS - AUG_CONST, lanes), F32)], axis=0)
    pad_rows = jnp.zeros((k_ref.shape[1] - dh - AUG_ROWS, lanes), BF16)
    blk_iota = lax.broadcasted_iota(jnp.int32, (PEN_BLOCKS, Q_BLOCK), 0)

    def far_logits(u):
        ua = jnp.minimum(u, last_step)
        grp0 = pl.multiple_of((ua // steps_per_group) * PEN_BLOCKS, PEN_BLOCKS)
        blk = blk_iota + (u // steps_per_group) * PEN_BLOCKS
        pens = []
        for h in range(SLC_QBLOCKS):
            pen = jnp.where((sel_ref[h, pl.ds(grp0, PEN_BLOCKS), :] > 0.5) & (blk < far_limits[h]), 0.0, NEG)
            pens += [pen] * GQA_GROUP
        qa = jnp.concatenate([qt, jnp.concatenate([const_rows, jnp.concatenate(pens, axis=1)], axis=0).astype(BF16),
                              pad_rows], axis=0)
        return _dot(k_ref[pl.ds(pl.multiple_of(ua * FAR_KEYS, FAR_KEYS), FAR_KEYS), :], qa)

    def stage(slot, u):
        s_new = far_logits(u).astype(BF16)
        sbuf[slot] = s_new
        groups = s_new.reshape(FAR_KEYS // BF16_ROWS, BF16_ROWS, lanes)
        return jnp.max(jnp.max(groups, axis=0).astype(F32), axis=0, keepdims=True)

    def far_update(m, acc, s_ref, mx, key0):
        m_new = jnp.maximum(m, mx)
        p = jnp.exp2(s_ref[...] - m_new.astype(BF16))
        cols = pl.ds(pl.multiple_of(key0, FAR_KEYS), FAR_KEYS)
        return m_new, jnp.exp2(m - m_new) * acc + _dot(vt_ref[:, cols], p)

    def far_round(v, carry):
        m, acc = carry[:2]
        ahead = list(carry[2:])
        for slot in range(FAR_BUFFERS):
            u = FAR_BUFFERS * v + slot
            ahead.append(stage((slot + FAR_AHEAD) % FAR_BUFFERS, u + FAR_AHEAD))
            ua = jnp.minimum(u, last_step)
            m, acc = far_update(m, acc, sbuf.at[slot], ahead.pop(0), ua * FAR_KEYS)
        return (m, acc, *ahead)

    ahead = [stage(u, u) for u in range(FAR_AHEAD)]
    init = (jnp.full((1, lanes), NEG, F32), jnp.zeros((vt_ref.shape[0], lanes), F32), *ahead)
    m, acc = lax.fori_loop(0, (n_steps + FAR_BUFFERS - 1) // FAR_BUFFERS, far_round, init)[:2]

    for h, qi in enumerate(qis):
        cols = slice(h * ROWS, (h + 1) * ROWS)
        qd = jnp.concatenate([qt[:, cols], jnp.zeros((k_ref.shape[1] - dh, ROWS), BF16)], axis=0)
        prev = jnp.maximum(qi - 1, 0)
        tiles, vts = [], []
        for chunk, kind in ((prev, jnp.where(qi >= 1, 1, 3)), (qi, 0)):
            rows = pl.ds(pl.multiple_of(chunk * KEY_CHUNK, KEY_CHUNK), KEY_CHUNK)
            s = _dot(k_ref[rows, :], qd) + sb_ref[kind]
            for j in range(per):
                srow = sel_ref[h, pl.ds(chunk * per + j, 1), :]
                srow = jnp.concatenate([srow] * GQA_GROUP, axis=1)
                tiles.append(jnp.where(srow > 0.5, s[j * SLC_BLOCK:(j + 1) * SLC_BLOCK, :], NEG))
            vts.append(vt_ref[:, rows])
        s = jnp.concatenate(tiles, axis=0)
        m_new = jnp.maximum(m[:, cols], jnp.max(s, axis=0, keepdims=True))
        p = jnp.exp2(s - m_new).astype(BF16)
        acc_h = jnp.exp2(m[:, cols] - m_new) * acc[:, cols] + _dot(jnp.concatenate(vts, axis=1), p)
        os_ref[:, cols] = acc_h[:dh] / jnp.maximum(acc_h[dh:dh + 1], 1e-30)


def _slc(qt, ks, vst, sel, sb, crow):
    b, g, dh = qt.shape[0], N_KV_HEADS, HEAD_DIM
    s_len = ks.shape[1]
    nqb = s_len // Q_BLOCK
    total = nqb * ROWS
    n_slc = sel.shape[3]
    blk_q = pl.BlockSpec((None, None, dh, SLC_QBLOCKS * ROWS), lambda i, j, q: (i, j, 0, q))
    return pl.pallas_call(
        _slc_body,
        grid=(b, g, nqb // SLC_QBLOCKS),
        in_specs=[_q_spec(SLC_QBLOCKS),
                  pl.BlockSpec((None, s_len, K_SLAB), lambda i, j, q: (i, 0, j)),
                  pl.BlockSpec((None, VT_ROWS, s_len), lambda i, j, q: (i, j, 0)),
                  pl.BlockSpec((None, None, SLC_QBLOCKS, n_slc, Q_BLOCK), lambda i, j, q: (i, j, q, 0, 0)),
                  pl.BlockSpec((None, 4, KEY_CHUNK, ROWS), lambda i, j, q: (j, 0, 0, 0)),
                  pl.BlockSpec((None, 1, ROWS), lambda i, j, q: (j, 0, 0))],
        out_specs=blk_q,
        out_shape=jax.ShapeDtypeStruct((b, g, dh, total), F32),
        scratch_shapes=[pltpu.VMEM((FAR_BUFFERS, FAR_KEYS, SLC_QBLOCKS * ROWS), BF16)],
        compiler_params=_params(3),
        name="slc_attention",
    )(qt, ks, vst, sel, sb, crow)


WIN_CHUNKS = WINDOW // KEY_CHUNK + 1


def _win_body(q_ref, k_ref, vt_ref, wb_ref, oc_ref, os_ref, gate_ref, o_ref):
    qi = pl.program_id(2)
    qt = _group_slots(_heads_to_lanes(q_ref[...]), pl.program_id(1))
    tiles, vts = [], []
    for delta in range(WIN_CHUNKS - 1, -1, -1):
        c = jnp.maximum(qi - delta, 0)
        rows = pl.ds(pl.multiple_of(c * KEY_CHUNK, KEY_CHUNK), KEY_CHUNK)
        kind = jnp.where(qi >= delta, delta, WIN_CHUNKS)
        tiles.append(_dot(k_ref[rows, :], qt) + wb_ref[kind])
        vts.append(vt_ref[:, rows])
    s = jnp.concatenate(tiles, axis=0)
    m = jnp.max(s, axis=0, keepdims=True)
    p = jnp.exp2(s - m)
    l = jnp.sum(p, axis=0, keepdims=True)
    o_w = _dot(jnp.concatenate(vts, axis=1), p.astype(BF16)) / jnp.maximum(l, 1e-30)
    gate = lambda br: jnp.concatenate([gate_ref[br * GQA_GROUP + r:br * GQA_GROUP + r + 1, :]
                                       for r in range(GQA_GROUP)], axis=1)
    o = gate(0) * oc_ref[...] + gate(1) * os_ref[...] + gate(2) * o_w
    for r in range(GQA_GROUP):
        o_ref[r * HEAD_DIM:(r + 1) * HEAD_DIM, :] = o[:, r * Q_BLOCK:(r + 1) * Q_BLOCK].astype(o_ref.dtype)


def _win_mix(qt, kw, vwt, wb, oc, osl, gates):
    b, g, dh = qt.shape[0], N_KV_HEADS, HEAD_DIM
    s_len = kw.shape[1]
    blk_q = pl.BlockSpec((None, None, dh, ROWS), lambda i, j, q: (i, j, 0, q))
    return pl.pallas_call(
        _win_body,
        grid=(b, g, s_len // Q_BLOCK),
        in_specs=[_q_spec(1),
                  pl.BlockSpec((None, s_len, kw.shape[2]), lambda i, j, q: (i, 0, 0)),
                  pl.BlockSpec((None, dh, s_len), lambda i, j, q: (i, j, 0)),
                  pl.BlockSpec((None, WIN_CHUNKS + 1, KEY_CHUNK, ROWS), lambda i, j, q: (j, 0, 0, 0)),
                  blk_q, blk_q,
                  pl.BlockSpec((None, GATE_ROWS, Q_BLOCK), lambda i, j, q: (i, j, q))],
        out_specs=_q_spec(1),
        out_shape=jax.ShapeDtypeStruct(qt.shape, BF16),
        compiler_params=_params(3),
        name="window_mix",
    )(qt, kw, vwt, wb, oc, osl, gates)


def _outproj_body(a_ref, gm_ref, h_ref, wa_ref, wb_ref, g_ref, o_ref):
    y = _dot_tn(a_ref[...], wa_ref[...]) + _dot(gm_ref[...], wb_ref[...])
    o_ref[...] = h_ref[...] + _rms(y, g_ref[...])


def _outproj(a, gm, h, wa, wb, gpost, tm):
    n, d = h.shape
    per_b = a.shape[2] // tm
    full = _resident
    row = lambda w: pl.BlockSpec((tm, w), lambda i: (i, 0))
    return pl.pallas_call(
        _outproj_body,
        grid=(n // tm,),
        in_specs=[pl.BlockSpec((None, a.shape[1], tm), lambda i: (i // per_b, 0, i % per_b)),
                  row(gm.shape[1]), row(d), full(wa), full(wb), full(gpost)],
        out_specs=row(d),
        out_shape=jax.ShapeDtypeStruct((n, d), F32),
        compiler_params=_params(1),
        name="outproj",
    )(a, gm, h, wa, wb, gpost)


FFN_TILE = 256


def _ffn_body(h_ref, gpre_ref, wg_ref, wu_ref, wd_ref, gpost_ref, o_ref):
    h = h_ref[...]
    xb = _rms(h, gpre_ref[...]).astype(BF16)
    acc = jnp.zeros(h.shape, F32)
    for j in range(wg_ref.shape[1] // FFN_TILE):
        cols = slice(j * FFN_TILE, (j + 1) * FFN_TILE)
        gate = _dot(xb, wg_ref[:, cols])
        up = _dot(xb, wu_ref[:, cols])
        act = (gate * _sigmoid(gate) * up).astype(BF16)
        acc = acc + _dot(act, wd_ref[cols, :])
    o_ref[...] = h + _rms(acc, gpost_ref[...])


def _ffn(h, gpre, wg, wu, wd, gpost, tm):
    n, d = h.shape
    full = _resident
    row = pl.BlockSpec((tm, d), lambda i: (i, 0))
    return pl.pallas_call(
        _ffn_body,
        grid=(n // tm,),
        in_specs=[row, full(gpre), full(wg), full(wu), full(wd), full(gpost)],
        out_specs=row,
        out_shape=jax.ShapeDtypeStruct((n, d), F32),
        compiler_params=_params(1),
        name="ffn",
    )(h, gpre, wg, wu, wd, gpost)


def _t5_bucket(dist):
    n = jnp.maximum(dist, 0)
    max_exact = N_BUCKETS // 2
    nf = jnp.maximum(n, max_exact).astype(F32)
    large = max_exact + (jnp.log(nf / max_exact) / math.log(REL_MAX_DISTANCE / max_exact)
                         * (N_BUCKETS - max_exact)).astype(jnp.int32)
    return jnp.where(n < max_exact, n, jnp.minimum(large, N_BUCKETS - 1))


def _bias_tile(table, dist, mask):
    onehot = (_t5_bucket(dist)[..., None] == jnp.arange(N_BUCKETS, dtype=jnp.int32)).astype(F32)
    b = jnp.einsum("kqn,nh->kqh", onehot, table.astype(F32), precision=lax.Precision.HIGHEST)
    b = jnp.where(mask[..., None], b * LOG2E, NEG)
    k = dist.shape[0]
    return b.reshape(k, Q_BLOCK, N_KV_HEADS, GQA_GROUP).transpose(2, 0, 3, 1).reshape(N_KV_HEADS, k, ROWS)


def _bias_tiles(rel_bias):
    ql = jnp.arange(Q_BLOCK, dtype=jnp.int32)[None, :]
    kl = jnp.arange(KEY_CHUNK, dtype=jnp.int32)[:, None]
    chunk_dist = lambda delta: delta * KEY_CHUNK + ql - kl
    everything = jnp.ones((KEY_CHUNK, Q_BLOCK), bool)
    nothing = jnp.zeros((KEY_CHUNK, Q_BLOCK), bool)
    far = jnp.full((KEY_CHUNK, Q_BLOCK), REL_MAX_DISTANCE, jnp.int32)
    d0 = chunk_dist(0)
    sb = jnp.stack([_bias_tile(rel_bias, d0, d0 >= 0),
                    _bias_tile(rel_bias, chunk_dist(1), everything),
                    _bias_tile(rel_bias, far, everything),
                    _bias_tile(rel_bias, far, nothing)], axis=1)
    wtiles = []
    for delta in range(WIN_CHUNKS):
        d = chunk_dist(delta)
        wtiles.append(_bias_tile(rel_bias, d, (d >= 0) & (d < WINDOW)))
    wtiles.append(_bias_tile(rel_bias, far, nothing))
    wb = jnp.stack(wtiles, axis=1)
    rel = jnp.arange(CMP_WINDOW, dtype=jnp.int32)[:, None] - CMP_FRONT_PAD
    dc = ql - CMP_STRIDE * rel - (CMP_BLOCK - 1)
    cb = _bias_tile(rel_bias, dc, dc >= 0)
    crow = _bias_tile(rel_bias, far[:1], everything[:1])
    return sb, wb, cb, crow


def _mixer(h, b, s, lw, tiles, tm):
    sb, wb, cb, crow = tiles
    qt, cv, ksa, kw, vst, vwt, gates, gm = _inproj(
        h, b, (lw["gpre"], lw["wqt"], lw["wnat"], lw["wvt"], lw["wgt"], lw["wu"], lw["wv"],
               lw["lng"], lw["lnb"], lw["ws"], lw["bs"], _key_augmentation(tm)), tm)
    g, dh = N_KV_HEADS, HEAD_DIM
    n_chunks = s // CMP_STRIDE
    xc = cv.reshape(b, n_chunks, CMP_STRIDE, 2, g, dh).transpose(0, 3, 4, 1, 2, 5)
    xc = xc.reshape(b, 2, g, n_chunks, CMP_STRIDE * dh)
    comp = _compress(xc, lw["cpos"], lw["cw1"], lw["cw2"])
    ncp = -(-(n_chunks + CMP_WINDOW) // CMP_CHUNK) * CMP_CHUNK
    back = ncp - CMP_FRONT_PAD - n_chunks
    comp = jnp.pad(comp, ((0, 0), (0, 0), (0, 0), (CMP_FRONT_PAD, back), (0, 0)))
    front = (jnp.arange(ncp) < CMP_FRONT_PAD).astype(F32)[:, None]
    aug = jnp.concatenate([jnp.ones((ncp, 2), F32), front, jnp.zeros((ncp, CMP_AUG_WIDTH - dh - 3), F32)], axis=1)
    kcp = jnp.concatenate([comp[:, 0], jnp.broadcast_to(aug, (b, g) + aug.shape)], axis=-1)
    vct = comp[:, 1].transpose(0, 1, 3, 2).astype(BF16)

    n_slc = s // SLC_BLOCK
    n_other = min(SLC_TOPK, n_slc) - (N_LOCAL_BLOCKS + 1)
    oc, sel = _cmp_select(qt, kcp, vct, crow, cb, n_slc, n_other)
    osl = _slc(qt, ksa.reshape(b, s, ksa.shape[1]), vst, sel, sb, crow)
    ot = _win_mix(qt, kw.reshape(b, s, kw.shape[1]), vwt, wb, oc, osl, gates)
    return _outproj(ot, gm, h, lw["wo_a"], lw["wo_b"], lw["gpost"], tm)


def _layer_weights(l, p):
    d = p["w_in"].shape[1]
    w_in = p["w_in"][l]
    o = NSA_WIDTH
    wqt = w_in[:, :o].T
    k_c, v_c, k_s, v_s, k_w, v_w = [w_in[:, o + i * KV_WIDTH:o + (i + 1) * KV_WIDTH] for i in range(6)]
    wnat = jnp.concatenate([k_c, v_c, k_s, k_w], axis=1)
    wvt = jnp.concatenate([v_s, v_w], axis=1).T
    o += 6 * KV_WIDTH
    wg = w_in[:, o:o + N_GATES].reshape(d, N_KV_HEADS, GQA_GROUP, N_BRANCHES).transpose(1, 3, 2, 0)
    wg = wg.reshape(N_KV_HEADS, N_BRANCHES * GQA_GROUP, d)
    wgt = jnp.pad(wg, ((0, 0), (0, GATE_ROWS - N_BRANCHES * GQA_GROUP), (0, 0))).reshape(N_KV_HEADS * GATE_ROWS, d)
    o += N_GATES
    gw = (w_in.shape[1] - o) // 2
    wu, wv = w_in[:, o:o + gw], w_in[:, o + gw:]
    causal = jnp.tril(jnp.ones((GMLP_CHUNK, GMLP_CHUNK), bool))
    ws = jnp.where(causal, p["gmlp_w_s"][l], 0.0)
    bs = jnp.repeat(p["gmlp_b_s"][l].T, gw // N_GMLP_GROUPS, axis=1)
    half = CMP_STRIDE * HEAD_DIM
    cpos = jnp.stack([p["cmp_pos_k"][l].reshape(2, half), p["cmp_pos_v"][l].reshape(2, half)])
    dff = p["w_down"].shape[1]
    row = lambda v: v[l].reshape(1, -1)
    bf = lambda w: w.astype(BF16)
    return dict(gpre=row(p["norm_mix_pre"]), wqt=bf(wqt), wnat=bf(wnat), wvt=bf(wvt), wgt=bf(wgt), wu=bf(wu), wv=bf(wv),
                lng=row(p["gmlp_ln_g"]), lnb=row(p["gmlp_ln_b"]), ws=bf(ws), bs=bs,
                cpos=cpos, cw1=bf(jnp.stack([p["cmp_w1_k"][l], p["cmp_w1_v"][l]])),
                cw2=bf(jnp.stack([p["cmp_w2_k"][l], p["cmp_w2_v"][l]])),
                wo_a=bf(p["w_out"][l][:NSA_WIDTH]), wo_b=bf(p["w_out"][l][NSA_WIDTH:]),
                gpost=row(p["norm_mix_post"]), fpre=row(p["norm_ffn_pre"]), fpost=row(p["norm_ffn_post"]),
                fwg=bf(p["w_gate_up"][l][:, :dff]), fwu=bf(p["w_gate_up"][l][:, dff:]), fwd=bf(p["w_down"][l]))


def _trunk(p, tm):
    x = p["x"]
    b, s, d = x.shape
    h = x.reshape(b * s, d)
    tiles = _bias_tiles(p["rel_bias"])
    for l in range(p["w_in"].shape[0]):
        lw = _layer_weights(l, p)
        h = _mixer(h, b, s, lw, tiles, tm)
        h = _ffn(h, lw["fpre"], lw["fwg"], lw["fwu"], lw["fwd"], lw["fpost"], tm)
    return h.reshape(b, s, d)


def kernel(x, rel_bias, norm_mix_pre, norm_mix_post, norm_ffn_pre, norm_ffn_post, w_in, cmp_pos_k, cmp_w1_k, cmp_w2_k, cmp_pos_v, cmp_w1_v, cmp_w2_v, gmlp_ln_g, gmlp_ln_b, gmlp_w_s, gmlp_b_s, w_out, w_gate_up, w_down):
    p = dict(x=x, rel_bias=rel_bias, norm_mix_pre=norm_mix_pre, norm_mix_post=norm_mix_post,
             norm_ffn_pre=norm_ffn_pre, norm_ffn_post=norm_ffn_post, w_in=w_in,
             cmp_pos_k=cmp_pos_k, cmp_w1_k=cmp_w1_k, cmp_w2_k=cmp_w2_k,
             cmp_pos_v=cmp_pos_v, cmp_w1_v=cmp_w1_v, cmp_w2_v=cmp_w2_v,
             gmlp_ln_g=gmlp_ln_g, gmlp_ln_b=gmlp_ln_b, gmlp_w_s=gmlp_w_s, gmlp_b_s=gmlp_b_s,
             w_out=w_out, w_gate_up=w_gate_up, w_down=w_down)
    return _trunk(p, tm=512)
```

```python
import functools
import math

import jax
import jax.numpy as jnp
import numpy as np
from jax import lax
from jax.experimental import pallas as pl
from jax.experimental.pallas import tpu as pltpu

F32 = jnp.float32
BF16 = jnp.bfloat16

N_NSA_HEADS = 8
N_KV_HEADS = 2
GQA_GROUP = N_NSA_HEADS // N_KV_HEADS
HEAD_DIM = 64
NSA_WIDTH = N_NSA_HEADS * HEAD_DIM
KV_WIDTH = N_KV_HEADS * HEAD_DIM
N_BRANCHES = 3
N_GATES = N_BRANCHES * N_NSA_HEADS
CMP_BLOCK = 32
CMP_STRIDE = 16
SLC_BLOCK = 64
SLC_TOPK = 16
N_LOCAL_BLOCKS = 2
WINDOW = 512
Q_BLOCK = 128
N_GMLP_GROUPS = 8
GMLP_CHUNK = 128
N_BUCKETS = 32
REL_MAX_DISTANCE = 128
RMS_EPS = 1e-6
LN_EPS = 1e-5

ROWS = GQA_GROUP * Q_BLOCK
KEY_CHUNK = 128
CMP_FRONT_PAD = 24
CMP_WINDOW = 32
NEG = -1e30
LANE = 128
GATE_PAD = 128
MXU_TILE = 256
BF16_ROWS = 16
VMEM_LIMIT = 48 * 1024 * 1024
LOG2E = math.log2(math.e)
Q_SCALE = HEAD_DIM ** -0.5 * LOG2E


def _dot(a, b):
    return jnp.dot(a, b, preferred_element_type=F32)


def _gelu(x):
    c = math.sqrt(2.0 / math.pi)
    return 0.5 * x * (1.0 + jnp.tanh(c * (x + 0.044715 * (x * x * x))))


def _sigmoid(x):
    return 1.0 / (1.0 + jnp.exp(-x))


def _rms(x, g):
    ms = jnp.mean(x * x, axis=-1, keepdims=True)
    return (x * lax.rsqrt(ms + RMS_EPS)) * g


def _resident(a):
    return pl.BlockSpec(a.shape, lambda i: (0,) * a.ndim, pipeline_mode=pl.Buffered(1))


def _params(n_axes):
    return pltpu.CompilerParams(dimension_semantics=("arbitrary",) * n_axes,
                                vmem_limit_bytes=VMEM_LIMIT)


def _dot_nt(a, b):
    return lax.dot_general(a, b, (((1,), (1,)), ((), ())), preferred_element_type=F32)


def _dot_tn(a, b):
    return lax.dot_general(a, b, (((0,), (0,)), ((), ())), preferred_element_type=F32)


def _inproj_body(x_ref, gpre_ref, wqt_ref, wnat_ref, wvt_ref, wgt_ref, wu_ref, wv_ref, lng_ref, lnb_ref,
                 ws_ref, bs_ref, aug_ref, qt_ref, cv_ref, ksa_ref, kw_ref, vst_ref, vwt_ref, gate_ref, gm_ref, cv_sc):
    tm = x_ref.shape[0]
    xb = _rms(x_ref[...], gpre_ref[...]).astype(BF16)
    qt_ref[...] = (_dot_nt(wqt_ref[...], xb) * Q_SCALE).astype(BF16)
    nat = _dot(xb, wnat_ref[...])
    chunks = tm // CMP_STRIDE
    left = lax.broadcasted_iota(jnp.int32, (chunks, LANE), 1) < HEAD_DIM
    for kv in range(2):
        cv_sc[kv] = nat[:, kv * KV_WIDTH:(kv + 1) * KV_WIDTH]
    for kv in range(2):
        for pair in range(CMP_STRIDE // 2):
            lanes = slice(pair * LANE, (pair + 1) * LANE)
            a = cv_sc[kv, pl.ds(2 * pair, chunks, stride=CMP_STRIDE), :]
            b = cv_sc[kv, pl.ds(2 * pair + 1, chunks, stride=CMP_STRIDE), :]
            cv_ref[kv, 0, :, lanes] = jnp.where(left, a, pltpu.roll(b, HEAD_DIM, 1))
            cv_ref[kv, 1, :, lanes] = jnp.where(left, pltpu.roll(a, HEAD_DIM, 1), b)
    k_slc = nat[:, 2 * KV_WIDTH:3 * KV_WIDTH].astype(BF16)
    ksa_ref[...] = jnp.concatenate(
        [piece for g in range(N_KV_HEADS) for piece in (k_slc[:, g * HEAD_DIM:(g + 1) * HEAD_DIM], aug_ref[...])], axis=1)
    kw_ref[...] = nat[:, 3 * KV_WIDTH:].astype(BF16)
    vt = _dot_nt(wvt_ref[...], xb).astype(BF16)
    denom = jnp.concatenate([jnp.ones((1, tm), BF16), jnp.zeros((BF16_ROWS - 1, tm), BF16)], axis=0)
    vst_ref[...] = jnp.concatenate(
        [piece for g in range(N_KV_HEADS) for piece in (vt[g * HEAD_DIM:(g + 1) * HEAD_DIM], denom)], axis=0)
    vwt_ref[...] = vt[KV_WIDTH:]
    gate_ref[...] = _sigmoid(_dot_nt(wgt_ref[...], xb))
    zu = _gelu(_dot(xb, wu_ref[...]))
    zv = _gelu(_dot(xb, wv_ref[...]))
    mu = jnp.mean(zv, axis=-1, keepdims=True)
    zc = zv - mu
    var = jnp.mean(zc * zc, axis=-1, keepdims=True)
    zv = ((zc * lax.rsqrt(var + LN_EPS)) * lng_ref[...] + lnb_ref[...]).astype(BF16)
    gdim = zv.shape[1] // N_GMLP_GROUPS
    left = lax.broadcasted_iota(jnp.int32, (GMLP_CHUNK, LANE), 1) < gdim
    for c in range(tm // GMLP_CHUNK):
        rows = slice(c * GMLP_CHUNK, (c + 1) * GMLP_CHUNK)
        for j in range(zv.shape[1] // LANE):
            cols = slice(j * LANE, (j + 1) * LANE)
            z = zv[rows, cols]
            sv = jnp.where(left, _dot(ws_ref[2 * j], z), _dot(ws_ref[2 * j + 1], z)) + bs_ref[:, cols]
            gm_ref[rows, cols] = (zu[rows, cols] * sv).astype(BF16)


VT_ROWS = HEAD_DIM + BF16_ROWS
GATE_ROWS = 16


def _inproj(x, b, weights, tm):
    n, d = x.shape
    s = n // b
    per_b = s // tm
    row = lambda w: pl.BlockSpec((tm, w), lambda i: (i, 0))
    col = lambda h: pl.BlockSpec((None, h, tm), lambda i: (i // per_b, 0, i % per_b))
    gw = weights[5].shape[1]
    flat = CMP_STRIDE * HEAD_DIM
    assert KV_WIDTH == LANE and N_KV_HEADS == 2
    return pl.pallas_call(
        _inproj_body,
        grid=(n // tm,),
        in_specs=[row(d)] + [_resident(a) for a in weights],
        out_specs=[col(NSA_WIDTH),
                   pl.BlockSpec((None, 2, N_KV_HEADS, tm // CMP_STRIDE, flat),
                                lambda i: (i // per_b, 0, 0, i % per_b, 0)),
                   row(K_AUG_WIDTH), row(KV_WIDTH),
                   col(N_KV_HEADS * VT_ROWS), col(KV_WIDTH), col(N_KV_HEADS * GATE_ROWS), row(gw)],
        out_shape=[jax.ShapeDtypeStruct((b, NSA_WIDTH, s), BF16),
                   jax.ShapeDtypeStruct((b, 2, N_KV_HEADS, s // CMP_STRIDE, flat), F32),
                   jax.ShapeDtypeStruct((n, K_AUG_WIDTH), BF16),
                   jax.ShapeDtypeStruct((n, KV_WIDTH), BF16),
                   jax.ShapeDtypeStruct((b, N_KV_HEADS * VT_ROWS, s), BF16),
                   jax.ShapeDtypeStruct((b, KV_WIDTH, s), BF16),
                   jax.ShapeDtypeStruct((b, N_KV_HEADS * GATE_ROWS, s), F32),
                   jax.ShapeDtypeStruct((n, gw), BF16)],
        scratch_shapes=[pltpu.VMEM((2, tm, KV_WIDTH), F32)],
        compiler_params=_params(1),
        name="inproj_gmlp",
    )(x, *weights)


def _compress_body(x_ref, pos_ref, w1_ref, w2_ref, o_ref):
    x = x_ref[...]
    half = x.shape[1]
    a = _dot((x + pos_ref[0:1, :]).astype(BF16), w1_ref[:half, :])
    b = _dot((x + pos_ref[1:2, :]).astype(BF16), w1_ref[half:, :])
    pre = a + pltpu.roll(b, x.shape[0] - 1, 0)
    o_ref[...] = _dot(_gelu(pre).astype(BF16), w2_ref[...])


def _compress(xc, pos, w1, w2):
    b, two, g, nch, width = xc.shape
    hid = w1.shape[2]
    dh = w2.shape[2]
    return pl.pallas_call(
        _compress_body,
        grid=(b, two, g),
        in_specs=[pl.BlockSpec((None, None, None, nch, width), lambda i, t, j: (i, t, j, 0, 0)),
                  pl.BlockSpec((None, 2, width), lambda i, t, j: (t, 0, 0)),
                  pl.BlockSpec((None, 2 * width, hid), lambda i, t, j: (t, 0, 0)),
                  pl.BlockSpec((None, hid, dh), lambda i, t, j: (t, 0, 0))],
        out_specs=pl.BlockSpec((None, None, None, nch, dh), lambda i, t, j: (i, t, j, 0, 0)),
        out_shape=jax.ShapeDtypeStruct((b, two, g, nch, dh), F32),
        compiler_params=_params(3),
        name="compress",
    )(xc, pos, w1, w2)


CMP_CHUNK = 256
CMP_AUG_WIDTH = 2 * HEAD_DIM


def _heads_to_lanes(blk):
    n = blk.shape[1] // Q_BLOCK
    return jnp.concatenate([blk[r * HEAD_DIM:(r + 1) * HEAD_DIM, h * Q_BLOCK:(h + 1) * Q_BLOCK]
                            for h in range(n) for r in range(GQA_GROUP)], axis=1)


def _group_slots(qt, g):
    return jnp.concatenate([jnp.where(g == j, qt, jnp.zeros_like(qt)) for j in range(N_KV_HEADS)], axis=0)


def _q_spec(n_blocks):
    return pl.BlockSpec((None, GQA_GROUP * HEAD_DIM, n_blocks * Q_BLOCK), lambda i, j, q: (i, j, q))


CMP_QBLOCKS = 2


def _cmp_select_body(q_ref, k_ref, vt_ref, crow_ref, cb_ref, oc_ref, sel_ref, sc_ref, *, n_other):
    qis = [pl.program_id(2) * CMP_QBLOCKS + h for h in range(CMP_QBLOCKS)]
    n_slc = sel_ref.shape[1]
    qt = _heads_to_lanes(q_ref[...])
    dh, lanes = qt.shape
    w0s = [pl.multiple_of(qi * (Q_BLOCK // CMP_STRIDE), 8) for qi in qis]
    n_chunks = (w0s[-1] + CMP_WINDOW + CMP_CHUNK - 1) // CMP_CHUNK
    row_iota = lax.broadcasted_iota(jnp.int32, (CMP_CHUNK, lanes), 0)
    lane_blk = lax.broadcasted_iota(jnp.int32, (1, lanes), 1) // ROWS
    w0_lane = w0s[0]
    for h in range(1, CMP_QBLOCKS):
        w0_lane = jnp.where(lane_blk >= h, w0s[h], w0_lane)

    @pl.when(qis[0] == 0)
    def _():
        sc_ref[...] = jnp.zeros_like(sc_ref)

    c = jnp.concatenate([crow_ref[...]] * CMP_QBLOCKS, axis=1)
    c_hi = c.astype(BF16).astype(F32)
    neg_row = jnp.full((1, lanes), NEG, F32)
    zeros = lambda n, dt: jnp.zeros((n, lanes), dt)
    tail = zeros(k_ref.shape[1] - dh - BF16_ROWS, BF16)
    qa = jnp.concatenate([qt, jnp.concatenate([c_hi, c - c_hi, neg_row, zeros(BF16_ROWS - 3, F32)]).astype(BF16), tail])
    qw = jnp.concatenate([qt, jnp.concatenate([zeros(2, F32), neg_row, zeros(BF16_ROWS - 3, F32)]).astype(BF16), tail])

    def rows_of(ch):
        return pl.ds(pl.multiple_of(ch * CMP_CHUNK, CMP_CHUNK), CMP_CHUNK)

    n_slabs = lanes // Q_BLOCK
    slab = lambda i: slice(i * Q_BLOCK, (i + 1) * Q_BLOCK)

    def put(rows, val, first=0):
        for i in range(val.shape[1] // Q_BLOCK):
            sc_ref[first + i, rows, :] = val[:, slab(i)]

    def get(rows):
        return jnp.concatenate([sc_ref[i, rows, :] for i in range(n_slabs)], axis=1)

    def logits(ch, m):
        s = _dot(k_ref[rows_of(ch), :].astype(BF16), qa)
        s = jnp.where(row_iota + ch * CMP_CHUNK < w0_lane, s, NEG)
        put(rows_of(ch), s)
        return jnp.maximum(m, jnp.max(s, axis=0, keepdims=True))

    m = lax.fori_loop(0, n_chunks, logits, jnp.full((1, lanes), NEG, F32))
    win_max = []
    for h, w0 in enumerate(w0s):
        win = pl.ds(w0, CMP_WINDOW)
        s_win = _dot(k_ref[win, :].astype(BF16), qw[:, h * ROWS:(h + 1) * ROWS]) + cb_ref[...]
        put(win, s_win, first=h * GQA_GROUP)
        win_max.append(jnp.max(s_win, axis=0, keepdims=True))
    m = jnp.maximum(m, jnp.concatenate(win_max, axis=1))

    def weigh(ch, carry):
        l, acc = carry
        p = jnp.exp2(get(rows_of(ch)) - m)
        put(rows_of(ch), p)
        return l + jnp.sum(p, axis=0, keepdims=True), acc + _dot(vt_ref[:, rows_of(ch)], p.astype(BF16))

    l, acc = lax.fori_loop(0, n_chunks, weigh, (jnp.zeros((1, lanes), F32), jnp.zeros(oc_ref.shape, F32)))
    scale = jnp.where(m > 0.5 * NEG, 1.0 / jnp.maximum(l, 1e-30), 0.0)
    oc_ref[...] = acc * scale

    per = SLC_BLOCK // CMP_STRIDE
    blk = lax.broadcasted_iota(jnp.int32, (n_slc, Q_BLOCK), 0)
    blk_f = blk.astype(F32)
    lane_pos = lax.broadcasted_iota(jnp.int32, (n_slc, Q_BLOCK), 1)
    for h, qi in enumerate(qis):
        imp = jnp.zeros((n_slc, Q_BLOCK), F32)
        for r in range(GQA_GROUP):
            i = h * GQA_GROUP + r
            part = lambda off: sc_ref[i, pl.ds(CMP_FRONT_PAD + off, n_slc, stride=per), :]
            tot = part(0)
            for k in range(1, per - 1):
                tot = tot + part(k)
            imp = imp + (tot + 0.5 * (part(per - 1) + part(-1))) * scale[:, slab(i)]

        jq = (qi * Q_BLOCK + lane_pos) // SLC_BLOCK
        valid = blk <= jq
        forced = (blk == 0) | (valid & (blk > jq - N_LOCAL_BLOCKS))
        free = valid & jnp.logical_not(forced)
        work = jnp.where(free, imp, -1.0)
        for _ in range(n_other):
            mx = jnp.max(work, axis=0, keepdims=True)
            first = jnp.min(jnp.where(work == mx, blk_f, float(n_slc)), axis=0, keepdims=True)
            work = jnp.where((blk_f == first) & (mx >= 0.0), -1.0, work)
        sel_ref[h] = jnp.where(forced | (free & (work < 0.0)), 1.0, 0.0)


def _cmp_select(qt, kcp, vct, crow, cb, n_slc, n_other):
    b, g, dh = qt.shape[0], N_KV_HEADS, HEAD_DIM
    nqb = qt.shape[2] // Q_BLOCK
    total = nqb * ROWS
    ncp = kcp.shape[2]
    blk_q = pl.BlockSpec((None, None, dh, CMP_QBLOCKS * ROWS), lambda i, j, q: (i, j, 0, q))
    return pl.pallas_call(
        functools.partial(_cmp_select_body, n_other=n_other),
        grid=(b, g, nqb // CMP_QBLOCKS),
        in_specs=[_q_spec(CMP_QBLOCKS),
                  pl.BlockSpec((None, None, ncp, kcp.shape[3]), lambda i, j, q: (i, j, 0, 0)),
                  pl.BlockSpec((None, None, dh, ncp), lambda i, j, q: (i, j, 0, 0)),
                  pl.BlockSpec((None, 1, ROWS), lambda i, j, q: (j, 0, 0)),
                  pl.BlockSpec((None, CMP_WINDOW, ROWS), lambda i, j, q: (j, 0, 0))],
        out_specs=[blk_q,
                   pl.BlockSpec((None, None, CMP_QBLOCKS, n_slc, Q_BLOCK), lambda i, j, q: (i, j, q, 0, 0))],
        out_shape=[jax.ShapeDtypeStruct((b, g, dh, total), F32),
                   jax.ShapeDtypeStruct((b, g, nqb, n_slc, Q_BLOCK), F32)],
        scratch_shapes=[pltpu.VMEM((CMP_QBLOCKS * GQA_GROUP, ncp, Q_BLOCK), F32)],
        compiler_params=_params(3),
        name="cmp_select",
    )(qt, kcp, vct, crow, cb)


SLC_QBLOCKS = 2
FAR_KEYS = 256
FAR_BLOCKS = FAR_KEYS // SLC_BLOCK
FAR_CHUNKS = FAR_KEYS // KEY_CHUNK
FAR_BUFFERS = 4
FAR_AHEAD = 2
PEN_BLOCKS = 8
AUG_CONST = 2
AUG_ROWS = 16
K_SLAB = LANE
K_AUG_WIDTH = N_KV_HEADS * K_SLAB


def _key_augmentation(tm):
    assert tm % (PEN_BLOCKS * SLC_BLOCK) == 0
    pos = jnp.arange(tm, dtype=jnp.int32)
    onehot = ((pos[:, None] // SLC_BLOCK) % PEN_BLOCKS == jnp.arange(PEN_BLOCKS, dtype=jnp.int32)[None, :])
    return jnp.concatenate([jnp.ones((tm, AUG_CONST), BF16),
                            jnp.zeros((tm, AUG_ROWS - PEN_BLOCKS - AUG_CONST), BF16),
                            onehot.astype(BF16),
                            jnp.zeros((tm, K_SLAB - HEAD_DIM - AUG_ROWS), BF16)], axis=1)


def _slc_body(q_ref, k_ref, vt_ref, sel_ref, sb_ref, crow_ref, os_ref, sbuf):
    qis = [pl.program_id(2) * SLC_QBLOCKS + h for h in range(SLC_QBLOCKS)]
    qt = _heads_to_lanes(q_ref[...])
    dh, lanes = qt.shape
    per = KEY_CHUNK // SLC_BLOCK
    far_limits = [jnp.maximum(qi - 1, 0) * per for qi in qis]
    n_steps = (jnp.maximum(qis[-1] - 1, 0) + FAR_CHUNKS - 1) // FAR_CHUNKS
    last_step = k_ref.shape[0] // FAR_KEYS - 1
    steps_per_group = PEN_BLOCKS // FAR_BLOCKS

    c = jnp.concatenate([crow_ref[...]] * SLC_QBLOCKS, axis=1)
    c_hi = c.astype(BF16).astype(F32)
    const_rows = jnp.concatenate([c_hi, c - c_hi, jnp.zeros((AUG_ROWS - PEN_BLOCKS - AUG_CONST, lanes), F32)], axis=0)
    pad_rows = jnp.zeros((k_ref.shape[1] - dh - AUG_ROWS, lanes), BF16)
    blk_iota = lax.broadcasted_iota(jnp.int32, (PEN_BLOCKS, Q_BLOCK), 0)

    def far_logits(u):
        ua = jnp.minimum(u, last_step)
        grp0 = pl.multiple_of((ua // steps_per_group) * PEN_BLOCKS, PEN_BLOCKS)
        blk = blk_iota + (u // steps_per_group) * PEN_BLOCKS
        pens = []
        for h in range(SLC_QBLOCKS):
            pen = jnp.where((sel_ref[h, pl.ds(grp0, PEN_BLOCKS), :] > 0.5) & (blk < far_limits[h]), 0.0, NEG)
            pens += [pen] * GQA_GROUP
        qa = jnp.concatenate([qt, jnp.concatenate([const_rows, jnp.concatenate(pens, axis=1)], axis=0).astype(BF16),
                              pad_rows], axis=0)
        return _dot(k_ref[pl.ds(pl.multiple_of(ua * FAR_KEYS, FAR_KEYS), FAR_KEYS), :], qa)

    def stage(slot, u):
        s_new = far_logits(u).astype(BF16)
        sbuf[slot] = s_new
        groups = s_new.reshape(FAR_KEYS // BF16_ROWS, BF16_ROWS, lanes)
        return jnp.max(jnp.max(groups, axis=0).astype(F32), axis=0, keepdims=True)

    def far_update(m, acc, s_ref, mx, key0):
        m_new = jnp.maximum(m, mx)
        p = jnp.exp2(s_ref[...] - m_new.astype(BF16))
        cols = pl.ds(pl.multiple_of(key0, FAR_KEYS), FAR_KEYS)
        return m_new, jnp.exp2(m - m_new) * acc + _dot(vt_ref[:, cols], p)

    def far_round(v, carry):
        m, acc = carry[:2]
        ahead = list(carry[2:])
        for slot in range(FAR_BUFFERS):
            u = FAR_BUFFERS * v + slot
            ahead.append(stage((slot + FAR_AHEAD) % FAR_BUFFERS, u + FAR_AHEAD))
            ua = jnp.minimum(u, last_step)
            m, acc = far_update(m, acc, sbuf.at[slot], ahead.pop(0), ua * FAR_KEYS)
        return (m, acc, *ahead)

    ahead = [stage(u, u) for u in range(FAR_AHEAD)]
    init = (jnp.full((1, lanes), NEG, F32), jnp.zeros((vt_ref.shape[0], lanes), F32), *ahead)
    m, acc = lax.fori_loop(0, (n_steps + FAR_BUFFERS - 1) // FAR_BUFFERS, far_round, init)[:2]

    for h, qi in enumerate(qis):
        cols = slice(h * ROWS, (h + 1) * ROWS)
        qd = jnp.concatenate([qt[:, cols], jnp.zeros((k_ref.shape[1] - dh, ROWS), BF16)], axis=0)
        prev = jnp.maximum(qi - 1, 0)
        tiles, vts = [], []
        for chunk, kind in ((prev, jnp.where(qi >= 1, 1, 3)), (qi, 0)):
            rows = pl.ds(pl.multiple_of(chunk * KEY_CHUNK, KEY_CHUNK), KEY_CHUNK)
            s = _dot(k_ref[rows, :], qd) + sb_ref[kind]
            for j in range(per):
                srow = sel_ref[h, pl.ds(chunk * per + j, 1), :]
                srow = jnp.concatenate([srow] * GQA_GROUP, axis=1)
                tiles.append(jnp.where(srow > 0.5, s[j * SLC_BLOCK:(j + 1) * SLC_BLOCK, :], NEG))
            vts.append(vt_ref[:, rows])
        s = jnp.concatenate(tiles, axis=0)
        m_new = jnp.maximum(m[:, cols], jnp.max(s, axis=0, keepdims=True))
        p = jnp.exp2(s - m_new).astype(BF16)
        acc_h = jnp.exp2(m[:, cols] - m_new) * acc[:, cols] + _dot(jnp.concatenate(vts, axis=1), p)
        os_ref[:, cols] = acc_h[:dh] / jnp.maximum(acc_h[dh:dh + 1], 1e-30)


def _slc(qt, ks, vst, sel, sb, crow):
    b, g, dh = qt.shape[0], N_KV_HEADS, HEAD_DIM
    s_len = ks.shape[1]
    nqb = s_len // Q_BLOCK
    total = nqb * ROWS
    n_slc = sel.shape[3]
    blk_q = pl.BlockSpec((None, None, dh, SLC_QBLOCKS * ROWS), lambda i, j, q: (i, j, 0, q))
    return pl.pallas_call(
        _slc_body,
        grid=(b, g, nqb // SLC_QBLOCKS),
        in_specs=[_q_spec(SLC_QBLOCKS),
                  pl.BlockSpec((None, s_len, K_SLAB), lambda i, j, q: (i, 0, j)),
                  pl.BlockSpec((None, VT_ROWS, s_len), lambda i, j, q: (i, j, 0)),
                  pl.BlockSpec((None, None, SLC_QBLOCKS, n_slc, Q_BLOCK), lambda i, j, q: (i, j, q, 0, 0)),
                  pl.BlockSpec((None, 4, KEY_CHUNK, ROWS), lambda i, j, q: (j, 0, 0, 0)),
                  pl.BlockSpec((None, 1, ROWS), lambda i, j, q: (j, 0, 0))],
        out_specs=blk_q,
        out_shape=jax.ShapeDtypeStruct((b, g, dh, total), F32),
        scratch_shapes=[pltpu.VMEM((FAR_BUFFERS, FAR_KEYS, SLC_QBLOCKS * ROWS), BF16)],
        compiler_params=_params(3),
        name="slc_attention",
    )(qt, ks, vst, sel, sb, crow)


WIN_CHUNKS = WINDOW // KEY_CHUNK + 1
WIN_QBLOCKS = 4


def _win_body(q_ref, k_ref, vt_ref, wb_ref, oc_ref, os_ref, gate_ref, o_ref):
    q_all = _group_slots(_heads_to_lanes(q_ref[...]), pl.program_id(1))
    for h in range(WIN_QBLOCKS):
        qi = pl.program_id(2) * WIN_QBLOCKS + h
        lanes = slice(h * ROWS, (h + 1) * ROWS)
        tokens = slice(h * Q_BLOCK, (h + 1) * Q_BLOCK)
        qt = q_all[:, lanes]
        tiles, vts = [], []
        for delta in range(WIN_CHUNKS - 1, -1, -1):
            c = jnp.maximum(qi - delta, 0)
            rows = pl.ds(pl.multiple_of(c * KEY_CHUNK, KEY_CHUNK), KEY_CHUNK)
            kind = jnp.where(qi >= delta, delta, WIN_CHUNKS)
            tiles.append(_dot(k_ref[rows, :], qt) + wb_ref[kind])
            vts.append(vt_ref[:, rows])
        s = jnp.concatenate(tiles, axis=0)
        m = jnp.max(s, axis=0, keepdims=True)
        p = jnp.exp2(s - m)
        l = jnp.sum(p, axis=0, keepdims=True)
        o_w = _dot(jnp.concatenate(vts, axis=1), p.astype(BF16)) / jnp.maximum(l, 1e-30)
        gate = lambda br: jnp.concatenate([gate_ref[br * GQA_GROUP + r:br * GQA_GROUP + r + 1, tokens]
                                           for r in range(GQA_GROUP)], axis=1)
        o = gate(0) * oc_ref[:, lanes] + gate(1) * os_ref[:, lanes] + gate(2) * o_w
        for r in range(GQA_GROUP):
            o_ref[r * HEAD_DIM:(r + 1) * HEAD_DIM, tokens] = o[:, r * Q_BLOCK:(r + 1) * Q_BLOCK].astype(o_ref.dtype)


def _win_mix(qt, kw, vwt, wb, oc, osl, gates):
    b, g, dh = qt.shape[0], N_KV_HEADS, HEAD_DIM
    s_len = kw.shape[1]
    blk_q = pl.BlockSpec((None, None, dh, WIN_QBLOCKS * ROWS), lambda i, j, q: (i, j, 0, q))
    return pl.pallas_call(
        _win_body,
        grid=(b, g, s_len // (WIN_QBLOCKS * Q_BLOCK)),
        in_specs=[_q_spec(WIN_QBLOCKS),
                  pl.BlockSpec((None, s_len, kw.shape[2]), lambda i, j, q: (i, 0, 0)),
                  pl.BlockSpec((None, dh, s_len), lambda i, j, q: (i, j, 0)),
                  pl.BlockSpec((None, WIN_CHUNKS + 1, KEY_CHUNK, ROWS), lambda i, j, q: (j, 0, 0, 0)),
                  blk_q, blk_q,
                  pl.BlockSpec((None, GATE_ROWS, WIN_QBLOCKS * Q_BLOCK), lambda i, j, q: (i, j, q))],
        out_specs=_q_spec(WIN_QBLOCKS),
        out_shape=jax.ShapeDtypeStruct(qt.shape, BF16),
        compiler_params=_params(3),
        name="window_mix",
    )(qt, kw, vwt, wb, oc, osl, gates)


def _outproj_body(a_ref, gm_ref, h_ref, wa_ref, wb_ref, g_ref, o_ref):
    y = _dot_tn(a_ref[...], wa_ref[...]) + _dot(gm_ref[...], wb_ref[...])
    o_ref[...] = h_ref[...] + _rms(y, g_ref[...])


def _outproj(a, gm, h, wa, wb, gpost, tm):
    n, d = h.shape
    per_b = a.shape[2] // tm
    full = _resident
    row = lambda w: pl.BlockSpec((tm, w), lambda i: (i, 0))
    return pl.pallas_call(
        _outproj_body,
        grid=(n // tm,),
        in_specs=[pl.BlockSpec((None, a.shape[1], tm), lambda i: (i // per_b, 0, i % per_b)),
                  row(gm.shape[1]), row(d), full(wa), full(wb), full(gpost)],
        out_specs=row(d),
        out_shape=jax.ShapeDtypeStruct((n, d), F32),
        compiler_params=_params(1),
        name="outproj",
    )(a, gm, h, wa, wb, gpost)


FFN_TILE = 256


def _ffn_body(h_ref, gpre_ref, wg_ref, wu_ref, wd_ref, gpost_ref, o_ref):
    h = h_ref[...]
    xb = _rms(h, gpre_ref[...]).astype(BF16)
    acc = jnp.zeros(h.shape, F32)
    for j in range(wg_ref.shape[1] // FFN_TILE):
        cols = slice(j * FFN_TILE, (j + 1) * FFN_TILE)
        gate = _dot(xb, wg_ref[:, cols])
        up = _dot(xb, wu_ref[:, cols])
        act = (gate * _sigmoid(gate) * up).astype(BF16)
        acc = acc + _dot(act, wd_ref[cols, :])
    o_ref[...] = h + _rms(acc, gpost_ref[...])


def _ffn(h, gpre, wg, wu, wd, gpost, tm):
    n, d = h.shape
    full = _resident
    row = pl.BlockSpec((tm, d), lambda i: (i, 0))
    return pl.pallas_call(
        _ffn_body,
        grid=(n // tm,),
        in_specs=[row, full(gpre), full(wg), full(wu), full(wd), full(gpost)],
        out_specs=row,
        out_shape=jax.ShapeDtypeStruct((n, d), F32),
        compiler_params=_params(1),
        name="ffn",
    )(h, gpre, wg, wu, wd, gpost)


def _t5_bucket(dist):
    n = jnp.maximum(dist, 0)
    max_exact = N_BUCKETS // 2
    nf = jnp.maximum(n, max_exact).astype(F32)
    large = max_exact + (jnp.log(nf / max_exact) / math.log(REL_MAX_DISTANCE / max_exact)
                         * (N_BUCKETS - max_exact)).astype(jnp.int32)
    return jnp.where(n < max_exact, n, jnp.minimum(large, N_BUCKETS - 1))


def _bias_tile(table, dist, mask):
    onehot = (_t5_bucket(dist)[..., None] == jnp.arange(N_BUCKETS, dtype=jnp.int32)).astype(F32)
    b = jnp.einsum("kqn,nh->kqh", onehot, table.astype(F32), precision=lax.Precision.HIGHEST)
    b = jnp.where(mask[..., None], b * LOG2E, NEG)
    k = dist.shape[0]
    return b.reshape(k, Q_BLOCK, N_KV_HEADS, GQA_GROUP).transpose(2, 0, 3, 1).reshape(N_KV_HEADS, k, ROWS)


def _bias_tiles(rel_bias):
    ql = jnp.arange(Q_BLOCK, dtype=jnp.int32)[None, :]
    kl = jnp.arange(KEY_CHUNK, dtype=jnp.int32)[:, None]
    chunk_dist = lambda delta: delta * KEY_CHUNK + ql - kl
    everything = jnp.ones((KEY_CHUNK, Q_BLOCK), bool)
    nothing = jnp.zeros((KEY_CHUNK, Q_BLOCK), bool)
    far = jnp.full((KEY_CHUNK, Q_BLOCK), REL_MAX_DISTANCE, jnp.int32)
    d0 = chunk_dist(0)
    sb = jnp.stack([_bias_tile(rel_bias, d0, d0 >= 0),
                    _bias_tile(rel_bias, chunk_dist(1), everything),
                    _bias_tile(rel_bias, far, everything),
                    _bias_tile(rel_bias, far, nothing)], axis=1)
    wtiles = []
    for delta in range(WIN_CHUNKS):
        d = chunk_dist(delta)
        wtiles.append(_bias_tile(rel_bias, d, (d >= 0) & (d < WINDOW)))
    wtiles.append(_bias_tile(rel_bias, far, nothing))
    wb = jnp.stack(wtiles, axis=1)
    rel = jnp.arange(CMP_WINDOW, dtype=jnp.int32)[:, None] - CMP_FRONT_PAD
    dc = ql - CMP_STRIDE * rel - (CMP_BLOCK - 1)
    cb = _bias_tile(rel_bias, dc, dc >= 0)
    crow = _bias_tile(rel_bias, far[:1], everything[:1])
    return sb, wb, cb, crow


def _mixer(h, b, s, lw, tiles, tm):
    sb, wb, cb, crow = tiles
    qt, cv, ksa, kw, vst, vwt, gates, gm = _inproj(
        h, b, (lw["gpre"], lw["wqt"], lw["wnat"], lw["wvt"], lw["wgt"], lw["wu"], lw["wv"],
               lw["lng"], lw["lnb"], lw["ws"], lw["bs"], _key_augmentation(tm)), tm)
    g, dh = N_KV_HEADS, HEAD_DIM
    n_chunks = s // CMP_STRIDE
    comp = _compress(cv, lw["cpos"], lw["cw1"], lw["cw2"])
    ncp = -(-(n_chunks + CMP_WINDOW) // CMP_CHUNK) * CMP_CHUNK
    back = ncp - CMP_FRONT_PAD - n_chunks
    comp = jnp.pad(comp, ((0, 0), (0, 0), (0, 0), (CMP_FRONT_PAD, back), (0, 0)))
    front = (jnp.arange(ncp) < CMP_FRONT_PAD).astype(F32)[:, None]
    aug = jnp.concatenate([jnp.ones((ncp, 2), F32), front, jnp.zeros((ncp, CMP_AUG_WIDTH - dh - 3), F32)], axis=1)
    kcp = jnp.concatenate([comp[:, 0], jnp.broadcast_to(aug, (b, g) + aug.shape)], axis=-1)
    vct = comp[:, 1].transpose(0, 1, 3, 2).astype(BF16)

    n_slc = s // SLC_BLOCK
    n_other = min(SLC_TOPK, n_slc) - (N_LOCAL_BLOCKS + 1)
    oc, sel = _cmp_select(qt, kcp, vct, crow, cb, n_slc, n_other)
    osl = _slc(qt, ksa.reshape(b, s, ksa.shape[1]), vst, sel, sb, crow)
    ot = _win_mix(qt, kw.reshape(b, s, kw.shape[1]), vwt, wb, oc, osl, gates)
    return _outproj(ot, gm, h, lw["wo_a"], lw["wo_b"], lw["gpost"], tm)


def _layer_weights(l, p):
    d = p["w_in"].shape[1]
    w_in = p["w_in"][l]
    o = NSA_WIDTH
    wqt = w_in[:, :o].T
    k_c, v_c, k_s, v_s, k_w, v_w = [w_in[:, o + i * KV_WIDTH:o + (i + 1) * KV_WIDTH] for i in range(6)]
    wnat = jnp.concatenate([k_c, v_c, k_s, k_w], axis=1)
    wvt = jnp.concatenate([v_s, v_w], axis=1).T
    o += 6 * KV_WIDTH
    wg = w_in[:, o:o + N_GATES].reshape(d, N_KV_HEADS, GQA_GROUP, N_BRANCHES).transpose(1, 3, 2, 0)
    wg = wg.reshape(N_KV_HEADS, N_BRANCHES * GQA_GROUP, d)
    wgt = jnp.pad(wg, ((0, 0), (0, GATE_ROWS - N_BRANCHES * GQA_GROUP), (0, 0))).reshape(N_KV_HEADS * GATE_ROWS, d)
    o += N_GATES
    gw = (w_in.shape[1] - o) // 2
    wu, wv = w_in[:, o:o + gw], w_in[:, o + gw:]
    causal = jnp.tril(jnp.ones((GMLP_CHUNK, GMLP_CHUNK), bool))
    ws = jnp.where(causal, p["gmlp_w_s"][l], 0.0)
    bs = jnp.repeat(p["gmlp_b_s"][l].T, gw // N_GMLP_GROUPS, axis=1)
    half = CMP_STRIDE * HEAD_DIM
    cpos = jnp.stack([p["cmp_pos_k"][l].reshape(2, half), p["cmp_pos_v"][l].reshape(2, half)])
    dff = p["w_down"].shape[1]
    row = lambda v: v[l].reshape(1, -1)
    bf = lambda w: w.astype(BF16)
    return dict(gpre=row(p["norm_mix_pre"]), wqt=bf(wqt), wnat=bf(wnat), wvt=bf(wvt), wgt=bf(wgt), wu=bf(wu), wv=bf(wv),
                lng=row(p["gmlp_ln_g"]), lnb=row(p["gmlp_ln_b"]), ws=bf(ws), bs=bs,
                cpos=cpos, cw1=bf(jnp.stack([p["cmp_w1_k"][l], p["cmp_w1_v"][l]])),
                cw2=bf(jnp.stack([p["cmp_w2_k"][l], p["cmp_w2_v"][l]])),
                wo_a=bf(p["w_out"][l][:NSA_WIDTH]), wo_b=bf(p["w_out"][l][NSA_WIDTH:]),
                gpost=row(p["norm_mix_post"]), fpre=row(p["norm_ffn_pre"]), fpost=row(p["norm_ffn_post"]),
                fwg=bf(p["w_gate_up"][l][:, :dff]), fwu=bf(p["w_gate_up"][l][:, dff:]), fwd=bf(p["w_down"][l]))


def _trunk(p, tm):
    x = p["x"]
    b, s, d = x.shape
    h = x.reshape(b * s, d)
    tiles = _bias_tiles(p["rel_bias"])
    for l in range(p["w_in"].shape[0]):
        lw = _layer_weights(l, p)
        h = _mixer(h, b, s, lw, tiles, tm)
        h = _ffn(h, lw["fpre"], lw["fwg"], lw["fwu"], lw["fwd"], lw["fpost"], tm)
    return h.reshape(b, s, d)


def kernel(x, rel_bias, norm_mix_pre, norm_mix_post, norm_ffn_pre, norm_ffn_post, w_in, cmp_pos_k, cmp_w1_k, cmp_w2_k, cmp_pos_v, cmp_w1_v, cmp_w2_v, gmlp_ln_g, gmlp_ln_b, gmlp_w_s, gmlp_b_s, w_out, w_gate_up, w_down):
    p = dict(x=x, rel_bias=rel_bias, norm_mix_pre=norm_mix_pre, norm_mix_post=norm_mix_post,
             norm_ffn_pre=norm_ffn_pre, norm_ffn_post=norm_ffn_post, w_in=w_in,
             cmp_pos_k=cmp_pos_k, cmp_w1_k=cmp_w1_k, cmp_w2_k=cmp_w2_k,
             cmp_pos_v=cmp_pos_v, cmp_w1_v=cmp_w1_v, cmp_w2_v=cmp_w2_v,
             gmlp_ln_g=gmlp_ln_g, gmlp_ln_b=gmlp_ln_b, gmlp_w_s=gmlp_w_s, gmlp_b_s=gmlp_b_s,
             w_out=w_out, w_gate_up=w_gate_up, w_down=w_down)
    return _trunk(p, tm=512)
```

```python
import functools
import math

import jax
import jax.numpy as jnp
import numpy as np
from jax import lax
from jax.experimental import pallas as pl
from jax.experimental.pallas import tpu as pltpu

F32 = jnp.float32
BF16 = jnp.bfloat16

N_NSA_HEADS = 8
N_KV_HEADS = 2
GQA_GROUP = N_NSA_HEADS // N_KV_HEADS
HEAD_DIM = 64
NSA_WIDTH = N_NSA_HEADS * HEAD_DIM
KV_WIDTH = N_KV_HEADS * HEAD_DIM
N_BRANCHES = 3
N_GATES = N_BRANCHES * N_NSA_HEADS
CMP_BLOCK = 32
CMP_STRIDE = 16
SLC_BLOCK = 64
SLC_TOPK = 16
N_LOCAL_BLOCKS = 2
WINDOW = 512
Q_BLOCK = 128
N_GMLP_GROUPS = 8
GMLP_CHUNK = 128
N_BUCKETS = 32
REL_MAX_DISTANCE = 128
RMS_EPS = 1e-6
LN_EPS = 1e-5

ROWS = GQA_GROUP * Q_BLOCK
KEY_CHUNK = 128
CMP_FRONT_PAD = 24
CMP_WINDOW = 32
NEG = -1e30
LANE = 128
GATE_PAD = 128
MXU_TILE = 256
BF16_ROWS = 16
VMEM_LIMIT = 48 * 1024 * 1024
LOG2E = math.log2(math.e)
Q_SCALE = HEAD_DIM ** -0.5 * LOG2E


def _dot(a, b):
    return jnp.dot(a, b, preferred_element_type=F32)


def _gelu(x):
    c = math.sqrt(2.0 / math.pi)
    return 0.5 * x * (1.0 + jnp.tanh(c * (x + 0.044715 * (x * x * x))))


def _sigmoid(x):
    return 1.0 / (1.0 + jnp.exp(-x))


def _rms(x, g):
    ms = jnp.mean(x * x, axis=-1, keepdims=True)
    return (x * lax.rsqrt(ms + RMS_EPS)) * g


def _resident(a):
    return pl.BlockSpec(a.shape, lambda i: (0,) * a.ndim, pipeline_mode=pl.Buffered(1))


def _params(n_axes):
    return pltpu.CompilerParams(dimension_semantics=("arbitrary",) * n_axes,
                                vmem_limit_bytes=VMEM_LIMIT)


def _dot_nt(a, b):
    return lax.dot_general(a, b, (((1,), (1,)), ((), ())), preferred_element_type=F32)


def _dot_tn(a, b):
    return lax.dot_general(a, b, (((0,), (0,)), ((), ())), preferred_element_type=F32)


def _inproj_body(x_ref, gpre_ref, wqt_ref, wnat_ref, wvt_ref, wgt_ref, wu_ref, wv_ref, lng_ref, lnb_ref,
                 ws_ref, bs_ref, aug_ref, qt_ref, cv_ref, ksa_ref, kw_ref, vst_ref, vwt_ref, gate_ref, gm_ref, cv_sc):
    tm = x_ref.shape[0]
    xb = _rms(x_ref[...], gpre_ref[...]).astype(BF16)
    qt_ref[...] = (_dot_nt(wqt_ref[...], xb) * Q_SCALE).astype(BF16)
    nat = _dot(xb, wnat_ref[...])
    chunks = tm // CMP_STRIDE
    left = lax.broadcasted_iota(jnp.int32, (chunks, LANE), 1) < HEAD_DIM
    for kv in range(2):
        cv_sc[kv] = nat[:, kv * KV_WIDTH:(kv + 1) * KV_WIDTH]
    for kv in range(2):
        for pair in range(CMP_STRIDE // 2):
            lanes = slice(pair * LANE, (pair + 1) * LANE)
            a = cv_sc[kv, pl.ds(2 * pair, chunks, stride=CMP_STRIDE), :]
            b = cv_sc[kv, pl.ds(2 * pair + 1, chunks, stride=CMP_STRIDE), :]
            cv_ref[kv, 0, :, lanes] = jnp.where(left, a, pltpu.roll(b, HEAD_DIM, 1))
            cv_ref[kv, 1, :, lanes] = jnp.where(left, pltpu.roll(a, HEAD_DIM, 1), b)
    k_slc = nat[:, 2 * KV_WIDTH:3 * KV_WIDTH].astype(BF16)
    ksa_ref[...] = jnp.concatenate(
        [piece for g in range(N_KV_HEADS) for piece in (k_slc[:, g * HEAD_DIM:(g + 1) * HEAD_DIM], aug_ref[...])], axis=1)
    kw_ref[...] = nat[:, 3 * KV_WIDTH:].astype(BF16)
    vt = _dot_nt(wvt_ref[...], xb).astype(BF16)
    denom = jnp.concatenate([jnp.ones((1, tm), BF16), jnp.zeros((BF16_ROWS - 1, tm), BF16)], axis=0)
    vst_ref[...] = jnp.concatenate(
        [piece for g in range(N_KV_HEADS) for piece in (vt[g * HEAD_DIM:(g + 1) * HEAD_DIM], denom)], axis=0)
    vwt_ref[...] = vt[KV_WIDTH:]
    gate_ref[...] = _sigmoid(_dot_nt(wgt_ref[...], xb))
    zu = _gelu(_dot(xb, wu_ref[...]))
    zv = _gelu(_dot(xb, wv_ref[...]))
    mu = jnp.mean(zv, axis=-1, keepdims=True)
    zc = zv - mu
    var = jnp.mean(zc * zc, axis=-1, keepdims=True)
    zv = ((zc * lax.rsqrt(var + LN_EPS)) * lng_ref[...] + lnb_ref[...]).astype(BF16)
    gdim = zv.shape[1] // N_GMLP_GROUPS
    left = lax.broadcasted_iota(jnp.int32, (GMLP_CHUNK, LANE), 1) < gdim
    for c in range(tm // GMLP_CHUNK):
        rows = slice(c * GMLP_CHUNK, (c + 1) * GMLP_CHUNK)
        for j in range(zv.shape[1] // LANE):
            cols = slice(j * LANE, (j + 1) * LANE)
            z = zv[rows, cols]
            sv = jnp.where(left, _dot(ws_ref[2 * j], z), _dot(ws_ref[2 * j + 1], z)) + bs_ref[:, cols]
            gm_ref[rows, cols] = (zu[rows, cols] * sv).astype(BF16)


VT_ROWS = HEAD_DIM + BF16_ROWS
GATE_ROWS = 16


def _inproj(x, b, weights, tm):
    n, d = x.shape
    s = n // b
    per_b = s // tm
    row = lambda w: pl.BlockSpec((tm, w), lambda i: (i, 0))
    col = lambda h: pl.BlockSpec((None, h, tm), lambda i: (i // per_b, 0, i % per_b))
    gw = weights[5].shape[1]
    flat = CMP_STRIDE * HEAD_DIM
    assert KV_WIDTH == LANE and N_KV_HEADS == 2
    return pl.pallas_call(
        _inproj_body,
        grid=(n // tm,),
        in_specs=[row(d)] + [_resident(a) for a in weights],
        out_specs=[col(NSA_WIDTH),
                   pl.BlockSpec((None, 2, N_KV_HEADS, tm // CMP_STRIDE, flat),
                                lambda i: (i // per_b, 0, 0, i % per_b, 0)),
                   row(K_AUG_WIDTH), row(KV_WIDTH),
                   col(N_KV_HEADS * VT_ROWS), col(KV_WIDTH), col(N_KV_HEADS * GATE_ROWS), row(gw)],
        out_shape=[jax.ShapeDtypeStruct((b, NSA_WIDTH, s), BF16),
                   jax.ShapeDtypeStruct((b, 2, N_KV_HEADS, s // CMP_STRIDE, flat), F32),
                   jax.ShapeDtypeStruct((n, K_AUG_WIDTH), BF16),
                   jax.ShapeDtypeStruct((n, KV_WIDTH), BF16),
                   jax.ShapeDtypeStruct((b, N_KV_HEADS * VT_ROWS, s), BF16),
                   jax.ShapeDtypeStruct((b, KV_WIDTH, s), BF16),
                   jax.ShapeDtypeStruct((b, N_KV_HEADS * GATE_ROWS, s), F32),
                   jax.ShapeDtypeStruct((n, gw), BF16)],
        scratch_shapes=[pltpu.VMEM((2, tm, KV_WIDTH), F32)],
        compiler_params=_params(1),
        name="inproj_gmlp",
    )(x, *weights)


def _compress_body(x_ref, pos_ref, w1_ref, w2_ref, o_ref):
    x = x_ref[...]
    half = x.shape[1]
    a = _dot((x + pos_ref[0:1, :]).astype(BF16), w1_ref[:half, :])
    b = _dot((x + pos_ref[1:2, :]).astype(BF16), w1_ref[half:, :])
    pre = a + pltpu.roll(b, x.shape[0] - 1, 0)
    o_ref[...] = _dot(_gelu(pre).astype(BF16), w2_ref[...])


def _compress(xc, pos, w1, w2):
    b, two, g, nch, width = xc.shape
    hid = w1.shape[2]
    dh = w2.shape[2]
    return pl.pallas_call(
        _compress_body,
        grid=(b, two, g),
        in_specs=[pl.BlockSpec((None, None, None, nch, width), lambda i, t, j: (i, t, j, 0, 0)),
                  pl.BlockSpec((None, 2, width), lambda i, t, j: (t, 0, 0)),
                  pl.BlockSpec((None, 2 * width, hid), lambda i, t, j: (t, 0, 0)),
                  pl.BlockSpec((None, hid, dh), lambda i, t, j: (t, 0, 0))],
        out_specs=pl.BlockSpec((None, None, None, nch, dh), lambda i, t, j: (i, t, j, 0, 0)),
        out_shape=jax.ShapeDtypeStruct((b, two, g, nch, dh), F32),
        compiler_params=_params(3),
        name="compress",
    )(xc, pos, w1, w2)


CMP_CHUNK = 256
CMP_AUG_WIDTH = 2 * HEAD_DIM


def _heads_to_lanes(blk):
    n = blk.shape[1] // Q_BLOCK
    return jnp.concatenate([blk[r * HEAD_DIM:(r + 1) * HEAD_DIM, h * Q_BLOCK:(h + 1) * Q_BLOCK]
                            for h in range(n) for r in range(GQA_GROUP)], axis=1)


def _group_slots(qt, g):
    return jnp.concatenate([jnp.where(g == j, qt, jnp.zeros_like(qt)) for j in range(N_KV_HEADS)], axis=0)


def _q_spec(n_blocks):
    return pl.BlockSpec((None, GQA_GROUP * HEAD_DIM, n_blocks * Q_BLOCK), lambda i, j, q: (i, j, q))


CMP_QBLOCKS = 4


def _cmp_select_body(q_ref, k_ref, vt_ref, crow_ref, cb_ref, oc_ref, sel_ref, sc_ref, *, n_other):
    qis = [pl.program_id(2) * CMP_QBLOCKS + h for h in range(CMP_QBLOCKS)]
    n_slc = sel_ref.shape[1]
    qt = _heads_to_lanes(q_ref[...])
    dh, lanes = qt.shape
    w0s = [pl.multiple_of(qi * (Q_BLOCK // CMP_STRIDE), 8) for qi in qis]
    n_chunks = (w0s[-1] + CMP_WINDOW + CMP_CHUNK - 1) // CMP_CHUNK
    row_iota = lax.broadcasted_iota(jnp.int32, (CMP_CHUNK, lanes), 0)
    lane_blk = lax.broadcasted_iota(jnp.int32, (1, lanes), 1) // ROWS
    w0_lane = w0s[0]
    for h in range(1, CMP_QBLOCKS):
        w0_lane = jnp.where(lane_blk >= h, w0s[h], w0_lane)

    @pl.when(qis[0] == 0)
    def _():
        sc_ref[...] = jnp.zeros_like(sc_ref)

    c = jnp.concatenate([crow_ref[...]] * CMP_QBLOCKS, axis=1)
    c_hi = c.astype(BF16).astype(F32)
    neg_row = jnp.full((1, lanes), NEG, F32)
    zeros = lambda n, dt: jnp.zeros((n, lanes), dt)
    tail = zeros(k_ref.shape[1] - dh - BF16_ROWS, BF16)
    qa = jnp.concatenate([qt, jnp.concatenate([c_hi, c - c_hi, neg_row, zeros(BF16_ROWS - 3, F32)]).astype(BF16), tail])
    qw = jnp.concatenate([qt, jnp.concatenate([zeros(2, F32), neg_row, zeros(BF16_ROWS - 3, F32)]).astype(BF16), tail])

    def rows_of(ch):
        return pl.ds(pl.multiple_of(ch * CMP_CHUNK, CMP_CHUNK), CMP_CHUNK)

    n_slabs = lanes // Q_BLOCK
    slab = lambda i: slice(i * Q_BLOCK, (i + 1) * Q_BLOCK)

    def put(rows, val, first=0):
        for i in range(val.shape[1] // Q_BLOCK):
            sc_ref[first + i, rows, :] = val[:, slab(i)]

    def get(rows):
        return jnp.concatenate([sc_ref[i, rows, :] for i in range(n_slabs)], axis=1)

    def logits(ch, m):
        s = _dot(k_ref[rows_of(ch), :].astype(BF16), qa)
        s = jnp.where(row_iota + ch * CMP_CHUNK < w0_lane, s, NEG)
        put(rows_of(ch), s)
        return jnp.maximum(m, jnp.max(s, axis=0, keepdims=True))

    m = lax.fori_loop(0, n_chunks, logits, jnp.full((1, lanes), NEG, F32))
    win_max = []
    for h, w0 in enumerate(w0s):
        win = pl.ds(w0, CMP_WINDOW)
        s_win = _dot(k_ref[win, :].astype(BF16), qw[:, h * ROWS:(h + 1) * ROWS]) + cb_ref[...]
        put(win, s_win, first=h * GQA_GROUP)
        win_max.append(jnp.max(s_win, axis=0, keepdims=True))
    m = jnp.maximum(m, jnp.concatenate(win_max, axis=1))

    def weigh(ch, carry):
        l, acc = carry
        p = jnp.exp2(get(rows_of(ch)) - m)
        put(rows_of(ch), p)
        return l + jnp.sum(p, axis=0, keepdims=True), acc + _dot(vt_ref[:, rows_of(ch)], p.astype(BF16))

    l, acc = lax.fori_loop(0, n_chunks, weigh, (jnp.zeros((1, lanes), F32), jnp.zeros(oc_ref.shape, F32)))
    scale = jnp.where(m > 0.5 * NEG, 1.0 / jnp.maximum(l, 1e-30), 0.0)
    oc_ref[...] = acc * scale

    per = SLC_BLOCK // CMP_STRIDE
    blk = lax.broadcasted_iota(jnp.int32, (n_slc, Q_BLOCK), 0)
    blk_f = blk.astype(F32)
    lane_pos = lax.broadcasted_iota(jnp.int32, (n_slc, Q_BLOCK), 1)
    for h, qi in enumerate(qis):
        imp = jnp.zeros((n_slc, Q_BLOCK), F32)
        for r in range(GQA_GROUP):
            i = h * GQA_GROUP + r
            part = lambda off: sc_ref[i, pl.ds(CMP_FRONT_PAD + off, n_slc, stride=per), :]
            tot = part(0)
            for k in range(1, per - 1):
                tot = tot + part(k)
            imp = imp + (tot + 0.5 * (part(per - 1) + part(-1))) * scale[:, slab(i)]

        jq = (qi * Q_BLOCK + lane_pos) // SLC_BLOCK
        valid = blk <= jq
        forced = (blk == 0) | (valid & (blk > jq - N_LOCAL_BLOCKS))
        free = valid & jnp.logical_not(forced)
        work = jnp.where(free, imp, -1.0)
        for _ in range(n_other):
            mx = jnp.max(work, axis=0, keepdims=True)
            first = jnp.min(jnp.where(work == mx, blk_f, float(n_slc)), axis=0, keepdims=True)
            work = jnp.where((blk_f == first) & (mx >= 0.0), -1.0, work)
        sel_ref[h] = jnp.where(forced | (free & (work < 0.0)), 1.0, 0.0)


def _cmp_select(qt, kcp, vct, crow, cb, n_slc, n_other):
    b, g, dh = qt.shape[0], N_KV_HEADS, HEAD_DIM
    nqb = qt.shape[2] // Q_BLOCK
    total = nqb * ROWS
    ncp = kcp.shape[2]
    blk_q = pl.BlockSpec((None, None, dh, CMP_QBLOCKS * ROWS), lambda i, j, q: (i, j, 0, q))
    return pl.pallas_call(
        functools.partial(_cmp_select_body, n_other=n_other),
        grid=(b, g, nqb // CMP_QBLOCKS),
        in_specs=[_q_spec(CMP_QBLOCKS),
                  pl.BlockSpec((None, None, ncp, kcp.shape[3]), lambda i, j, q: (i, j, 0, 0)),
                  pl.BlockSpec((None, None, dh, ncp), lambda i, j, q: (i, j, 0, 0)),
                  pl.BlockSpec((None, 1, ROWS), lambda i, j, q: (j, 0, 0)),
                  pl.BlockSpec((None, CMP_WINDOW, ROWS), lambda i, j, q: (j, 0, 0))],
        out_specs=[blk_q,
                   pl.BlockSpec((None, None, CMP_QBLOCKS, n_slc, Q_BLOCK), lambda i, j, q: (i, j, q, 0, 0))],
        out_shape=[jax.ShapeDtypeStruct((b, g, dh, total), F32),
                   jax.ShapeDtypeStruct((b, g, nqb, n_slc, Q_BLOCK), F32)],
        scratch_shapes=[pltpu.VMEM((CMP_QBLOCKS * GQA_GROUP, ncp, Q_BLOCK), F32)],
        compiler_params=_params(3),
        name="cmp_select",
    )(qt, kcp, vct, crow, cb)


SLC_QBLOCKS = 2
FAR_KEYS = 256
FAR_BLOCKS = FAR_KEYS // SLC_BLOCK
FAR_CHUNKS = FAR_KEYS // KEY_CHUNK
FAR_BUFFERS = 4
FAR_AHEAD = 2
PEN_BLOCKS = 8
AUG_CONST = 2
AUG_ROWS = 16
K_SLAB = LANE
K_AUG_WIDTH = N_KV_HEADS * K_SLAB


def _key_augmentation(tm):
    assert tm % (PEN_BLOCKS * SLC_BLOCK) == 0
    pos = jnp.arange(tm, dtype=jnp.int32)
    onehot = ((pos[:, None] // SLC_BLOCK) % PEN_BLOCKS == jnp.arange(PEN_BLOCKS, dtype=jnp.int32)[None, :])
    return jnp.concatenate([jnp.ones((tm, AUG_CONST), BF16),
                            jnp.zeros((tm, AUG_ROWS - PEN_BLOCKS - AUG_CONST), BF16),
                            onehot.astype(BF16),
                            jnp.zeros((tm, K_SLAB - HEAD_DIM - AUG_ROWS), BF16)], axis=1)


def _slc_body(q_ref, k_ref, vt_ref, sel_ref, sb_ref, crow_ref, os_ref, sbuf, acc_ref):
    qis = [pl.program_id(2) * SLC_QBLOCKS + h for h in range(SLC_QBLOCKS)]
    qt = _heads_to_lanes(q_ref[...])
    dh, lanes = qt.shape
    per = KEY_CHUNK // SLC_BLOCK
    far_limits = [jnp.maximum(qi - 1, 0) * per for qi in qis]
    n_steps = (jnp.maximum(qis[-1] - 1, 0) + FAR_CHUNKS - 1) // FAR_CHUNKS
    last_step = k_ref.shape[0] // FAR_KEYS - 1
    steps_per_group = PEN_BLOCKS // FAR_BLOCKS

    c = jnp.concatenate([crow_ref[...]] * SLC_QBLOCKS, axis=1)
    c_hi = c.astype(BF16).astype(F32)
    const_rows = jnp.concatenate([c_hi, c - c_hi, jnp.zeros((AUG_ROWS - PEN_BLOCKS - AUG_CONST, lanes), F32)], axis=0)
    pad_rows = jnp.zeros((k_ref.shape[1] - dh - AUG_ROWS, lanes), BF16)
    blk_iota = lax.broadcasted_iota(jnp.int32, (PEN_BLOCKS, Q_BLOCK), 0)

    def far_logits(u):
        ua = jnp.minimum(u, last_step)
        grp0 = pl.multiple_of((ua // steps_per_group) * PEN_BLOCKS, PEN_BLOCKS)
        blk = blk_iota + (u // steps_per_group) * PEN_BLOCKS
        pens = []
        for h in range(SLC_QBLOCKS):
            pen = jnp.where((sel_ref[h, pl.ds(grp0, PEN_BLOCKS), :] > 0.5) & (blk < far_limits[h]), 0.0, NEG)
            pens += [pen] * GQA_GROUP
        qa = jnp.concatenate([qt, jnp.concatenate([const_rows, jnp.concatenate(pens, axis=1)], axis=0).astype(BF16),
                              pad_rows], axis=0)
        return _dot(k_ref[pl.ds(pl.multiple_of(ua * FAR_KEYS, FAR_KEYS), FAR_KEYS), :], qa)

    def stage(slot, u):
        s_new = far_logits(u).astype(BF16)
        sbuf[slot] = s_new
        groups = s_new.reshape(FAR_KEYS // BF16_ROWS, BF16_ROWS, lanes)
        return jnp.max(jnp.max(groups, axis=0).astype(F32), axis=0, keepdims=True)

    def far_update(m, s_ref, mx, key0):
        m_new = jnp.maximum(m, mx)
        p = jnp.exp2(s_ref[...] - m_new.astype(BF16))
        cols = pl.ds(pl.multiple_of(key0, FAR_KEYS), FAR_KEYS)
        acc_ref[...] = jnp.exp2(m - m_new) * acc_ref[...] + _dot(vt_ref[:, cols], p)
        return m_new

    def far_round(v, carry):
        m = carry[0]
        ahead = list(carry[1:])
        for slot in range(FAR_BUFFERS):
            u = FAR_BUFFERS * v + slot
            ahead.append(stage((slot + FAR_AHEAD) % FAR_BUFFERS, u + FAR_AHEAD))
            ua = jnp.minimum(u, last_step)
            m = far_update(m, sbuf.at[slot], ahead.pop(0), ua * FAR_KEYS)
        return (m, *ahead)

    ahead = [stage(u, u) for u in range(FAR_AHEAD)]
    acc_ref[...] = jnp.zeros_like(acc_ref)
    init = (jnp.full((1, lanes), NEG, F32), *ahead)
    m = lax.fori_loop(0, (n_steps + FAR_BUFFERS - 1) // FAR_BUFFERS, far_round, init)[0]
    acc = acc_ref[...]

    for h, qi in enumerate(qis):
        cols = slice(h * ROWS, (h + 1) * ROWS)
        qd = jnp.concatenate([qt[:, cols], jnp.zeros((k_ref.shape[1] - dh, ROWS), BF16)], axis=0)
        prev = jnp.maximum(qi - 1, 0)
        tiles, vts = [], []
        for chunk, kind in ((prev, jnp.where(qi >= 1, 1, 3)), (qi, 0)):
            rows = pl.ds(pl.multiple_of(chunk * KEY_CHUNK, KEY_CHUNK), KEY_CHUNK)
            s = _dot(k_ref[rows, :], qd) + sb_ref[kind]
            for j in range(per):
                srow = sel_ref[h, pl.ds(chunk * per + j, 1), :]
                srow = jnp.concatenate([srow] * GQA_GROUP, axis=1)
                tiles.append(jnp.where(srow > 0.5, s[j * SLC_BLOCK:(j + 1) * SLC_BLOCK, :], NEG))
            vts.append(vt_ref[:, rows])
        s = jnp.concatenate(tiles, axis=0)
        m_new = jnp.maximum(m[:, cols], jnp.max(s, axis=0, keepdims=True))
        p = jnp.exp2(s - m_new).astype(BF16)
        acc_h = jnp.exp2(m[:, cols] - m_new) * acc[:, cols] + _dot(jnp.concatenate(vts, axis=1), p)
        os_ref[:, cols] = acc_h[:dh] / jnp.maximum(acc_h[dh:dh + 1], 1e-30)


def _slc(qt, ks, vst, sel, sb, crow):
    b, g, dh = qt.shape[0], N_KV_HEADS, HEAD_DIM
    s_len = ks.shape[1]
    nqb = s_len // Q_BLOCK
    total = nqb * ROWS
    n_slc = sel.shape[3]
    blk_q = pl.BlockSpec((None, None, dh, SLC_QBLOCKS * ROWS), lambda i, j, q: (i, j, 0, q))
    return pl.pallas_call(
        _slc_body,
        grid=(b, g, nqb // SLC_QBLOCKS),
        in_specs=[_q_spec(SLC_QBLOCKS),
                  pl.BlockSpec((None, s_len, K_SLAB), lambda i, j, q: (i, 0, j)),
                  pl.BlockSpec((None, VT_ROWS, s_len), lambda i, j, q: (i, j, 0)),
                  pl.BlockSpec((None, None, SLC_QBLOCKS, n_slc, Q_BLOCK), lambda i, j, q: (i, j, q, 0, 0)),
                  pl.BlockSpec((None, 4, KEY_CHUNK, ROWS), lambda i, j, q: (j, 0, 0, 0)),
                  pl.BlockSpec((None, 1, ROWS), lambda i, j, q: (j, 0, 0))],
        out_specs=blk_q,
        out_shape=jax.ShapeDtypeStruct((b, g, dh, total), F32),
        scratch_shapes=[pltpu.VMEM((FAR_BUFFERS, FAR_KEYS, SLC_QBLOCKS * ROWS), BF16),
                        pltpu.VMEM((VT_ROWS, SLC_QBLOCKS * ROWS), F32)],
        compiler_params=_params(3),
        name="slc_attention",
    )(qt, ks, vst, sel, sb, crow)


WIN_CHUNKS = WINDOW // KEY_CHUNK + 1
WIN_QBLOCKS = 4


def _win_body(q_ref, k_ref, vt_ref, wb_ref, oc_ref, os_ref, gate_ref, o_ref):
    q_all = _group_slots(_heads_to_lanes(q_ref[...]), pl.program_id(1))
    for h in range(WIN_QBLOCKS):
        qi = pl.program_id(2) * WIN_QBLOCKS + h
        lanes = slice(h * ROWS, (h + 1) * ROWS)
        tokens = slice(h * Q_BLOCK, (h + 1) * Q_BLOCK)
        qt = q_all[:, lanes]
        tiles, vts = [], []
        for delta in range(WIN_CHUNKS - 1, -1, -1):
            c = jnp.maximum(qi - delta, 0)
            rows = pl.ds(pl.multiple_of(c * KEY_CHUNK, KEY_CHUNK), KEY_CHUNK)
            kind = jnp.where(qi >= delta, delta, WIN_CHUNKS)
            tiles.append(_dot(k_ref[rows, :], qt) + wb_ref[kind])
            vts.append(vt_ref[:, rows])
        s = jnp.concatenate(tiles, axis=0)
        m = jnp.max(s, axis=0, keepdims=True)
        p = jnp.exp2(s - m)
        l = jnp.sum(p, axis=0, keepdims=True)
        o_w = _dot(jnp.concatenate(vts, axis=1), p.astype(BF16)) / jnp.maximum(l, 1e-30)
        gate = lambda br: jnp.concatenate([gate_ref[br * GQA_GROUP + r:br * GQA_GROUP + r + 1, tokens]
                                           for r in range(GQA_GROUP)], axis=1)
        o = gate(0) * oc_ref[:, lanes] + gate(1) * os_ref[:, lanes] + gate(2) * o_w
        for r in range(GQA_GROUP):
            o_ref[r * HEAD_DIM:(r + 1) * HEAD_DIM, tokens] = o[:, r * Q_BLOCK:(r + 1) * Q_BLOCK].astype(o_ref.dtype)


def _win_mix(qt, kw, vwt, wb, oc, osl, gates):
    b, g, dh = qt.shape[0], N_KV_HEADS, HEAD_DIM
    s_len = kw.shape[1]
    blk_q = pl.BlockSpec((None, None, dh, WIN_QBLOCKS * ROWS), lambda i, j, q: (i, j, 0, q))
    return pl.pallas_call(
        _win_body,
        grid=(b, g, s_len // (WIN_QBLOCKS * Q_BLOCK)),
        in_specs=[_q_spec(WIN_QBLOCKS),
                  pl.BlockSpec((None, s_len, kw.shape[2]), lambda i, j, q: (i, 0, 0)),
                  pl.BlockSpec((None, dh, s_len), lambda i, j, q: (i, j, 0)),
                  pl.BlockSpec((None, WIN_CHUNKS + 1, KEY_CHUNK, ROWS), lambda i, j, q: (j, 0, 0, 0)),
                  blk_q, blk_q,
                  pl.BlockSpec((None, GATE_ROWS, WIN_QBLOCKS * Q_BLOCK), lambda i, j, q: (i, j, q))],
        out_specs=_q_spec(WIN_QBLOCKS),
        out_shape=jax.ShapeDtypeStruct(qt.shape, BF16),
        compiler_params=_params(3),
        name="window_mix",
    )(qt, kw, vwt, wb, oc, osl, gates)


def _outproj_body(a_ref, gm_ref, h_ref, wa_ref, wb_ref, g_ref, o_ref):
    y = _dot_tn(a_ref[...], wa_ref[...]) + _dot(gm_ref[...], wb_ref[...])
    o_ref[...] = h_ref[...] + _rms(y, g_ref[...])


def _outproj(a, gm, h, wa, wb, gpost, tm):
    n, d = h.shape
    per_b = a.shape[2] // tm
    full = _resident
    row = lambda w: pl.BlockSpec((tm, w), lambda i: (i, 0))
    return pl.pallas_call(
        _outproj_body,
        grid=(n // tm,),
        in_specs=[pl.BlockSpec((None, a.shape[1], tm), lambda i: (i // per_b, 0, i % per_b)),
                  row(gm.shape[1]), row(d), full(wa), full(wb), full(gpost)],
        out_specs=row(d),
        out_shape=jax.ShapeDtypeStruct((n, d), F32),
        compiler_params=_params(1),
        name="outproj",
    )(a, gm, h, wa, wb, gpost)


FFN_TILE = 256


def _ffn_body(h_ref, gpre_ref, wg_ref, wu_ref, wd_ref, gpost_ref, o_ref):
    h = h_ref[...]
    xb = _rms(h, gpre_ref[...]).astype(BF16)
    acc = jnp.zeros(h.shape, F32)
    for j in range(wg_ref.shape[1] // FFN_TILE):
        cols = slice(j * FFN_TILE, (j + 1) * FFN_TILE)
        gate = _dot(xb, wg_ref[:, cols])
        up = _dot(xb, wu_ref[:, cols])
        act = (gate * _sigmoid(gate) * up).astype(BF16)
        acc = acc + _dot(act, wd_ref[cols, :])
    o_ref[...] = h + _rms(acc, gpost_ref[...])


def _ffn(h, gpre, wg, wu, wd, gpost, tm):
    n, d = h.shape
    full = _resident
    row = pl.BlockSpec((tm, d), lambda i: (i, 0))
    return pl.pallas_call(
        _ffn_body,
        grid=(n // tm,),
        in_specs=[row, full(gpre), full(wg), full(wu), full(wd), full(gpost)],
        out_specs=row,
        out_shape=jax.ShapeDtypeStruct((n, d), F32),
        compiler_params=_params(1),
        name="ffn",
    )(h, gpre, wg, wu, wd, gpost)


def _t5_bucket(dist):
    n = jnp.maximum(dist, 0)
    max_exact = N_BUCKETS // 2
    nf = jnp.maximum(n, max_exact).astype(F32)
    large = max_exact + (jnp.log(nf / max_exact) / math.log(REL_MAX_DISTANCE / max_exact)
                         * (N_BUCKETS - max_exact)).astype(jnp.int32)
    return jnp.where(n < max_exact, n, jnp.minimum(large, N_BUCKETS - 1))


def _bias_tile(table, dist, mask):
    onehot = (_t5_bucket(dist)[..., None] == jnp.arange(N_BUCKETS, dtype=jnp.int32)).astype(F32)
    b = jnp.einsum("kqn,nh->kqh", onehot, table.astype(F32), precision=lax.Precision.HIGHEST)
    b = jnp.where(mask[..., None], b * LOG2E, NEG)
    k = dist.shape[0]
    return b.reshape(k, Q_BLOCK, N_KV_HEADS, GQA_GROUP).transpose(2, 0, 3, 1).reshape(N_KV_HEADS, k, ROWS)


def _bias_tiles(rel_bias):
    ql = jnp.arange(Q_BLOCK, dtype=jnp.int32)[None, :]
    kl = jnp.arange(KEY_CHUNK, dtype=jnp.int32)[:, None]
    chunk_dist = lambda delta: delta * KEY_CHUNK + ql - kl
    everything = jnp.ones((KEY_CHUNK, Q_BLOCK), bool)
    nothing = jnp.zeros((KEY_CHUNK, Q_BLOCK), bool)
    far = jnp.full((KEY_CHUNK, Q_BLOCK), REL_MAX_DISTANCE, jnp.int32)
    d0 = chunk_dist(0)
    sb = jnp.stack([_bias_tile(rel_bias, d0, d0 >= 0),
                    _bias_tile(rel_bias, chunk_dist(1), everything),
                    _bias_tile(rel_bias, far, everything),
                    _bias_tile(rel_bias, far, nothing)], axis=1)
    wtiles = []
    for delta in range(WIN_CHUNKS):
        d = chunk_dist(delta)
        wtiles.append(_bias_tile(rel_bias, d, (d >= 0) & (d < WINDOW)))
    wtiles.append(_bias_tile(rel_bias, far, nothing))
    wb = jnp.stack(wtiles, axis=1)
    rel = jnp.arange(CMP_WINDOW, dtype=jnp.int32)[:, None] - CMP_FRONT_PAD
    dc = ql - CMP_STRIDE * rel - (CMP_BLOCK - 1)
    cb = _bias_tile(rel_bias, dc, dc >= 0)
    crow = _bias_tile(rel_bias, far[:1], everything[:1])
    return sb, wb, cb, crow


def _mixer(h, b, s, lw, tiles, tm):
    sb, wb, cb, crow = tiles
    qt, cv, ksa, kw, vst, vwt, gates, gm = _inproj(
        h, b, (lw["gpre"], lw["wqt"], lw["wnat"], lw["wvt"], lw["wgt"], lw["wu"], lw["wv"],
               lw["lng"], lw["lnb"], lw["ws"], lw["bs"], _key_augmentation(tm)), tm)
    g, dh = N_KV_HEADS, HEAD_DIM
    n_chunks = s // CMP_STRIDE
    comp = _compress(cv, lw["cpos"], lw["cw1"], lw["cw2"])
    ncp = -(-(n_chunks + CMP_WINDOW) // CMP_CHUNK) * CMP_CHUNK
    back = ncp - CMP_FRONT_PAD - n_chunks
    comp = jnp.pad(comp, ((0, 0), (0, 0), (0, 0), (CMP_FRONT_PAD, back), (0, 0)))
    front = (jnp.arange(ncp) < CMP_FRONT_PAD).astype(F32)[:, None]
    aug = jnp.concatenate([jnp.ones((ncp, 2), F32), front, jnp.zeros((ncp, CMP_AUG_WIDTH - dh - 3), F32)], axis=1)
    kcp = jnp.concatenate([comp[:, 0], jnp.broadcast_to(aug, (b, g) + aug.shape)], axis=-1)
    vct = comp[:, 1].transpose(0, 1, 3, 2).astype(BF16)

    n_slc = s // SLC_BLOCK
    n_other = min(SLC_TOPK, n_slc) - (N_LOCAL_BLOCKS + 1)
    oc, sel = _cmp_select(qt, kcp, vct, crow, cb, n_slc, n_other)
    osl = _slc(qt, ksa.reshape(b, s, ksa.shape[1]), vst, sel, sb, crow)
    ot = _win_mix(qt, kw.reshape(b, s, kw.shape[1]), vwt, wb, oc, osl, gates)
    return _outproj(ot, gm, h, lw["wo_a"], lw["wo_b"], lw["gpost"], tm)


def _layer_weights(l, p):
    d = p["w_in"].shape[1]
    w_in = p["w_in"][l]
    o = NSA_WIDTH
    wqt = w_in[:, :o].T
    k_c, v_c, k_s, v_s, k_w, v_w = [w_in[:, o + i * KV_WIDTH:o + (i + 1) * KV_WIDTH] for i in range(6)]
    wnat = jnp.concatenate([k_c, v_c, k_s, k_w], axis=1)
    wvt = jnp.concatenate([v_s, v_w], axis=1).T
    o += 6 * KV_WIDTH
    wg = w_in[:, o:o + N_GATES].reshape(d, N_KV_HEADS, GQA_GROUP, N_BRANCHES).transpose(1, 3, 2, 0)
    wg = wg.reshape(N_KV_HEADS, N_BRANCHES * GQA_GROUP, d)
    wgt = jnp.pad(wg, ((0, 0), (0, GATE_ROWS - N_BRANCHES * GQA_GROUP), (0, 0))).reshape(N_KV_HEADS * GATE_ROWS, d)
    o += N_GATES
    gw = (w_in.shape[1] - o) // 2
    wu, wv = w_in[:, o:o + gw], w_in[:, o + gw:]
    causal = jnp.tril(jnp.ones((GMLP_CHUNK, GMLP_CHUNK), bool))
    ws = jnp.where(causal, p["gmlp_w_s"][l], 0.0)
    bs = jnp.repeat(p["gmlp_b_s"][l].T, gw // N_GMLP_GROUPS, axis=1)
    half = CMP_STRIDE * HEAD_DIM
    cpos = jnp.stack([p["cmp_pos_k"][l].reshape(2, half), p["cmp_pos_v"][l].reshape(2, half)])
    dff = p["w_down"].shape[1]
    row = lambda v: v[l].reshape(1, -1)
    bf = lambda w: w.astype(BF16)
    return dict(gpre=row(p["norm_mix_pre"]), wqt=bf(wqt), wnat=bf(wnat), wvt=bf(wvt), wgt=bf(wgt), wu=bf(wu), wv=bf(wv),
                lng=row(p["gmlp_ln_g"]), lnb=row(p["gmlp_ln_b"]), ws=bf(ws), bs=bs,
                cpos=cpos, cw1=bf(jnp.stack([p["cmp_w1_k"][l], p["cmp_w1_v"][l]])),
                cw2=bf(jnp.stack([p["cmp_w2_k"][l], p["cmp_w2_v"][l]])),
                wo_a=bf(p["w_out"][l][:NSA_WIDTH]), wo_b=bf(p["w_out"][l][NSA_WIDTH:]),
                gpost=row(p["norm_mix_post"]), fpre=row(p["norm_ffn_pre"]), fpost=row(p["norm_ffn_post"]),
                fwg=bf(p["w_gate_up"][l][:, :dff]), fwu=bf(p["w_gate_up"][l][:, dff:]), fwd=bf(p["w_down"][l]))


def _trunk(p, tm):
    x = p["x"]
    b, s, d = x.shape
    h = x.reshape(b * s, d)
    tiles = _bias_tiles(p["rel_bias"])
    for l in range(p["w_in"].shape[0]):
        lw = _layer_weights(l, p)
        h = _mixer(h, b, s, lw, tiles, tm)
        h = _ffn(h, lw["fpre"], lw["fwg"], lw["fwu"], lw["fwd"], lw["fpost"], tm)
    return h.reshape(b, s, d)


def kernel(x, rel_bias, norm_mix_pre, norm_mix_post, norm_ffn_pre, norm_ffn_post, w_in, cmp_pos_k, cmp_w1_k, cmp_w2_k, cmp_pos_v, cmp_w1_v, cmp_w2_v, gmlp_ln_g, gmlp_ln_b, gmlp_w_s, gmlp_b_s, w_out, w_gate_up, w_down):
    p = dict(x=x, rel_bias=rel_bias, norm_mix_pre=norm_mix_pre, norm_mix_post=norm_mix_post,
             norm_ffn_pre=norm_ffn_pre, norm_ffn_post=norm_ffn_post, w_in=w_in,
             cmp_pos_k=cmp_pos_k, cmp_w1_k=cmp_w1_k, cmp_w2_k=cmp_w2_k,
             cmp_pos_v=cmp_pos_v, cmp_w1_v=cmp_w1_v, cmp_w2_v=cmp_w2_v,
             gmlp_ln_g=gmlp_ln_g, gmlp_ln_b=gmlp_ln_b, gmlp_w_s=gmlp_w_s, gmlp_b_s=gmlp_b_s,
             w_out=w_out, w_gate_up=w_gate_up, w_down=w_down)
    return _trunk(p, tm=512)
```

```python
import functools
import math

import jax
import jax.numpy as jnp
import numpy as np
from jax import lax
from jax.experimental import pallas as pl
from jax.experimental.pallas import tpu as pltpu

F32 = jnp.float32
BF16 = jnp.bfloat16

N_NSA_HEADS = 8
N_KV_HEADS = 2
GQA_GROUP = N_NSA_HEADS // N_KV_HEADS
HEAD_DIM = 64
NSA_WIDTH = N_NSA_HEADS * HEAD_DIM
KV_WIDTH = N_KV_HEADS * HEAD_DIM
N_BRANCHES = 3
N_GATES = N_BRANCHES * N_NSA_HEADS
CMP_BLOCK = 32
CMP_STRIDE = 16
SLC_BLOCK = 64
SLC_TOPK = 16
N_LOCAL_BLOCKS = 2
WINDOW = 512
Q_BLOCK = 128
N_GMLP_GROUPS = 8
GMLP_CHUNK = 128
N_BUCKETS = 32
REL_MAX_DISTANCE = 128
RMS_EPS = 1e-6
LN_EPS = 1e-5

ROWS = GQA_GROUP * Q_BLOCK
KEY_CHUNK = 128
CMP_FRONT_PAD = 24
CMP_WINDOW = 32
NEG = -1e30
LANE = 128
GATE_PAD = 128
MXU_TILE = 256
BF16_ROWS = 16
VMEM_LIMIT = 48 * 1024 * 1024
LOG2E = math.log2(math.e)
Q_SCALE = HEAD_DIM ** -0.5 * LOG2E


def _dot(a, b):
    return jnp.dot(a, b, preferred_element_type=F32)


def _gelu(x):
    c = math.sqrt(2.0 / math.pi)
    return 0.5 * x * (1.0 + jnp.tanh(c * (x + 0.044715 * (x * x * x))))


def _sigmoid(x):
    return 1.0 / (1.0 + jnp.exp(-x))


def _rms(x, g):
    ms = jnp.mean(x * x, axis=-1, keepdims=True)
    return (x * lax.rsqrt(ms + RMS_EPS)) * g


def _resident(a):
    return pl.BlockSpec(a.shape, lambda i: (0,) * a.ndim, pipeline_mode=pl.Buffered(1))


def _params(n_axes):
    return pltpu.CompilerParams(dimension_semantics=("arbitrary",) * n_axes,
                                vmem_limit_bytes=VMEM_LIMIT)


def _dot_nt(a, b):
    return lax.dot_general(a, b, (((1,), (1,)), ((), ())), preferred_element_type=F32)


def _dot_tn(a, b):
    return lax.dot_general(a, b, (((0,), (0,)), ((), ())), preferred_element_type=F32)


def _inproj_body(x_ref, gpre_ref, wqt_ref, wnat_ref, wvt_ref, wgt_ref, wu_ref, wv_ref, lng_ref, lnb_ref,
                 ws_ref, bs_ref, aug_ref, qt_ref, cv_ref, ksa_ref, kw_ref, vst_ref, vwt_ref, gate_ref, gm_ref, cv_sc):
    tm = x_ref.shape[0]
    xb = _rms(x_ref[...], gpre_ref[...]).astype(BF16)
    qt_ref[...] = (_dot_nt(wqt_ref[...], xb) * Q_SCALE).astype(BF16)
    nat = _dot(xb, wnat_ref[...])
    chunks = tm // CMP_STRIDE
    left = lax.broadcasted_iota(jnp.int32, (chunks, LANE), 1) < HEAD_DIM
    for kv in range(2):
        cv_sc[kv] = nat[:, kv * KV_WIDTH:(kv + 1) * KV_WIDTH]
    for kv in range(2):
        for pair in range(CMP_STRIDE // 2):
            lanes = slice(pair * LANE, (pair + 1) * LANE)
            a = cv_sc[kv, pl.ds(2 * pair, chunks, stride=CMP_STRIDE), :]
            b = cv_sc[kv, pl.ds(2 * pair + 1, chunks, stride=CMP_STRIDE), :]
            cv_ref[kv, 0, :, lanes] = jnp.where(left, a, pltpu.roll(b, HEAD_DIM, 1))
            cv_ref[kv, 1, :, lanes] = jnp.where(left, pltpu.roll(a, HEAD_DIM, 1), b)
    k_slc = nat[:, 2 * KV_WIDTH:3 * KV_WIDTH].astype(BF16)
    ksa_ref[...] = jnp.concatenate(
        [piece for g in range(N_KV_HEADS) for piece in (k_slc[:, g * HEAD_DIM:(g + 1) * HEAD_DIM], aug_ref[...])], axis=1)
    kw_ref[...] = nat[:, 3 * KV_WIDTH:].astype(BF16)
    vt = _dot_nt(wvt_ref[...], xb).astype(BF16)
    denom = jnp.concatenate([jnp.ones((1, tm), BF16), jnp.zeros((BF16_ROWS - 1, tm), BF16)], axis=0)
    vst_ref[...] = jnp.concatenate(
        [piece for g in range(N_KV_HEADS) for piece in (vt[g * HEAD_DIM:(g + 1) * HEAD_DIM], denom)], axis=0)
    vwt_ref[...] = vt[KV_WIDTH:]
    gate_ref[...] = _sigmoid(_dot_nt(wgt_ref[...], xb))
    zu = _gelu(_dot(xb, wu_ref[...]))
    zv = _gelu(_dot(xb, wv_ref[...]))
    mu = jnp.mean(zv, axis=-1, keepdims=True)
    zc = zv - mu
    var = jnp.mean(zc * zc, axis=-1, keepdims=True)
    zv = ((zc * lax.rsqrt(var + LN_EPS)) * lng_ref[...] + lnb_ref[...]).astype(BF16)
    gdim = zv.shape[1] // N_GMLP_GROUPS
    left = lax.broadcasted_iota(jnp.int32, (GMLP_CHUNK, LANE), 1) < gdim
    for c in range(tm // GMLP_CHUNK):
        rows = slice(c * GMLP_CHUNK, (c + 1) * GMLP_CHUNK)
        for j in range(zv.shape[1] // LANE):
            cols = slice(j * LANE, (j + 1) * LANE)
            z = zv[rows, cols]
            sv = jnp.where(left, _dot(ws_ref[2 * j], z), _dot(ws_ref[2 * j + 1], z)) + bs_ref[:, cols]
            gm_ref[rows, cols] = (zu[rows, cols] * sv).astype(BF16)


VT_ROWS = HEAD_DIM + BF16_ROWS
GATE_ROWS = 16


def _inproj(x, b, weights, tm):
    n, d = x.shape
    s = n // b
    per_b = s // tm
    row = lambda w: pl.BlockSpec((tm, w), lambda i: (i, 0))
    col = lambda h: pl.BlockSpec((None, h, tm), lambda i: (i // per_b, 0, i % per_b))
    gw = weights[5].shape[1]
    flat = CMP_STRIDE * HEAD_DIM
    assert KV_WIDTH == LANE and N_KV_HEADS == 2
    return pl.pallas_call(
        _inproj_body,
        grid=(n // tm,),
        in_specs=[row(d)] + [_resident(a) for a in weights],
        out_specs=[col(NSA_WIDTH),
                   pl.BlockSpec((None, 2, N_KV_HEADS, tm // CMP_STRIDE, flat),
                                lambda i: (i // per_b, 0, 0, i % per_b, 0)),
                   row(K_AUG_WIDTH), row(KV_WIDTH),
                   col(N_KV_HEADS * VT_ROWS), col(KV_WIDTH), col(N_KV_HEADS * GATE_ROWS), row(gw)],
        out_shape=[jax.ShapeDtypeStruct((b, NSA_WIDTH, s), BF16),
                   jax.ShapeDtypeStruct((b, 2, N_KV_HEADS, s // CMP_STRIDE, flat), F32),
                   jax.ShapeDtypeStruct((n, K_AUG_WIDTH), BF16),
                   jax.ShapeDtypeStruct((n, KV_WIDTH), BF16),
                   jax.ShapeDtypeStruct((b, N_KV_HEADS * VT_ROWS, s), BF16),
                   jax.ShapeDtypeStruct((b, KV_WIDTH, s), BF16),
                   jax.ShapeDtypeStruct((b, N_KV_HEADS * GATE_ROWS, s), F32),
                   jax.ShapeDtypeStruct((n, gw), BF16)],
        scratch_shapes=[pltpu.VMEM((2, tm, KV_WIDTH), F32)],
        compiler_params=_params(1),
        name="inproj_gmlp",
    )(x, *weights)


def _compress_body(x_ref, pos_ref, w1_ref, w2_ref, o_ref):
    x = x_ref[...]
    half = x.shape[1]
    a = _dot((x + pos_ref[0:1, :]).astype(BF16), w1_ref[:half, :])
    b = _dot((x + pos_ref[1:2, :]).astype(BF16), w1_ref[half:, :])
    pre = a + pltpu.roll(b, x.shape[0] - 1, 0)
    o_ref[...] = _dot(_gelu(pre).astype(BF16), w2_ref[...])


def _compress(xc, pos, w1, w2):
    b, two, g, nch, width = xc.shape
    hid = w1.shape[2]
    dh = w2.shape[2]
    return pl.pallas_call(
        _compress_body,
        grid=(b, two, g),
        in_specs=[pl.BlockSpec((None, None, None, nch, width), lambda i, t, j: (i, t, j, 0, 0)),
                  pl.BlockSpec((None, 2, width), lambda i, t, j: (t, 0, 0)),
                  pl.BlockSpec((None, 2 * width, hid), lambda i, t, j: (t, 0, 0)),
                  pl.BlockSpec((None, hid, dh), lambda i, t, j: (t, 0, 0))],
        out_specs=pl.BlockSpec((None, None, None, nch, dh), lambda i, t, j: (i, t, j, 0, 0)),
        out_shape=jax.ShapeDtypeStruct((b, two, g, nch, dh), F32),
        compiler_params=_params(3),
        name="compress",
    )(xc, pos, w1, w2)


CMP_CHUNK = 256
CMP_AUG_WIDTH = 2 * HEAD_DIM


def _heads_to_lanes(blk):
    n = blk.shape[1] // Q_BLOCK
    return jnp.concatenate([blk[r * HEAD_DIM:(r + 1) * HEAD_DIM, h * Q_BLOCK:(h + 1) * Q_BLOCK]
                            for h in range(n) for r in range(GQA_GROUP)], axis=1)


def _group_slots(qt, g):
    return jnp.concatenate([jnp.where(g == j, qt, jnp.zeros_like(qt)) for j in range(N_KV_HEADS)], axis=0)


def _q_spec(n_blocks):
    return pl.BlockSpec((None, GQA_GROUP * HEAD_DIM, n_blocks * Q_BLOCK), lambda i, j, q: (i, j, q))


CMP_QBLOCKS = 4
SUBLANES = 8
SORT_KEEP = 16


def _bitonic_merge(xs):
    if len(xs) == 1:
        return xs
    half = len(xs) // 2
    hi = [jnp.maximum(xs[i], xs[i + half]) for i in range(half)]
    lo = [jnp.minimum(xs[i], xs[i + half]) for i in range(half)]
    return _bitonic_merge(hi) + _bitonic_merge(lo)


def _bitonic_sort(xs):
    if len(xs) == 1:
        return xs
    half = len(xs) // 2
    return _bitonic_merge(_bitonic_sort(xs[:half]) + _bitonic_sort(xs[half:])[::-1])


def _top_of_two(a, b):
    n = len(a)
    return _bitonic_merge([jnp.maximum(a[i], b[n - 1 - i]) for i in range(n)])


def _kth_largest(x, k):
    assert k <= SORT_KEEP and x.shape[0] % SUBLANES == 0
    tiles = [x[i * SUBLANES:(i + 1) * SUBLANES] for i in range(x.shape[0] // SUBLANES)]
    tiles += [jnp.full_like(tiles[0], -2.0)] * (-len(tiles) % SORT_KEEP)
    top = _bitonic_sort(tiles[:SORT_KEEP])
    for j in range(SORT_KEEP, len(tiles), SORT_KEEP):
        top = _top_of_two(top, _bitonic_sort(tiles[j:j + SORT_KEEP]))
    shift = SUBLANES // 2
    while shift:
        top = _top_of_two(top, [pltpu.roll(v, shift, 0) for v in top])
        shift //= 2
    return top[k - 1][0:1]


def _cmp_select_body(q_ref, k_ref, vt_ref, crow_ref, cb_ref, oc_ref, sel_ref, sc_ref, *, n_other):
    qis = [pl.program_id(2) * CMP_QBLOCKS + h for h in range(CMP_QBLOCKS)]
    n_slc = sel_ref.shape[1]
    qt = _heads_to_lanes(q_ref[...])
    dh, lanes = qt.shape
    w0s = [pl.multiple_of(qi * (Q_BLOCK // CMP_STRIDE), 8) for qi in qis]
    n_chunks = (w0s[-1] + CMP_WINDOW + CMP_CHUNK - 1) // CMP_CHUNK
    row_iota = lax.broadcasted_iota(jnp.int32, (CMP_CHUNK, lanes), 0)
    lane_blk = lax.broadcasted_iota(jnp.int32, (1, lanes), 1) // ROWS
    w0_lane = w0s[0]
    for h in range(1, CMP_QBLOCKS):
        w0_lane = jnp.where(lane_blk >= h, w0s[h], w0_lane)

    @pl.when(qis[0] == 0)
    def _():
        sc_ref[...] = jnp.zeros_like(sc_ref)

    c = jnp.concatenate([crow_ref[...]] * CMP_QBLOCKS, axis=1)
    c_hi = c.astype(BF16).astype(F32)
    neg_row = jnp.full((1, lanes), NEG, F32)
    zeros = lambda n, dt: jnp.zeros((n, lanes), dt)
    tail = zeros(k_ref.shape[1] - dh - BF16_ROWS, BF16)
    qa = jnp.concatenate([qt, jnp.concatenate([c_hi, c - c_hi, neg_row, zeros(BF16_ROWS - 3, F32)]).astype(BF16), tail])
    qw = jnp.concatenate([qt, jnp.concatenate([zeros(2, F32), neg_row, zeros(BF16_ROWS - 3, F32)]).astype(BF16), tail])

    def rows_of(ch):
        return pl.ds(pl.multiple_of(ch * CMP_CHUNK, CMP_CHUNK), CMP_CHUNK)

    n_slabs = lanes // Q_BLOCK
    slab = lambda i: slice(i * Q_BLOCK, (i + 1) * Q_BLOCK)

    def put(rows, val, first=0):
        for i in range(val.shape[1] // Q_BLOCK):
            sc_ref[first + i, rows, :] = val[:, slab(i)]

    def get(rows):
        return jnp.concatenate([sc_ref[i, rows, :] for i in range(n_slabs)], axis=1)

    def logits(ch, m):
        s = _dot(k_ref[rows_of(ch), :].astype(BF16), qa)
        s = jnp.where(row_iota + ch * CMP_CHUNK < w0_lane, s, NEG)
        put(rows_of(ch), s)
        return jnp.maximum(m, jnp.max(s, axis=0, keepdims=True))

    m = lax.fori_loop(0, n_chunks, logits, jnp.full((1, lanes), NEG, F32))
    win_max = []
    for h, w0 in enumerate(w0s):
        win = pl.ds(w0, CMP_WINDOW)
        s_win = _dot(k_ref[win, :].astype(BF16), qw[:, h * ROWS:(h + 1) * ROWS]) + cb_ref[...]
        put(win, s_win, first=h * GQA_GROUP)
        win_max.append(jnp.max(s_win, axis=0, keepdims=True))
    m = jnp.maximum(m, jnp.concatenate(win_max, axis=1))

    def weigh(ch, carry):
        l, acc = carry
        p = jnp.exp2(get(rows_of(ch)) - m)
        put(rows_of(ch), p)
        return l + jnp.sum(p, axis=0, keepdims=True), acc + _dot(vt_ref[:, rows_of(ch)], p.astype(BF16))

    l, acc = lax.fori_loop(0, n_chunks, weigh, (jnp.zeros((1, lanes), F32), jnp.zeros(oc_ref.shape, F32)))
    scale = jnp.where(m > 0.5 * NEG, 1.0 / jnp.maximum(l, 1e-30), 0.0)
    oc_ref[...] = acc * scale

    per = SLC_BLOCK // CMP_STRIDE
    blk = lax.broadcasted_iota(jnp.int32, (n_slc, Q_BLOCK), 0)
    blk_f = blk.astype(F32)
    lane_pos = lax.broadcasted_iota(jnp.int32, (n_slc, Q_BLOCK), 1)
    def candidates(h):
        imp = jnp.zeros((n_slc, Q_BLOCK), F32)
        for r in range(GQA_GROUP):
            i = h * GQA_GROUP + r
            part = lambda off: sc_ref[i, pl.ds(CMP_FRONT_PAD + off, n_slc, stride=per), :]
            tot = part(0)
            for k in range(1, per - 1):
                tot = tot + part(k)
            imp = imp + (tot + 0.5 * (part(per - 1) + part(-1))) * scale[:, slab(i)]
        jq = (qis[h] * Q_BLOCK + lane_pos) // SLC_BLOCK
        valid = blk <= jq
        forced = (blk == 0) | (valid & (blk > jq - N_LOCAL_BLOCKS))
        free = valid & jnp.logical_not(forced)
        return jnp.where(free, imp, -1.0), forced, free

    tied = []
    for h in range(CMP_QBLOCKS):
        work, forced, free = candidates(h)
        t = _kth_largest(work, n_other)
        picked = (work > t) | (free & (work == t))
        sel_ref[h] = jnp.where(forced | picked, 1.0, 0.0)
        tied.append(jnp.max(jnp.sum(jnp.where(picked, 1.0, 0.0), axis=0, keepdims=True)) > n_other)

    for h in range(CMP_QBLOCKS):
        @pl.when(tied[h])
        def _(h=h):
            work, forced, free = candidates(h)
            for _ in range(n_other):
                mx = jnp.max(work, axis=0, keepdims=True)
                first = jnp.min(jnp.where(work == mx, blk_f, float(n_slc)), axis=0, keepdims=True)
                work = jnp.where((blk_f == first) & (mx >= 0.0), -1.0, work)
            sel_ref[h] = jnp.where(forced | (free & (work < 0.0)), 1.0, 0.0)


def _cmp_select(qt, kcp, vct, crow, cb, n_slc, n_other):
    b, g, dh = qt.shape[0], N_KV_HEADS, HEAD_DIM
    nqb = qt.shape[2] // Q_BLOCK
    total = nqb * ROWS
    ncp = kcp.shape[2]
    blk_q = pl.BlockSpec((None, None, dh, CMP_QBLOCKS * ROWS), lambda i, j, q: (i, j, 0, q))
    return pl.pallas_call(
        functools.partial(_cmp_select_body, n_other=n_other),
        grid=(b, g, nqb // CMP_QBLOCKS),
        in_specs=[_q_spec(CMP_QBLOCKS),
                  pl.BlockSpec((None, None, ncp, kcp.shape[3]), lambda i, j, q: (i, j, 0, 0)),
                  pl.BlockSpec((None, None, dh, ncp), lambda i, j, q: (i, j, 0, 0)),
                  pl.BlockSpec((None, 1, ROWS), lambda i, j, q: (j, 0, 0)),
                  pl.BlockSpec((None, CMP_WINDOW, ROWS), lambda i, j, q: (j, 0, 0))],
        out_specs=[blk_q,
                   pl.BlockSpec((None, None, CMP_QBLOCKS, n_slc, Q_BLOCK), lambda i, j, q: (i, j, q, 0, 0))],
        out_shape=[jax.ShapeDtypeStruct((b, g, dh, total), F32),
                   jax.ShapeDtypeStruct((b, g, nqb, n_slc, Q_BLOCK), F32)],
        scratch_shapes=[pltpu.VMEM((CMP_QBLOCKS * GQA_GROUP, ncp, Q_BLOCK), F32)],
        compiler_params=_params(3),
        name="cmp_select",
    )(qt, kcp, vct, crow, cb)


SLC_QBLOCKS = 2
FAR_KEYS = 256
FAR_BLOCKS = FAR_KEYS // SLC_BLOCK
FAR_CHUNKS = FAR_KEYS // KEY_CHUNK
FAR_BUFFERS = 4
FAR_AHEAD = 2
PEN_BLOCKS = 8
AUG_CONST = 2
AUG_ROWS = 16
K_SLAB = LANE
K_AUG_WIDTH = N_KV_HEADS * K_SLAB


def _key_augmentation(tm):
    assert tm % (PEN_BLOCKS * SLC_BLOCK) == 0
    pos = jnp.arange(tm, dtype=jnp.int32)
    onehot = ((pos[:, None] // SLC_BLOCK) % PEN_BLOCKS == jnp.arange(PEN_BLOCKS, dtype=jnp.int32)[None, :])
    return jnp.concatenate([jnp.ones((tm, AUG_CONST), BF16),
                            jnp.zeros((tm, AUG_ROWS - PEN_BLOCKS - AUG_CONST), BF16),
                            onehot.astype(BF16),
                            jnp.zeros((tm, K_SLAB - HEAD_DIM - AUG_ROWS), BF16)], axis=1)


def _slc_body(q_ref, k_ref, vt_ref, sel_ref, sb_ref, crow_ref, os_ref, sbuf, acc_ref):
    qis = [pl.program_id(2) * SLC_QBLOCKS + h for h in range(SLC_QBLOCKS)]
    qt = _heads_to_lanes(q_ref[...])
    dh, lanes = qt.shape
    per = KEY_CHUNK // SLC_BLOCK
    far_limits = [jnp.maximum(qi - 1, 0) * per for qi in qis]
    n_steps = (jnp.maximum(qis[-1] - 1, 0) + FAR_CHUNKS - 1) // FAR_CHUNKS
    last_step = k_ref.shape[0] // FAR_KEYS - 1
    steps_per_group = PEN_BLOCKS // FAR_BLOCKS

    c = jnp.concatenate([crow_ref[...]] * SLC_QBLOCKS, axis=1)
    c_hi = c.astype(BF16).astype(F32)
    const_rows = jnp.concatenate([c_hi, c - c_hi, jnp.zeros((AUG_ROWS - PEN_BLOCKS - AUG_CONST, lanes), F32)], axis=0)
    pad_rows = jnp.zeros((k_ref.shape[1] - dh - AUG_ROWS, lanes), BF16)
    blk_iota = lax.broadcasted_iota(jnp.int32, (PEN_BLOCKS, Q_BLOCK), 0)

    def far_logits(u):
        ua = jnp.minimum(u, last_step)
        grp0 = pl.multiple_of((ua // steps_per_group) * PEN_BLOCKS, PEN_BLOCKS)
        blk = blk_iota + (u // steps_per_group) * PEN_BLOCKS
        pens = []
        for h in range(SLC_QBLOCKS):
            pen = jnp.where((sel_ref[h, pl.ds(grp0, PEN_BLOCKS), :] > 0.5) & (blk < far_limits[h]), 0.0, NEG)
            pens += [pen] * GQA_GROUP
        qa = jnp.concatenate([qt, jnp.concatenate([const_rows, jnp.concatenate(pens, axis=1)], axis=0).astype(BF16),
                              pad_rows], axis=0)
        return _dot(k_ref[pl.ds(pl.multiple_of(ua * FAR_KEYS, FAR_KEYS), FAR_KEYS), :], qa)

    def stage(slot, u):
        s_new = far_logits(u).astype(BF16)
        sbuf[slot] = s_new
        groups = s_new.reshape(FAR_KEYS // BF16_ROWS, BF16_ROWS, lanes)
        return jnp.max(jnp.max(groups, axis=0).astype(F32), axis=0, keepdims=True)

    def far_update(m, s_ref, mx, key0):
        m_new = jnp.maximum(m, mx)
        p = jnp.exp2(s_ref[...] - m_new.astype(BF16))
        cols = pl.ds(pl.multiple_of(key0, FAR_KEYS), FAR_KEYS)
        acc_ref[...] = jnp.exp2(m - m_new) * acc_ref[...] + _dot(vt_ref[:, cols], p)
        return m_new

    def far_round(v, carry):
        m = carry[0]
        ahead = list(carry[1:])
        for slot in range(FAR_BUFFERS):
            u = FAR_BUFFERS * v + slot
            ahead.append(stage((slot + FAR_AHEAD) % FAR_BUFFERS, u + FAR_AHEAD))
            ua = jnp.minimum(u, last_step)
            m = far_update(m, sbuf.at[slot], ahead.pop(0), ua * FAR_KEYS)
        return (m, *ahead)

    ahead = [stage(u, u) for u in range(FAR_AHEAD)]
    acc_ref[...] = jnp.zeros_like(acc_ref)
    init = (jnp.full((1, lanes), NEG, F32), *ahead)
    m = lax.fori_loop(0, (n_steps + FAR_BUFFERS - 1) // FAR_BUFFERS, far_round, init)[0]
    acc = acc_ref[...]

    for h, qi in enumerate(qis):
        cols = slice(h * ROWS, (h + 1) * ROWS)
        qd = jnp.concatenate([qt[:, cols], jnp.zeros((k_ref.shape[1] - dh, ROWS), BF16)], axis=0)
        prev = jnp.maximum(qi - 1, 0)
        tiles, vts = [], []
        for chunk, kind in ((prev, jnp.where(qi >= 1, 1, 3)), (qi, 0)):
            rows = pl.ds(pl.multiple_of(chunk * KEY_CHUNK, KEY_CHUNK), KEY_CHUNK)
            s = _dot(k_ref[rows, :], qd) + sb_ref[kind]
            for j in range(per):
                srow = sel_ref[h, pl.ds(chunk * per + j, 1), :]
                srow = jnp.concatenate([srow] * GQA_GROUP, axis=1)
                tiles.append(jnp.where(srow > 0.5, s[j * SLC_BLOCK:(j + 1) * SLC_BLOCK, :], NEG))
            vts.append(vt_ref[:, rows])
        s = jnp.concatenate(tiles, axis=0)
        m_new = jnp.maximum(m[:, cols], jnp.max(s, axis=0, keepdims=True))
        p = jnp.exp2(s - m_new).astype(BF16)
        acc_h = jnp.exp2(m[:, cols] - m_new) * acc[:, cols] + _dot(jnp.concatenate(vts, axis=1), p)
        os_ref[:, cols] = acc_h[:dh] / jnp.maximum(acc_h[dh:dh + 1], 1e-30)


def _slc(qt, ks, vst, sel, sb, crow):
    b, g, dh = qt.shape[0], N_KV_HEADS, HEAD_DIM
    s_len = ks.shape[1]
    nqb = s_len // Q_BLOCK
    total = nqb * ROWS
    n_slc = sel.shape[3]
    blk_q = pl.BlockSpec((None, None, dh, SLC_QBLOCKS * ROWS), lambda i, j, q: (i, j, 0, q))
    return pl.pallas_call(
        _slc_body,
        grid=(b, g, nqb // SLC_QBLOCKS),
        in_specs=[_q_spec(SLC_QBLOCKS),
                  pl.BlockSpec((None, s_len, K_SLAB), lambda i, j, q: (i, 0, j)),
                  pl.BlockSpec((None, VT_ROWS, s_len), lambda i, j, q: (i, j, 0)),
                  pl.BlockSpec((None, None, SLC_QBLOCKS, n_slc, Q_BLOCK), lambda i, j, q: (i, j, q, 0, 0)),
                  pl.BlockSpec((None, 4, KEY_CHUNK, ROWS), lambda i, j, q: (j, 0, 0, 0)),
                  pl.BlockSpec((None, 1, ROWS), lambda i, j, q: (j, 0, 0))],
        out_specs=blk_q,
        out_shape=jax.ShapeDtypeStruct((b, g, dh, total), F32),
        scratch_shapes=[pltpu.VMEM((FAR_BUFFERS, FAR_KEYS, SLC_QBLOCKS * ROWS), BF16),
                        pltpu.VMEM((VT_ROWS, SLC_QBLOCKS * ROWS), F32)],
        compiler_params=_params(3),
        name="slc_attention",
    )(qt, ks, vst, sel, sb, crow)


WIN_CHUNKS = WINDOW // KEY_CHUNK + 1
WIN_QBLOCKS = 4


def _win_body(q_ref, k_ref, vt_ref, wb_ref, oc_ref, os_ref, gate_ref, o_ref):
    q_all = _group_slots(_heads_to_lanes(q_ref[...]), pl.program_id(1))
    for h in range(WIN_QBLOCKS):
        qi = pl.program_id(2) * WIN_QBLOCKS + h
        lanes = slice(h * ROWS, (h + 1) * ROWS)
        tokens = slice(h * Q_BLOCK, (h + 1) * Q_BLOCK)
        qt = q_all[:, lanes]
        tiles, vts = [], []
        for delta in range(WIN_CHUNKS - 1, -1, -1):
            c = jnp.maximum(qi - delta, 0)
            rows = pl.ds(pl.multiple_of(c * KEY_CHUNK, KEY_CHUNK), KEY_CHUNK)
            kind = jnp.where(qi >= delta, delta, WIN_CHUNKS)
            tiles.append(_dot(k_ref[rows, :], qt) + wb_ref[kind])
            vts.append(vt_ref[:, rows])
        s = jnp.concatenate(tiles, axis=0)
        m = jnp.max(s, axis=0, keepdims=True)
        p = jnp.exp2(s - m)
        l = jnp.sum(p, axis=0, keepdims=True)
        o_w = _dot(jnp.concatenate(vts, axis=1), p.astype(BF16)) / jnp.maximum(l, 1e-30)
        gate = lambda br: jnp.concatenate([gate_ref[br * GQA_GROUP + r:br * GQA_GROUP + r + 1, tokens]
                                           for r in range(GQA_GROUP)], axis=1)
        o = gate(0) * oc_ref[:, lanes] + gate(1) * os_ref[:, lanes] + gate(2) * o_w
        for r in range(GQA_GROUP):
            o_ref[r * HEAD_DIM:(r + 1) * HEAD_DIM, tokens] = o[:, r * Q_BLOCK:(r + 1) * Q_BLOCK].astype(o_ref.dtype)


def _win_mix(qt, kw, vwt, wb, oc, osl, gates):
    b, g, dh = qt.shape[0], N_KV_HEADS, HEAD_DIM
    s_len = kw.shape[1]
    blk_q = pl.BlockSpec((None, None, dh, WIN_QBLOCKS * ROWS), lambda i, j, q: (i, j, 0, q))
    return pl.pallas_call(
        _win_body,
        grid=(b, g, s_len // (WIN_QBLOCKS * Q_BLOCK)),
        in_specs=[_q_spec(WIN_QBLOCKS),
                  pl.BlockSpec((None, s_len, kw.shape[2]), lambda i, j, q: (i, 0, 0)),
                  pl.BlockSpec((None, dh, s_len), lambda i, j, q: (i, j, 0)),
                  pl.BlockSpec((None, WIN_CHUNKS + 1, KEY_CHUNK, ROWS), lambda i, j, q: (j, 0, 0, 0)),
                  blk_q, blk_q,
                  pl.BlockSpec((None, GATE_ROWS, WIN_QBLOCKS * Q_BLOCK), lambda i, j, q: (i, j, q))],
        out_specs=_q_spec(WIN_QBLOCKS),
        out_shape=jax.ShapeDtypeStruct(qt.shape, BF16),
        compiler_params=_params(3),
        name="window_mix",
    )(qt, kw, vwt, wb, oc, osl, gates)


def _outproj_body(a_ref, gm_ref, h_ref, wa_ref, wb_ref, g_ref, o_ref):
    y = _dot_tn(a_ref[...], wa_ref[...]) + _dot(gm_ref[...], wb_ref[...])
    o_ref[...] = h_ref[...] + _rms(y, g_ref[...])


def _outproj(a, gm, h, wa, wb, gpost, tm):
    n, d = h.shape
    per_b = a.shape[2] // tm
    full = _resident
    row = lambda w: pl.BlockSpec((tm, w), lambda i: (i, 0))
    return pl.pallas_call(
        _outproj_body,
        grid=(n // tm,),
        in_specs=[pl.BlockSpec((None, a.shape[1], tm), lambda i: (i // per_b, 0, i % per_b)),
                  row(gm.shape[1]), row(d), full(wa), full(wb), full(gpost)],
        out_specs=row(d),
        out_shape=jax.ShapeDtypeStruct((n, d), F32),
        compiler_params=_params(1),
        name="outproj",
    )(a, gm, h, wa, wb, gpost)


FFN_TILE = 256


def _ffn_body(h_ref, gpre_ref, wg_ref, wu_ref, wd_ref, gpost_ref, o_ref):
    h = h_ref[...]
    xb = _rms(h, gpre_ref[...]).astype(BF16)
    acc = jnp.zeros(h.shape, F32)
    for j in range(wg_ref.shape[1] // FFN_TILE):
        cols = slice(j * FFN_TILE, (j + 1) * FFN_TILE)
        gate = _dot(xb, wg_ref[:, cols])
        up = _dot(xb, wu_ref[:, cols])
        act = (gate * _sigmoid(gate) * up).astype(BF16)
        acc = acc + _dot(act, wd_ref[cols, :])
    o_ref[...] = h + _rms(acc, gpost_ref[...])


def _ffn(h, gpre, wg, wu, wd, gpost, tm):
    n, d = h.shape
    full = _resident
    row = pl.BlockSpec((tm, d), lambda i: (i, 0))
    return pl.pallas_call(
        _ffn_body,
        grid=(n // tm,),
        in_specs=[row, full(gpre), full(wg), full(wu), full(wd), full(gpost)],
        out_specs=row,
        out_shape=jax.ShapeDtypeStruct((n, d), F32),
        compiler_params=_params(1),
        name="ffn",
    )(h, gpre, wg, wu, wd, gpost)


def _t5_bucket(dist):
    n = jnp.maximum(dist, 0)
    max_exact = N_BUCKETS // 2
    nf = jnp.maximum(n, max_exact).astype(F32)
    large = max_exact + (jnp.log(nf / max_exact) / math.log(REL_MAX_DISTANCE / max_exact)
                         * (N_BUCKETS - max_exact)).astype(jnp.int32)
    return jnp.where(n < max_exact, n, jnp.minimum(large, N_BUCKETS - 1))


def _bias_tile(table, dist, mask):
    onehot = (_t5_bucket(dist)[..., None] == jnp.arange(N_BUCKETS, dtype=jnp.int32)).astype(F32)
    b = jnp.einsum("kqn,nh->kqh", onehot, table.astype(F32), precision=lax.Precision.HIGHEST)
    b = jnp.where(mask[..., None], b * LOG2E, NEG)
    k = dist.shape[0]
    return b.reshape(k, Q_BLOCK, N_KV_HEADS, GQA_GROUP).transpose(2, 0, 3, 1).reshape(N_KV_HEADS, k, ROWS)


def _bias_tiles(rel_bias):
    ql = jnp.arange(Q_BLOCK, dtype=jnp.int32)[None, :]
    kl = jnp.arange(KEY_CHUNK, dtype=jnp.int32)[:, None]
    chunk_dist = lambda delta: delta * KEY_CHUNK + ql - kl
    everything = jnp.ones((KEY_CHUNK, Q_BLOCK), bool)
    nothing = jnp.zeros((KEY_CHUNK, Q_BLOCK), bool)
    far = jnp.full((KEY_CHUNK, Q_BLOCK), REL_MAX_DISTANCE, jnp.int32)
    d0 = chunk_dist(0)
    sb = jnp.stack([_bias_tile(rel_bias, d0, d0 >= 0),
                    _bias_tile(rel_bias, chunk_dist(1), everything),
                    _bias_tile(rel_bias, far, everything),
                    _bias_tile(rel_bias, far, nothing)], axis=1)
    wtiles = []
    for delta in range(WIN_CHUNKS):
        d = chunk_dist(delta)
        wtiles.append(_bias_tile(rel_bias, d, (d >= 0) & (d < WINDOW)))
    wtiles.append(_bias_tile(rel_bias, far, nothing))
    wb = jnp.stack(wtiles, axis=1)
    rel = jnp.arange(CMP_WINDOW, dtype=jnp.int32)[:, None] - CMP_FRONT_PAD
    dc = ql - CMP_STRIDE * rel - (CMP_BLOCK - 1)
    cb = _bias_tile(rel_bias, dc, dc >= 0)
    crow = _bias_tile(rel_bias, far[:1], everything[:1])
    return sb, wb, cb, crow


def _mixer(h, b, s, lw, tiles, tm):
    sb, wb, cb, crow = tiles
    qt, cv, ksa, kw, vst, vwt, gates, gm = _inproj(
        h, b, (lw["gpre"], lw["wqt"], lw["wnat"], lw["wvt"], lw["wgt"], lw["wu"], lw["wv"],
               lw["lng"], lw["lnb"], lw["ws"], lw["bs"], _key_augmentation(tm)), tm)
    g, dh = N_KV_HEADS, HEAD_DIM
    n_chunks = s // CMP_STRIDE
    comp = _compress(cv, lw["cpos"], lw["cw1"], lw["cw2"])
    ncp = -(-(n_chunks + CMP_WINDOW) // CMP_CHUNK) * CMP_CHUNK
    back = ncp - CMP_FRONT_PAD - n_chunks
    comp = jnp.pad(comp, ((0, 0), (0, 0), (0, 0), (CMP_FRONT_PAD, back), (0, 0)))
    front = (jnp.arange(ncp) < CMP_FRONT_PAD).astype(F32)[:, None]
    aug = jnp.concatenate([jnp.ones((ncp, 2), F32), front, jnp.zeros((ncp, CMP_AUG_WIDTH - dh - 3), F32)], axis=1)
    kcp = jnp.concatenate([comp[:, 0], jnp.broadcast_to(aug, (b, g) + aug.shape)], axis=-1)
    vct = comp[:, 1].transpose(0, 1, 3, 2).astype(BF16)

    n_slc = s // SLC_BLOCK
    n_other = min(SLC_TOPK, n_slc) - (N_LOCAL_BLOCKS + 1)
    oc, sel = _cmp_select(qt, kcp, vct, crow, cb, n_slc, n_other)
    osl = _slc(qt, ksa.reshape(b, s, ksa.shape[1]), vst, sel, sb, crow)
    ot = _win_mix(qt, kw.reshape(b, s, kw.shape[1]), vwt, wb, oc, osl, gates)
    return _outproj(ot, gm, h, lw["wo_a"], lw["wo_b"], lw["gpost"], tm)


def _layer_weights(l, p):
    d = p["w_in"].shape[1]
    w_in = p["w_in"][l]
    o = NSA_WIDTH
    wqt = w_in[:, :o].T
    k_c, v_c, k_s, v_s, k_w, v_w = [w_in[:, o + i * KV_WIDTH:o + (i + 1) * KV_WIDTH] for i in range(6)]
    wnat = jnp.concatenate([k_c, v_c, k_s, k_w], axis=1)
    wvt = jnp.concatenate([v_s, v_w], axis=1).T
    o += 6 * KV_WIDTH
    wg = w_in[:, o:o + N_GATES].reshape(d, N_KV_HEADS, GQA_GROUP, N_BRANCHES).transpose(1, 3, 2, 0)
    wg = wg.reshape(N_KV_HEADS, N_BRANCHES * GQA_GROUP, d)
    wgt = jnp.pad(wg, ((0, 0), (0, GATE_ROWS - N_BRANCHES * GQA_GROUP), (0, 0))).reshape(N_KV_HEADS * GATE_ROWS, d)
    o += N_GATES
    gw = (w_in.shape[1] - o) // 2
    wu, wv = w_in[:, o:o + gw], w_in[:, o + gw:]
    causal = jnp.tril(jnp.ones((GMLP_CHUNK, GMLP_CHUNK), bool))
    ws = jnp.where(causal, p["gmlp_w_s"][l], 0.0)
    bs = jnp.repeat(p["gmlp_b_s"][l].T, gw // N_GMLP_GROUPS, axis=1)
    half = CMP_STRIDE * HEAD_DIM
    cpos = jnp.stack([p["cmp_pos_k"][l].reshape(2, half), p["cmp_pos_v"][l].reshape(2, half)])
    dff = p["w_down"].shape[1]
    row = lambda v: v[l].reshape(1, -1)
    bf = lambda w: w.astype(BF16)
    return dict(gpre=row(p["norm_mix_pre"]), wqt=bf(wqt), wnat=bf(wnat), wvt=bf(wvt), wgt=bf(wgt), wu=bf(wu), wv=bf(wv),
                lng=row(p["gmlp_ln_g"]), lnb=row(p["gmlp_ln_b"]), ws=bf(ws), bs=bs,
                cpos=cpos, cw1=bf(jnp.stack([p["cmp_w1_k"][l], p["cmp_w1_v"][l]])),
                cw2=bf(jnp.stack([p["cmp_w2_k"][l], p["cmp_w2_v"][l]])),
                wo_a=bf(p["w_out"][l][:NSA_WIDTH]), wo_b=bf(p["w_out"][l][NSA_WIDTH:]),
                gpost=row(p["norm_mix_post"]), fpre=row(p["norm_ffn_pre"]), fpost=row(p["norm_ffn_post"]),
                fwg=bf(p["w_gate_up"][l][:, :dff]), fwu=bf(p["w_gate_up"][l][:, dff:]), fwd=bf(p["w_down"][l]))


def _trunk(p, tm):
    x = p["x"]
    b, s, d = x.shape
    h = x.reshape(b * s, d)
    tiles = _bias_tiles(p["rel_bias"])
    for l in range(p["w_in"].shape[0]):
        lw = _layer_weights(l, p)
        h = _mixer(h, b, s, lw, tiles, tm)
        h = _ffn(h, lw["fpre"], lw["fwg"], lw["fwu"], lw["fwd"], lw["fpost"], tm)
    return h.reshape(b, s, d)


def kernel(x, rel_bias, norm_mix_pre, norm_mix_post, norm_ffn_pre, norm_ffn_post, w_in, cmp_pos_k, cmp_w1_k, cmp_w2_k, cmp_pos_v, cmp_w1_v, cmp_w2_v, gmlp_ln_g, gmlp_ln_b, gmlp_w_s, gmlp_b_s, w_out, w_gate_up, w_down):
    p = dict(x=x, rel_bias=rel_bias, norm_mix_pre=norm_mix_pre, norm_mix_post=norm_mix_post,
             norm_ffn_pre=norm_ffn_pre, norm_ffn_post=norm_ffn_post, w_in=w_in,
             cmp_pos_k=cmp_pos_k, cmp_w1_k=cmp_w1_k, cmp_w2_k=cmp_w2_k,
             cmp_pos_v=cmp_pos_v, cmp_w1_v=cmp_w1_v, cmp_w2_v=cmp_w2_v,
             gmlp_ln_g=gmlp_ln_g, gmlp_ln_b=gmlp_ln_b, gmlp_w_s=gmlp_w_s, gmlp_b_s=gmlp_b_s,
             w_out=w_out, w_gate_up=w_gate_up, w_down=w_down)
    return _trunk(p, tm=512)
```

```python
import functools
import math

import jax
import jax.numpy as jnp
import numpy as np
from jax import lax
from jax.experimental import pallas as pl
from jax.experimental.pallas import tpu as pltpu

F32 = jnp.float32
BF16 = jnp.bfloat16

N_NSA_HEADS = 8
N_KV_HEADS = 2
GQA_GROUP = N_NSA_HEADS // N_KV_HEADS
HEAD_DIM = 64
NSA_WIDTH = N_NSA_HEADS * HEAD_DIM
KV_WIDTH = N_KV_HEADS * HEAD_DIM
N_BRANCHES = 3
N_GATES = N_BRANCHES * N_NSA_HEADS
CMP_BLOCK = 32
CMP_STRIDE = 16
SLC_BLOCK = 64
SLC_TOPK = 16
N_LOCAL_BLOCKS = 2
WINDOW = 512
Q_BLOCK = 128
N_GMLP_GROUPS = 8
GMLP_CHUNK = 128
N_BUCKETS = 32
REL_MAX_DISTANCE = 128
RMS_EPS = 1e-6
LN_EPS = 1e-5

ROWS = GQA_GROUP * Q_BLOCK
KEY_CHUNK = 128
CMP_FRONT_PAD = 24
CMP_WINDOW = 32
NEG = -1e30
LANE = 128
GATE_PAD = 128
MXU_TILE = 256
BF16_ROWS = 16
VMEM_LIMIT = 48 * 1024 * 1024
LOG2E = math.log2(math.e)
Q_SCALE = HEAD_DIM ** -0.5 * LOG2E


def _dot(a, b):
    return jnp.dot(a, b, preferred_element_type=F32)


def _gelu(x):
    c = math.sqrt(2.0 / math.pi)
    return 0.5 * x * (1.0 + jnp.tanh(c * (x + 0.044715 * (x * x * x))))


def _sigmoid(x):
    return 1.0 / (1.0 + jnp.exp(-x))


def _rms(x, g):
    ms = jnp.mean(x * x, axis=-1, keepdims=True)
    return (x * lax.rsqrt(ms + RMS_EPS)) * g


def _resident(a):
    return pl.BlockSpec(a.shape, lambda i: (0,) * a.ndim, pipeline_mode=pl.Buffered(1))


def _params(n_axes):
    return pltpu.CompilerParams(dimension_semantics=("arbitrary",) * n_axes,
                                vmem_limit_bytes=VMEM_LIMIT)


def _dot_nt(a, b):
    return lax.dot_general(a, b, (((1,), (1,)), ((), ())), preferred_element_type=F32)


def _dot_tn(a, b):
    return lax.dot_general(a, b, (((0,), (0,)), ((), ())), preferred_element_type=F32)


def _inproj_body(x_ref, gpre_ref, wqt_ref, wnat_ref, wvt_ref, wgt_ref, wu_ref, wv_ref, lng_ref, lnb_ref,
                 ws_ref, bs_ref, aug_ref, qt_ref, cv_ref, ksa_ref, kw_ref, vst_ref, vwt_ref, gate_ref, gm_ref, cv_sc):
    tm = x_ref.shape[0]
    xb = _rms(x_ref[...], gpre_ref[...]).astype(BF16)
    qt_ref[...] = (_dot_nt(wqt_ref[...], xb) * Q_SCALE).astype(BF16)
    nat = _dot(xb, wnat_ref[...])
    chunks = tm // CMP_STRIDE
    left = lax.broadcasted_iota(jnp.int32, (chunks, LANE), 1) < HEAD_DIM
    for kv in range(2):
        cv_sc[kv] = nat[:, kv * KV_WIDTH:(kv + 1) * KV_WIDTH]
    for kv in range(2):
        for pair in range(CMP_STRIDE // 2):
            lanes = slice(pair * LANE, (pair + 1) * LANE)
            a = cv_sc[kv, pl.ds(2 * pair, chunks, stride=CMP_STRIDE), :]
            b = cv_sc[kv, pl.ds(2 * pair + 1, chunks, stride=CMP_STRIDE), :]
            cv_ref[kv, 0, :, lanes] = jnp.where(left, a, pltpu.roll(b, HEAD_DIM, 1))
            cv_ref[kv, 1, :, lanes] = jnp.where(left, pltpu.roll(a, HEAD_DIM, 1), b)
    k_slc = nat[:, 2 * KV_WIDTH:3 * KV_WIDTH].astype(BF16)
    ksa_ref[...] = jnp.concatenate(
        [piece for g in range(N_KV_HEADS) for piece in (k_slc[:, g * HEAD_DIM:(g + 1) * HEAD_DIM], aug_ref[...])], axis=1)
    kw_ref[...] = nat[:, 3 * KV_WIDTH:].astype(BF16)
    vt = _dot_nt(wvt_ref[...], xb).astype(BF16)
    denom = jnp.concatenate([jnp.ones((1, tm), BF16), jnp.zeros((BF16_ROWS - 1, tm), BF16)], axis=0)
    for ref, first in ((vst_ref, 0), (vwt_ref, KV_WIDTH)):
        ref[...] = jnp.concatenate([piece for g in range(N_KV_HEADS)
                                    for piece in (vt[first + g * HEAD_DIM:first + (g + 1) * HEAD_DIM], denom)], axis=0)
    gate_ref[...] = _sigmoid(_dot_nt(wgt_ref[...], xb))
    zu = _gelu(_dot(xb, wu_ref[...]))
    zv = _gelu(_dot(xb, wv_ref[...]))
    mu = jnp.mean(zv, axis=-1, keepdims=True)
    zc = zv - mu
    var = jnp.mean(zc * zc, axis=-1, keepdims=True)
    zv = ((zc * lax.rsqrt(var + LN_EPS)) * lng_ref[...] + lnb_ref[...]).astype(BF16)
    gdim = zv.shape[1] // N_GMLP_GROUPS
    left = lax.broadcasted_iota(jnp.int32, (GMLP_CHUNK, LANE), 1) < gdim
    for c in range(tm // GMLP_CHUNK):
        rows = slice(c * GMLP_CHUNK, (c + 1) * GMLP_CHUNK)
        for j in range(zv.shape[1] // LANE):
            cols = slice(j * LANE, (j + 1) * LANE)
            z = zv[rows, cols]
            sv = jnp.where(left, _dot(ws_ref[2 * j], z), _dot(ws_ref[2 * j + 1], z)) + bs_ref[:, cols]
            gm_ref[rows, cols] = (zu[rows, cols] * sv).astype(BF16)


VT_ROWS = HEAD_DIM + BF16_ROWS
GATE_ROWS = 16


def _inproj(x, b, weights, tm):
    n, d = x.shape
    s = n // b
    per_b = s // tm
    row = lambda w: pl.BlockSpec((tm, w), lambda i: (i, 0))
    col = lambda h: pl.BlockSpec((None, h, tm), lambda i: (i // per_b, 0, i % per_b))
    gw = weights[5].shape[1]
    flat = CMP_STRIDE * HEAD_DIM
    assert KV_WIDTH == LANE and N_KV_HEADS == 2
    return pl.pallas_call(
        _inproj_body,
        grid=(n // tm,),
        in_specs=[row(d)] + [_resident(a) for a in weights],
        out_specs=[col(NSA_WIDTH),
                   pl.BlockSpec((None, 2, N_KV_HEADS, tm // CMP_STRIDE, flat),
                                lambda i: (i // per_b, 0, 0, i % per_b, 0)),
                   row(K_AUG_WIDTH), row(KV_WIDTH),
                   col(N_KV_HEADS * VT_ROWS), col(N_KV_HEADS * VT_ROWS), col(N_KV_HEADS * GATE_ROWS), row(gw)],
        out_shape=[jax.ShapeDtypeStruct((b, NSA_WIDTH, s), BF16),
                   jax.ShapeDtypeStruct((b, 2, N_KV_HEADS, s // CMP_STRIDE, flat), F32),
                   jax.ShapeDtypeStruct((n, K_AUG_WIDTH), BF16),
                   jax.ShapeDtypeStruct((n, KV_WIDTH), BF16),
                   jax.ShapeDtypeStruct((b, N_KV_HEADS * VT_ROWS, s), BF16),
                   jax.ShapeDtypeStruct((b, N_KV_HEADS * VT_ROWS, s), BF16),
                   jax.ShapeDtypeStruct((b, N_KV_HEADS * GATE_ROWS, s), F32),
                   jax.ShapeDtypeStruct((n, gw), BF16)],
        scratch_shapes=[pltpu.VMEM((2, tm, KV_WIDTH), F32)],
        compiler_params=_params(1),
        name="inproj_gmlp",
    )(x, *weights)


def _compress_body(x_ref, pos_ref, w1_ref, w2_ref, o_ref):
    x = x_ref[...]
    half = x.shape[1]
    a = _dot((x + pos_ref[0:1, :]).astype(BF16), w1_ref[:half, :])
    b = _dot((x + pos_ref[1:2, :]).astype(BF16), w1_ref[half:, :])
    pre = a + pltpu.roll(b, x.shape[0] - 1, 0)
    o_ref[...] = _dot(_gelu(pre).astype(BF16), w2_ref[...])


def _compress(xc, pos, w1, w2):
    b, two, g, nch, width = xc.shape
    hid = w1.shape[2]
    dh = w2.shape[2]
    return pl.pallas_call(
        _compress_body,
        grid=(b, two, g),
        in_specs=[pl.BlockSpec((None, None, None, nch, width), lambda i, t, j: (i, t, j, 0, 0)),
                  pl.BlockSpec((None, 2, width), lambda i, t, j: (t, 0, 0)),
                  pl.BlockSpec((None, 2 * width, hid), lambda i, t, j: (t, 0, 0)),
                  pl.BlockSpec((None, hid, dh), lambda i, t, j: (t, 0, 0))],
        out_specs=pl.BlockSpec((None, None, None, nch, dh), lambda i, t, j: (i, t, j, 0, 0)),
        out_shape=jax.ShapeDtypeStruct((b, two, g, nch, dh), F32),
        compiler_params=_params(3),
        name="compress",
    )(xc, pos, w1, w2)


CMP_CHUNK = 256
CMP_AUG_WIDTH = 2 * HEAD_DIM


def _heads_to_lanes(blk):
    n = blk.shape[1] // Q_BLOCK
    return jnp.concatenate([blk[r * HEAD_DIM:(r + 1) * HEAD_DIM, h * Q_BLOCK:(h + 1) * Q_BLOCK]
                            for h in range(n) for r in range(GQA_GROUP)], axis=1)


def _group_slots(qt, g):
    return jnp.concatenate([jnp.where(g == j, qt, jnp.zeros_like(qt)) for j in range(N_KV_HEADS)], axis=0)


def _q_spec(n_blocks):
    return pl.BlockSpec((None, GQA_GROUP * HEAD_DIM, n_blocks * Q_BLOCK), lambda i, j, q: (i, j, q))


CMP_QBLOCKS = 4
SUBLANES = 8
SORT_KEEP = 16


def _bitonic_merge(xs):
    if len(xs) == 1:
        return xs
    half = len(xs) // 2
    hi = [jnp.maximum(xs[i], xs[i + half]) for i in range(half)]
    lo = [jnp.minimum(xs[i], xs[i + half]) for i in range(half)]
    return _bitonic_merge(hi) + _bitonic_merge(lo)


def _bitonic_sort(xs):
    if len(xs) == 1:
        return xs
    half = len(xs) // 2
    return _bitonic_merge(_bitonic_sort(xs[:half]) + _bitonic_sort(xs[half:])[::-1])


def _top_of_two(a, b):
    n = len(a)
    return _bitonic_merge([jnp.maximum(a[i], b[n - 1 - i]) for i in range(n)])


def _kth_largest(x, k):
    assert k <= SORT_KEEP and x.shape[0] % SUBLANES == 0
    tiles = [x[i * SUBLANES:(i + 1) * SUBLANES] for i in range(x.shape[0] // SUBLANES)]
    tiles += [jnp.full_like(tiles[0], -2.0)] * (-len(tiles) % SORT_KEEP)
    top = _bitonic_sort(tiles[:SORT_KEEP])
    for j in range(SORT_KEEP, len(tiles), SORT_KEEP):
        top = _top_of_two(top, _bitonic_sort(tiles[j:j + SORT_KEEP]))
    shift = SUBLANES // 2
    while shift:
        top = _top_of_two(top, [pltpu.roll(v, shift, 0) for v in top])
        shift //= 2
    return top[k - 1][0:1]


def _cmp_select_body(q_ref, k_ref, vt_ref, crow_ref, cb_ref, oc_ref, sel_ref, sc_ref, *, n_other):
    qis = [pl.program_id(2) * CMP_QBLOCKS + h for h in range(CMP_QBLOCKS)]
    n_slc = sel_ref.shape[1]
    qt = _heads_to_lanes(q_ref[...])
    dh, lanes = qt.shape
    w0s = [pl.multiple_of(qi * (Q_BLOCK // CMP_STRIDE), 8) for qi in qis]
    n_chunks = (w0s[-1] + CMP_WINDOW + CMP_CHUNK - 1) // CMP_CHUNK
    row_iota = lax.broadcasted_iota(jnp.int32, (CMP_CHUNK, lanes), 0)
    lane_blk = lax.broadcasted_iota(jnp.int32, (1, lanes), 1) // ROWS
    w0_lane = w0s[0]
    for h in range(1, CMP_QBLOCKS):
        w0_lane = jnp.where(lane_blk >= h, w0s[h], w0_lane)

    @pl.when(qis[0] == 0)
    def _():
        sc_ref[...] = jnp.zeros_like(sc_ref)

    c = jnp.concatenate([crow_ref[...]] * CMP_QBLOCKS, axis=1)
    c_hi = c.astype(BF16).astype(F32)
    neg_row = jnp.full((1, lanes), NEG, F32)
    zeros = lambda n, dt: jnp.zeros((n, lanes), dt)
    tail = zeros(k_ref.shape[1] - dh - BF16_ROWS, BF16)
    qa = jnp.concatenate([qt, jnp.concatenate([c_hi, c - c_hi, neg_row, zeros(BF16_ROWS - 3, F32)]).astype(BF16), tail])
    qw = jnp.concatenate([qt, jnp.concatenate([zeros(2, F32), neg_row, zeros(BF16_ROWS - 3, F32)]).astype(BF16), tail])

    def rows_of(ch):
        return pl.ds(pl.multiple_of(ch * CMP_CHUNK, CMP_CHUNK), CMP_CHUNK)

    n_slabs = lanes // Q_BLOCK
    slab = lambda i: slice(i * Q_BLOCK, (i + 1) * Q_BLOCK)

    def put(rows, val, first=0):
        for i in range(val.shape[1] // Q_BLOCK):
            sc_ref[first + i, rows, :] = val[:, slab(i)]

    def get(rows):
        return jnp.concatenate([sc_ref[i, rows, :] for i in range(n_slabs)], axis=1)

    def logits(ch, m):
        s = _dot(k_ref[rows_of(ch), :].astype(BF16), qa)
        s = jnp.where(row_iota + ch * CMP_CHUNK < w0_lane, s, NEG)
        put(rows_of(ch), s)
        return jnp.maximum(m, jnp.max(s, axis=0, keepdims=True))

    m = lax.fori_loop(0, n_chunks, logits, jnp.full((1, lanes), NEG, F32))
    win_max = []
    for h, w0 in enumerate(w0s):
        win = pl.ds(w0, CMP_WINDOW)
        s_win = _dot(k_ref[win, :].astype(BF16), qw[:, h * ROWS:(h + 1) * ROWS]) + cb_ref[...]
        put(win, s_win, first=h * GQA_GROUP)
        win_max.append(jnp.max(s_win, axis=0, keepdims=True))
    m = jnp.maximum(m, jnp.concatenate(win_max, axis=1))

    def weigh(ch, carry):
        l, acc = carry
        p = jnp.exp2(get(rows_of(ch)) - m)
        put(rows_of(ch), p)
        return l + jnp.sum(p, axis=0, keepdims=True), acc + _dot(vt_ref[:, rows_of(ch)], p.astype(BF16))

    l, acc = lax.fori_loop(0, n_chunks, weigh, (jnp.zeros((1, lanes), F32), jnp.zeros(oc_ref.shape, F32)))
    scale = jnp.where(m > 0.5 * NEG, 1.0 / jnp.maximum(l, 1e-30), 0.0)
    oc_ref[...] = acc * scale

    per = SLC_BLOCK // CMP_STRIDE
    blk = lax.broadcasted_iota(jnp.int32, (n_slc, Q_BLOCK), 0)
    blk_f = blk.astype(F32)
    lane_pos = lax.broadcasted_iota(jnp.int32, (n_slc, Q_BLOCK), 1)
    def candidates(h):
        imp = jnp.zeros((n_slc, Q_BLOCK), F32)
        for r in range(GQA_GROUP):
            i = h * GQA_GROUP + r
            part = lambda off: sc_ref[i, pl.ds(CMP_FRONT_PAD + off, n_slc, stride=per), :]
            tot = part(0)
            for k in range(1, per - 1):
                tot = tot + part(k)
            imp = imp + (tot + 0.5 * (part(per - 1) + part(-1))) * scale[:, slab(i)]
        jq = (qis[h] * Q_BLOCK + lane_pos) // SLC_BLOCK
        valid = blk <= jq
        forced = (blk == 0) | (valid & (blk > jq - N_LOCAL_BLOCKS))
        free = valid & jnp.logical_not(forced)
        return jnp.where(free, imp, -1.0), forced, free

    tied = []
    for h in range(CMP_QBLOCKS):
        work, forced, free = candidates(h)
        t = _kth_largest(work, n_other)
        picked = (work > t) | (free & (work == t))
        sel_ref[h] = jnp.where(forced | picked, 1.0, 0.0)
        tied.append(jnp.max(jnp.sum(jnp.where(picked, 1.0, 0.0), axis=0, keepdims=True)) > n_other)

    for h in range(CMP_QBLOCKS):
        @pl.when(tied[h])
        def _(h=h):
            work, forced, free = candidates(h)
            for _ in range(n_other):
                mx = jnp.max(work, axis=0, keepdims=True)
                first = jnp.min(jnp.where(work == mx, blk_f, float(n_slc)), axis=0, keepdims=True)
                work = jnp.where((blk_f == first) & (mx >= 0.0), -1.0, work)
            sel_ref[h] = jnp.where(forced | (free & (work < 0.0)), 1.0, 0.0)


def _cmp_select(qt, kcp, vct, crow, cb, n_slc, n_other):
    b, g, dh = qt.shape[0], N_KV_HEADS, HEAD_DIM
    nqb = qt.shape[2] // Q_BLOCK
    total = nqb * ROWS
    ncp = kcp.shape[2]
    blk_q = pl.BlockSpec((None, None, dh, CMP_QBLOCKS * ROWS), lambda i, j, q: (i, j, 0, q))
    return pl.pallas_call(
        functools.partial(_cmp_select_body, n_other=n_other),
        grid=(b, g, nqb // CMP_QBLOCKS),
        in_specs=[_q_spec(CMP_QBLOCKS),
                  pl.BlockSpec((None, None, ncp, kcp.shape[3]), lambda i, j, q: (i, j, 0, 0)),
                  pl.BlockSpec((None, None, dh, ncp), lambda i, j, q: (i, j, 0, 0)),
                  pl.BlockSpec((None, 1, ROWS), lambda i, j, q: (j, 0, 0)),
                  pl.BlockSpec((None, CMP_WINDOW, ROWS), lambda i, j, q: (j, 0, 0))],
        out_specs=[blk_q,
                   pl.BlockSpec((None, None, CMP_QBLOCKS, n_slc, Q_BLOCK), lambda i, j, q: (i, j, q, 0, 0))],
        out_shape=[jax.ShapeDtypeStruct((b, g, dh, total), F32),
                   jax.ShapeDtypeStruct((b, g, nqb, n_slc, Q_BLOCK), F32)],
        scratch_shapes=[pltpu.VMEM((CMP_QBLOCKS * GQA_GROUP, ncp, Q_BLOCK), F32)],
        compiler_params=_params(3),
        name="cmp_select",
    )(qt, kcp, vct, crow, cb)


SLC_QBLOCKS = 2
FAR_KEYS = 256
FAR_BLOCKS = FAR_KEYS // SLC_BLOCK
FAR_CHUNKS = FAR_KEYS // KEY_CHUNK
FAR_BUFFERS = 4
FAR_AHEAD = 2
PEN_BLOCKS = 8
AUG_CONST = 2
AUG_ROWS = 16
K_SLAB = LANE
K_AUG_WIDTH = N_KV_HEADS * K_SLAB


def _key_augmentation(tm):
    assert tm % (PEN_BLOCKS * SLC_BLOCK) == 0
    pos = jnp.arange(tm, dtype=jnp.int32)
    onehot = ((pos[:, None] // SLC_BLOCK) % PEN_BLOCKS == jnp.arange(PEN_BLOCKS, dtype=jnp.int32)[None, :])
    return jnp.concatenate([jnp.ones((tm, AUG_CONST), BF16),
                            jnp.zeros((tm, AUG_ROWS - PEN_BLOCKS - AUG_CONST), BF16),
                            onehot.astype(BF16),
                            jnp.zeros((tm, K_SLAB - HEAD_DIM - AUG_ROWS), BF16)], axis=1)


def _slc_body(q_ref, k_ref, vt_ref, sel_ref, sb_ref, crow_ref, os_ref, sbuf, acc_ref):
    qis = [pl.program_id(2) * SLC_QBLOCKS + h for h in range(SLC_QBLOCKS)]
    qt = _heads_to_lanes(q_ref[...])
    dh, lanes = qt.shape
    per = KEY_CHUNK // SLC_BLOCK
    far_limits = [jnp.maximum(qi - 1, 0) * per for qi in qis]
    n_steps = (jnp.maximum(qis[-1] - 1, 0) + FAR_CHUNKS - 1) // FAR_CHUNKS
    last_step = k_ref.shape[0] // FAR_KEYS - 1
    steps_per_group = PEN_BLOCKS // FAR_BLOCKS

    c = jnp.concatenate([crow_ref[...]] * SLC_QBLOCKS, axis=1)
    c_hi = c.astype(BF16).astype(F32)
    const_rows = jnp.concatenate([c_hi, c - c_hi, jnp.zeros((AUG_ROWS - PEN_BLOCKS - AUG_CONST, lanes), F32)], axis=0)
    pad_rows = jnp.zeros((k_ref.shape[1] - dh - AUG_ROWS, lanes), BF16)
    blk_iota = lax.broadcasted_iota(jnp.int32, (PEN_BLOCKS, Q_BLOCK), 0)

    def far_logits(u):
        ua = jnp.minimum(u, last_step)
        grp0 = pl.multiple_of((ua // steps_per_group) * PEN_BLOCKS, PEN_BLOCKS)
        blk = blk_iota + (u // steps_per_group) * PEN_BLOCKS
        pens = []
        for h in range(SLC_QBLOCKS):
            pen = jnp.where((sel_ref[h, pl.ds(grp0, PEN_BLOCKS), :] > 0.5) & (blk < far_limits[h]), 0.0, NEG)
            pens += [pen] * GQA_GROUP
        qa = jnp.concatenate([qt, jnp.concatenate([const_rows, jnp.concatenate(pens, axis=1)], axis=0).astype(BF16),
                              pad_rows], axis=0)
        return _dot(k_ref[pl.ds(pl.multiple_of(ua * FAR_KEYS, FAR_KEYS), FAR_KEYS), :], qa)

    def stage(slot, u):
        s_new = far_logits(u).astype(BF16)
        sbuf[slot] = s_new
        groups = s_new.reshape(FAR_KEYS // BF16_ROWS, BF16_ROWS, lanes)
        return jnp.max(jnp.max(groups, axis=0).astype(F32), axis=0, keepdims=True)

    def far_update(m, s_ref, mx, key0):
        m_new = jnp.maximum(m, mx)
        p = jnp.exp2(s_ref[...] - m_new.astype(BF16))
        cols = pl.ds(pl.multiple_of(key0, FAR_KEYS), FAR_KEYS)
        acc_ref[...] = jnp.exp2(m - m_new) * acc_ref[...] + _dot(vt_ref[:, cols], p)
        return m_new

    def far_round(v, carry):
        m = carry[0]
        ahead = list(carry[1:])
        for slot in range(FAR_BUFFERS):
            u = FAR_BUFFERS * v + slot
            ahead.append(stage((slot + FAR_AHEAD) % FAR_BUFFERS, u + FAR_AHEAD))
            ua = jnp.minimum(u, last_step)
            m = far_update(m, sbuf.at[slot], ahead.pop(0), ua * FAR_KEYS)
        return (m, *ahead)

    ahead = [stage(u, u) for u in range(FAR_AHEAD)]
    acc_ref[...] = jnp.zeros_like(acc_ref)
    init = (jnp.full((1, lanes), NEG, F32), *ahead)
    m = lax.fori_loop(0, (n_steps + FAR_BUFFERS - 1) // FAR_BUFFERS, far_round, init)[0]
    acc = acc_ref[...]

    for h, qi in enumerate(qis):
        cols = slice(h * ROWS, (h + 1) * ROWS)
        qd = jnp.concatenate([qt[:, cols], jnp.zeros((k_ref.shape[1] - dh, ROWS), BF16)], axis=0)
        prev = jnp.maximum(qi - 1, 0)
        tiles, vts = [], []
        for chunk, kind in ((prev, jnp.where(qi >= 1, 1, 3)), (qi, 0)):
            rows = pl.ds(pl.multiple_of(chunk * KEY_CHUNK, KEY_CHUNK), KEY_CHUNK)
            s = _dot(k_ref[rows, :], qd) + sb_ref[kind]
            for j in range(per):
                srow = sel_ref[h, pl.ds(chunk * per + j, 1), :]
                srow = jnp.concatenate([srow] * GQA_GROUP, axis=1)
                tiles.append(jnp.where(srow > 0.5, s[j * SLC_BLOCK:(j + 1) * SLC_BLOCK, :], NEG))
            vts.append(vt_ref[:, rows])
        s = jnp.concatenate(tiles, axis=0)
        m_new = jnp.maximum(m[:, cols], jnp.max(s, axis=0, keepdims=True))
        p = jnp.exp2(s - m_new).astype(BF16)
        acc_h = jnp.exp2(m[:, cols] - m_new) * acc[:, cols] + _dot(jnp.concatenate(vts, axis=1), p)
        os_ref[:, cols] = acc_h[:dh] / jnp.maximum(acc_h[dh:dh + 1], 1e-30)


def _slc(qt, ks, vst, sel, sb, crow):
    b, g, dh = qt.shape[0], N_KV_HEADS, HEAD_DIM
    s_len = ks.shape[1]
    nqb = s_len // Q_BLOCK
    total = nqb * ROWS
    n_slc = sel.shape[3]
    blk_q = pl.BlockSpec((None, None, dh, SLC_QBLOCKS * ROWS), lambda i, j, q: (i, j, 0, q))
    return pl.pallas_call(
        _slc_body,
        grid=(b, g, nqb // SLC_QBLOCKS),
        in_specs=[_q_spec(SLC_QBLOCKS),
                  pl.BlockSpec((None, s_len, K_SLAB), lambda i, j, q: (i, 0, j)),
                  pl.BlockSpec((None, VT_ROWS, s_len), lambda i, j, q: (i, j, 0)),
                  pl.BlockSpec((None, None, SLC_QBLOCKS, n_slc, Q_BLOCK), lambda i, j, q: (i, j, q, 0, 0)),
                  pl.BlockSpec((None, 4, KEY_CHUNK, ROWS), lambda i, j, q: (j, 0, 0, 0)),
                  pl.BlockSpec((None, 1, ROWS), lambda i, j, q: (j, 0, 0))],
        out_specs=blk_q,
        out_shape=jax.ShapeDtypeStruct((b, g, dh, total), F32),
        scratch_shapes=[pltpu.VMEM((FAR_BUFFERS, FAR_KEYS, SLC_QBLOCKS * ROWS), BF16),
                        pltpu.VMEM((VT_ROWS, SLC_QBLOCKS * ROWS), F32)],
        compiler_params=_params(3),
        name="slc_attention",
    )(qt, ks, vst, sel, sb, crow)


WIN_CHUNKS = WINDOW // KEY_CHUNK + 1
WIN_QBLOCKS = 4


def _win_body(q_ref, k_ref, vt_ref, wb_ref, oc_ref, os_ref, gate_ref, o_ref):
    q_all = _group_slots(_heads_to_lanes(q_ref[...]), pl.program_id(1))
    for h in range(WIN_QBLOCKS):
        qi = pl.program_id(2) * WIN_QBLOCKS + h
        lanes = slice(h * ROWS, (h + 1) * ROWS)
        tokens = slice(h * Q_BLOCK, (h + 1) * Q_BLOCK)
        qt = q_all[:, lanes]
        tiles, vts = [], []
        for delta in range(WIN_CHUNKS - 1, -1, -1):
            c = jnp.maximum(qi - delta, 0)
            rows = pl.ds(pl.multiple_of(c * KEY_CHUNK, KEY_CHUNK), KEY_CHUNK)
            kind = jnp.where(qi >= delta, delta, WIN_CHUNKS)
            tiles.append((_dot(k_ref[rows, :], qt) + wb_ref[kind]).astype(BF16))
            vts.append(vt_ref[:, rows])
        s = jnp.concatenate(tiles, axis=0)
        groups = s.reshape(s.shape[0] // BF16_ROWS, BF16_ROWS, ROWS)
        m = jnp.max(jnp.max(groups, axis=0).astype(F32), axis=0, keepdims=True)
        pv = _dot(jnp.concatenate(vts, axis=1), jnp.exp2(s - m.astype(BF16)))
        o_w = pv[:HEAD_DIM] / jnp.maximum(pv[HEAD_DIM:HEAD_DIM + 1], 1e-30)
        gate = lambda br: jnp.concatenate([gate_ref[br * GQA_GROUP + r:br * GQA_GROUP + r + 1, tokens]
                                           for r in range(GQA_GROUP)], axis=1)
        o = gate(0) * oc_ref[:, lanes] + gate(1) * os_ref[:, lanes] + gate(2) * o_w
        for r in range(GQA_GROUP):
            o_ref[r * HEAD_DIM:(r + 1) * HEAD_DIM, tokens] = o[:, r * Q_BLOCK:(r + 1) * Q_BLOCK].astype(o_ref.dtype)


def _win_mix(qt, kw, vwt, wb, oc, osl, gates):
    b, g, dh = qt.shape[0], N_KV_HEADS, HEAD_DIM
    s_len = kw.shape[1]
    blk_q = pl.BlockSpec((None, None, dh, WIN_QBLOCKS * ROWS), lambda i, j, q: (i, j, 0, q))
    return pl.pallas_call(
        _win_body,
        grid=(b, g, s_len // (WIN_QBLOCKS * Q_BLOCK)),
        in_specs=[_q_spec(WIN_QBLOCKS),
                  pl.BlockSpec((None, s_len, kw.shape[2]), lambda i, j, q: (i, 0, 0)),
                  pl.BlockSpec((None, VT_ROWS, s_len), lambda i, j, q: (i, j, 0)),
                  pl.BlockSpec((None, WIN_CHUNKS + 1, KEY_CHUNK, ROWS), lambda i, j, q: (j, 0, 0, 0)),
                  blk_q, blk_q,
                  pl.BlockSpec((None, GATE_ROWS, WIN_QBLOCKS * Q_BLOCK), lambda i, j, q: (i, j, q))],
        out_specs=_q_spec(WIN_QBLOCKS),
        out_shape=jax.ShapeDtypeStruct(qt.shape, BF16),
        compiler_params=_params(3),
        name="window_mix",
    )(qt, kw, vwt, wb, oc, osl, gates)


def _outproj_body(a_ref, gm_ref, h_ref, wa_ref, wb_ref, g_ref, o_ref):
    y = _dot_tn(a_ref[...], wa_ref[...]) + _dot(gm_ref[...], wb_ref[...])
    o_ref[...] = h_ref[...] + _rms(y, g_ref[...])


def _outproj(a, gm, h, wa, wb, gpost, tm):
    n, d = h.shape
    per_b = a.shape[2] // tm
    full = _resident
    row = lambda w: pl.BlockSpec((tm, w), lambda i: (i, 0))
    return pl.pallas_call(
        _outproj_body,
        grid=(n // tm,),
        in_specs=[pl.BlockSpec((None, a.shape[1], tm), lambda i: (i // per_b, 0, i % per_b)),
                  row(gm.shape[1]), row(d), full(wa), full(wb), full(gpost)],
        out_specs=row(d),
        out_shape=jax.ShapeDtypeStruct((n, d), F32),
        compiler_params=_params(1),
        name="outproj",
    )(a, gm, h, wa, wb, gpost)


FFN_TILE = 256


def _ffn_body(h_ref, gpre_ref, wg_ref, wu_ref, wd_ref, gpost_ref, o_ref):
    h = h_ref[...]
    xb = _rms(h, gpre_ref[...]).astype(BF16)
    acc = jnp.zeros(h.shape, F32)
    for j in range(wg_ref.shape[1] // FFN_TILE):
        cols = slice(j * FFN_TILE, (j + 1) * FFN_TILE)
        gate = _dot(xb, wg_ref[:, cols])
        up = _dot(xb, wu_ref[:, cols])
        act = (gate * _sigmoid(gate) * up).astype(BF16)
        acc = acc + _dot(act, wd_ref[cols, :])
    o_ref[...] = h + _rms(acc, gpost_ref[...])


def _ffn(h, gpre, wg, wu, wd, gpost, tm):
    n, d = h.shape
    full = _resident
    row = pl.BlockSpec((tm, d), lambda i: (i, 0))
    return pl.pallas_call(
        _ffn_body,
        grid=(n // tm,),
        in_specs=[row, full(gpre), full(wg), full(wu), full(wd), full(gpost)],
        out_specs=row,
        out_shape=jax.ShapeDtypeStruct((n, d), F32),
        compiler_params=_params(1),
        name="ffn",
    )(h, gpre, wg, wu, wd, gpost)


def _t5_bucket(dist):
    n = jnp.maximum(dist, 0)
    max_exact = N_BUCKETS // 2
    nf = jnp.maximum(n, max_exact).astype(F32)
    large = max_exact + (jnp.log(nf / max_exact) / math.log(REL_MAX_DISTANCE / max_exact)
                         * (N_BUCKETS - max_exact)).astype(jnp.int32)
    return jnp.where(n < max_exact, n, jnp.minimum(large, N_BUCKETS - 1))


def _bias_tile(table, dist, mask):
    onehot = (_t5_bucket(dist)[..., None] == jnp.arange(N_BUCKETS, dtype=jnp.int32)).astype(F32)
    b = jnp.einsum("kqn,nh->kqh", onehot, table.astype(F32), precision=lax.Precision.HIGHEST)
    b = jnp.where(mask[..., None], b * LOG2E, NEG)
    k = dist.shape[0]
    return b.reshape(k, Q_BLOCK, N_KV_HEADS, GQA_GROUP).transpose(2, 0, 3, 1).reshape(N_KV_HEADS, k, ROWS)


def _bias_tiles(rel_bias):
    ql = jnp.arange(Q_BLOCK, dtype=jnp.int32)[None, :]
    kl = jnp.arange(KEY_CHUNK, dtype=jnp.int32)[:, None]
    chunk_dist = lambda delta: delta * KEY_CHUNK + ql - kl
    everything = jnp.ones((KEY_CHUNK, Q_BLOCK), bool)
    nothing = jnp.zeros((KEY_CHUNK, Q_BLOCK), bool)
    far = jnp.full((KEY_CHUNK, Q_BLOCK), REL_MAX_DISTANCE, jnp.int32)
    d0 = chunk_dist(0)
    sb = jnp.stack([_bias_tile(rel_bias, d0, d0 >= 0),
                    _bias_tile(rel_bias, chunk_dist(1), everything),
                    _bias_tile(rel_bias, far, everything),
                    _bias_tile(rel_bias, far, nothing)], axis=1)
    wtiles = []
    for delta in range(WIN_CHUNKS):
        d = chunk_dist(delta)
        wtiles.append(_bias_tile(rel_bias, d, (d >= 0) & (d < WINDOW)))
    wtiles.append(_bias_tile(rel_bias, far, nothing))
    wb = jnp.stack(wtiles, axis=1)
    rel = jnp.arange(CMP_WINDOW, dtype=jnp.int32)[:, None] - CMP_FRONT_PAD
    dc = ql - CMP_STRIDE * rel - (CMP_BLOCK - 1)
    cb = _bias_tile(rel_bias, dc, dc >= 0)
    crow = _bias_tile(rel_bias, far[:1], everything[:1])
    return sb, wb, cb, crow


def _mixer(h, b, s, lw, tiles, tm):
    sb, wb, cb, crow = tiles
    qt, cv, ksa, kw, vst, vwt, gates, gm = _inproj(
        h, b, (lw["gpre"], lw["wqt"], lw["wnat"], lw["wvt"], lw["wgt"], lw["wu"], lw["wv"],
               lw["lng"], lw["lnb"], lw["ws"], lw["bs"], _key_augmentation(tm)), tm)
    g, dh = N_KV_HEADS, HEAD_DIM
    n_chunks = s // CMP_STRIDE
    comp = _compress(cv, lw["cpos"], lw["cw1"], lw["cw2"])
    ncp = -(-(n_chunks + CMP_WINDOW) // CMP_CHUNK) * CMP_CHUNK
    back = ncp - CMP_FRONT_PAD - n_chunks
    comp = jnp.pad(comp, ((0, 0), (0, 0), (0, 0), (CMP_FRONT_PAD, back), (0, 0)))
    front = (jnp.arange(ncp) < CMP_FRONT_PAD).astype(F32)[:, None]
    aug = jnp.concatenate([jnp.ones((ncp, 2), F32), front, jnp.zeros((ncp, CMP_AUG_WIDTH - dh - 3), F32)], axis=1)
    kcp = jnp.concatenate([comp[:, 0], jnp.broadcast_to(aug, (b, g) + aug.shape)], axis=-1)
    vct = comp[:, 1].transpose(0, 1, 3, 2).astype(BF16)

    n_slc = s // SLC_BLOCK
    n_other = min(SLC_TOPK, n_slc) - (N_LOCAL_BLOCKS + 1)
    oc, sel = _cmp_select(qt, kcp, vct, crow, cb, n_slc, n_other)
    osl = _slc(qt, ksa.reshape(b, s, ksa.shape[1]), vst, sel, sb, crow)
    ot = _win_mix(qt, kw.reshape(b, s, kw.shape[1]), vwt, wb, oc, osl, gates)
    return _outproj(ot, gm, h, lw["wo_a"], lw["wo_b"], lw["gpost"], tm)


def _layer_weights(l, p):
    d = p["w_in"].shape[1]
    w_in = p["w_in"][l]
    o = NSA_WIDTH
    wqt = w_in[:, :o].T
    k_c, v_c, k_s, v_s, k_w, v_w = [w_in[:, o + i * KV_WIDTH:o + (i + 1) * KV_WIDTH] for i in range(6)]
    wnat = jnp.concatenate([k_c, v_c, k_s, k_w], axis=1)
    wvt = jnp.concatenate([v_s, v_w], axis=1).T
    o += 6 * KV_WIDTH
    wg = w_in[:, o:o + N_GATES].reshape(d, N_KV_HEADS, GQA_GROUP, N_BRANCHES).transpose(1, 3, 2, 0)
    wg = wg.reshape(N_KV_HEADS, N_BRANCHES * GQA_GROUP, d)
    wgt = jnp.pad(wg, ((0, 0), (0, GATE_ROWS - N_BRANCHES * GQA_GROUP), (0, 0))).reshape(N_KV_HEADS * GATE_ROWS, d)
    o += N_GATES
    gw = (w_in.shape[1] - o) // 2
    wu, wv = w_in[:, o:o + gw], w_in[:, o + gw:]
    causal = jnp.tril(jnp.ones((GMLP_CHUNK, GMLP_CHUNK), bool))
    ws = jnp.where(causal, p["gmlp_w_s"][l], 0.0)
    bs = jnp.repeat(p["gmlp_b_s"][l].T, gw // N_GMLP_GROUPS, axis=1)
    half = CMP_STRIDE * HEAD_DIM
    cpos = jnp.stack([p["cmp_pos_k"][l].reshape(2, half), p["cmp_pos_v"][l].reshape(2, half)])
    dff = p["w_down"].shape[1]
    row = lambda v: v[l].reshape(1, -1)
    bf = lambda w: w.astype(BF16)
    return dict(gpre=row(p["norm_mix_pre"]), wqt=bf(wqt), wnat=bf(wnat), wvt=bf(wvt), wgt=bf(wgt), wu=bf(wu), wv=bf(wv),
                lng=row(p["gmlp_ln_g"]), lnb=row(p["gmlp_ln_b"]), ws=bf(ws), bs=bs,
                cpos=cpos, cw1=bf(jnp.stack([p["cmp_w1_k"][l], p["cmp_w1_v"][l]])),
                cw2=bf(jnp.stack([p["cmp_w2_k"][l], p["cmp_w2_v"][l]])),
                wo_a=bf(p["w_out"][l][:NSA_WIDTH]), wo_b=bf(p["w_out"][l][NSA_WIDTH:]),
                gpost=row(p["norm_mix_post"]), fpre=row(p["norm_ffn_pre"]), fpost=row(p["norm_ffn_post"]),
                fwg=bf(p["w_gate_up"][l][:, :dff]), fwu=bf(p["w_gate_up"][l][:, dff:]), fwd=bf(p["w_down"][l]))


def _trunk(p, tm):
    x = p["x"]
    b, s, d = x.shape
    h = x.reshape(b * s, d)
    tiles = _bias_tiles(p["rel_bias"])
    for l in range(p["w_in"].shape[0]):
        lw = _layer_weights(l, p)
        h = _mixer(h, b, s, lw, tiles, tm)
        h = _ffn(h, lw["fpre"], lw["fwg"], lw["fwu"], lw["fwd"], lw["fpost"], tm)
    return h.reshape(b, s, d)


def kernel(x, rel_bias, norm_mix_pre, norm_mix_post, norm_ffn_pre, norm_ffn_post, w_in, cmp_pos_k, cmp_w1_k, cmp_w2_k, cmp_pos_v, cmp_w1_v, cmp_w2_v, gmlp_ln_g, gmlp_ln_b, gmlp_w_s, gmlp_b_s, w_out, w_gate_up, w_down):
    p = dict(x=x, rel_bias=rel_bias, norm_mix_pre=norm_mix_pre, norm_mix_post=norm_mix_post,
             norm_ffn_pre=norm_ffn_pre, norm_ffn_post=norm_ffn_post, w_in=w_in,
             cmp_pos_k=cmp_pos_k, cmp_w1_k=cmp_w1_k, cmp_w2_k=cmp_w2_k,
             cmp_pos_v=cmp_pos_v, cmp_w1_v=cmp_w1_v, cmp_w2_v=cmp_w2_v,
             gmlp_ln_g=gmlp_ln_g, gmlp_ln_b=gmlp_ln_b, gmlp_w_s=gmlp_w_s, gmlp_b_s=gmlp_b_s,
             w_out=w_out, w_gate_up=w_gate_up, w_down=w_down)
    return _trunk(p, tm=512)
```

```python
import functools
import math

import jax
import jax.numpy as jnp
import numpy as np
from jax import lax
from jax.experimental import pallas as pl
from jax.experimental.pallas import tpu as pltpu

F32 = jnp.float32
BF16 = jnp.bfloat16

N_NSA_HEADS = 8
N_KV_HEADS = 2
GQA_GROUP = N_NSA_HEADS // N_KV_HEADS
HEAD_DIM = 64
NSA_WIDTH = N_NSA_HEADS * HEAD_DIM
KV_WIDTH = N_KV_HEADS * HEAD_DIM
N_BRANCHES = 3
N_GATES = N_BRANCHES * N_NSA_HEADS
CMP_BLOCK = 32
CMP_STRIDE = 16
SLC_BLOCK = 64
SLC_TOPK = 16
N_LOCAL_BLOCKS = 2
WINDOW = 512
Q_BLOCK = 128
N_GMLP_GROUPS = 8
GMLP_CHUNK = 128
N_BUCKETS = 32
REL_MAX_DISTANCE = 128
RMS_EPS = 1e-6
LN_EPS = 1e-5

ROWS = GQA_GROUP * Q_BLOCK
KEY_CHUNK = 128
CMP_FRONT_PAD = 24
CMP_WINDOW = 32
NEG = -1e30
LANE = 128
GATE_PAD = 128
MXU_TILE = 256
BF16_ROWS = 16
VMEM_LIMIT = 48 * 1024 * 1024
LOG2E = math.log2(math.e)
Q_SCALE = HEAD_DIM ** -0.5 * LOG2E


def _dot(a, b):
    return jnp.dot(a, b, preferred_element_type=F32)


def _gelu(x):
    c = math.sqrt(2.0 / math.pi)
    return 0.5 * x * (1.0 + jnp.tanh(c * (x + 0.044715 * (x * x * x))))


def _sigmoid(x):
    return 1.0 / (1.0 + jnp.exp(-x))


def _rms(x, g):
    ms = jnp.mean(x * x, axis=-1, keepdims=True)
    return (x * lax.rsqrt(ms + RMS_EPS)) * g


def _resident(a):
    return pl.BlockSpec(a.shape, lambda i: (0,) * a.ndim, pipeline_mode=pl.Buffered(1))


def _params(n_axes):
    return pltpu.CompilerParams(dimension_semantics=("arbitrary",) * n_axes,
                                vmem_limit_bytes=VMEM_LIMIT)


def _dot_nt(a, b):
    return lax.dot_general(a, b, (((1,), (1,)), ((), ())), preferred_element_type=F32)


def _dot_tn(a, b):
    return lax.dot_general(a, b, (((0,), (0,)), ((), ())), preferred_element_type=F32)


def _inproj_body(x_ref, gpre_ref, wqt_ref, wnat_ref, wvt_ref, wgt_ref, wu_ref, wv_ref, lng_ref, lnb_ref,
                 ws_ref, bs_ref, aug_ref, qt_ref, cv_ref, ksa_ref, kw_ref, vst_ref, vwt_ref, gate_ref, gm_ref, cv_sc):
    tm = x_ref.shape[0]
    xb = _rms(x_ref[...], gpre_ref[...]).astype(BF16)
    qt_ref[...] = (_dot_nt(wqt_ref[...], xb) * Q_SCALE).astype(BF16)
    nat = _dot(xb, wnat_ref[...])
    chunks = tm // CMP_STRIDE
    left = lax.broadcasted_iota(jnp.int32, (chunks, LANE), 1) < HEAD_DIM
    for kv in range(2):
        cv_sc[kv] = nat[:, kv * KV_WIDTH:(kv + 1) * KV_WIDTH]
    for kv in range(2):
        for pair in range(CMP_STRIDE // 2):
            lanes = slice(pair * LANE, (pair + 1) * LANE)
            a = cv_sc[kv, pl.ds(2 * pair, chunks, stride=CMP_STRIDE), :]
            b = cv_sc[kv, pl.ds(2 * pair + 1, chunks, stride=CMP_STRIDE), :]
            cv_ref[kv, 0, :, lanes] = jnp.where(left, a, pltpu.roll(b, HEAD_DIM, 1))
            cv_ref[kv, 1, :, lanes] = jnp.where(left, pltpu.roll(a, HEAD_DIM, 1), b)
    k_slc = nat[:, 2 * KV_WIDTH:3 * KV_WIDTH].astype(BF16)
    ksa_ref[...] = jnp.concatenate(
        [piece for g in range(N_KV_HEADS) for piece in (k_slc[:, g * HEAD_DIM:(g + 1) * HEAD_DIM], aug_ref[...])], axis=1)
    kw_ref[...] = nat[:, 3 * KV_WIDTH:].astype(BF16)
    vt = _dot_nt(wvt_ref[...], xb).astype(BF16)
    denom = jnp.concatenate([jnp.ones((1, tm), BF16), jnp.zeros((BF16_ROWS - 1, tm), BF16)], axis=0)
    vst_ref[...] = jnp.concatenate(
        [piece for g in range(N_KV_HEADS) for piece in (vt[g * HEAD_DIM:(g + 1) * HEAD_DIM], denom)], axis=0)
    vwt_ref[...] = vt[KV_WIDTH:]
    gate_ref[...] = _sigmoid(_dot_nt(wgt_ref[...], xb))
    zu = _gelu(_dot(xb, wu_ref[...]))
    zv = _gelu(_dot(xb, wv_ref[...]))
    mu = jnp.mean(zv, axis=-1, keepdims=True)
    zc = zv - mu
    var = jnp.mean(zc * zc, axis=-1, keepdims=True)
    zv = ((zc * lax.rsqrt(var + LN_EPS)) * lng_ref[...] + lnb_ref[...]).astype(BF16)
    gdim = zv.shape[1] // N_GMLP_GROUPS
    left = lax.broadcasted_iota(jnp.int32, (GMLP_CHUNK, LANE), 1) < gdim
    for c in range(tm // GMLP_CHUNK):
        rows = slice(c * GMLP_CHUNK, (c + 1) * GMLP_CHUNK)
        for j in range(zv.shape[1] // LANE):
            cols = slice(j * LANE, (j + 1) * LANE)
            z = zv[rows, cols]
            sv = jnp.where(left, _dot(ws_ref[2 * j], z), _dot(ws_ref[2 * j + 1], z)) + bs_ref[:, cols]
            gm_ref[rows, cols] = (zu[rows, cols] * sv).astype(BF16)


VT_ROWS = HEAD_DIM + BF16_ROWS
GATE_ROWS = 16


def _inproj(x, b, weights, tm):
    n, d = x.shape
    s = n // b
    per_b = s // tm
    row = lambda w: pl.BlockSpec((tm, w), lambda i: (i, 0))
    col = lambda h: pl.BlockSpec((None, h, tm), lambda i: (i // per_b, 0, i % per_b))
    gw = weights[5].shape[1]
    flat = CMP_STRIDE * HEAD_DIM
    assert KV_WIDTH == LANE and N_KV_HEADS == 2
    return pl.pallas_call(
        _inproj_body,
        grid=(n // tm,),
        in_specs=[row(d)] + [_resident(a) for a in weights],
        out_specs=[col(NSA_WIDTH),
                   pl.BlockSpec((None, 2, N_KV_HEADS, tm // CMP_STRIDE, flat),
                                lambda i: (i // per_b, 0, 0, i % per_b, 0)),
                   row(K_AUG_WIDTH), row(KV_WIDTH),
                   col(N_KV_HEADS * VT_ROWS), col(KV_WIDTH), col(N_KV_HEADS * GATE_ROWS), row(gw)],
        out_shape=[jax.ShapeDtypeStruct((b, NSA_WIDTH, s), BF16),
                   jax.ShapeDtypeStruct((b, 2, N_KV_HEADS, s // CMP_STRIDE, flat), F32),
                   jax.ShapeDtypeStruct((n, K_AUG_WIDTH), BF16),
                   jax.ShapeDtypeStruct((n, KV_WIDTH), BF16),
                   jax.ShapeDtypeStruct((b, N_KV_HEADS * VT_ROWS, s), BF16),
                   jax.ShapeDtypeStruct((b, KV_WIDTH, s), BF16),
                   jax.ShapeDtypeStruct((b, N_KV_HEADS * GATE_ROWS, s), F32),
                   jax.ShapeDtypeStruct((n, gw), BF16)],
        scratch_shapes=[pltpu.VMEM((2, tm, KV_WIDTH), F32)],
        compiler_params=_params(1),
        name="inproj_gmlp",
    )(x, *weights)


def _compress_body(x_ref, pos_ref, w1_ref, w2_ref, o_ref):
    x = x_ref[...]
    half = x.shape[1]
    a = _dot((x + pos_ref[0:1, :]).astype(BF16), w1_ref[:half, :])
    b = _dot((x + pos_ref[1:2, :]).astype(BF16), w1_ref[half:, :])
    pre = a + pltpu.roll(b, x.shape[0] - 1, 0)
    o_ref[...] = _dot(_gelu(pre).astype(BF16), w2_ref[...])


def _compress(xc, pos, w1, w2):
    b, two, g, nch, width = xc.shape
    hid = w1.shape[2]
    dh = w2.shape[2]
    return pl.pallas_call(
        _compress_body,
        grid=(b, two, g),
        in_specs=[pl.BlockSpec((None, None, None, nch, width), lambda i, t, j: (i, t, j, 0, 0)),
                  pl.BlockSpec((None, 2, width), lambda i, t, j: (t, 0, 0)),
                  pl.BlockSpec((None, 2 * width, hid), lambda i, t, j: (t, 0, 0)),
                  pl.BlockSpec((None, hid, dh), lambda i, t, j: (t, 0, 0))],
        out_specs=pl.BlockSpec((None, None, None, nch, dh), lambda i, t, j: (i, t, j, 0, 0)),
        out_shape=jax.ShapeDtypeStruct((b, two, g, nch, dh), F32),
        compiler_params=_params(3),
        name="compress",
    )(xc, pos, w1, w2)


CMP_CHUNK = 256
CMP_AUG_WIDTH = 2 * HEAD_DIM


def _heads_to_lanes(blk):
    n = blk.shape[1] // Q_BLOCK
    return jnp.concatenate([blk[r * HEAD_DIM:(r + 1) * HEAD_DIM, h * Q_BLOCK:(h + 1) * Q_BLOCK]
                            for h in range(n) for r in range(GQA_GROUP)], axis=1)


def _group_slots(qt, g):
    return jnp.concatenate([jnp.where(g == j, qt, jnp.zeros_like(qt)) for j in range(N_KV_HEADS)], axis=0)


def _q_spec(n_blocks):
    return pl.BlockSpec((None, GQA_GROUP * HEAD_DIM, n_blocks * Q_BLOCK), lambda i, j, q: (i, j, q))


CMP_QBLOCKS = 4
SUBLANES = 8
SORT_KEEP = 16


def _bitonic_merge(xs):
    if len(xs) == 1:
        return xs
    half = len(xs) // 2
    hi = [jnp.maximum(xs[i], xs[i + half]) for i in range(half)]
    lo = [jnp.minimum(xs[i], xs[i + half]) for i in range(half)]
    return _bitonic_merge(hi) + _bitonic_merge(lo)


def _bitonic_sort(xs):
    if len(xs) == 1:
        return xs
    half = len(xs) // 2
    return _bitonic_merge(_bitonic_sort(xs[:half]) + _bitonic_sort(xs[half:])[::-1])


def _top_of_two(a, b):
    n = len(a)
    return _bitonic_merge([jnp.maximum(a[i], b[n - 1 - i]) for i in range(n)])


def _kth_largest(x, k):
    assert k <= SORT_KEEP and x.shape[0] % SUBLANES == 0
    tiles = [x[i * SUBLANES:(i + 1) * SUBLANES] for i in range(x.shape[0] // SUBLANES)]
    tiles += [jnp.full_like(tiles[0], -2.0)] * (-len(tiles) % SORT_KEEP)
    top = _bitonic_sort(tiles[:SORT_KEEP])
    for j in range(SORT_KEEP, len(tiles), SORT_KEEP):
        top = _top_of_two(top, _bitonic_sort(tiles[j:j + SORT_KEEP]))
    shift = SUBLANES // 2
    while shift:
        top = _top_of_two(top, [pltpu.roll(v, shift, 0) for v in top])
        shift //= 2
    return top[k - 1][0:1]


def _cmp_select_body(q_ref, k_ref, vt_ref, crow_ref, cb_ref, oc_ref, sel_ref, sc_ref, *, n_other):
    qis = [pl.program_id(2) * CMP_QBLOCKS + h for h in range(CMP_QBLOCKS)]
    n_slc = sel_ref.shape[1]
    qt = _heads_to_lanes(q_ref[...])
    dh, lanes = qt.shape
    w0s = [pl.multiple_of(qi * (Q_BLOCK // CMP_STRIDE), 8) for qi in qis]
    n_chunks = (w0s[-1] + CMP_WINDOW + CMP_CHUNK - 1) // CMP_CHUNK
    row_iota = lax.broadcasted_iota(jnp.int32, (CMP_CHUNK, lanes), 0)
    lane_blk = lax.broadcasted_iota(jnp.int32, (1, lanes), 1) // ROWS
    w0_lane = w0s[0]
    for h in range(1, CMP_QBLOCKS):
        w0_lane = jnp.where(lane_blk >= h, w0s[h], w0_lane)

    @pl.when(qis[0] == 0)
    def _():
        sc_ref[...] = jnp.zeros_like(sc_ref)

    c = jnp.concatenate([crow_ref[...]] * CMP_QBLOCKS, axis=1)
    c_hi = c.astype(BF16).astype(F32)
    neg_row = jnp.full((1, lanes), NEG, F32)
    zeros = lambda n, dt: jnp.zeros((n, lanes), dt)
    tail = zeros(k_ref.shape[1] - dh - BF16_ROWS, BF16)
    qa = jnp.concatenate([qt, jnp.concatenate([c_hi, c - c_hi, neg_row, zeros(BF16_ROWS - 3, F32)]).astype(BF16), tail])
    qw = jnp.concatenate([qt, jnp.concatenate([zeros(2, F32), neg_row, zeros(BF16_ROWS - 3, F32)]).astype(BF16), tail])

    def rows_of(ch):
        return pl.ds(pl.multiple_of(ch * CMP_CHUNK, CMP_CHUNK), CMP_CHUNK)

    n_slabs = lanes // Q_BLOCK
    slab = lambda i: slice(i * Q_BLOCK, (i + 1) * Q_BLOCK)

    def put(rows, val, first=0):
        for i in range(val.shape[1] // Q_BLOCK):
            sc_ref[first + i, rows, :] = val[:, slab(i)]

    def get(rows):
        return jnp.concatenate([sc_ref[i, rows, :] for i in range(n_slabs)], axis=1)

    def logits(ch, m):
        s = _dot(k_ref[rows_of(ch), :].astype(BF16), qa)
        s = jnp.where(row_iota + ch * CMP_CHUNK < w0_lane, s, NEG)
        put(rows_of(ch), s)
        return jnp.maximum(m, jnp.max(s, axis=0, keepdims=True))

    m = lax.fori_loop(0, n_chunks, logits, jnp.full((1, lanes), NEG, F32))
    win_max = []
    for h, w0 in enumerate(w0s):
        win = pl.ds(w0, CMP_WINDOW)
        s_win = _dot(k_ref[win, :].astype(BF16), qw[:, h * ROWS:(h + 1) * ROWS]) + cb_ref[...]
        put(win, s_win, first=h * GQA_GROUP)
        win_max.append(jnp.max(s_win, axis=0, keepdims=True))
    m = jnp.maximum(m, jnp.concatenate(win_max, axis=1))

    def weigh(ch, carry):
        l, acc = carry
        p = jnp.exp2(get(rows_of(ch)) - m)
        put(rows_of(ch), p)
        return l + jnp.sum(p, axis=0, keepdims=True), acc + _dot(vt_ref[:, rows_of(ch)], p.astype(BF16))

    l, acc = lax.fori_loop(0, n_chunks, weigh, (jnp.zeros((1, lanes), F32), jnp.zeros(oc_ref.shape, F32)))
    scale = jnp.where(m > 0.5 * NEG, 1.0 / jnp.maximum(l, 1e-30), 0.0)
    oc_ref[...] = acc * scale

    per = SLC_BLOCK // CMP_STRIDE
    blk = lax.broadcasted_iota(jnp.int32, (n_slc, Q_BLOCK), 0)
    blk_f = blk.astype(F32)
    lane_pos = lax.broadcasted_iota(jnp.int32, (n_slc, Q_BLOCK), 1)
    def candidates(h):
        imp = jnp.zeros((n_slc, Q_BLOCK), F32)
        for r in range(GQA_GROUP):
            i = h * GQA_GROUP + r
            part = lambda off: sc_ref[i, pl.ds(CMP_FRONT_PAD + off, n_slc, stride=per), :]
            tot = part(0)
            for k in range(1, per - 1):
                tot = tot + part(k)
            imp = imp + (tot + 0.5 * (part(per - 1) + part(-1))) * scale[:, slab(i)]
        jq = (qis[h] * Q_BLOCK + lane_pos) // SLC_BLOCK
        valid = blk <= jq
        forced = (blk == 0) | (valid & (blk > jq - N_LOCAL_BLOCKS))
        free = valid & jnp.logical_not(forced)
        return jnp.where(free, imp, -1.0), forced, free

    tied = []
    for h in range(CMP_QBLOCKS):
        work, forced, free = candidates(h)
        t = _kth_largest(work, n_other)
        picked = (work > t) | (free & (work == t))
        sel_ref[h] = jnp.where(forced | picked, 1.0, 0.0)
        tied.append(jnp.max(jnp.sum(jnp.where(picked, 1.0, 0.0), axis=0, keepdims=True)) > n_other)

    for h in range(CMP_QBLOCKS):
        @pl.when(tied[h])
        def _(h=h):
            work, forced, free = candidates(h)
            for _ in range(n_other):
                mx = jnp.max(work, axis=0, keepdims=True)
                first = jnp.min(jnp.where(work == mx, blk_f, float(n_slc)), axis=0, keepdims=True)
                work = jnp.where((blk_f == first) & (mx >= 0.0), -1.0, work)
            sel_ref[h] = jnp.where(forced | (free & (work < 0.0)), 1.0, 0.0)


def _cmp_select(qt, kcp, vct, crow, cb, n_slc, n_other):
    b, g, dh = qt.shape[0], N_KV_HEADS, HEAD_DIM
    nqb = qt.shape[2] // Q_BLOCK
    total = nqb * ROWS
    ncp = kcp.shape[2]
    blk_q = pl.BlockSpec((None, None, dh, CMP_QBLOCKS * ROWS), lambda i, j, q: (i, j, 0, q))
    return pl.pallas_call(
        functools.partial(_cmp_select_body, n_other=n_other),
        grid=(b, g, nqb // CMP_QBLOCKS),
        in_specs=[_q_spec(CMP_QBLOCKS),
                  pl.BlockSpec((None, None, ncp, kcp.shape[3]), lambda i, j, q: (i, j, 0, 0)),
                  pl.BlockSpec((None, None, dh, ncp), lambda i, j, q: (i, j, 0, 0)),
                  pl.BlockSpec((None, 1, ROWS), lambda i, j, q: (j, 0, 0)),
                  pl.BlockSpec((None, CMP_WINDOW, ROWS), lambda i, j, q: (j, 0, 0))],
        out_specs=[blk_q,
                   pl.BlockSpec((None, None, CMP_QBLOCKS, n_slc, Q_BLOCK), lambda i, j, q: (i, j, q, 0, 0))],
        out_shape=[jax.ShapeDtypeStruct((b, g, dh, total), F32),
                   jax.ShapeDtypeStruct((b, g, nqb, n_slc, Q_BLOCK), F32)],
        scratch_shapes=[pltpu.VMEM((CMP_QBLOCKS * GQA_GROUP, ncp, Q_BLOCK), F32)],
        compiler_params=_params(3),
        name="cmp_select",
    )(qt, kcp, vct, crow, cb)


SLC_QBLOCKS = 2
FAR_KEYS = 256
FAR_BLOCKS = FAR_KEYS // SLC_BLOCK
FAR_CHUNKS = FAR_KEYS // KEY_CHUNK
FAR_BUFFERS = 4
FAR_AHEAD = 2
PEN_BLOCKS = 8
AUG_CONST = 2
AUG_ROWS = 16
K_SLAB = LANE
K_AUG_WIDTH = N_KV_HEADS * K_SLAB


def _key_augmentation(tm):
    assert tm % (PEN_BLOCKS * SLC_BLOCK) == 0
    pos = jnp.arange(tm, dtype=jnp.int32)
    onehot = ((pos[:, None] // SLC_BLOCK) % PEN_BLOCKS == jnp.arange(PEN_BLOCKS, dtype=jnp.int32)[None, :])
    return jnp.concatenate([jnp.ones((tm, AUG_CONST), BF16),
                            jnp.zeros((tm, AUG_ROWS - PEN_BLOCKS - AUG_CONST), BF16),
                            onehot.astype(BF16),
                            jnp.zeros((tm, K_SLAB - HEAD_DIM - AUG_ROWS), BF16)], axis=1)


def _slc_body(q_ref, k_ref, vt_ref, sel_ref, sb_ref, crow_ref, os_ref, sbuf, acc_ref):
    qis = [pl.program_id(2) * SLC_QBLOCKS + h for h in range(SLC_QBLOCKS)]
    qt = _heads_to_lanes(q_ref[...])
    dh, lanes = qt.shape
    per = KEY_CHUNK // SLC_BLOCK
    far_limits = [jnp.maximum(qi - 1, 0) * per for qi in qis]
    n_steps = (jnp.maximum(qis[-1] - 1, 0) + FAR_CHUNKS - 1) // FAR_CHUNKS
    last_step = k_ref.shape[0] // FAR_KEYS - 1
    steps_per_group = PEN_BLOCKS // FAR_BLOCKS

    c = jnp.concatenate([crow_ref[...]] * SLC_QBLOCKS, axis=1)
    c_hi = c.astype(BF16).astype(F32)
    const_rows = jnp.concatenate([c_hi, c - c_hi, jnp.zeros((AUG_ROWS - PEN_BLOCKS - AUG_CONST, lanes), F32)], axis=0)
    pad_rows = jnp.zeros((k_ref.shape[1] - dh - AUG_ROWS, lanes), BF16)
    blk_iota = lax.broadcasted_iota(jnp.int32, (PEN_BLOCKS, Q_BLOCK), 0)

    def far_logits(u):
        ua = jnp.minimum(u, last_step)
        grp0 = pl.multiple_of((ua // steps_per_group) * PEN_BLOCKS, PEN_BLOCKS)
        blk = blk_iota + (u // steps_per_group) * PEN_BLOCKS
        pens = []
        for h in range(SLC_QBLOCKS):
            pen = jnp.where((sel_ref[h, pl.ds(grp0, PEN_BLOCKS), :] > 0.5) & (blk < far_limits[h]), 0.0, NEG)
            pens += [pen] * GQA_GROUP
        qa = jnp.concatenate([qt, jnp.concatenate([const_rows, jnp.concatenate(pens, axis=1)], axis=0).astype(BF16),
                              pad_rows], axis=0)
        return _dot(k_ref[pl.ds(pl.multiple_of(ua * FAR_KEYS, FAR_KEYS), FAR_KEYS), :], qa)

    def stage(slot, u):
        s_new = far_logits(u).astype(BF16)
        sbuf[slot] = s_new
        groups = s_new.reshape(FAR_KEYS // BF16_ROWS, BF16_ROWS, lanes)
        return jnp.max(jnp.max(groups, axis=0).astype(F32), axis=0, keepdims=True)

    def far_update(m, s_ref, mx, key0):
        m_new = jnp.maximum(m, mx)
        p = jnp.exp2(s_ref[...] - m_new.astype(BF16))
        cols = pl.ds(pl.multiple_of(key0, FAR_KEYS), FAR_KEYS)
        acc_ref[...] = jnp.exp2(m - m_new) * acc_ref[...] + _dot(vt_ref[:, cols], p)
        return m_new

    def far_round(v, carry):
        m = carry[0]
        ahead = list(carry[1:])
        for slot in range(FAR_BUFFERS):
            u = FAR_BUFFERS * v + slot
            ahead.append(stage((slot + FAR_AHEAD) % FAR_BUFFERS, u + FAR_AHEAD))
            ua = jnp.minimum(u, last_step)
            m = far_update(m, sbuf.at[slot], ahead.pop(0), ua * FAR_KEYS)
        return (m, *ahead)

    ahead = [stage(u, u) for u in range(FAR_AHEAD)]
    acc_ref[...] = jnp.zeros_like(acc_ref)
    init = (jnp.full((1, lanes), NEG, F32), *ahead)
    m = lax.fori_loop(0, (n_steps + FAR_BUFFERS - 1) // FAR_BUFFERS, far_round, init)[0]
    acc = acc_ref[...]

    for h, qi in enumerate(qis):
        cols = slice(h * ROWS, (h + 1) * ROWS)
        qd = jnp.concatenate([qt[:, cols], jnp.zeros((k_ref.shape[1] - dh, ROWS), BF16)], axis=0)
        prev = jnp.maximum(qi - 1, 0)
        tiles, vts = [], []
        for chunk, kind in ((prev, jnp.where(qi >= 1, 1, 3)), (qi, 0)):
            rows = pl.ds(pl.multiple_of(chunk * KEY_CHUNK, KEY_CHUNK), KEY_CHUNK)
            s = _dot(k_ref[rows, :], qd) + sb_ref[kind]
            for j in range(per):
                srow = sel_ref[h, pl.ds(chunk * per + j, 1), :]
                srow = jnp.concatenate([srow] * GQA_GROUP, axis=1)
                tiles.append(jnp.where(srow > 0.5, s[j * SLC_BLOCK:(j + 1) * SLC_BLOCK, :], NEG))
            vts.append(vt_ref[:, rows])
        s = jnp.concatenate(tiles, axis=0)
        m_new = jnp.maximum(m[:, cols], jnp.max(s, axis=0, keepdims=True))
        p = jnp.exp2(s - m_new).astype(BF16)
        acc_h = jnp.exp2(m[:, cols] - m_new) * acc[:, cols] + _dot(jnp.concatenate(vts, axis=1), p)
        os_ref[:, cols] = acc_h[:dh] / jnp.maximum(acc_h[dh:dh + 1], 1e-30)


def _slc(qt, ks, vst, sel, sb, crow):
    b, g, dh = qt.shape[0], N_KV_HEADS, HEAD_DIM
    s_len = ks.shape[1]
    nqb = s_len // Q_BLOCK
    total = nqb * ROWS
    n_slc = sel.shape[3]
    blk_q = pl.BlockSpec((None, None, dh, SLC_QBLOCKS * ROWS), lambda i, j, q: (i, j, 0, q))
    return pl.pallas_call(
        _slc_body,
        grid=(b, g, nqb // SLC_QBLOCKS),
        in_specs=[_q_spec(SLC_QBLOCKS),
                  pl.BlockSpec((None, s_len, K_SLAB), lambda i, j, q: (i, 0, j)),
                  pl.BlockSpec((None, VT_ROWS, s_len), lambda i, j, q: (i, j, 0)),
                  pl.BlockSpec((None, None, SLC_QBLOCKS, n_slc, Q_BLOCK), lambda i, j, q: (i, j, q, 0, 0)),
                  pl.BlockSpec((None, 4, KEY_CHUNK, ROWS), lambda i, j, q: (j, 0, 0, 0)),
                  pl.BlockSpec((None, 1, ROWS), lambda i, j, q: (j, 0, 0))],
        out_specs=blk_q,
        out_shape=jax.ShapeDtypeStruct((b, g, dh, total), F32),
        scratch_shapes=[pltpu.VMEM((FAR_BUFFERS, FAR_KEYS, SLC_QBLOCKS * ROWS), BF16),
                        pltpu.VMEM((VT_ROWS, SLC_QBLOCKS * ROWS), F32)],
        compiler_params=_params(3),
        name="slc_attention",
    )(qt, ks, vst, sel, sb, crow)


WIN_CHUNKS = WINDOW // KEY_CHUNK + 1
WIN_QBLOCKS = 4


def _win_body(q_ref, k_ref, vt_ref, wb_ref, oc_ref, os_ref, gate_ref, o_ref):
    q_all = _group_slots(_heads_to_lanes(q_ref[...]), pl.program_id(1))
    for h in range(WIN_QBLOCKS):
        qi = pl.program_id(2) * WIN_QBLOCKS + h
        lanes = slice(h * ROWS, (h + 1) * ROWS)
        tokens = slice(h * Q_BLOCK, (h + 1) * Q_BLOCK)
        qt = q_all[:, lanes]
        tiles, vts = [], []
        for delta in range(WIN_CHUNKS - 1, -1, -1):
            c = jnp.maximum(qi - delta, 0)
            rows = pl.ds(pl.multiple_of(c * KEY_CHUNK, KEY_CHUNK), KEY_CHUNK)
            kind = jnp.where(qi >= delta, delta, WIN_CHUNKS)
            tiles.append(_dot(k_ref[rows, :], qt) + wb_ref[kind])
            vts.append(vt_ref[:, rows])
        s = jnp.concatenate(tiles, axis=0)
        m = jnp.max(s, axis=0, keepdims=True)
        p = jnp.exp2(s - m)
        l = jnp.sum(p, axis=0, keepdims=True)
        o_w = _dot(jnp.concatenate(vts, axis=1), p.astype(BF16)) / jnp.maximum(l, 1e-30)
        gate = lambda br: jnp.concatenate([gate_ref[br * GQA_GROUP + r:br * GQA_GROUP + r + 1, tokens]
                                           for r in range(GQA_GROUP)], axis=1)
        o = gate(0) * oc_ref[:, lanes] + gate(1) * os_ref[:, lanes] + gate(2) * o_w
        for r in range(GQA_GROUP):
            o_ref[r * HEAD_DIM:(r + 1) * HEAD_DIM, tokens] = o[:, r * Q_BLOCK:(r + 1) * Q_BLOCK].astype(o_ref.dtype)


def _win_mix(qt, kw, vwt, wb, oc, osl, gates):
    b, g, dh = qt.shape[0], N_KV_HEADS, HEAD_DIM
    s_len = kw.shape[1]
    blk_q = pl.BlockSpec((None, None, dh, WIN_QBLOCKS * ROWS), lambda i, j, q: (i, j, 0, q))
    return pl.pallas_call(
        _win_body,
        grid=(b, g, s_len // (WIN_QBLOCKS * Q_BLOCK)),
        in_specs=[_q_spec(WIN_QBLOCKS),
                  pl.BlockSpec((None, s_len, kw.shape[2]), lambda i, j, q: (i, 0, 0)),
                  pl.BlockSpec((None, dh, s_len), lambda i, j, q: (i, j, 0)),
                  pl.BlockSpec((None, WIN_CHUNKS + 1, KEY_CHUNK, ROWS), lambda i, j, q: (j, 0, 0, 0)),
                  blk_q, blk_q,
                  pl.BlockSpec((None, GATE_ROWS, WIN_QBLOCKS * Q_BLOCK), lambda i, j, q: (i, j, q))],
        out_specs=_q_spec(WIN_QBLOCKS),
        out_shape=jax.ShapeDtypeStruct(qt.shape, BF16),
        compiler_params=_params(3),
        name="window_mix",
    )(qt, kw, vwt, wb, oc, osl, gates)


FFN_TILE = 256


def _outproj_ffn_body(a_ref, gm_ref, h_ref, wa_ref, wb_ref, gmix_ref, gpre_ref, wg_ref, wu_ref, wd_ref, gpost_ref,
                      o_ref):
    y = _dot_tn(a_ref[...], wa_ref[...]) + _dot(gm_ref[...], wb_ref[...])
    h = h_ref[...] + _rms(y, gmix_ref[...])
    xb = _rms(h, gpre_ref[...]).astype(BF16)
    acc = jnp.zeros(h.shape, F32)
    for j in range(wg_ref.shape[1] // FFN_TILE):
        cols = slice(j * FFN_TILE, (j + 1) * FFN_TILE)
        gate = _dot(xb, wg_ref[:, cols])
        up = _dot(xb, wu_ref[:, cols])
        act = (gate * _sigmoid(gate) * up).astype(BF16)
        acc = acc + _dot(act, wd_ref[cols, :])
    o_ref[...] = h + _rms(acc, gpost_ref[...])


def _outproj_ffn(a, gm, h, weights, tm):
    n, d = h.shape
    per_b = a.shape[2] // tm
    row = lambda w: pl.BlockSpec((tm, w), lambda i: (i, 0))
    return pl.pallas_call(
        _outproj_ffn_body,
        grid=(n // tm,),
        in_specs=[pl.BlockSpec((None, a.shape[1], tm), lambda i: (i // per_b, 0, i % per_b)),
                  row(gm.shape[1]), row(d)] + [_resident(w) for w in weights],
        out_specs=row(d),
        out_shape=jax.ShapeDtypeStruct((n, d), F32),
        compiler_params=_params(1),
        name="outproj_ffn",
    )(a, gm, h, *weights)


def _t5_bucket(dist):
    n = jnp.maximum(dist, 0)
    max_exact = N_BUCKETS // 2
    nf = jnp.maximum(n, max_exact).astype(F32)
    large = max_exact + (jnp.log(nf / max_exact) / math.log(REL_MAX_DISTANCE / max_exact)
                         * (N_BUCKETS - max_exact)).astype(jnp.int32)
    return jnp.where(n < max_exact, n, jnp.minimum(large, N_BUCKETS - 1))


def _bias_tile(table, dist, mask):
    onehot = (_t5_bucket(dist)[..., None] == jnp.arange(N_BUCKETS, dtype=jnp.int32)).astype(F32)
    b = jnp.einsum("kqn,nh->kqh", onehot, table.astype(F32), precision=lax.Precision.HIGHEST)
    b = jnp.where(mask[..., None], b * LOG2E, NEG)
    k = dist.shape[0]
    return b.reshape(k, Q_BLOCK, N_KV_HEADS, GQA_GROUP).transpose(2, 0, 3, 1).reshape(N_KV_HEADS, k, ROWS)


def _bias_tiles(rel_bias):
    ql = jnp.arange(Q_BLOCK, dtype=jnp.int32)[None, :]
    kl = jnp.arange(KEY_CHUNK, dtype=jnp.int32)[:, None]
    chunk_dist = lambda delta: delta * KEY_CHUNK + ql - kl
    everything = jnp.ones((KEY_CHUNK, Q_BLOCK), bool)
    nothing = jnp.zeros((KEY_CHUNK, Q_BLOCK), bool)
    far = jnp.full((KEY_CHUNK, Q_BLOCK), REL_MAX_DISTANCE, jnp.int32)
    d0 = chunk_dist(0)
    sb = jnp.stack([_bias_tile(rel_bias, d0, d0 >= 0),
                    _bias_tile(rel_bias, chunk_dist(1), everything),
                    _bias_tile(rel_bias, far, everything),
                    _bias_tile(rel_bias, far, nothing)], axis=1)
    wtiles = []
    for delta in range(WIN_CHUNKS):
        d = chunk_dist(delta)
        wtiles.append(_bias_tile(rel_bias, d, (d >= 0) & (d < WINDOW)))
    wtiles.append(_bias_tile(rel_bias, far, nothing))
    wb = jnp.stack(wtiles, axis=1)
    rel = jnp.arange(CMP_WINDOW, dtype=jnp.int32)[:, None] - CMP_FRONT_PAD
    dc = ql - CMP_STRIDE * rel - (CMP_BLOCK - 1)
    cb = _bias_tile(rel_bias, dc, dc >= 0)
    crow = _bias_tile(rel_bias, far[:1], everything[:1])
    return sb, wb, cb, crow


def _mixer_ffn(h, b, s, lw, tiles, tm):
    sb, wb, cb, crow = tiles
    qt, cv, ksa, kw, vst, vwt, gates, gm = _inproj(
        h, b, (lw["gpre"], lw["wqt"], lw["wnat"], lw["wvt"], lw["wgt"], lw["wu"], lw["wv"],
               lw["lng"], lw["lnb"], lw["ws"], lw["bs"], _key_augmentation(tm)), tm)
    g, dh = N_KV_HEADS, HEAD_DIM
    n_chunks = s // CMP_STRIDE
    comp = _compress(cv, lw["cpos"], lw["cw1"], lw["cw2"])
    ncp = -(-(n_chunks + CMP_WINDOW) // CMP_CHUNK) * CMP_CHUNK
    back = ncp - CMP_FRONT_PAD - n_chunks
    comp = jnp.pad(comp, ((0, 0), (0, 0), (0, 0), (CMP_FRONT_PAD, back), (0, 0)))
    front = (jnp.arange(ncp) < CMP_FRONT_PAD).astype(F32)[:, None]
    aug = jnp.concatenate([jnp.ones((ncp, 2), F32), front, jnp.zeros((ncp, CMP_AUG_WIDTH - dh - 3), F32)], axis=1)
    kcp = jnp.concatenate([comp[:, 0], jnp.broadcast_to(aug, (b, g) + aug.shape)], axis=-1)
    vct = comp[:, 1].transpose(0, 1, 3, 2).astype(BF16)

    n_slc = s // SLC_BLOCK
    n_other = min(SLC_TOPK, n_slc) - (N_LOCAL_BLOCKS + 1)
    oc, sel = _cmp_select(qt, kcp, vct, crow, cb, n_slc, n_other)
    osl = _slc(qt, ksa.reshape(b, s, ksa.shape[1]), vst, sel, sb, crow)
    ot = _win_mix(qt, kw.reshape(b, s, kw.shape[1]), vwt, wb, oc, osl, gates)
    return _outproj_ffn(ot, gm, h, (lw["wo_a"], lw["wo_b"], lw["gpost"], lw["fpre"], lw["fwg"], lw["fwu"], lw["fwd"],
                                    lw["fpost"]), tm)


def _layer_weights(l, p):
    d = p["w_in"].shape[1]
    w_in = p["w_in"][l]
    o = NSA_WIDTH
    wqt = w_in[:, :o].T
    k_c, v_c, k_s, v_s, k_w, v_w = [w_in[:, o + i * KV_WIDTH:o + (i + 1) * KV_WIDTH] for i in range(6)]
    wnat = jnp.concatenate([k_c, v_c, k_s, k_w], axis=1)
    wvt = jnp.concatenate([v_s, v_w], axis=1).T
    o += 6 * KV_WIDTH
    wg = w_in[:, o:o + N_GATES].reshape(d, N_KV_HEADS, GQA_GROUP, N_BRANCHES).transpose(1, 3, 2, 0)
    wg = wg.reshape(N_KV_HEADS, N_BRANCHES * GQA_GROUP, d)
    wgt = jnp.pad(wg, ((0, 0), (0, GATE_ROWS - N_BRANCHES * GQA_GROUP), (0, 0))).reshape(N_KV_HEADS * GATE_ROWS, d)
    o += N_GATES
    gw = (w_in.shape[1] - o) // 2
    wu, wv = w_in[:, o:o + gw], w_in[:, o + gw:]
    causal = jnp.tril(jnp.ones((GMLP_CHUNK, GMLP_CHUNK), bool))
    ws = jnp.where(causal, p["gmlp_w_s"][l], 0.0)
    bs = jnp.repeat(p["gmlp_b_s"][l].T, gw // N_GMLP_GROUPS, axis=1)
    half = CMP_STRIDE * HEAD_DIM
    cpos = jnp.stack([p["cmp_pos_k"][l].reshape(2, half), p["cmp_pos_v"][l].reshape(2, half)])
    dff = p["w_down"].shape[1]
    row = lambda v: v[l].reshape(1, -1)
    bf = lambda w: w.astype(BF16)
    return dict(gpre=row(p["norm_mix_pre"]), wqt=bf(wqt), wnat=bf(wnat), wvt=bf(wvt), wgt=bf(wgt), wu=bf(wu), wv=bf(wv),
                lng=row(p["gmlp_ln_g"]), lnb=row(p["gmlp_ln_b"]), ws=bf(ws), bs=bs,
                cpos=cpos, cw1=bf(jnp.stack([p["cmp_w1_k"][l], p["cmp_w1_v"][l]])),
                cw2=bf(jnp.stack([p["cmp_w2_k"][l], p["cmp_w2_v"][l]])),
                wo_a=bf(p["w_out"][l][:NSA_WIDTH]), wo_b=bf(p["w_out"][l][NSA_WIDTH:]),
                gpost=row(p["norm_mix_post"]), fpre=row(p["norm_ffn_pre"]), fpost=row(p["norm_ffn_post"]),
                fwg=bf(p["w_gate_up"][l][:, :dff]), fwu=bf(p["w_gate_up"][l][:, dff:]), fwd=bf(p["w_down"][l]))


def _trunk(p, tm):
    x = p["x"]
    b, s, d = x.shape
    h = x.reshape(b * s, d)
    tiles = _bias_tiles(p["rel_bias"])
    for l in range(p["w_in"].shape[0]):
        lw = _layer_weights(l, p)
        h = _mixer_ffn(h, b, s, lw, tiles, tm)
    return h.reshape(b, s, d)


def kernel(x, rel_bias, norm_mix_pre, norm_mix_post, norm_ffn_pre, norm_ffn_post, w_in, cmp_pos_k, cmp_w1_k, cmp_w2_k, cmp_pos_v, cmp_w1_v, cmp_w2_v, gmlp_ln_g, gmlp_ln_b, gmlp_w_s, gmlp_b_s, w_out, w_gate_up, w_down):
    p = dict(x=x, rel_bias=rel_bias, norm_mix_pre=norm_mix_pre, norm_mix_post=norm_mix_post,
             norm_ffn_pre=norm_ffn_pre, norm_ffn_post=norm_ffn_post, w_in=w_in,
             cmp_pos_k=cmp_pos_k, cmp_w1_k=cmp_w1_k, cmp_w2_k=cmp_w2_k,
             cmp_pos_v=cmp_pos_v, cmp_w1_v=cmp_w1_v, cmp_w2_v=cmp_w2_v,
             gmlp_ln_g=gmlp_ln_g, gmlp_ln_b=gmlp_ln_b, gmlp_w_s=gmlp_w_s, gmlp_b_s=gmlp_b_s,
             w_out=w_out, w_gate_up=w_gate_up, w_down=w_down)
    return _trunk(p, tm=512)
```

```python
import functools
import math

import jax
import jax.numpy as jnp
import numpy as np
from jax import lax
from jax.experimental import pallas as pl
from jax.experimental.pallas import tpu as pltpu

F32 = jnp.float32
BF16 = jnp.bfloat16

N_NSA_HEADS = 8
N_KV_HEADS = 2
GQA_GROUP = N_NSA_HEADS // N_KV_HEADS
HEAD_DIM = 64
NSA_WIDTH = N_NSA_HEADS * HEAD_DIM
KV_WIDTH = N_KV_HEADS * HEAD_DIM
N_BRANCHES = 3
N_GATES = N_BRANCHES * N_NSA_HEADS
CMP_BLOCK = 32
CMP_STRIDE = 16
SLC_BLOCK = 64
SLC_TOPK = 16
N_LOCAL_BLOCKS = 2
WINDOW = 512
Q_BLOCK = 128
N_GMLP_GROUPS = 8
GMLP_CHUNK = 128
N_BUCKETS = 32
REL_MAX_DISTANCE = 128
RMS_EPS = 1e-6
LN_EPS = 1e-5

ROWS = GQA_GROUP * Q_BLOCK
KEY_CHUNK = 128
CMP_FRONT_PAD = 24
CMP_WINDOW = 32
NEG = -1e30
LANE = 128
GATE_PAD = 128
MXU_TILE = 256
BF16_ROWS = 16
VMEM_LIMIT = 48 * 1024 * 1024
LOG2E = math.log2(math.e)
Q_SCALE = HEAD_DIM ** -0.5 * LOG2E


def _dot(a, b):
    return jnp.dot(a, b, preferred_element_type=F32)


def _gelu(x):
    c = math.sqrt(2.0 / math.pi)
    return 0.5 * x * (1.0 + jnp.tanh(c * (x + 0.044715 * (x * x * x))))


def _sigmoid(x):
    return 1.0 / (1.0 + jnp.exp(-x))


def _rms(x, g):
    ms = jnp.mean(x * x, axis=-1, keepdims=True)
    return (x * lax.rsqrt(ms + RMS_EPS)) * g


def _resident(a):
    return pl.BlockSpec(a.shape, lambda i: (0,) * a.ndim, pipeline_mode=pl.Buffered(1))


def _params(n_axes):
    return pltpu.CompilerParams(dimension_semantics=("arbitrary",) * n_axes,
                                vmem_limit_bytes=VMEM_LIMIT)


def _dot_nt(a, b):
    return lax.dot_general(a, b, (((1,), (1,)), ((), ())), preferred_element_type=F32)


def _dot_tn(a, b):
    return lax.dot_general(a, b, (((0,), (0,)), ((), ())), preferred_element_type=F32)


def _inproj_body(x_ref, gpre_ref, wqt_ref, wnat_ref, wvt_ref, wgt_ref, wu_ref, wv_ref, lng_ref, lnb_ref,
                 ws_ref, bs_ref, aug_ref, qt_ref, cv_ref, ksa_ref, kw_ref, vst_ref, vwt_ref, gate_ref, gm_ref, cv_sc):
    tm = x_ref.shape[0]
    xb = _rms(x_ref[...], gpre_ref[...]).astype(BF16)
    qt_ref[...] = (_dot_nt(wqt_ref[...], xb) * Q_SCALE).astype(BF16)
    nat = _dot(xb, wnat_ref[...])
    chunks = tm // CMP_STRIDE
    left = lax.broadcasted_iota(jnp.int32, (chunks, LANE), 1) < HEAD_DIM
    for kv in range(2):
        cv_sc[kv] = nat[:, kv * KV_WIDTH:(kv + 1) * KV_WIDTH]
    for kv in range(2):
        for pair in range(CMP_STRIDE // 2):
            lanes = slice(pair * LANE, (pair + 1) * LANE)
            a = cv_sc[kv, pl.ds(2 * pair, chunks, stride=CMP_STRIDE), :]
            b = cv_sc[kv, pl.ds(2 * pair + 1, chunks, stride=CMP_STRIDE), :]
            cv_ref[kv, 0, :, lanes] = jnp.where(left, a, pltpu.roll(b, HEAD_DIM, 1))
            cv_ref[kv, 1, :, lanes] = jnp.where(left, pltpu.roll(a, HEAD_DIM, 1), b)
    k_slc = nat[:, 2 * KV_WIDTH:3 * KV_WIDTH].astype(BF16)
    ksa_ref[...] = jnp.concatenate(
        [piece for g in range(N_KV_HEADS) for piece in (k_slc[:, g * HEAD_DIM:(g + 1) * HEAD_DIM], aug_ref[...])], axis=1)
    kw_ref[...] = nat[:, 3 * KV_WIDTH:].astype(BF16)
    vt = _dot_nt(wvt_ref[...], xb).astype(BF16)
    denom = jnp.concatenate([jnp.ones((1, tm), BF16), jnp.zeros((BF16_ROWS - 1, tm), BF16)], axis=0)
    vst_ref[...] = jnp.concatenate(
        [piece for g in range(N_KV_HEADS) for piece in (vt[g * HEAD_DIM:(g + 1) * HEAD_DIM], denom)], axis=0)
    vwt_ref[...] = vt[KV_WIDTH:]
    gate_ref[...] = _sigmoid(_dot_nt(wgt_ref[...], xb))
    zu = _gelu(_dot(xb, wu_ref[...]))
    zv = _gelu(_dot(xb, wv_ref[...]))
    mu = jnp.mean(zv, axis=-1, keepdims=True)
    zc = zv - mu
    var = jnp.mean(zc * zc, axis=-1, keepdims=True)
    zv = ((zc * lax.rsqrt(var + LN_EPS)) * lng_ref[...] + lnb_ref[...]).astype(BF16)
    gdim = zv.shape[1] // N_GMLP_GROUPS
    left = lax.broadcasted_iota(jnp.int32, (GMLP_CHUNK, LANE), 1) < gdim
    for c in range(tm // GMLP_CHUNK):
        rows = slice(c * GMLP_CHUNK, (c + 1) * GMLP_CHUNK)
        for j in range(zv.shape[1] // LANE):
            cols = slice(j * LANE, (j + 1) * LANE)
            z = zv[rows, cols]
            sv = jnp.where(left, _dot(ws_ref[2 * j], z), _dot(ws_ref[2 * j + 1], z)) + bs_ref[:, cols]
            gm_ref[rows, cols] = (zu[rows, cols] * sv).astype(BF16)


VT_ROWS = HEAD_DIM + BF16_ROWS
GATE_ROWS = 16


def _inproj(x, b, weights, tm):
    n, d = x.shape
    s = n // b
    per_b = s // tm
    row = lambda w: pl.BlockSpec((tm, w), lambda i: (i, 0))
    col = lambda h: pl.BlockSpec((None, h, tm), lambda i: (i // per_b, 0, i % per_b))
    gw = weights[5].shape[1]
    flat = CMP_STRIDE * HEAD_DIM
    assert KV_WIDTH == LANE and N_KV_HEADS == 2
    return pl.pallas_call(
        _inproj_body,
        grid=(n // tm,),
        in_specs=[row(d)] + [_resident(a) for a in weights],
        out_specs=[col(NSA_WIDTH),
                   pl.BlockSpec((None, 2, N_KV_HEADS, tm // CMP_STRIDE, flat),
                                lambda i: (i // per_b, 0, 0, i % per_b, 0)),
                   row(K_AUG_WIDTH), row(KV_WIDTH),
                   col(N_KV_HEADS * VT_ROWS), col(KV_WIDTH), col(N_KV_HEADS * GATE_ROWS), row(gw)],
        out_shape=[jax.ShapeDtypeStruct((b, NSA_WIDTH, s), BF16),
                   jax.ShapeDtypeStruct((b, 2, N_KV_HEADS, s // CMP_STRIDE, flat), F32),
                   jax.ShapeDtypeStruct((n, K_AUG_WIDTH), BF16),
                   jax.ShapeDtypeStruct((n, KV_WIDTH), BF16),
                   jax.ShapeDtypeStruct((b, N_KV_HEADS * VT_ROWS, s), BF16),
                   jax.ShapeDtypeStruct((b, KV_WIDTH, s), BF16),
                   jax.ShapeDtypeStruct((b, N_KV_HEADS * GATE_ROWS, s), F32),
                   jax.ShapeDtypeStruct((n, gw), BF16)],
        scratch_shapes=[pltpu.VMEM((2, tm, KV_WIDTH), F32)],
        compiler_params=_params(1),
        name="inproj_gmlp",
    )(x, *weights)


def _compress_body(x_ref, pos_ref, w1_ref, w2_ref, o_ref):
    x = x_ref[...]
    half = x.shape[1]
    a = _dot((x + pos_ref[0:1, :]).astype(BF16), w1_ref[:half, :])
    b = _dot((x + pos_ref[1:2, :]).astype(BF16), w1_ref[half:, :])
    pre = a + pltpu.roll(b, x.shape[0] - 1, 0)
    o_ref[...] = _dot(_gelu(pre).astype(BF16), w2_ref[...])


def _compress(xc, pos, w1, w2):
    b, two, g, nch, width = xc.shape
    hid = w1.shape[2]
    dh = w2.shape[2]
    return pl.pallas_call(
        _compress_body,
        grid=(b, two, g),
        in_specs=[pl.BlockSpec((None, None, None, nch, width), lambda i, t, j: (i, t, j, 0, 0)),
                  pl.BlockSpec((None, 2, width), lambda i, t, j: (t, 0, 0)),
                  pl.BlockSpec((None, 2 * width, hid), lambda i, t, j: (t, 0, 0)),
                  pl.BlockSpec((None, hid, dh), lambda i, t, j: (t, 0, 0))],
        out_specs=pl.BlockSpec((None, None, None, nch, dh), lambda i, t, j: (i, t, j, 0, 0)),
        out_shape=jax.ShapeDtypeStruct((b, two, g, nch, dh), F32),
        compiler_params=_params(3),
        name="compress",
    )(xc, pos, w1, w2)


CMP_CHUNK = 256
CMP_AUG_WIDTH = 2 * HEAD_DIM


def _heads_to_lanes(blk):
    n = blk.shape[1] // Q_BLOCK
    return jnp.concatenate([blk[r * HEAD_DIM:(r + 1) * HEAD_DIM, h * Q_BLOCK:(h + 1) * Q_BLOCK]
                            for h in range(n) for r in range(GQA_GROUP)], axis=1)


def _group_slots(qt, g):
    return jnp.concatenate([jnp.where(g == j, qt, jnp.zeros_like(qt)) for j in range(N_KV_HEADS)], axis=0)


def _q_spec(n_blocks):
    return pl.BlockSpec((None, GQA_GROUP * HEAD_DIM, n_blocks * Q_BLOCK), lambda i, j, q: (i, j, q))


CMP_QBLOCKS = 4
SUBLANES = 8
SORT_KEEP = 16


def _bitonic_merge(xs):
    if len(xs) == 1:
        return xs
    half = len(xs) // 2
    hi = [jnp.maximum(xs[i], xs[i + half]) for i in range(half)]
    lo = [jnp.minimum(xs[i], xs[i + half]) for i in range(half)]
    return _bitonic_merge(hi) + _bitonic_merge(lo)


def _bitonic_sort(xs):
    if len(xs) == 1:
        return xs
    half = len(xs) // 2
    return _bitonic_merge(_bitonic_sort(xs[:half]) + _bitonic_sort(xs[half:])[::-1])


def _top_of_two(a, b):
    n = len(a)
    return _bitonic_merge([jnp.maximum(a[i], b[n - 1 - i]) for i in range(n)])


def _kth_largest(x, k):
    assert k <= SORT_KEEP and x.shape[0] % SUBLANES == 0
    tiles = [x[i * SUBLANES:(i + 1) * SUBLANES] for i in range(x.shape[0] // SUBLANES)]
    tiles += [jnp.full_like(tiles[0], -2.0)] * (-len(tiles) % SORT_KEEP)
    top = _bitonic_sort(tiles[:SORT_KEEP])
    for j in range(SORT_KEEP, len(tiles), SORT_KEEP):
        top = _top_of_two(top, _bitonic_sort(tiles[j:j + SORT_KEEP]))
    shift = SUBLANES // 2
    while shift:
        top = _top_of_two(top, [pltpu.roll(v, shift, 0) for v in top])
        shift //= 2
    return top[k - 1][0:1]


def _cmp_select_body(q_ref, k_ref, vt_ref, crow_ref, cb_ref, oc_ref, sel_ref, sc_ref, *, n_other):
    qis = [pl.program_id(2) * CMP_QBLOCKS + h for h in range(CMP_QBLOCKS)]
    n_slc = sel_ref.shape[1]
    qt = _heads_to_lanes(q_ref[...])
    dh, lanes = qt.shape
    w0s = [pl.multiple_of(qi * (Q_BLOCK // CMP_STRIDE), 8) for qi in qis]
    n_chunks = (w0s[-1] + CMP_WINDOW + CMP_CHUNK - 1) // CMP_CHUNK
    row_iota = lax.broadcasted_iota(jnp.int32, (CMP_CHUNK, lanes), 0)
    lane_blk = lax.broadcasted_iota(jnp.int32, (1, lanes), 1) // ROWS
    w0_lane = w0s[0]
    for h in range(1, CMP_QBLOCKS):
        w0_lane = jnp.where(lane_blk >= h, w0s[h], w0_lane)

    @pl.when(qis[0] == 0)
    def _():
        sc_ref[...] = jnp.zeros_like(sc_ref)

    c = jnp.concatenate([crow_ref[...]] * CMP_QBLOCKS, axis=1)
    c_hi = c.astype(BF16).astype(F32)
    neg_row = jnp.full((1, lanes), NEG, F32)
    zeros = lambda n, dt: jnp.zeros((n, lanes), dt)
    tail = zeros(k_ref.shape[1] - dh - BF16_ROWS, BF16)
    qa = jnp.concatenate([qt, jnp.concatenate([c_hi, c - c_hi, neg_row, zeros(BF16_ROWS - 3, F32)]).astype(BF16), tail])
    qw = jnp.concatenate([qt, jnp.concatenate([zeros(2, F32), neg_row, zeros(BF16_ROWS - 3, F32)]).astype(BF16), tail])

    def rows_of(ch):
        return pl.ds(pl.multiple_of(ch * CMP_CHUNK, CMP_CHUNK), CMP_CHUNK)

    n_slabs = lanes // Q_BLOCK
    slab = lambda i: slice(i * Q_BLOCK, (i + 1) * Q_BLOCK)

    def put(rows, val, first=0):
        for i in range(val.shape[1] // Q_BLOCK):
            sc_ref[first + i, rows, :] = val[:, slab(i)]

    def get(rows):
        return jnp.concatenate([sc_ref[i, rows, :] for i in range(n_slabs)], axis=1)

    def logits(ch, m):
        s = _dot(k_ref[rows_of(ch), :].astype(BF16), qa)
        s = jnp.where(row_iota + ch * CMP_CHUNK < w0_lane, s, NEG)
        put(rows_of(ch), s)
        return jnp.maximum(m, jnp.max(s, axis=0, keepdims=True))

    m = lax.fori_loop(0, n_chunks, logits, jnp.full((1, lanes), NEG, F32))
    win_max = []
    for h, w0 in enumerate(w0s):
        win = pl.ds(w0, CMP_WINDOW)
        s_win = _dot(k_ref[win, :].astype(BF16), qw[:, h * ROWS:(h + 1) * ROWS]) + cb_ref[...]
        put(win, s_win, first=h * GQA_GROUP)
        win_max.append(jnp.max(s_win, axis=0, keepdims=True))
    m = jnp.maximum(m, jnp.concatenate(win_max, axis=1))

    def weigh(ch, l):
        p = jnp.exp2(get(rows_of(ch)) - m)
        put(rows_of(ch), p)
        oc_ref[...] += _dot(vt_ref[:, rows_of(ch)], p.astype(BF16))
        return l + jnp.sum(p, axis=0, keepdims=True)

    oc_ref[...] = jnp.zeros_like(oc_ref)
    l = lax.fori_loop(0, n_chunks, weigh, jnp.zeros((1, lanes), F32))
    scale = jnp.where(m > 0.5 * NEG, 1.0 / jnp.maximum(l, 1e-30), 0.0)
    oc_ref[...] = oc_ref[...] * scale

    per = SLC_BLOCK // CMP_STRIDE
    blk = lax.broadcasted_iota(jnp.int32, (n_slc, Q_BLOCK), 0)
    blk_f = blk.astype(F32)
    lane_pos = lax.broadcasted_iota(jnp.int32, (n_slc, Q_BLOCK), 1)
    def candidates(h):
        imp = jnp.zeros((n_slc, Q_BLOCK), F32)
        for r in range(GQA_GROUP):
            i = h * GQA_GROUP + r
            part = lambda off: sc_ref[i, pl.ds(CMP_FRONT_PAD + off, n_slc, stride=per), :]
            tot = part(0)
            for k in range(1, per - 1):
                tot = tot + part(k)
            imp = imp + (tot + 0.5 * (part(per - 1) + part(-1))) * scale[:, slab(i)]
        jq = (qis[h] * Q_BLOCK + lane_pos) // SLC_BLOCK
        valid = blk <= jq
        forced = (blk == 0) | (valid & (blk > jq - N_LOCAL_BLOCKS))
        free = valid & jnp.logical_not(forced)
        return jnp.where(free, imp, -1.0), forced, free

    tied = []
    for h in range(CMP_QBLOCKS):
        work, forced, free = candidates(h)
        t = _kth_largest(work, n_other)
        picked = (work > t) | (free & (work == t))
        sel_ref[h] = jnp.where(forced | picked, 1.0, 0.0)
        tied.append(jnp.max(jnp.sum(jnp.where(picked, 1.0, 0.0), axis=0, keepdims=True)) > n_other)

    for h in range(CMP_QBLOCKS):
        @pl.when(tied[h])
        def _(h=h):
            work, forced, free = candidates(h)
            for _ in range(n_other):
                mx = jnp.max(work, axis=0, keepdims=True)
                first = jnp.min(jnp.where(work == mx, blk_f, float(n_slc)), axis=0, keepdims=True)
                work = jnp.where((blk_f == first) & (mx >= 0.0), -1.0, work)
            sel_ref[h] = jnp.where(forced | (free & (work < 0.0)), 1.0, 0.0)


def _cmp_select(qt, kcp, vct, crow, cb, n_slc, n_other):
    b, g, dh = qt.shape[0], N_KV_HEADS, HEAD_DIM
    nqb = qt.shape[2] // Q_BLOCK
    total = nqb * ROWS
    ncp = kcp.shape[2]
    blk_q = pl.BlockSpec((None, None, dh, CMP_QBLOCKS * ROWS), lambda i, j, q: (i, j, 0, q))
    return pl.pallas_call(
        functools.partial(_cmp_select_body, n_other=n_other),
        grid=(b, g, nqb // CMP_QBLOCKS),
        in_specs=[_q_spec(CMP_QBLOCKS),
                  pl.BlockSpec((None, None, ncp, kcp.shape[3]), lambda i, j, q: (i, j, 0, 0)),
                  pl.BlockSpec((None, None, dh, ncp), lambda i, j, q: (i, j, 0, 0)),
                  pl.BlockSpec((None, 1, ROWS), lambda i, j, q: (j, 0, 0)),
                  pl.BlockSpec((None, CMP_WINDOW, ROWS), lambda i, j, q: (j, 0, 0))],
        out_specs=[blk_q,
                   pl.BlockSpec((None, None, CMP_QBLOCKS, n_slc, Q_BLOCK), lambda i, j, q: (i, j, q, 0, 0))],
        out_shape=[jax.ShapeDtypeStruct((b, g, dh, total), F32),
                   jax.ShapeDtypeStruct((b, g, nqb, n_slc, Q_BLOCK), F32)],
        scratch_shapes=[pltpu.VMEM((CMP_QBLOCKS * GQA_GROUP, ncp, Q_BLOCK), F32)],
        compiler_params=_params(3),
        name="cmp_select",
    )(qt, kcp, vct, crow, cb)


SLC_QBLOCKS = 2
FAR_KEYS = 256
FAR_BLOCKS = FAR_KEYS // SLC_BLOCK
FAR_CHUNKS = FAR_KEYS // KEY_CHUNK
FAR_BUFFERS = 4
FAR_AHEAD = 2
PEN_BLOCKS = 8
AUG_CONST = 2
AUG_ROWS = 16
K_SLAB = LANE
K_AUG_WIDTH = N_KV_HEADS * K_SLAB


def _key_augmentation(tm):
    assert tm % (PEN_BLOCKS * SLC_BLOCK) == 0
    pos = jnp.arange(tm, dtype=jnp.int32)
    onehot = ((pos[:, None] // SLC_BLOCK) % PEN_BLOCKS == jnp.arange(PEN_BLOCKS, dtype=jnp.int32)[None, :])
    return jnp.concatenate([jnp.ones((tm, AUG_CONST), BF16),
                            jnp.zeros((tm, AUG_ROWS - PEN_BLOCKS - AUG_CONST), BF16),
                            onehot.astype(BF16),
                            jnp.zeros((tm, K_SLAB - HEAD_DIM - AUG_ROWS), BF16)], axis=1)


def _slc_body(q_ref, k_ref, vt_ref, sel_ref, sb_ref, crow_ref, os_ref, sbuf, acc_ref):
    qis = [pl.program_id(2) * SLC_QBLOCKS + h for h in range(SLC_QBLOCKS)]
    qt = _heads_to_lanes(q_ref[...])
    dh, lanes = qt.shape
    per = KEY_CHUNK // SLC_BLOCK
    far_limits = [jnp.maximum(qi - 1, 0) * per for qi in qis]
    n_steps = (jnp.maximum(qis[-1] - 1, 0) + FAR_CHUNKS - 1) // FAR_CHUNKS
    last_step = k_ref.shape[0] // FAR_KEYS - 1
    steps_per_group = PEN_BLOCKS // FAR_BLOCKS

    c = jnp.concatenate([crow_ref[...]] * SLC_QBLOCKS, axis=1)
    c_hi = c.astype(BF16).astype(F32)
    const_rows = jnp.concatenate([c_hi, c - c_hi, jnp.zeros((AUG_ROWS - PEN_BLOCKS - AUG_CONST, lanes), F32)], axis=0)
    pad_rows = jnp.zeros((k_ref.shape[1] - dh - AUG_ROWS, lanes), BF16)
    blk_iota = lax.broadcasted_iota(jnp.int32, (PEN_BLOCKS, Q_BLOCK), 0)

    def far_logits(u):
        ua = jnp.minimum(u, last_step)
        grp0 = pl.multiple_of((ua // steps_per_group) * PEN_BLOCKS, PEN_BLOCKS)
        blk = blk_iota + (u // steps_per_group) * PEN_BLOCKS
        pens = []
        for h in range(SLC_QBLOCKS):
            pen = jnp.where((sel_ref[h, pl.ds(grp0, PEN_BLOCKS), :] > 0.5) & (blk < far_limits[h]), 0.0, NEG)
            pens += [pen] * GQA_GROUP
        qa = jnp.concatenate([qt, jnp.concatenate([const_rows, jnp.concatenate(pens, axis=1)], axis=0).astype(BF16),
                              pad_rows], axis=0)
        return _dot(k_ref[pl.ds(pl.multiple_of(ua * FAR_KEYS, FAR_KEYS), FAR_KEYS), :], qa)

    def stage(slot, u):
        s_new = far_logits(u).astype(BF16)
        sbuf[slot] = s_new
        groups = s_new.reshape(FAR_KEYS // BF16_ROWS, BF16_ROWS, lanes)
        return jnp.max(jnp.max(groups, axis=0).astype(F32), axis=0, keepdims=True)

    def far_update(m, s_ref, mx, key0):
        m_new = jnp.maximum(m, mx)
        p = jnp.exp2(s_ref[...] - m_new.astype(BF16))
        cols = pl.ds(pl.multiple_of(key0, FAR_KEYS), FAR_KEYS)
        acc_ref[...] = jnp.exp2(m - m_new) * acc_ref[...] + _dot(vt_ref[:, cols], p)
        return m_new

    def far_round(v, carry, staged_slots=FAR_BUFFERS):
        m = carry[0]
        ahead = list(carry[1:])
        for slot in range(FAR_BUFFERS):
            u = FAR_BUFFERS * v + slot
            if slot < staged_slots:
                ahead.append(stage((slot + FAR_AHEAD) % FAR_BUFFERS, u + FAR_AHEAD))
            ua = jnp.minimum(u, last_step)
            m = far_update(m, sbuf.at[slot], ahead.pop(0), ua * FAR_KEYS)
        return (m, *ahead)

    ahead = [stage(u, u) for u in range(FAR_AHEAD)]
    acc_ref[...] = jnp.zeros_like(acc_ref)
    init = (jnp.full((1, lanes), NEG, F32), *ahead)
    rounds = (n_steps + FAR_BUFFERS - 1) // FAR_BUFFERS
    carry = lax.fori_loop(0, jnp.maximum(rounds - 1, 0), far_round, init)
    m = lax.cond(rounds > 0,
                 lambda c: far_round(rounds - 1, c, staged_slots=FAR_BUFFERS - FAR_AHEAD)[0],
                 lambda c: c[0], carry)
    acc = acc_ref[...]

    for h, qi in enumerate(qis):
        cols = slice(h * ROWS, (h + 1) * ROWS)
        qd = jnp.concatenate([qt[:, cols], jnp.zeros((k_ref.shape[1] - dh, ROWS), BF16)], axis=0)
        prev = jnp.maximum(qi - 1, 0)
        tiles, vts = [], []
        for chunk, kind in ((prev, jnp.where(qi >= 1, 1, 3)), (qi, 0)):
            rows = pl.ds(pl.multiple_of(chunk * KEY_CHUNK, KEY_CHUNK), KEY_CHUNK)
            s = _dot(k_ref[rows, :], qd) + sb_ref[kind]
            for j in range(per):
                srow = sel_ref[h, pl.ds(chunk * per + j, 1), :]
                srow = jnp.concatenate([srow] * GQA_GROUP, axis=1)
                tiles.append(jnp.where(srow > 0.5, s[j * SLC_BLOCK:(j + 1) * SLC_BLOCK, :], NEG))
            vts.append(vt_ref[:, rows])
        s = jnp.concatenate(tiles, axis=0)
        m_new = jnp.maximum(m[:, cols], jnp.max(s, axis=0, keepdims=True))
        p = jnp.exp2(s - m_new).astype(BF16)
        acc_h = jnp.exp2(m[:, cols] - m_new) * acc[:, cols] + _dot(jnp.concatenate(vts, axis=1), p)
        os_ref[:, cols] = acc_h[:dh] / jnp.maximum(acc_h[dh:dh + 1], 1e-30)


def _slc(qt, ks, vst, sel, sb, crow):
    b, g, dh = qt.shape[0], N_KV_HEADS, HEAD_DIM
    s_len = ks.shape[1]
    nqb = s_len // Q_BLOCK
    total = nqb * ROWS
    n_slc = sel.shape[3]
    blk_q = pl.BlockSpec((None, None, dh, SLC_QBLOCKS * ROWS), lambda i, j, q: (i, j, 0, q))
    return pl.pallas_call(
        _slc_body,
        grid=(b, g, nqb // SLC_QBLOCKS),
        in_specs=[_q_spec(SLC_QBLOCKS),
                  pl.BlockSpec((None, s_len, K_SLAB), lambda i, j, q: (i, 0, j)),
                  pl.BlockSpec((None, VT_ROWS, s_len), lambda i, j, q: (i, j, 0)),
                  pl.BlockSpec((None, None, SLC_QBLOCKS, n_slc, Q_BLOCK), lambda i, j, q: (i, j, q, 0, 0)),
                  pl.BlockSpec((None, 4, KEY_CHUNK, ROWS), lambda i, j, q: (j, 0, 0, 0)),
                  pl.BlockSpec((None, 1, ROWS), lambda i, j, q: (j, 0, 0))],
        out_specs=blk_q,
        out_shape=jax.ShapeDtypeStruct((b, g, dh, total), F32),
        scratch_shapes=[pltpu.VMEM((FAR_BUFFERS, FAR_KEYS, SLC_QBLOCKS * ROWS), BF16),
                        pltpu.VMEM((VT_ROWS, SLC_QBLOCKS * ROWS), F32)],
        compiler_params=_params(3),
        name="slc_attention",
    )(qt, ks, vst, sel, sb, crow)


WIN_CHUNKS = WINDOW // KEY_CHUNK + 1
WIN_QBLOCKS = 4


def _win_body(q_ref, k_ref, vt_ref, wb_ref, oc_ref, os_ref, gate_ref, o_ref):
    q_all = _group_slots(_heads_to_lanes(q_ref[...]), pl.program_id(1))
    for h in range(WIN_QBLOCKS):
        qi = pl.program_id(2) * WIN_QBLOCKS + h
        lanes = slice(h * ROWS, (h + 1) * ROWS)
        tokens = slice(h * Q_BLOCK, (h + 1) * Q_BLOCK)
        qt = q_all[:, lanes]
        tiles, vts = [], []
        for delta in range(WIN_CHUNKS - 1, -1, -1):
            c = jnp.maximum(qi - delta, 0)
            rows = pl.ds(pl.multiple_of(c * KEY_CHUNK, KEY_CHUNK), KEY_CHUNK)
            kind = jnp.where(qi >= delta, delta, WIN_CHUNKS)
            tiles.append(_dot(k_ref[rows, :], qt) + wb_ref[kind])
            vts.append(vt_ref[:, rows])
        s = jnp.concatenate(tiles, axis=0)
        m = jnp.max(s, axis=0, keepdims=True)
        p = jnp.exp2(s - m)
        l = jnp.sum(p, axis=0, keepdims=True)
        o_w = _dot(jnp.concatenate(vts, axis=1), p.astype(BF16)) / jnp.maximum(l, 1e-30)
        gate = lambda br: jnp.concatenate([gate_ref[br * GQA_GROUP + r:br * GQA_GROUP + r + 1, tokens]
                                           for r in range(GQA_GROUP)], axis=1)
        o = gate(0) * oc_ref[:, lanes] + gate(1) * os_ref[:, lanes] + gate(2) * o_w
        for r in range(GQA_GROUP):
            o_ref[r * HEAD_DIM:(r + 1) * HEAD_DIM, tokens] = o[:, r * Q_BLOCK:(r + 1) * Q_BLOCK].astype(o_ref.dtype)


def _win_mix(qt, kw, vwt, wb, oc, osl, gates):
    b, g, dh = qt.shape[0], N_KV_HEADS, HEAD_DIM
    s_len = kw.shape[1]
    blk_q = pl.BlockSpec((None, None, dh, WIN_QBLOCKS * ROWS), lambda i, j, q: (i, j, 0, q))
    return pl.pallas_call(
        _win_body,
        grid=(b, g, s_len // (WIN_QBLOCKS * Q_BLOCK)),
        in_specs=[_q_spec(WIN_QBLOCKS),
                  pl.BlockSpec((None, s_len, kw.shape[2]), lambda i, j, q: (i, 0, 0)),
                  pl.BlockSpec((None, dh, s_len), lambda i, j, q: (i, j, 0)),
                  pl.BlockSpec((None, WIN_CHUNKS + 1, KEY_CHUNK, ROWS), lambda i, j, q: (j, 0, 0, 0)),
                  blk_q, blk_q,
                  pl.BlockSpec((None, GATE_ROWS, WIN_QBLOCKS * Q_BLOCK), lambda i, j, q: (i, j, q))],
        out_specs=_q_spec(WIN_QBLOCKS),
        out_shape=jax.ShapeDtypeStruct(qt.shape, BF16),
        compiler_params=_params(3),
        name="window_mix",
    )(qt, kw, vwt, wb, oc, osl, gates)


FFN_TILE = 256


def _outproj_ffn_body(a_ref, gm_ref, h_ref, wa_ref, wb_ref, gmix_ref, gpre_ref, wg_ref, wu_ref, wd_ref, gpost_ref,
                      o_ref):
    y = _dot_tn(a_ref[...], wa_ref[...]) + _dot(gm_ref[...], wb_ref[...])
    h = h_ref[...] + _rms(y, gmix_ref[...])
    xb = _rms(h, gpre_ref[...]).astype(BF16)
    acc = jnp.zeros(h.shape, F32)
    for j in range(wg_ref.shape[1] // FFN_TILE):
        cols = slice(j * FFN_TILE, (j + 1) * FFN_TILE)
        gate = _dot(xb, wg_ref[:, cols])
        up = _dot(xb, wu_ref[:, cols])
        act = (gate * _sigmoid(gate) * up).astype(BF16)
        acc = acc + _dot(act, wd_ref[cols, :])
    o_ref[...] = h + _rms(acc, gpost_ref[...])


def _outproj_ffn(a, gm, h, weights, tm):
    n, d = h.shape
    per_b = a.shape[2] // tm
    row = lambda w: pl.BlockSpec((tm, w), lambda i: (i, 0))
    return pl.pallas_call(
        _outproj_ffn_body,
        grid=(n // tm,),
        in_specs=[pl.BlockSpec((None, a.shape[1], tm), lambda i: (i // per_b, 0, i % per_b)),
                  row(gm.shape[1]), row(d)] + [_resident(w) for w in weights],
        out_specs=row(d),
        out_shape=jax.ShapeDtypeStruct((n, d), F32),
        compiler_params=_params(1),
        name="outproj_ffn",
    )(a, gm, h, *weights)


def _t5_bucket(dist):
    n = jnp.maximum(dist, 0)
    max_exact = N_BUCKETS // 2
    nf = jnp.maximum(n, max_exact).astype(F32)
    large = max_exact + (jnp.log(nf / max_exact) / math.log(REL_MAX_DISTANCE / max_exact)
                         * (N_BUCKETS - max_exact)).astype(jnp.int32)
    return jnp.where(n < max_exact, n, jnp.minimum(large, N_BUCKETS - 1))


def _bias_tile(table, dist, mask):
    onehot = (_t5_bucket(dist)[..., None] == jnp.arange(N_BUCKETS, dtype=jnp.int32)).astype(F32)
    b = jnp.einsum("kqn,nh->kqh", onehot, table.astype(F32), precision=lax.Precision.HIGHEST)
    b = jnp.where(mask[..., None], b * LOG2E, NEG)
    k = dist.shape[0]
    return b.reshape(k, Q_BLOCK, N_KV_HEADS, GQA_GROUP).transpose(2, 0, 3, 1).reshape(N_KV_HEADS, k, ROWS)


def _bias_tiles(rel_bias):
    ql = jnp.arange(Q_BLOCK, dtype=jnp.int32)[None, :]
    kl = jnp.arange(KEY_CHUNK, dtype=jnp.int32)[:, None]
    chunk_dist = lambda delta: delta * KEY_CHUNK + ql - kl
    everything = jnp.ones((KEY_CHUNK, Q_BLOCK), bool)
    nothing = jnp.zeros((KEY_CHUNK, Q_BLOCK), bool)
    far = jnp.full((KEY_CHUNK, Q_BLOCK), REL_MAX_DISTANCE, jnp.int32)
    d0 = chunk_dist(0)
    sb = jnp.stack([_bias_tile(rel_bias, d0, d0 >= 0),
                    _bias_tile(rel_bias, chunk_dist(1), everything),
                    _bias_tile(rel_bias, far, everything),
                    _bias_tile(rel_bias, far, nothing)], axis=1)
    wtiles = []
    for delta in range(WIN_CHUNKS):
        d = chunk_dist(delta)
        wtiles.append(_bias_tile(rel_bias, d, (d >= 0) & (d < WINDOW)))
    wtiles.append(_bias_tile(rel_bias, far, nothing))
    wb = jnp.stack(wtiles, axis=1)
    rel = jnp.arange(CMP_WINDOW, dtype=jnp.int32)[:, None] - CMP_FRONT_PAD
    dc = ql - CMP_STRIDE * rel - (CMP_BLOCK - 1)
    cb = _bias_tile(rel_bias, dc, dc >= 0)
    crow = _bias_tile(rel_bias, far[:1], everything[:1])
    return sb, wb, cb, crow


def _mixer_ffn(h, b, s, lw, tiles, tm):
    sb, wb, cb, crow = tiles
    qt, cv, ksa, kw, vst, vwt, gates, gm = _inproj(
        h, b, (lw["gpre"], lw["wqt"], lw["wnat"], lw["wvt"], lw["wgt"], lw["wu"], lw["wv"],
               lw["lng"], lw["lnb"], lw["ws"], lw["bs"], _key_augmentation(tm)), tm)
    g, dh = N_KV_HEADS, HEAD_DIM
    n_chunks = s // CMP_STRIDE
    comp = _compress(cv, lw["cpos"], lw["cw1"], lw["cw2"])
    ncp = -(-(n_chunks + CMP_WINDOW) // CMP_CHUNK) * CMP_CHUNK
    back = ncp - CMP_FRONT_PAD - n_chunks
    comp = jnp.pad(comp, ((0, 0), (0, 0), (0, 0), (CMP_FRONT_PAD, back), (0, 0)))
    front = (jnp.arange(ncp) < CMP_FRONT_PAD).astype(F32)[:, None]
    aug = jnp.concatenate([jnp.ones((ncp, 2), F32), front, jnp.zeros((ncp, CMP_AUG_WIDTH - dh - 3), F32)], axis=1)
    kcp = jnp.concatenate([comp[:, 0], jnp.broadcast_to(aug, (b, g) + aug.shape)], axis=-1)
    vct = comp[:, 1].transpose(0, 1, 3, 2).astype(BF16)

    n_slc = s // SLC_BLOCK
    n_other = min(SLC_TOPK, n_slc) - (N_LOCAL_BLOCKS + 1)
    oc, sel = _cmp_select(qt, kcp, vct, crow, cb, n_slc, n_other)
    osl = _slc(qt, ksa.reshape(b, s, ksa.shape[1]), vst, sel, sb, crow)
    ot = _win_mix(qt, kw.reshape(b, s, kw.shape[1]), vwt, wb, oc, osl, gates)
    return _outproj_ffn(ot, gm, h, (lw["wo_a"], lw["wo_b"], lw["gpost"], lw["fpre"], lw["fwg"], lw["fwu"], lw["fwd"],
                                    lw["fpost"]), tm)


def _layer_weights(l, p):
    d = p["w_in"].shape[1]
    w_in = p["w_in"][l]
    o = NSA_WIDTH
    wqt = w_in[:, :o].T
    k_c, v_c, k_s, v_s, k_w, v_w = [w_in[:, o + i * KV_WIDTH:o + (i + 1) * KV_WIDTH] for i in range(6)]
    wnat = jnp.concatenate([k_c, v_c, k_s, k_w], axis=1)
    wvt = jnp.concatenate([v_s, v_w], axis=1).T
    o += 6 * KV_WIDTH
    wg = w_in[:, o:o + N_GATES].reshape(d, N_KV_HEADS, GQA_GROUP, N_BRANCHES).transpose(1, 3, 2, 0)
    wg = wg.reshape(N_KV_HEADS, N_BRANCHES * GQA_GROUP, d)
    wgt = jnp.pad(wg, ((0, 0), (0, GATE_ROWS - N_BRANCHES * GQA_GROUP), (0, 0))).reshape(N_KV_HEADS * GATE_ROWS, d)
    o += N_GATES
    gw = (w_in.shape[1] - o) // 2
    wu, wv = w_in[:, o:o + gw], w_in[:, o + gw:]
    causal = jnp.tril(jnp.ones((GMLP_CHUNK, GMLP_CHUNK), bool))
    ws = jnp.where(causal, p["gmlp_w_s"][l], 0.0)
    bs = jnp.repeat(p["gmlp_b_s"][l].T, gw // N_GMLP_GROUPS, axis=1)
    half = CMP_STRIDE * HEAD_DIM
    cpos = jnp.stack([p["cmp_pos_k"][l].reshape(2, half), p["cmp_pos_v"][l].reshape(2, half)])
    dff = p["w_down"].shape[1]
    row = lambda v: v[l].reshape(1, -1)
    bf = lambda w: w.astype(BF16)
    return dict(gpre=row(p["norm_mix_pre"]), wqt=bf(wqt), wnat=bf(wnat), wvt=bf(wvt), wgt=bf(wgt), wu=bf(wu), wv=bf(wv),
                lng=row(p["gmlp_ln_g"]), lnb=row(p["gmlp_ln_b"]), ws=bf(ws), bs=bs,
                cpos=cpos, cw1=bf(jnp.stack([p["cmp_w1_k"][l], p["cmp_w1_v"][l]])),
                cw2=bf(jnp.stack([p["cmp_w2_k"][l], p["cmp_w2_v"][l]])),
                wo_a=bf(p["w_out"][l][:NSA_WIDTH]), wo_b=bf(p["w_out"][l][NSA_WIDTH:]),
                gpost=row(p["norm_mix_post"]), fpre=row(p["norm_ffn_pre"]), fpost=row(p["norm_ffn_post"]),
                fwg=bf(p["w_gate_up"][l][:, :dff]), fwu=bf(p["w_gate_up"][l][:, dff:]), fwd=bf(p["w_down"][l]))


def _trunk(p, tm):
    x = p["x"]
    b, s, d = x.shape
    h = x.reshape(b * s, d)
    tiles = _bias_tiles(p["rel_bias"])
    for l in range(p["w_in"].shape[0]):
        lw = _layer_weights(l, p)
        h = _mixer_ffn(h, b, s, lw, tiles, tm)
    return h.reshape(b, s, d)


def kernel(x, rel_bias, norm_mix_pre, norm_mix_post, norm_ffn_pre, norm_ffn_post, w_in, cmp_pos_k, cmp_w1_k, cmp_w2_k, cmp_pos_v, cmp_w1_v, cmp_w2_v, gmlp_ln_g, gmlp_ln_b, gmlp_w_s, gmlp_b_s, w_out, w_gate_up, w_down):
    p = dict(x=x, rel_bias=rel_bias, norm_mix_pre=norm_mix_pre, norm_mix_post=norm_mix_post,
             norm_ffn_pre=norm_ffn_pre, norm_ffn_post=norm_ffn_post, w_in=w_in,
             cmp_pos_k=cmp_pos_k, cmp_w1_k=cmp_w1_k, cmp_w2_k=cmp_w2_k,
             cmp_pos_v=cmp_pos_v, cmp_w1_v=cmp_w1_v, cmp_w2_v=cmp_w2_v,
             gmlp_ln_g=gmlp_ln_g, gmlp_ln_b=gmlp_ln_b, gmlp_w_s=gmlp_w_s, gmlp_b_s=gmlp_b_s,
             w_out=w_out, w_gate_up=w_gate_up, w_down=w_down)
    return _trunk(p, tm=512)
```

```python
import functools
import math

import jax
import jax.numpy as jnp
import numpy as np
from jax import lax
from jax.experimental import pallas as pl
from jax.experimental.pallas import tpu as pltpu

F32 = jnp.float32
BF16 = jnp.bfloat16

N_NSA_HEADS = 8
N_KV_HEADS = 2
GQA_GROUP = N_NSA_HEADS // N_KV_HEADS
HEAD_DIM = 64
NSA_WIDTH = N_NSA_HEADS * HEAD_DIM
KV_WIDTH = N_KV_HEADS * HEAD_DIM
N_BRANCHES = 3
N_GATES = N_BRANCHES * N_NSA_HEADS
CMP_BLOCK = 32
CMP_STRIDE = 16
SLC_BLOCK = 64
SLC_TOPK = 16
N_LOCAL_BLOCKS = 2
WINDOW = 512
Q_BLOCK = 128
N_GMLP_GROUPS = 8
GMLP_CHUNK = 128
N_BUCKETS = 32
REL_MAX_DISTANCE = 128
RMS_EPS = 1e-6
LN_EPS = 1e-5

ROWS = GQA_GROUP * Q_BLOCK
KEY_CHUNK = 128
CMP_FRONT_PAD = 24
CMP_WINDOW = 32
NEG = -1e30
LANE = 128
GATE_PAD = 128
MXU_TILE = 256
BF16_ROWS = 16
VMEM_LIMIT = 48 * 1024 * 1024
LOG2E = math.log2(math.e)
Q_SCALE = HEAD_DIM ** -0.5 * LOG2E


def _dot(a, b):
    return jnp.dot(a, b, preferred_element_type=F32)


def _gelu(x):
    c = math.sqrt(2.0 / math.pi)
    return 0.5 * x * (1.0 + jnp.tanh(c * (x + 0.044715 * (x * x * x))))


def _sigmoid(x):
    return 1.0 / (1.0 + jnp.exp(-x))


def _rms(x, g):
    ms = jnp.mean(x * x, axis=-1, keepdims=True)
    return (x * lax.rsqrt(ms + RMS_EPS)) * g


def _resident(a):
    return pl.BlockSpec(a.shape, lambda i: (0,) * a.ndim, pipeline_mode=pl.Buffered(1))


def _params(n_axes):
    return pltpu.CompilerParams(dimension_semantics=("arbitrary",) * n_axes,
                                vmem_limit_bytes=VMEM_LIMIT)


def _dot_nt(a, b):
    return lax.dot_general(a, b, (((1,), (1,)), ((), ())), preferred_element_type=F32)


def _dot_tn(a, b):
    return lax.dot_general(a, b, (((0,), (0,)), ((), ())), preferred_element_type=F32)


def _inproj_body(x_ref, gpre_ref, wqt_ref, wnat_ref, wvt_ref, wgt_ref, wu_ref, wv_ref, lng_ref, lnb_ref,
                 ws_ref, bs_ref, aug_ref, qt_ref, cv_ref, ksa_ref, kw_ref, vst_ref, vwt_ref, gate_ref, gm_ref, cv_sc):
    tm = x_ref.shape[0]
    xb = _rms(x_ref[...], gpre_ref[...]).astype(BF16)
    qt_ref[...] = (_dot_nt(wqt_ref[...], xb) * Q_SCALE).astype(BF16)
    nat = _dot(xb, wnat_ref[...])
    chunks = tm // CMP_STRIDE
    left = lax.broadcasted_iota(jnp.int32, (chunks, LANE), 1) < HEAD_DIM
    for kv in range(2):
        cv_sc[kv] = nat[:, kv * KV_WIDTH:(kv + 1) * KV_WIDTH]
    for kv in range(2):
        for pair in range(CMP_STRIDE // 2):
            lanes = slice(pair * LANE, (pair + 1) * LANE)
            a = cv_sc[kv, pl.ds(2 * pair, chunks, stride=CMP_STRIDE), :]
            b = cv_sc[kv, pl.ds(2 * pair + 1, chunks, stride=CMP_STRIDE), :]
            cv_ref[kv, 0, :, lanes] = jnp.where(left, a, pltpu.roll(b, HEAD_DIM, 1))
            cv_ref[kv, 1, :, lanes] = jnp.where(left, pltpu.roll(a, HEAD_DIM, 1), b)
    k_slc = nat[:, 2 * KV_WIDTH:3 * KV_WIDTH].astype(BF16)
    ksa_ref[...] = jnp.concatenate(
        [piece for g in range(N_KV_HEADS) for piece in (k_slc[:, g * HEAD_DIM:(g + 1) * HEAD_DIM], aug_ref[...])], axis=1)
    kw_ref[...] = nat[:, 3 * KV_WIDTH:].astype(BF16)
    vt = _dot_nt(wvt_ref[...], xb).astype(BF16)
    denom = jnp.concatenate([jnp.ones((1, tm), BF16), jnp.zeros((BF16_ROWS - 1, tm), BF16)], axis=0)
    vst_ref[...] = jnp.concatenate(
        [piece for g in range(N_KV_HEADS) for piece in (vt[g * HEAD_DIM:(g + 1) * HEAD_DIM], denom)], axis=0)
    vwt_ref[...] = vt[KV_WIDTH:]
    gate_ref[...] = _sigmoid(_dot_nt(wgt_ref[...], xb))
    zu = _gelu(_dot(xb, wu_ref[...]))
    zv = _gelu(_dot(xb, wv_ref[...]))
    mu = jnp.mean(zv, axis=-1, keepdims=True)
    zc = zv - mu
    var = jnp.mean(zc * zc, axis=-1, keepdims=True)
    zv = ((zc * lax.rsqrt(var + LN_EPS)) * lng_ref[...] + lnb_ref[...]).astype(BF16)
    gdim = zv.shape[1] // N_GMLP_GROUPS
    left = lax.broadcasted_iota(jnp.int32, (GMLP_CHUNK, LANE), 1) < gdim
    for c in range(tm // GMLP_CHUNK):
        rows = slice(c * GMLP_CHUNK, (c + 1) * GMLP_CHUNK)
        for j in range(zv.shape[1] // LANE):
            cols = slice(j * LANE, (j + 1) * LANE)
            z = zv[rows, cols]
            sv = jnp.where(left, _dot(ws_ref[2 * j], z), _dot(ws_ref[2 * j + 1], z)) + bs_ref[:, cols]
            gm_ref[rows, cols] = (zu[rows, cols] * sv).astype(BF16)


IN_TILE = 1024
VT_ROWS = HEAD_DIM + BF16_ROWS
GATE_ROWS = 16


def _inproj(x, b, weights, tm):
    n, d = x.shape
    s = n // b
    per_b = s // tm
    row = lambda w: pl.BlockSpec((tm, w), lambda i: (i, 0))
    col = lambda h: pl.BlockSpec((None, h, tm), lambda i: (i // per_b, 0, i % per_b))
    gw = weights[5].shape[1]
    flat = CMP_STRIDE * HEAD_DIM
    assert KV_WIDTH == LANE and N_KV_HEADS == 2
    return pl.pallas_call(
        _inproj_body,
        grid=(n // tm,),
        in_specs=[row(d)] + [_resident(a) for a in weights],
        out_specs=[col(NSA_WIDTH),
                   pl.BlockSpec((None, 2, N_KV_HEADS, tm // CMP_STRIDE, flat),
                                lambda i: (i // per_b, 0, 0, i % per_b, 0)),
                   row(K_AUG_WIDTH), row(KV_WIDTH),
                   col(N_KV_HEADS * VT_ROWS), col(KV_WIDTH), col(N_KV_HEADS * GATE_ROWS), row(gw)],
        out_shape=[jax.ShapeDtypeStruct((b, NSA_WIDTH, s), BF16),
                   jax.ShapeDtypeStruct((b, 2, N_KV_HEADS, s // CMP_STRIDE, flat), F32),
                   jax.ShapeDtypeStruct((n, K_AUG_WIDTH), BF16),
                   jax.ShapeDtypeStruct((n, KV_WIDTH), BF16),
                   jax.ShapeDtypeStruct((b, N_KV_HEADS * VT_ROWS, s), BF16),
                   jax.ShapeDtypeStruct((b, KV_WIDTH, s), BF16),
                   jax.ShapeDtypeStruct((b, N_KV_HEADS * GATE_ROWS, s), F32),
                   jax.ShapeDtypeStruct((n, gw), BF16)],
        scratch_shapes=[pltpu.VMEM((2, tm, KV_WIDTH), F32)],
        compiler_params=_params(1),
        name="inproj_gmlp",
    )(x, *weights)


def _compress_body(x_ref, pos_ref, w1_ref, w2_ref, o_ref):
    x = x_ref[...]
    half = x.shape[1]
    a = _dot((x + pos_ref[0:1, :]).astype(BF16), w1_ref[:half, :])
    b = _dot((x + pos_ref[1:2, :]).astype(BF16), w1_ref[half:, :])
    pre = a + pltpu.roll(b, x.shape[0] - 1, 0)
    o_ref[...] = _dot(_gelu(pre).astype(BF16), w2_ref[...])


def _compress(xc, pos, w1, w2):
    b, two, g, nch, width = xc.shape
    hid = w1.shape[2]
    dh = w2.shape[2]
    return pl.pallas_call(
        _compress_body,
        grid=(b, two, g),
        in_specs=[pl.BlockSpec((None, None, None, nch, width), lambda i, t, j: (i, t, j, 0, 0)),
                  pl.BlockSpec((None, 2, width), lambda i, t, j: (t, 0, 0)),
                  pl.BlockSpec((None, 2 * width, hid), lambda i, t, j: (t, 0, 0)),
                  pl.BlockSpec((None, hid, dh), lambda i, t, j: (t, 0, 0))],
        out_specs=pl.BlockSpec((None, None, None, nch, dh), lambda i, t, j: (i, t, j, 0, 0)),
        out_shape=jax.ShapeDtypeStruct((b, two, g, nch, dh), F32),
        compiler_params=_params(3),
        name="compress",
    )(xc, pos, w1, w2)


CMP_CHUNK = 256
CMP_AUG_WIDTH = 2 * HEAD_DIM


def _heads_to_lanes(blk):
    n = blk.shape[1] // Q_BLOCK
    return jnp.concatenate([blk[r * HEAD_DIM:(r + 1) * HEAD_DIM, h * Q_BLOCK:(h + 1) * Q_BLOCK]
                            for h in range(n) for r in range(GQA_GROUP)], axis=1)


def _group_slots(qt, g):
    return jnp.concatenate([jnp.where(g == j, qt, jnp.zeros_like(qt)) for j in range(N_KV_HEADS)], axis=0)


def _q_spec(n_blocks):
    return pl.BlockSpec((None, GQA_GROUP * HEAD_DIM, n_blocks * Q_BLOCK), lambda i, j, q: (i, j, q))


CMP_QBLOCKS = 4
SUBLANES = 8
SORT_KEEP = 16


def _bitonic_merge(xs):
    if len(xs) == 1:
        return xs
    half = len(xs) // 2
    hi = [jnp.maximum(xs[i], xs[i + half]) for i in range(half)]
    lo = [jnp.minimum(xs[i], xs[i + half]) for i in range(half)]
    return _bitonic_merge(hi) + _bitonic_merge(lo)


def _bitonic_sort(xs):
    if len(xs) == 1:
        return xs
    half = len(xs) // 2
    return _bitonic_merge(_bitonic_sort(xs[:half]) + _bitonic_sort(xs[half:])[::-1])


def _top_of_two(a, b):
    n = len(a)
    return _bitonic_merge([jnp.maximum(a[i], b[n - 1 - i]) for i in range(n)])


def _kth_largest(x, k):
    assert k <= SORT_KEEP and x.shape[0] % SUBLANES == 0
    tiles = [x[i * SUBLANES:(i + 1) * SUBLANES] for i in range(x.shape[0] // SUBLANES)]
    tiles += [jnp.full_like(tiles[0], -2.0)] * (-len(tiles) % SORT_KEEP)
    top = _bitonic_sort(tiles[:SORT_KEEP])
    for j in range(SORT_KEEP, len(tiles), SORT_KEEP):
        top = _top_of_two(top, _bitonic_sort(tiles[j:j + SORT_KEEP]))
    shift = SUBLANES // 2
    while shift:
        top = _top_of_two(top, [pltpu.roll(v, shift, 0) for v in top])
        shift //= 2
    return top[k - 1][0:1]


def _cmp_select_body(q_ref, k_ref, vt_ref, crow_ref, cb_ref, oc_ref, sel_ref, sc_ref, *, n_other):
    qis = [pl.program_id(2) * CMP_QBLOCKS + h for h in range(CMP_QBLOCKS)]
    n_slc = sel_ref.shape[1]
    qt = _heads_to_lanes(q_ref[...])
    dh, lanes = qt.shape
    w0s = [pl.multiple_of(qi * (Q_BLOCK // CMP_STRIDE), 8) for qi in qis]
    n_chunks = (w0s[-1] + CMP_WINDOW + CMP_CHUNK - 1) // CMP_CHUNK
    row_iota = lax.broadcasted_iota(jnp.int32, (CMP_CHUNK, lanes), 0)
    lane_blk = lax.broadcasted_iota(jnp.int32, (1, lanes), 1) // ROWS
    w0_lane = w0s[0]
    for h in range(1, CMP_QBLOCKS):
        w0_lane = jnp.where(lane_blk >= h, w0s[h], w0_lane)

    @pl.when(qis[0] == 0)
    def _():
        sc_ref[...] = jnp.zeros_like(sc_ref)

    c = jnp.concatenate([crow_ref[...]] * CMP_QBLOCKS, axis=1)
    c_hi = c.astype(BF16).astype(F32)
    neg_row = jnp.full((1, lanes), NEG, F32)
    zeros = lambda n, dt: jnp.zeros((n, lanes), dt)
    tail = zeros(k_ref.shape[1] - dh - BF16_ROWS, BF16)
    qa = jnp.concatenate([qt, jnp.concatenate([c_hi, c - c_hi, neg_row, zeros(BF16_ROWS - 3, F32)]).astype(BF16), tail])
    qw = jnp.concatenate([qt, jnp.concatenate([zeros(2, F32), neg_row, zeros(BF16_ROWS - 3, F32)]).astype(BF16), tail])

    def rows_of(ch):
        return pl.ds(pl.multiple_of(ch * CMP_CHUNK, CMP_CHUNK), CMP_CHUNK)

    n_slabs = lanes // Q_BLOCK
    slab = lambda i: slice(i * Q_BLOCK, (i + 1) * Q_BLOCK)

    def put(rows, val, first=0):
        for i in range(val.shape[1] // Q_BLOCK):
            sc_ref[first + i, rows, :] = val[:, slab(i)]

    def get(rows):
        return jnp.concatenate([sc_ref[i, rows, :] for i in range(n_slabs)], axis=1)

    def logits(ch, m):
        s = _dot(k_ref[rows_of(ch), :].astype(BF16), qa)
        s = jnp.where(row_iota + ch * CMP_CHUNK < w0_lane, s, NEG)
        put(rows_of(ch), s)
        return jnp.maximum(m, jnp.max(s, axis=0, keepdims=True))

    m = lax.fori_loop(0, n_chunks, logits, jnp.full((1, lanes), NEG, F32))
    win_max = []
    for h, w0 in enumerate(w0s):
        win = pl.ds(w0, CMP_WINDOW)
        s_win = _dot(k_ref[win, :].astype(BF16), qw[:, h * ROWS:(h + 1) * ROWS]) + cb_ref[...]
        put(win, s_win, first=h * GQA_GROUP)
        win_max.append(jnp.max(s_win, axis=0, keepdims=True))
    m = jnp.maximum(m, jnp.concatenate(win_max, axis=1))

    def weigh(ch, l):
        p = jnp.exp2(get(rows_of(ch)) - m)
        put(rows_of(ch), p)
        oc_ref[...] += _dot(vt_ref[:, rows_of(ch)], p.astype(BF16))
        return l + jnp.sum(p, axis=0, keepdims=True)

    oc_ref[...] = jnp.zeros_like(oc_ref)
    l = lax.fori_loop(0, n_chunks, weigh, jnp.zeros((1, lanes), F32))
    scale = jnp.where(m > 0.5 * NEG, 1.0 / jnp.maximum(l, 1e-30), 0.0)
    oc_ref[...] = oc_ref[...] * scale

    per = SLC_BLOCK // CMP_STRIDE
    blk = lax.broadcasted_iota(jnp.int32, (n_slc, Q_BLOCK), 0)
    blk_f = blk.astype(F32)
    lane_pos = lax.broadcasted_iota(jnp.int32, (n_slc, Q_BLOCK), 1)
    def candidates(h):
        imp = jnp.zeros((n_slc, Q_BLOCK), F32)
        for r in range(GQA_GROUP):
            i = h * GQA_GROUP + r
            part = lambda off: sc_ref[i, pl.ds(CMP_FRONT_PAD + off, n_slc, stride=per), :]
            tot = part(0)
            for k in range(1, per - 1):
                tot = tot + part(k)
            imp = imp + (tot + 0.5 * (part(per - 1) + part(-1))) * scale[:, slab(i)]
        jq = (qis[h] * Q_BLOCK + lane_pos) // SLC_BLOCK
        valid = blk <= jq
        forced = (blk == 0) | (valid & (blk > jq - N_LOCAL_BLOCKS))
        free = valid & jnp.logical_not(forced)
        return jnp.where(free, imp, -1.0), forced, free

    tied = []
    for h in range(CMP_QBLOCKS):
        work, forced, free = candidates(h)
        t = _kth_largest(work, n_other)
        picked = (work > t) | (free & (work == t))
        sel_ref[h] = jnp.where(forced | picked, 1.0, 0.0)
        tied.append(jnp.max(jnp.sum(jnp.where(picked, 1.0, 0.0), axis=0, keepdims=True)) > n_other)

    for h in range(CMP_QBLOCKS):
        @pl.when(tied[h])
        def _(h=h):
            work, forced, free = candidates(h)
            for _ in range(n_other):
                mx = jnp.max(work, axis=0, keepdims=True)
                first = jnp.min(jnp.where(work == mx, blk_f, float(n_slc)), axis=0, keepdims=True)
                work = jnp.where((blk_f == first) & (mx >= 0.0), -1.0, work)
            sel_ref[h] = jnp.where(forced | (free & (work < 0.0)), 1.0, 0.0)


def _cmp_select(qt, kcp, vct, crow, cb, n_slc, n_other):
    b, g, dh = qt.shape[0], N_KV_HEADS, HEAD_DIM
    nqb = qt.shape[2] // Q_BLOCK
    total = nqb * ROWS
    ncp = kcp.shape[2]
    blk_q = pl.BlockSpec((None, None, dh, CMP_QBLOCKS * ROWS), lambda i, j, q: (i, j, 0, q))
    return pl.pallas_call(
        functools.partial(_cmp_select_body, n_other=n_other),
        grid=(b, g, nqb // CMP_QBLOCKS),
        in_specs=[_q_spec(CMP_QBLOCKS),
                  pl.BlockSpec((None, None, ncp, kcp.shape[3]), lambda i, j, q: (i, j, 0, 0)),
                  pl.BlockSpec((None, None, dh, ncp), lambda i, j, q: (i, j, 0, 0)),
                  pl.BlockSpec((None, 1, ROWS), lambda i, j, q: (j, 0, 0)),
                  pl.BlockSpec((None, CMP_WINDOW, ROWS), lambda i, j, q: (j, 0, 0))],
        out_specs=[blk_q,
                   pl.BlockSpec((None, None, CMP_QBLOCKS, n_slc, Q_BLOCK), lambda i, j, q: (i, j, q, 0, 0))],
        out_shape=[jax.ShapeDtypeStruct((b, g, dh, total), F32),
                   jax.ShapeDtypeStruct((b, g, nqb, n_slc, Q_BLOCK), F32)],
        scratch_shapes=[pltpu.VMEM((CMP_QBLOCKS * GQA_GROUP, ncp, Q_BLOCK), F32)],
        compiler_params=_params(3),
        name="cmp_select",
    )(qt, kcp, vct, crow, cb)


SLC_QBLOCKS = 2
FAR_KEYS = 256
FAR_BLOCKS = FAR_KEYS // SLC_BLOCK
FAR_CHUNKS = FAR_KEYS // KEY_CHUNK
FAR_BUFFERS = 4
FAR_AHEAD = 2
PEN_BLOCKS = 8
AUG_CONST = 2
AUG_ROWS = 16
K_SLAB = LANE
K_AUG_WIDTH = N_KV_HEADS * K_SLAB


def _key_augmentation(tm):
    assert tm % (PEN_BLOCKS * SLC_BLOCK) == 0
    pos = jnp.arange(tm, dtype=jnp.int32)
    onehot = ((pos[:, None] // SLC_BLOCK) % PEN_BLOCKS == jnp.arange(PEN_BLOCKS, dtype=jnp.int32)[None, :])
    return jnp.concatenate([jnp.ones((tm, AUG_CONST), BF16),
                            jnp.zeros((tm, AUG_ROWS - PEN_BLOCKS - AUG_CONST), BF16),
                            onehot.astype(BF16),
                            jnp.zeros((tm, K_SLAB - HEAD_DIM - AUG_ROWS), BF16)], axis=1)


def _slc_body(q_ref, k_ref, vt_ref, sel_ref, sb_ref, crow_ref, os_ref, sbuf, acc_ref):
    qis = [pl.program_id(2) * SLC_QBLOCKS + h for h in range(SLC_QBLOCKS)]
    qt = _heads_to_lanes(q_ref[...])
    dh, lanes = qt.shape
    per = KEY_CHUNK // SLC_BLOCK
    far_limits = [jnp.maximum(qi - 1, 0) * per for qi in qis]
    n_steps = (jnp.maximum(qis[-1] - 1, 0) + FAR_CHUNKS - 1) // FAR_CHUNKS
    last_step = k_ref.shape[0] // FAR_KEYS - 1
    steps_per_group = PEN_BLOCKS // FAR_BLOCKS

    c = jnp.concatenate([crow_ref[...]] * SLC_QBLOCKS, axis=1)
    c_hi = c.astype(BF16).astype(F32)
    const_rows = jnp.concatenate([c_hi, c - c_hi, jnp.zeros((AUG_ROWS - PEN_BLOCKS - AUG_CONST, lanes), F32)], axis=0)
    pad_rows = jnp.zeros((k_ref.shape[1] - dh - AUG_ROWS, lanes), BF16)
    blk_iota = lax.broadcasted_iota(jnp.int32, (PEN_BLOCKS, Q_BLOCK), 0)

    def far_logits(u):
        ua = jnp.minimum(u, last_step)
        grp0 = pl.multiple_of((ua // steps_per_group) * PEN_BLOCKS, PEN_BLOCKS)
        blk = blk_iota + (u // steps_per_group) * PEN_BLOCKS
        pens = []
        for h in range(SLC_QBLOCKS):
            pen = jnp.where((sel_ref[h, pl.ds(grp0, PEN_BLOCKS), :] > 0.5) & (blk < far_limits[h]), 0.0, NEG)
            pens += [pen] * GQA_GROUP
        qa = jnp.concatenate([qt, jnp.concatenate([const_rows, jnp.concatenate(pens, axis=1)], axis=0).astype(BF16),
                              pad_rows], axis=0)
        return _dot(k_ref[pl.ds(pl.multiple_of(ua * FAR_KEYS, FAR_KEYS), FAR_KEYS), :], qa)

    def stage(slot, u):
        s_new = far_logits(u).astype(BF16)
        sbuf[slot] = s_new
        groups = s_new.reshape(FAR_KEYS // BF16_ROWS, BF16_ROWS, lanes)
        return jnp.max(jnp.max(groups, axis=0).astype(F32), axis=0, keepdims=True)

    def far_update(m, s_ref, mx, key0):
        m_new = jnp.maximum(m, mx)
        p = jnp.exp2(s_ref[...] - m_new.astype(BF16))
        cols = pl.ds(pl.multiple_of(key0, FAR_KEYS), FAR_KEYS)
        acc_ref[...] = jnp.exp2(m - m_new) * acc_ref[...] + _dot(vt_ref[:, cols], p)
        return m_new

    def far_round(v, carry, staged_slots=FAR_BUFFERS):
        m = carry[0]
        ahead = list(carry[1:])
        for slot in range(FAR_BUFFERS):
            u = FAR_BUFFERS * v + slot
            if slot < staged_slots:
                ahead.append(stage((slot + FAR_AHEAD) % FAR_BUFFERS, u + FAR_AHEAD))
            ua = jnp.minimum(u, last_step)
            m = far_update(m, sbuf.at[slot], ahead.pop(0), ua * FAR_KEYS)
        return (m, *ahead)

    ahead = [stage(u, u) for u in range(FAR_AHEAD)]

    m_parts = []
    for h, qi in enumerate(qis):
        cols = slice(h * ROWS, (h + 1) * ROWS)
        qd = jnp.concatenate([qt[:, cols], jnp.zeros((k_ref.shape[1] - dh, ROWS), BF16)], axis=0)
        prev = jnp.maximum(qi - 1, 0)
        tiles, vts = [], []
        for chunk, kind in ((prev, jnp.where(qi >= 1, 1, 3)), (qi, 0)):
            rows = pl.ds(pl.multiple_of(chunk * KEY_CHUNK, KEY_CHUNK), KEY_CHUNK)
            s = _dot(k_ref[rows, :], qd) + sb_ref[kind]
            for j in range(per):
                srow = sel_ref[h, pl.ds(chunk * per + j, 1), :]
                srow = jnp.concatenate([srow] * GQA_GROUP, axis=1)
                tiles.append(jnp.where(srow > 0.5, s[j * SLC_BLOCK:(j + 1) * SLC_BLOCK, :], NEG))
            vts.append(vt_ref[:, rows])
        s = jnp.concatenate(tiles, axis=0)
        m_h = jnp.max(s, axis=0, keepdims=True)
        acc_ref[:, cols] = _dot(jnp.concatenate(vts, axis=1), jnp.exp2(s - m_h).astype(BF16))
        m_parts.append(m_h)

    init = (jnp.concatenate(m_parts, axis=1), *ahead)
    rounds = (n_steps + FAR_BUFFERS - 1) // FAR_BUFFERS
    carry = lax.fori_loop(0, jnp.maximum(rounds - 1, 0), far_round, init)
    lax.cond(rounds > 0,
             lambda c: far_round(rounds - 1, c, staged_slots=FAR_BUFFERS - FAR_AHEAD)[0],
             lambda c: c[0], carry)
    acc = acc_ref[...]
    os_ref[...] = acc[:dh] / jnp.maximum(acc[dh:dh + 1], 1e-30)


def _slc(qt, ks, vst, sel, sb, crow):
    b, g, dh = qt.shape[0], N_KV_HEADS, HEAD_DIM
    s_len = ks.shape[1]
    nqb = s_len // Q_BLOCK
    total = nqb * ROWS
    n_slc = sel.shape[3]
    blk_q = pl.BlockSpec((None, None, dh, SLC_QBLOCKS * ROWS), lambda i, j, q: (i, j, 0, q))
    return pl.pallas_call(
        _slc_body,
        grid=(b, g, nqb // SLC_QBLOCKS),
        in_specs=[_q_spec(SLC_QBLOCKS),
                  pl.BlockSpec((None, s_len, K_SLAB), lambda i, j, q: (i, 0, j)),
                  pl.BlockSpec((None, VT_ROWS, s_len), lambda i, j, q: (i, j, 0)),
                  pl.BlockSpec((None, None, SLC_QBLOCKS, n_slc, Q_BLOCK), lambda i, j, q: (i, j, q, 0, 0)),
                  pl.BlockSpec((None, 4, KEY_CHUNK, ROWS), lambda i, j, q: (j, 0, 0, 0)),
                  pl.BlockSpec((None, 1, ROWS), lambda i, j, q: (j, 0, 0))],
        out_specs=blk_q,
        out_shape=jax.ShapeDtypeStruct((b, g, dh, total), F32),
        scratch_shapes=[pltpu.VMEM((FAR_BUFFERS, FAR_KEYS, SLC_QBLOCKS * ROWS), BF16),
                        pltpu.VMEM((VT_ROWS, SLC_QBLOCKS * ROWS), F32)],
        compiler_params=_params(3),
        name="slc_attention",
    )(qt, ks, vst, sel, sb, crow)


WIN_CHUNKS = WINDOW // KEY_CHUNK + 1
WIN_QBLOCKS = 4


def _win_body(q_ref, k_ref, vt_ref, wb_ref, oc_ref, os_ref, gate_ref, o_ref):
    q_all = _group_slots(_heads_to_lanes(q_ref[...]), pl.program_id(1))
    for h in range(WIN_QBLOCKS):
        qi = pl.program_id(2) * WIN_QBLOCKS + h
        lanes = slice(h * ROWS, (h + 1) * ROWS)
        tokens = slice(h * Q_BLOCK, (h + 1) * Q_BLOCK)
        qt = q_all[:, lanes]
        tiles, vts = [], []
        for delta in range(WIN_CHUNKS - 1, -1, -1):
            c = jnp.maximum(qi - delta, 0)
            rows = pl.ds(pl.multiple_of(c * KEY_CHUNK, KEY_CHUNK), KEY_CHUNK)
            kind = jnp.where(qi >= delta, delta, WIN_CHUNKS)
            tiles.append(_dot(k_ref[rows, :], qt) + wb_ref[kind])
            vts.append(vt_ref[:, rows])
        s = jnp.concatenate(tiles, axis=0)
        m = jnp.max(s, axis=0, keepdims=True)
        p = jnp.exp2(s - m)
        l = jnp.sum(p, axis=0, keepdims=True)
        o_w = _dot(jnp.concatenate(vts, axis=1), p.astype(BF16)) / jnp.maximum(l, 1e-30)
        gate = lambda br: jnp.concatenate([gate_ref[br * GQA_GROUP + r:br * GQA_GROUP + r + 1, tokens]
                                           for r in range(GQA_GROUP)], axis=1)
        o = gate(0) * oc_ref[:, lanes] + gate(1) * os_ref[:, lanes] + gate(2) * o_w
        for r in range(GQA_GROUP):
            o_ref[r * HEAD_DIM:(r + 1) * HEAD_DIM, tokens] = o[:, r * Q_BLOCK:(r + 1) * Q_BLOCK].astype(o_ref.dtype)


def _win_mix(qt, kw, vwt, wb, oc, osl, gates):
    b, g, dh = qt.shape[0], N_KV_HEADS, HEAD_DIM
    s_len = kw.shape[1]
    blk_q = pl.BlockSpec((None, None, dh, WIN_QBLOCKS * ROWS), lambda i, j, q: (i, j, 0, q))
    return pl.pallas_call(
        _win_body,
        grid=(b, g, s_len // (WIN_QBLOCKS * Q_BLOCK)),
        in_specs=[_q_spec(WIN_QBLOCKS),
                  pl.BlockSpec((None, s_len, kw.shape[2]), lambda i, j, q: (i, 0, 0)),
                  pl.BlockSpec((None, dh, s_len), lambda i, j, q: (i, j, 0)),
                  pl.BlockSpec((None, WIN_CHUNKS + 1, KEY_CHUNK, ROWS), lambda i, j, q: (j, 0, 0, 0)),
                  blk_q, blk_q,
                  pl.BlockSpec((None, GATE_ROWS, WIN_QBLOCKS * Q_BLOCK), lambda i, j, q: (i, j, q))],
        out_specs=_q_spec(WIN_QBLOCKS),
        out_shape=jax.ShapeDtypeStruct(qt.shape, BF16),
        compiler_params=_params(3),
        name="window_mix",
    )(qt, kw, vwt, wb, oc, osl, gates)


FFN_TILE = 256


def _outproj_ffn_body(a_ref, gm_ref, h_ref, wa_ref, wb_ref, gmix_ref, gpre_ref, wg_ref, wu_ref, wd_ref, gpost_ref,
                      o_ref):
    y = _dot_tn(a_ref[...], wa_ref[...]) + _dot(gm_ref[...], wb_ref[...])
    h = h_ref[...] + _rms(y, gmix_ref[...])
    xb = _rms(h, gpre_ref[...]).astype(BF16)
    acc = jnp.zeros(h.shape, F32)
    for j in range(wg_ref.shape[1] // FFN_TILE):
        cols = slice(j * FFN_TILE, (j + 1) * FFN_TILE)
        gate = _dot(xb, wg_ref[:, cols])
        up = _dot(xb, wu_ref[:, cols])
        act = (gate * _sigmoid(gate) * up).astype(BF16)
        acc = acc + _dot(act, wd_ref[cols, :])
    o_ref[...] = h + _rms(acc, gpost_ref[...])


def _outproj_ffn(a, gm, h, weights, tm):
    n, d = h.shape
    per_b = a.shape[2] // tm
    row = lambda w: pl.BlockSpec((tm, w), lambda i: (i, 0))
    return pl.pallas_call(
        _outproj_ffn_body,
        grid=(n // tm,),
        in_specs=[pl.BlockSpec((None, a.shape[1], tm), lambda i: (i // per_b, 0, i % per_b)),
                  row(gm.shape[1]), row(d)] + [_resident(w) for w in weights],
        out_specs=row(d),
        out_shape=jax.ShapeDtypeStruct((n, d), F32),
        compiler_params=_params(1),
        name="outproj_ffn",
    )(a, gm, h, *weights)


def _t5_bucket(dist):
    n = jnp.maximum(dist, 0)
    max_exact = N_BUCKETS // 2
    nf = jnp.maximum(n, max_exact).astype(F32)
    large = max_exact + (jnp.log(nf / max_exact) / math.log(REL_MAX_DISTANCE / max_exact)
                         * (N_BUCKETS - max_exact)).astype(jnp.int32)
    return jnp.where(n < max_exact, n, jnp.minimum(large, N_BUCKETS - 1))


def _bias_tile(table, dist, mask):
    onehot = (_t5_bucket(dist)[..., None] == jnp.arange(N_BUCKETS, dtype=jnp.int32)).astype(F32)
    b = jnp.einsum("kqn,nh->kqh", onehot, table.astype(F32), precision=lax.Precision.HIGHEST)
    b = jnp.where(mask[..., None], b * LOG2E, NEG)
    k = dist.shape[0]
    return b.reshape(k, Q_BLOCK, N_KV_HEADS, GQA_GROUP).transpose(2, 0, 3, 1).reshape(N_KV_HEADS, k, ROWS)


def _bias_tiles(rel_bias):
    ql = jnp.arange(Q_BLOCK, dtype=jnp.int32)[None, :]
    kl = jnp.arange(KEY_CHUNK, dtype=jnp.int32)[:, None]
    chunk_dist = lambda delta: delta * KEY_CHUNK + ql - kl
    everything = jnp.ones((KEY_CHUNK, Q_BLOCK), bool)
    nothing = jnp.zeros((KEY_CHUNK, Q_BLOCK), bool)
    far = jnp.full((KEY_CHUNK, Q_BLOCK), REL_MAX_DISTANCE, jnp.int32)
    d0 = chunk_dist(0)
    sb = jnp.stack([_bias_tile(rel_bias, d0, d0 >= 0),
                    _bias_tile(rel_bias, chunk_dist(1), everything),
                    _bias_tile(rel_bias, far, everything),
                    _bias_tile(rel_bias, far, nothing)], axis=1)
    wtiles = []
    for delta in range(WIN_CHUNKS):
        d = chunk_dist(delta)
        wtiles.append(_bias_tile(rel_bias, d, (d >= 0) & (d < WINDOW)))
    wtiles.append(_bias_tile(rel_bias, far, nothing))
    wb = jnp.stack(wtiles, axis=1)
    rel = jnp.arange(CMP_WINDOW, dtype=jnp.int32)[:, None] - CMP_FRONT_PAD
    dc = ql - CMP_STRIDE * rel - (CMP_BLOCK - 1)
    cb = _bias_tile(rel_bias, dc, dc >= 0)
    crow = _bias_tile(rel_bias, far[:1], everything[:1])
    return sb, wb, cb, crow


def _mixer_ffn(h, b, s, lw, tiles, tm):
    sb, wb, cb, crow = tiles
    qt, cv, ksa, kw, vst, vwt, gates, gm = _inproj(
        h, b, (lw["gpre"], lw["wqt"], lw["wnat"], lw["wvt"], lw["wgt"], lw["wu"], lw["wv"],
               lw["lng"], lw["lnb"], lw["ws"], lw["bs"], _key_augmentation(IN_TILE)), IN_TILE)
    g, dh = N_KV_HEADS, HEAD_DIM
    n_chunks = s // CMP_STRIDE
    comp = _compress(cv, lw["cpos"], lw["cw1"], lw["cw2"])
    ncp = -(-(n_chunks + CMP_WINDOW) // CMP_CHUNK) * CMP_CHUNK
    back = ncp - CMP_FRONT_PAD - n_chunks
    comp = jnp.pad(comp, ((0, 0), (0, 0), (0, 0), (CMP_FRONT_PAD, back), (0, 0)))
    front = (jnp.arange(ncp) < CMP_FRONT_PAD).astype(F32)[:, None]
    aug = jnp.concatenate([jnp.ones((ncp, 2), F32), front, jnp.zeros((ncp, CMP_AUG_WIDTH - dh - 3), F32)], axis=1)
    kcp = jnp.concatenate([comp[:, 0], jnp.broadcast_to(aug, (b, g) + aug.shape)], axis=-1)
    vct = comp[:, 1].transpose(0, 1, 3, 2).astype(BF16)

    n_slc = s // SLC_BLOCK
    n_other = min(SLC_TOPK, n_slc) - (N_LOCAL_BLOCKS + 1)
    oc, sel = _cmp_select(qt, kcp, vct, crow, cb, n_slc, n_other)
    osl = _slc(qt, ksa.reshape(b, s, ksa.shape[1]), vst, sel, sb, crow)
    ot = _win_mix(qt, kw.reshape(b, s, kw.shape[1]), vwt, wb, oc, osl, gates)
    return _outproj_ffn(ot, gm, h, (lw["wo_a"], lw["wo_b"], lw["gpost"], lw["fpre"], lw["fwg"], lw["fwu"], lw["fwd"],
                                    lw["fpost"]), tm)


def _layer_weights(l, p):
    d = p["w_in"].shape[1]
    w_in = p["w_in"][l]
    o = NSA_WIDTH
    wqt = w_in[:, :o].T
    k_c, v_c, k_s, v_s, k_w, v_w = [w_in[:, o + i * KV_WIDTH:o + (i + 1) * KV_WIDTH] for i in range(6)]
    wnat = jnp.concatenate([k_c, v_c, k_s, k_w], axis=1)
    wvt = jnp.concatenate([v_s, v_w], axis=1).T
    o += 6 * KV_WIDTH
    wg = w_in[:, o:o + N_GATES].reshape(d, N_KV_HEADS, GQA_GROUP, N_BRANCHES).transpose(1, 3, 2, 0)
    wg = wg.reshape(N_KV_HEADS, N_BRANCHES * GQA_GROUP, d)
    wgt = jnp.pad(wg, ((0, 0), (0, GATE_ROWS - N_BRANCHES * GQA_GROUP), (0, 0))).reshape(N_KV_HEADS * GATE_ROWS, d)
    o += N_GATES
    gw = (w_in.shape[1] - o) // 2
    wu, wv = w_in[:, o:o + gw], w_in[:, o + gw:]
    causal = jnp.tril(jnp.ones((GMLP_CHUNK, GMLP_CHUNK), bool))
    ws = jnp.where(causal, p["gmlp_w_s"][l], 0.0)
    bs = jnp.repeat(p["gmlp_b_s"][l].T, gw // N_GMLP_GROUPS, axis=1)
    half = CMP_STRIDE * HEAD_DIM
    cpos = jnp.stack([p["cmp_pos_k"][l].reshape(2, half), p["cmp_pos_v"][l].reshape(2, half)])
    dff = p["w_down"].shape[1]
    row = lambda v: v[l].reshape(1, -1)
    bf = lambda w: w.astype(BF16)
    return dict(gpre=row(p["norm_mix_pre"]), wqt=bf(wqt), wnat=bf(wnat), wvt=bf(wvt), wgt=bf(wgt), wu=bf(wu), wv=bf(wv),
                lng=row(p["gmlp_ln_g"]), lnb=row(p["gmlp_ln_b"]), ws=bf(ws), bs=bs,
                cpos=cpos, cw1=bf(jnp.stack([p["cmp_w1_k"][l], p["cmp_w1_v"][l]])),
                cw2=bf(jnp.stack([p["cmp_w2_k"][l], p["cmp_w2_v"][l]])),
                wo_a=bf(p["w_out"][l][:NSA_WIDTH]), wo_b=bf(p["w_out"][l][NSA_WIDTH:]),
                gpost=row(p["norm_mix_post"]), fpre=row(p["norm_ffn_pre"]), fpost=row(p["norm_ffn_post"]),
                fwg=bf(p["w_gate_up"][l][:, :dff]), fwu=bf(p["w_gate_up"][l][:, dff:]), fwd=bf(p["w_down"][l]))


def _trunk(p, tm):
    x = p["x"]
    b, s, d = x.shape
    h = x.reshape(b * s, d)
    tiles = _bias_tiles(p["rel_bias"])
    for l in range(p["w_in"].shape[0]):
        lw = _layer_weights(l, p)
        h = _mixer_ffn(h, b, s, lw, tiles, tm)
    return h.reshape(b, s, d)


def kernel(x, rel_bias, norm_mix_pre, norm_mix_post, norm_ffn_pre, norm_ffn_post, w_in, cmp_pos_k, cmp_w1_k, cmp_w2_k, cmp_pos_v, cmp_w1_v, cmp_w2_v, gmlp_ln_g, gmlp_ln_b, gmlp_w_s, gmlp_b_s, w_out, w_gate_up, w_down):
    p = dict(x=x, rel_bias=rel_bias, norm_mix_pre=norm_mix_pre, norm_mix_post=norm_mix_post,
             norm_ffn_pre=norm_ffn_pre, norm_ffn_post=norm_ffn_post, w_in=w_in,
             cmp_pos_k=cmp_pos_k, cmp_w1_k=cmp_w1_k, cmp_w2_k=cmp_w2_k,
             cmp_pos_v=cmp_pos_v, cmp_w1_v=cmp_w1_v, cmp_w2_v=cmp_w2_v,
             gmlp_ln_g=gmlp_ln_g, gmlp_ln_b=gmlp_ln_b, gmlp_w_s=gmlp_w_s, gmlp_b_s=gmlp_b_s,
             w_out=w_out, w_gate_up=w_gate_up, w_down=w_down)
    return _trunk(p, tm=512)
```

```python
import functools
import math

import jax
import jax.numpy as jnp
import numpy as np
from jax import lax
from jax.experimental import pallas as pl
from jax.experimental.pallas import tpu as pltpu

F32 = jnp.float32
BF16 = jnp.bfloat16

N_NSA_HEADS = 8
N_KV_HEADS = 2
GQA_GROUP = N_NSA_HEADS // N_KV_HEADS
HEAD_DIM = 64
NSA_WIDTH = N_NSA_HEADS * HEAD_DIM
KV_WIDTH = N_KV_HEADS * HEAD_DIM
N_BRANCHES = 3
N_GATES = N_BRANCHES * N_NSA_HEADS
CMP_BLOCK = 32
CMP_STRIDE = 16
SLC_BLOCK = 64
SLC_TOPK = 16
N_LOCAL_BLOCKS = 2
WINDOW = 512
Q_BLOCK = 128
N_GMLP_GROUPS = 8
GMLP_CHUNK = 128
N_BUCKETS = 32
REL_MAX_DISTANCE = 128
RMS_EPS = 1e-6
LN_EPS = 1e-5

ROWS = GQA_GROUP * Q_BLOCK
KEY_CHUNK = 128
CMP_FRONT_PAD = 24
CMP_WINDOW = 32
NEG = -1e30
LANE = 128
GATE_PAD = 128
MXU_TILE = 256
BF16_ROWS = 16
VMEM_LIMIT = 48 * 1024 * 1024
LOG2E = math.log2(math.e)
Q_SCALE = HEAD_DIM ** -0.5 * LOG2E


def _dot(a, b):
    return jnp.dot(a, b, preferred_element_type=F32)


def _gelu(x):
    c = math.sqrt(2.0 / math.pi)
    return 0.5 * x * (1.0 + jnp.tanh(c * (x + 0.044715 * (x * x * x))))


def _sigmoid(x):
    return 1.0 / (1.0 + jnp.exp(-x))


def _rms(x, g):
    ms = jnp.mean(x * x, axis=-1, keepdims=True)
    return (x * lax.rsqrt(ms + RMS_EPS)) * g


def _resident(a):
    return pl.BlockSpec(a.shape, lambda i: (0,) * a.ndim, pipeline_mode=pl.Buffered(1))


def _params(n_axes):
    return pltpu.CompilerParams(dimension_semantics=("arbitrary",) * n_axes,
                                vmem_limit_bytes=VMEM_LIMIT)


def _dot_nt(a, b):
    return lax.dot_general(a, b, (((1,), (1,)), ((), ())), preferred_element_type=F32)


def _dot_tn(a, b):
    return lax.dot_general(a, b, (((0,), (0,)), ((), ())), preferred_element_type=F32)


def _inproj_body(x_ref, gpre_ref, wqt_ref, wnat_ref, wvt_ref, wgt_ref, wu_ref, wv_ref, lng_ref, lnb_ref,
                 ws_ref, bs_ref, aug_ref, qt_ref, cv_ref, ksa_ref, kw_ref, vst_ref, vwt_ref, gate_ref, gm_ref, cv_sc):
    tm = x_ref.shape[0]
    xb = _rms(x_ref[...], gpre_ref[...]).astype(BF16)
    qt_ref[...] = (_dot_nt(wqt_ref[...], xb) * Q_SCALE).astype(BF16)
    nat = _dot(xb, wnat_ref[...])
    chunks = tm // CMP_STRIDE
    left = lax.broadcasted_iota(jnp.int32, (chunks, LANE), 1) < HEAD_DIM
    for kv in range(2):
        cv_sc[kv] = nat[:, kv * KV_WIDTH:(kv + 1) * KV_WIDTH]
    for kv in range(2):
        for pair in range(CMP_STRIDE // 2):
            lanes = slice(pair * LANE, (pair + 1) * LANE)
            a = cv_sc[kv, pl.ds(2 * pair, chunks, stride=CMP_STRIDE), :]
            b = cv_sc[kv, pl.ds(2 * pair + 1, chunks, stride=CMP_STRIDE), :]
            cv_ref[kv, 0, :, lanes] = jnp.where(left, a, pltpu.roll(b, HEAD_DIM, 1))
            cv_ref[kv, 1, :, lanes] = jnp.where(left, pltpu.roll(a, HEAD_DIM, 1), b)
    k_slc = nat[:, 2 * KV_WIDTH:3 * KV_WIDTH].astype(BF16)
    ksa_ref[...] = jnp.concatenate(
        [piece for g in range(N_KV_HEADS) for piece in (k_slc[:, g * HEAD_DIM:(g + 1) * HEAD_DIM], aug_ref[...])], axis=1)
    kw_ref[...] = nat[:, 3 * KV_WIDTH:].astype(BF16)
    vt = _dot_nt(wvt_ref[...], xb).astype(BF16)
    denom = jnp.concatenate([jnp.ones((1, tm), BF16), jnp.zeros((BF16_ROWS - 1, tm), BF16)], axis=0)
    vst_ref[...] = jnp.concatenate(
        [piece for g in range(N_KV_HEADS) for piece in (vt[g * HEAD_DIM:(g + 1) * HEAD_DIM], denom)], axis=0)
    vwt_ref[...] = vt[KV_WIDTH:]
    gate_ref[...] = _sigmoid(_dot_nt(wgt_ref[...], xb))
    zu = _gelu(_dot(xb, wu_ref[...]))
    zv = _gelu(_dot(xb, wv_ref[...]))
    mu = jnp.mean(zv, axis=-1, keepdims=True)
    zc = zv - mu
    var = jnp.mean(zc * zc, axis=-1, keepdims=True)
    zv = ((zc * lax.rsqrt(var + LN_EPS)) * lng_ref[...] + lnb_ref[...]).astype(BF16)
    gdim = zv.shape[1] // N_GMLP_GROUPS
    left = lax.broadcasted_iota(jnp.int32, (GMLP_CHUNK, LANE), 1) < gdim
    for c in range(tm // GMLP_CHUNK):
        rows = slice(c * GMLP_CHUNK, (c + 1) * GMLP_CHUNK)
        for j in range(zv.shape[1] // LANE):
            cols = slice(j * LANE, (j + 1) * LANE)
            z = zv[rows, cols]
            sv = jnp.where(left, _dot(ws_ref[2 * j], z), _dot(ws_ref[2 * j + 1], z)) + bs_ref[:, cols]
            gm_ref[rows, cols] = (zu[rows, cols] * sv).astype(BF16)


IN_TILE = 1024
VT_ROWS = HEAD_DIM + BF16_ROWS
GATE_ROWS = 16


def _inproj(x, b, weights, tm):
    n, d = x.shape
    s = n // b
    per_b = s // tm
    row = lambda w: pl.BlockSpec((tm, w), lambda i: (i, 0))
    col = lambda h: pl.BlockSpec((None, h, tm), lambda i: (i // per_b, 0, i % per_b))
    gw = weights[5].shape[1]
    flat = CMP_STRIDE * HEAD_DIM
    assert KV_WIDTH == LANE and N_KV_HEADS == 2
    return pl.pallas_call(
        _inproj_body,
        grid=(n // tm,),
        in_specs=[row(d)] + [_resident(a) for a in weights],
        out_specs=[col(NSA_WIDTH),
                   pl.BlockSpec((None, 2, N_KV_HEADS, tm // CMP_STRIDE, flat),
                                lambda i: (i // per_b, 0, 0, i % per_b, 0)),
                   row(K_AUG_WIDTH), row(KV_WIDTH),
                   col(N_KV_HEADS * VT_ROWS), col(KV_WIDTH), col(N_KV_HEADS * GATE_ROWS), row(gw)],
        out_shape=[jax.ShapeDtypeStruct((b, NSA_WIDTH, s), BF16),
                   jax.ShapeDtypeStruct((b, 2, N_KV_HEADS, s // CMP_STRIDE, flat), F32),
                   jax.ShapeDtypeStruct((n, K_AUG_WIDTH), BF16),
                   jax.ShapeDtypeStruct((n, KV_WIDTH), BF16),
                   jax.ShapeDtypeStruct((b, N_KV_HEADS * VT_ROWS, s), BF16),
                   jax.ShapeDtypeStruct((b, KV_WIDTH, s), BF16),
                   jax.ShapeDtypeStruct((b, N_KV_HEADS * GATE_ROWS, s), F32),
                   jax.ShapeDtypeStruct((n, gw), BF16)],
        scratch_shapes=[pltpu.VMEM((2, tm, KV_WIDTH), F32)],
        compiler_params=_params(1),
        name="inproj_gmlp",
    )(x, *weights)


def _compress_body(x_ref, pos_ref, w1_ref, w2_ref, o_ref):
    x = x_ref[...]
    half = x.shape[1]
    a = _dot((x + pos_ref[0:1, :]).astype(BF16), w1_ref[:half, :])
    b = _dot((x + pos_ref[1:2, :]).astype(BF16), w1_ref[half:, :])
    pre = a + pltpu.roll(b, x.shape[0] - 1, 0)
    o_ref[...] = _dot(_gelu(pre).astype(BF16), w2_ref[...])


def _compress(xc, pos, w1, w2):
    b, two, g, nch, width = xc.shape
    hid = w1.shape[2]
    dh = w2.shape[2]
    return pl.pallas_call(
        _compress_body,
        grid=(b, two, g),
        in_specs=[pl.BlockSpec((None, None, None, nch, width), lambda i, t, j: (i, t, j, 0, 0)),
                  pl.BlockSpec((None, 2, width), lambda i, t, j: (t, 0, 0)),
                  pl.BlockSpec((None, 2 * width, hid), lambda i, t, j: (t, 0, 0)),
                  pl.BlockSpec((None, hid, dh), lambda i, t, j: (t, 0, 0))],
        out_specs=pl.BlockSpec((None, None, None, nch, dh), lambda i, t, j: (i, t, j, 0, 0)),
        out_shape=jax.ShapeDtypeStruct((b, two, g, nch, dh), F32),
        compiler_params=_params(3),
        name="compress",
    )(xc, pos, w1, w2)


CMP_CHUNK = 256
CMP_AUG_WIDTH = 2 * HEAD_DIM


def _heads_to_lanes(blk):
    n = blk.shape[1] // Q_BLOCK
    return jnp.concatenate([blk[r * HEAD_DIM:(r + 1) * HEAD_DIM, h * Q_BLOCK:(h + 1) * Q_BLOCK]
                            for h in range(n) for r in range(GQA_GROUP)], axis=1)


def _group_slots(qt, g):
    return jnp.concatenate([jnp.where(g == j, qt, jnp.zeros_like(qt)) for j in range(N_KV_HEADS)], axis=0)


def _q_spec(n_blocks):
    return pl.BlockSpec((None, GQA_GROUP * HEAD_DIM, n_blocks * Q_BLOCK), lambda i, j, q: (i, j, q))


CMP_QBLOCKS = 4
SUBLANES = 8
SORT_KEEP = 16


def _bitonic_merge(xs):
    if len(xs) == 1:
        return xs
    half = len(xs) // 2
    hi = [jnp.maximum(xs[i], xs[i + half]) for i in range(half)]
    lo = [jnp.minimum(xs[i], xs[i + half]) for i in range(half)]
    return _bitonic_merge(hi) + _bitonic_merge(lo)


def _bitonic_sort(xs):
    if len(xs) == 1:
        return xs
    half = len(xs) // 2
    return _bitonic_merge(_bitonic_sort(xs[:half]) + _bitonic_sort(xs[half:])[::-1])


def _top_of_two(a, b):
    n = len(a)
    return _bitonic_merge([jnp.maximum(a[i], b[n - 1 - i]) for i in range(n)])


def _kth_largest(x, k):
    assert k <= SORT_KEEP and x.shape[0] % SUBLANES == 0
    tiles = [x[i * SUBLANES:(i + 1) * SUBLANES] for i in range(x.shape[0] // SUBLANES)]
    tiles += [jnp.full_like(tiles[0], -2.0)] * (-len(tiles) % SORT_KEEP)
    top = _bitonic_sort(tiles[:SORT_KEEP])
    for j in range(SORT_KEEP, len(tiles), SORT_KEEP):
        top = _top_of_two(top, _bitonic_sort(tiles[j:j + SORT_KEEP]))
    shift = SUBLANES // 2
    while shift:
        top = _top_of_two(top, [pltpu.roll(v, shift, 0) for v in top])
        shift //= 2
    return top[k - 1][0:1]


def _cmp_select_body(q_ref, k_ref, vt_ref, crow_ref, cb_ref, oc_ref, sel_ref, sc_ref, *, n_other):
    qis = [pl.program_id(2) * CMP_QBLOCKS + h for h in range(CMP_QBLOCKS)]
    n_slc = sel_ref.shape[1]
    qt = _heads_to_lanes(q_ref[...])
    dh, lanes = qt.shape
    w0s = [pl.multiple_of(qi * (Q_BLOCK // CMP_STRIDE), 8) for qi in qis]
    n_chunks = (w0s[-1] + CMP_WINDOW + CMP_CHUNK - 1) // CMP_CHUNK
    row_iota = lax.broadcasted_iota(jnp.int32, (CMP_CHUNK, lanes), 0)
    lane_blk = lax.broadcasted_iota(jnp.int32, (1, lanes), 1) // ROWS
    w0_lane = w0s[0]
    for h in range(1, CMP_QBLOCKS):
        w0_lane = jnp.where(lane_blk >= h, w0s[h], w0_lane)

    @pl.when(qis[0] == 0)
    def _():
        sc_ref[...] = jnp.zeros_like(sc_ref)

    c = jnp.concatenate([crow_ref[...]] * CMP_QBLOCKS, axis=1)
    c_hi = c.astype(BF16).astype(F32)
    neg_row = jnp.full((1, lanes), NEG, F32)
    zeros = lambda n, dt: jnp.zeros((n, lanes), dt)
    tail = zeros(k_ref.shape[1] - dh - BF16_ROWS, BF16)
    qa = jnp.concatenate([qt, jnp.concatenate([c_hi, c - c_hi, neg_row, zeros(BF16_ROWS - 3, F32)]).astype(BF16), tail])
    qw = jnp.concatenate([qt, jnp.concatenate([zeros(2, F32), neg_row, zeros(BF16_ROWS - 3, F32)]).astype(BF16), tail])

    def rows_of(ch):
        return pl.ds(pl.multiple_of(ch * CMP_CHUNK, CMP_CHUNK), CMP_CHUNK)

    n_slabs = lanes // Q_BLOCK
    slab = lambda i: slice(i * Q_BLOCK, (i + 1) * Q_BLOCK)

    def put(rows, val, first=0):
        for i in range(val.shape[1] // Q_BLOCK):
            sc_ref[first + i, rows, :] = val[:, slab(i)]

    def get(rows):
        return jnp.concatenate([sc_ref[i, rows, :] for i in range(n_slabs)], axis=1)

    def logits(ch, m):
        s = _dot(k_ref[rows_of(ch), :].astype(BF16), qa)
        s = jnp.where(row_iota + ch * CMP_CHUNK < w0_lane, s, NEG)
        put(rows_of(ch), s)
        return jnp.maximum(m, jnp.max(s, axis=0, keepdims=True))

    m = lax.fori_loop(0, n_chunks, logits, jnp.full((1, lanes), NEG, F32))
    win_max = []
    for h, w0 in enumerate(w0s):
        win = pl.ds(w0, CMP_WINDOW)
        s_win = _dot(k_ref[win, :].astype(BF16), qw[:, h * ROWS:(h + 1) * ROWS]) + cb_ref[...]
        put(win, s_win, first=h * GQA_GROUP)
        win_max.append(jnp.max(s_win, axis=0, keepdims=True))
    m = jnp.maximum(m, jnp.concatenate(win_max, axis=1))

    def weigh(ch, l):
        p = jnp.exp2(get(rows_of(ch)) - m)
        put(rows_of(ch), p)
        oc_ref[...] += _dot(vt_ref[:, rows_of(ch)], p.astype(BF16))
        return l + jnp.sum(p, axis=0, keepdims=True)

    oc_ref[...] = jnp.zeros_like(oc_ref)
    l = lax.fori_loop(0, n_chunks, weigh, jnp.zeros((1, lanes), F32))
    scale = jnp.where(m > 0.5 * NEG, 1.0 / jnp.maximum(l, 1e-30), 0.0)
    oc_ref[...] = oc_ref[...] * scale

    per = SLC_BLOCK // CMP_STRIDE
    blk = lax.broadcasted_iota(jnp.int32, (n_slc, Q_BLOCK), 0)
    blk_f = blk.astype(F32)
    lane_pos = lax.broadcasted_iota(jnp.int32, (n_slc, Q_BLOCK), 1)
    def candidates(h):
        imp = jnp.zeros((n_slc, Q_BLOCK), F32)
        for r in range(GQA_GROUP):
            i = h * GQA_GROUP + r
            part = lambda off: sc_ref[i, pl.ds(CMP_FRONT_PAD + off, n_slc, stride=per), :]
            tot = part(0)
            for k in range(1, per - 1):
                tot = tot + part(k)
            imp = imp + (tot + 0.5 * (part(per - 1) + part(-1))) * scale[:, slab(i)]
        jq = (qis[h] * Q_BLOCK + lane_pos) // SLC_BLOCK
        valid = blk <= jq
        forced = (blk == 0) | (valid & (blk > jq - N_LOCAL_BLOCKS))
        free = valid & jnp.logical_not(forced)
        return jnp.where(free, imp, -1.0), forced, free

    tied = []
    for h in range(CMP_QBLOCKS):
        work, forced, free = candidates(h)
        t = _kth_largest(work, n_other)
        picked = (work > t) | (free & (work == t))
        sel_ref[h] = jnp.where(forced | picked, 1.0, 0.0)
        tied.append(jnp.max(jnp.sum(jnp.where(picked, 1.0, 0.0), axis=0, keepdims=True)) > n_other)

    for h in range(CMP_QBLOCKS):
        @pl.when(tied[h])
        def _(h=h):
            work, forced, free = candidates(h)
            for _ in range(n_other):
                mx = jnp.max(work, axis=0, keepdims=True)
                first = jnp.min(jnp.where(work == mx, blk_f, float(n_slc)), axis=0, keepdims=True)
                work = jnp.where((blk_f == first) & (mx >= 0.0), -1.0, work)
            sel_ref[h] = jnp.where(forced | (free & (work < 0.0)), 1.0, 0.0)


def _cmp_select(qt, kcp, vct, crow, cb, n_slc, n_other):
    b, g, dh = qt.shape[0], N_KV_HEADS, HEAD_DIM
    nqb = qt.shape[2] // Q_BLOCK
    total = nqb * ROWS
    ncp = kcp.shape[2]
    blk_q = pl.BlockSpec((None, None, dh, CMP_QBLOCKS * ROWS), lambda i, j, q: (i, j, 0, q))
    return pl.pallas_call(
        functools.partial(_cmp_select_body, n_other=n_other),
        grid=(b, g, nqb // CMP_QBLOCKS),
        in_specs=[_q_spec(CMP_QBLOCKS),
                  pl.BlockSpec((None, None, ncp, kcp.shape[3]), lambda i, j, q: (i, j, 0, 0)),
                  pl.BlockSpec((None, None, dh, ncp), lambda i, j, q: (i, j, 0, 0)),
                  pl.BlockSpec((None, 1, ROWS), lambda i, j, q: (j, 0, 0)),
                  pl.BlockSpec((None, CMP_WINDOW, ROWS), lambda i, j, q: (j, 0, 0))],
        out_specs=[blk_q,
                   pl.BlockSpec((None, None, CMP_QBLOCKS, n_slc, Q_BLOCK), lambda i, j, q: (i, j, q, 0, 0))],
        out_shape=[jax.ShapeDtypeStruct((b, g, dh, total), F32),
                   jax.ShapeDtypeStruct((b, g, nqb, n_slc, Q_BLOCK), F32)],
        scratch_shapes=[pltpu.VMEM((CMP_QBLOCKS * GQA_GROUP, ncp, Q_BLOCK), F32)],
        compiler_params=_params(3),
        name="cmp_select",
    )(qt, kcp, vct, crow, cb)


SLC_QBLOCKS = 2
FAR_KEYS = 256
FAR_BLOCKS = FAR_KEYS // SLC_BLOCK
FAR_CHUNKS = FAR_KEYS // KEY_CHUNK
FAR_BUFFERS = 4
FAR_AHEAD = 2
PEN_BLOCKS = 8
AUG_CONST = 2
AUG_ROWS = 16
K_SLAB = LANE
K_AUG_WIDTH = N_KV_HEADS * K_SLAB


def _key_augmentation(tm):
    assert tm % (PEN_BLOCKS * SLC_BLOCK) == 0
    pos = jnp.arange(tm, dtype=jnp.int32)
    onehot = ((pos[:, None] // SLC_BLOCK) % PEN_BLOCKS == jnp.arange(PEN_BLOCKS, dtype=jnp.int32)[None, :])
    return jnp.concatenate([jnp.ones((tm, AUG_CONST), BF16),
                            jnp.zeros((tm, AUG_ROWS - PEN_BLOCKS - AUG_CONST), BF16),
                            onehot.astype(BF16),
                            jnp.zeros((tm, K_SLAB - HEAD_DIM - AUG_ROWS), BF16)], axis=1)


def _slc_body(q_ref, k_ref, vt_ref, sel_ref, sb_ref, crow_ref, os_ref, sbuf, acc_ref):
    qis = [pl.program_id(2) * SLC_QBLOCKS + h for h in range(SLC_QBLOCKS)]
    qt = _heads_to_lanes(q_ref[...])
    dh, lanes = qt.shape
    per = KEY_CHUNK // SLC_BLOCK
    far_limits = [jnp.maximum(qi - 1, 0) * per for qi in qis]
    n_steps = (jnp.maximum(qis[-1] - 1, 0) + FAR_CHUNKS - 1) // FAR_CHUNKS
    last_step = k_ref.shape[0] // FAR_KEYS - 1
    steps_per_group = PEN_BLOCKS // FAR_BLOCKS

    c = jnp.concatenate([crow_ref[...]] * SLC_QBLOCKS, axis=1)
    c_hi = c.astype(BF16).astype(F32)
    const_rows = jnp.concatenate([c_hi, c - c_hi, jnp.zeros((AUG_ROWS - PEN_BLOCKS - AUG_CONST, lanes), F32)], axis=0)
    pad_rows = jnp.zeros((k_ref.shape[1] - dh - AUG_ROWS, lanes), BF16)
    blk_iota = lax.broadcasted_iota(jnp.int32, (PEN_BLOCKS, Q_BLOCK), 0)

    def far_logits(u):
        ua = jnp.minimum(u, last_step)
        grp0 = pl.multiple_of((ua // steps_per_group) * PEN_BLOCKS, PEN_BLOCKS)
        blk = blk_iota + (u // steps_per_group) * PEN_BLOCKS
        pens = []
        for h in range(SLC_QBLOCKS):
            pen = jnp.where((sel_ref[h, pl.ds(grp0, PEN_BLOCKS), :] > 0.5) & (blk < far_limits[h]), 0.0, NEG)
            pens += [pen] * GQA_GROUP
        qa = jnp.concatenate([qt, jnp.concatenate([const_rows, jnp.concatenate(pens, axis=1)], axis=0).astype(BF16),
                              pad_rows], axis=0)
        return _dot(k_ref[pl.ds(pl.multiple_of(ua * FAR_KEYS, FAR_KEYS), FAR_KEYS), :], qa)

    def stage(slot, u):
        s_new = far_logits(u).astype(BF16)
        sbuf[slot] = s_new
        groups = s_new.reshape(FAR_KEYS // BF16_ROWS, BF16_ROWS, lanes)
        return jnp.max(jnp.max(groups, axis=0).astype(F32), axis=0, keepdims=True)

    def far_update(m, s_ref, mx, key0):
        m_new = jnp.maximum(m, mx)
        p = jnp.exp2(s_ref[...] - m_new.astype(BF16))
        cols = pl.ds(pl.multiple_of(key0, FAR_KEYS), FAR_KEYS)
        acc_ref[...] = jnp.exp2(m - m_new) * acc_ref[...] + _dot(vt_ref[:, cols], p)
        return m_new

    def far_round(v, carry, staged_slots=FAR_BUFFERS):
        m = carry[0]
        ahead = list(carry[1:])
        for slot in range(FAR_BUFFERS):
            u = FAR_BUFFERS * v + slot
            if slot < staged_slots:
                ahead.append(stage((slot + FAR_AHEAD) % FAR_BUFFERS, u + FAR_AHEAD))
            ua = jnp.minimum(u, last_step)
            m = far_update(m, sbuf.at[slot], ahead.pop(0), ua * FAR_KEYS)
        return (m, *ahead)

    ahead = [stage(u, u) for u in range(FAR_AHEAD)]

    diag = []
    for h, qi in enumerate(qis):
        qd = jnp.concatenate([qt[:, h * ROWS:(h + 1) * ROWS], jnp.zeros((k_ref.shape[1] - dh, ROWS), BF16)], axis=0)
        prev = jnp.maximum(qi - 1, 0)
        tiles, vts = [], []
        for chunk, kind in ((prev, jnp.where(qi >= 1, 1, 3)), (qi, 0)):
            rows = pl.ds(pl.multiple_of(chunk * KEY_CHUNK, KEY_CHUNK), KEY_CHUNK)
            s = _dot(k_ref[rows, :], qd) + sb_ref[kind]
            for j in range(per):
                srow = sel_ref[h, pl.ds(chunk * per + j, 1), :]
                srow = jnp.concatenate([srow] * GQA_GROUP, axis=1)
                tiles.append(jnp.where(srow > 0.5, s[j * SLC_BLOCK:(j + 1) * SLC_BLOCK, :], NEG))
            vts.append(vt_ref[:, rows])
        diag.append((jnp.concatenate(tiles, axis=0), jnp.concatenate(vts, axis=1)))
    m_parts = []
    for h, (s, vt) in enumerate(diag):
        m_h = jnp.max(s, axis=0, keepdims=True)
        acc_ref[:, h * ROWS:(h + 1) * ROWS] = _dot(vt, jnp.exp2(s - m_h).astype(BF16))
        m_parts.append(m_h)

    init = (jnp.concatenate(m_parts, axis=1), *ahead)
    rounds = (n_steps + FAR_BUFFERS - 1) // FAR_BUFFERS
    carry = lax.fori_loop(0, jnp.maximum(rounds - 1, 0), far_round, init)
    lax.cond(rounds > 0,
             lambda c: far_round(rounds - 1, c, staged_slots=FAR_BUFFERS - FAR_AHEAD)[0],
             lambda c: c[0], carry)
    acc = acc_ref[...]
    os_ref[...] = acc[:dh] / jnp.maximum(acc[dh:dh + 1], 1e-30)


def _slc(qt, ks, vst, sel, sb, crow):
    b, g, dh = qt.shape[0], N_KV_HEADS, HEAD_DIM
    s_len = ks.shape[1]
    nqb = s_len // Q_BLOCK
    total = nqb * ROWS
    n_slc = sel.shape[3]
    blk_q = pl.BlockSpec((None, None, dh, SLC_QBLOCKS * ROWS), lambda i, j, q: (i, j, 0, q))
    return pl.pallas_call(
        _slc_body,
        grid=(b, g, nqb // SLC_QBLOCKS),
        in_specs=[_q_spec(SLC_QBLOCKS),
                  pl.BlockSpec((None, s_len, K_SLAB), lambda i, j, q: (i, 0, j)),
                  pl.BlockSpec((None, VT_ROWS, s_len), lambda i, j, q: (i, j, 0)),
                  pl.BlockSpec((None, None, SLC_QBLOCKS, n_slc, Q_BLOCK), lambda i, j, q: (i, j, q, 0, 0)),
                  pl.BlockSpec((None, 4, KEY_CHUNK, ROWS), lambda i, j, q: (j, 0, 0, 0)),
                  pl.BlockSpec((None, 1, ROWS), lambda i, j, q: (j, 0, 0))],
        out_specs=blk_q,
        out_shape=jax.ShapeDtypeStruct((b, g, dh, total), F32),
        scratch_shapes=[pltpu.VMEM((FAR_BUFFERS, FAR_KEYS, SLC_QBLOCKS * ROWS), BF16),
                        pltpu.VMEM((VT_ROWS, SLC_QBLOCKS * ROWS), F32)],
        compiler_params=_params(3),
        name="slc_attention",
    )(qt, ks, vst, sel, sb, crow)


WIN_CHUNKS = WINDOW // KEY_CHUNK + 1
WIN_QBLOCKS = 8


def _win_body(q_ref, k_ref, vt_ref, wb_ref, oc_ref, os_ref, gate_ref, o_ref):
    q_all = _group_slots(_heads_to_lanes(q_ref[...]), pl.program_id(1))
    logits = []
    for h in range(WIN_QBLOCKS):
        qi = pl.program_id(2) * WIN_QBLOCKS + h
        qt = q_all[:, h * ROWS:(h + 1) * ROWS]
        tiles, vts = [], []
        for delta in range(WIN_CHUNKS - 1, -1, -1):
            c = jnp.maximum(qi - delta, 0)
            rows = pl.ds(pl.multiple_of(c * KEY_CHUNK, KEY_CHUNK), KEY_CHUNK)
            kind = jnp.where(qi >= delta, delta, WIN_CHUNKS)
            tiles.append(_dot(k_ref[rows, :], qt) + wb_ref[kind])
            vts.append(vt_ref[:, rows])
        logits.append((jnp.concatenate(tiles, axis=0), vts))
    for h in range(WIN_QBLOCKS):
        lanes = slice(h * ROWS, (h + 1) * ROWS)
        tokens = slice(h * Q_BLOCK, (h + 1) * Q_BLOCK)
        s, vts = logits[h]
        m = jnp.max(s, axis=0, keepdims=True)
        p = jnp.exp2(s - m)
        l = jnp.sum(p, axis=0, keepdims=True)
        o_w = _dot(jnp.concatenate(vts, axis=1), p.astype(BF16)) / jnp.maximum(l, 1e-30)
        gate = lambda br: jnp.concatenate([gate_ref[br * GQA_GROUP + r:br * GQA_GROUP + r + 1, tokens]
                                           for r in range(GQA_GROUP)], axis=1)
        o = gate(0) * oc_ref[:, lanes] + gate(1) * os_ref[:, lanes] + gate(2) * o_w
        for r in range(GQA_GROUP):
            o_ref[r * HEAD_DIM:(r + 1) * HEAD_DIM, tokens] = o[:, r * Q_BLOCK:(r + 1) * Q_BLOCK].astype(o_ref.dtype)


def _win_mix(qt, kw, vwt, wb, oc, osl, gates):
    b, g, dh = qt.shape[0], N_KV_HEADS, HEAD_DIM
    s_len = kw.shape[1]
    blk_q = pl.BlockSpec((None, None, dh, WIN_QBLOCKS * ROWS), lambda i, j, q: (i, j, 0, q))
    return pl.pallas_call(
        _win_body,
        grid=(b, g, s_len // (WIN_QBLOCKS * Q_BLOCK)),
        in_specs=[_q_spec(WIN_QBLOCKS),
                  pl.BlockSpec((None, s_len, kw.shape[2]), lambda i, j, q: (i, 0, 0)),
                  pl.BlockSpec((None, dh, s_len), lambda i, j, q: (i, j, 0)),
                  pl.BlockSpec((None, WIN_CHUNKS + 1, KEY_CHUNK, ROWS), lambda i, j, q: (j, 0, 0, 0)),
                  blk_q, blk_q,
                  pl.BlockSpec((None, GATE_ROWS, WIN_QBLOCKS * Q_BLOCK), lambda i, j, q: (i, j, q))],
        out_specs=_q_spec(WIN_QBLOCKS),
        out_shape=jax.ShapeDtypeStruct(qt.shape, BF16),
        compiler_params=_params(3),
        name="window_mix",
    )(qt, kw, vwt, wb, oc, osl, gates)


FFN_TILE = 256


def _outproj_ffn_body(a_ref, gm_ref, h_ref, wa_ref, wb_ref, gmix_ref, gpre_ref, wg_ref, wu_ref, wd_ref, gpost_ref,
                      o_ref):
    y = _dot_tn(a_ref[...], wa_ref[...]) + _dot(gm_ref[...], wb_ref[...])
    h = h_ref[...] + _rms(y, gmix_ref[...])
    xb = _rms(h, gpre_ref[...]).astype(BF16)
    acc = jnp.zeros(h.shape, F32)
    for j in range(wg_ref.shape[1] // FFN_TILE):
        cols = slice(j * FFN_TILE, (j + 1) * FFN_TILE)
        gate = _dot(xb, wg_ref[:, cols])
        up = _dot(xb, wu_ref[:, cols])
        act = (gate * _sigmoid(gate) * up).astype(BF16)
        acc = acc + _dot(act, wd_ref[cols, :])
    o_ref[...] = h + _rms(acc, gpost_ref[...])


def _outproj_ffn(a, gm, h, weights, tm):
    n, d = h.shape
    per_b = a.shape[2] // tm
    row = lambda w: pl.BlockSpec((tm, w), lambda i: (i, 0))
    return pl.pallas_call(
        _outproj_ffn_body,
        grid=(n // tm,),
        in_specs=[pl.BlockSpec((None, a.shape[1], tm), lambda i: (i // per_b, 0, i % per_b)),
                  row(gm.shape[1]), row(d)] + [_resident(w) for w in weights],
        out_specs=row(d),
        out_shape=jax.ShapeDtypeStruct((n, d), F32),
        compiler_params=_params(1),
        name="outproj_ffn",
    )(a, gm, h, *weights)


def _t5_bucket(dist):
    n = jnp.maximum(dist, 0)
    max_exact = N_BUCKETS // 2
    nf = jnp.maximum(n, max_exact).astype(F32)
    large = max_exact + (jnp.log(nf / max_exact) / math.log(REL_MAX_DISTANCE / max_exact)
                         * (N_BUCKETS - max_exact)).astype(jnp.int32)
    return jnp.where(n < max_exact, n, jnp.minimum(large, N_BUCKETS - 1))


def _bias_tile(table, dist, mask):
    onehot = (_t5_bucket(dist)[..., None] == jnp.arange(N_BUCKETS, dtype=jnp.int32)).astype(F32)
    b = jnp.einsum("kqn,nh->kqh", onehot, table.astype(F32), precision=lax.Precision.HIGHEST)
    b = jnp.where(mask[..., None], b * LOG2E, NEG)
    k = dist.shape[0]
    return b.reshape(k, Q_BLOCK, N_KV_HEADS, GQA_GROUP).transpose(2, 0, 3, 1).reshape(N_KV_HEADS, k, ROWS)


def _bias_tiles(rel_bias):
    ql = jnp.arange(Q_BLOCK, dtype=jnp.int32)[None, :]
    kl = jnp.arange(KEY_CHUNK, dtype=jnp.int32)[:, None]
    chunk_dist = lambda delta: delta * KEY_CHUNK + ql - kl
    everything = jnp.ones((KEY_CHUNK, Q_BLOCK), bool)
    nothing = jnp.zeros((KEY_CHUNK, Q_BLOCK), bool)
    far = jnp.full((KEY_CHUNK, Q_BLOCK), REL_MAX_DISTANCE, jnp.int32)
    d0 = chunk_dist(0)
    sb = jnp.stack([_bias_tile(rel_bias, d0, d0 >= 0),
                    _bias_tile(rel_bias, chunk_dist(1), everything),
                    _bias_tile(rel_bias, far, everything),
                    _bias_tile(rel_bias, far, nothing)], axis=1)
    wtiles = []
    for delta in range(WIN_CHUNKS):
        d = chunk_dist(delta)
        wtiles.append(_bias_tile(rel_bias, d, (d >= 0) & (d < WINDOW)))
    wtiles.append(_bias_tile(rel_bias, far, nothing))
    wb = jnp.stack(wtiles, axis=1)
    rel = jnp.arange(CMP_WINDOW, dtype=jnp.int32)[:, None] - CMP_FRONT_PAD
    dc = ql - CMP_STRIDE * rel - (CMP_BLOCK - 1)
    cb = _bias_tile(rel_bias, dc, dc >= 0)
    crow = _bias_tile(rel_bias, far[:1], everything[:1])
    return sb, wb, cb, crow


def _mixer_ffn(h, b, s, lw, tiles, tm):
    sb, wb, cb, crow = tiles
    qt, cv, ksa, kw, vst, vwt, gates, gm = _inproj(
        h, b, (lw["gpre"], lw["wqt"], lw["wnat"], lw["wvt"], lw["wgt"], lw["wu"], lw["wv"],
               lw["lng"], lw["lnb"], lw["ws"], lw["bs"], _key_augmentation(IN_TILE)), IN_TILE)
    g, dh = N_KV_HEADS, HEAD_DIM
    n_chunks = s // CMP_STRIDE
    comp = _compress(cv, lw["cpos"], lw["cw1"], lw["cw2"])
    ncp = -(-(n_chunks + CMP_WINDOW) // CMP_CHUNK) * CMP_CHUNK
    back = ncp - CMP_FRONT_PAD - n_chunks
    comp = jnp.pad(comp, ((0, 0), (0, 0), (0, 0), (CMP_FRONT_PAD, back), (0, 0)))
    front = (jnp.arange(ncp) < CMP_FRONT_PAD).astype(F32)[:, None]
    aug = jnp.concatenate([jnp.ones((ncp, 2), F32), front, jnp.zeros((ncp, CMP_AUG_WIDTH - dh - 3), F32)], axis=1)
    kcp = jnp.concatenate([comp[:, 0], jnp.broadcast_to(aug, (b, g) + aug.shape)], axis=-1)
    vct = comp[:, 1].transpose(0, 1, 3, 2).astype(BF16)

    n_slc = s // SLC_BLOCK
    n_other = min(SLC_TOPK, n_slc) - (N_LOCAL_BLOCKS + 1)
    oc, sel = _cmp_select(qt, kcp, vct, crow, cb, n_slc, n_other)
    osl = _slc(qt, ksa.reshape(b, s, ksa.shape[1]), vst, sel, sb, crow)
    ot = _win_mix(qt, kw.reshape(b, s, kw.shape[1]), vwt, wb, oc, osl, gates)
    return _outproj_ffn(ot, gm, h, (lw["wo_a"], lw["wo_b"], lw["gpost"], lw["fpre"], lw["fwg"], lw["fwu"], lw["fwd"],
                                    lw["fpost"]), tm)


def _layer_weights(l, p):
    d = p["w_in"].shape[1]
    w_in = p["w_in"][l]
    o = NSA_WIDTH
    wqt = w_in[:, :o].T
    k_c, v_c, k_s, v_s, k_w, v_w = [w_in[:, o + i * KV_WIDTH:o + (i + 1) * KV_WIDTH] for i in range(6)]
    wnat = jnp.concatenate([k_c, v_c, k_s, k_w], axis=1)
    wvt = jnp.concatenate([v_s, v_w], axis=1).T
    o += 6 * KV_WIDTH
    wg = w_in[:, o:o + N_GATES].reshape(d, N_KV_HEADS, GQA_GROUP, N_BRANCHES).transpose(1, 3, 2, 0)
    wg = wg.reshape(N_KV_HEADS, N_BRANCHES * GQA_GROUP, d)
    wgt = jnp.pad(wg, ((0, 0), (0, GATE_ROWS - N_BRANCHES * GQA_GROUP), (0, 0))).reshape(N_KV_HEADS * GATE_ROWS, d)
    o += N_GATES
    gw = (w_in.shape[1] - o) // 2
    wu, wv = w_in[:, o:o + gw], w_in[:, o + gw:]
    causal = jnp.tril(jnp.ones((GMLP_CHUNK, GMLP_CHUNK), bool))
    ws = jnp.where(causal, p["gmlp_w_s"][l], 0.0)
    bs = jnp.repeat(p["gmlp_b_s"][l].T, gw // N_GMLP_GROUPS, axis=1)
    half = CMP_STRIDE * HEAD_DIM
    cpos = jnp.stack([p["cmp_pos_k"][l].reshape(2, half), p["cmp_pos_v"][l].reshape(2, half)])
    dff = p["w_down"].shape[1]
    row = lambda v: v[l].reshape(1, -1)
    bf = lambda w: w.astype(BF16)
    return dict(gpre=row(p["norm_mix_pre"]), wqt=bf(wqt), wnat=bf(wnat), wvt=bf(wvt), wgt=bf(wgt), wu=bf(wu), wv=bf(wv),
                lng=row(p["gmlp_ln_g"]), lnb=row(p["gmlp_ln_b"]), ws=bf(ws), bs=bs,
                cpos=cpos, cw1=bf(jnp.stack([p["cmp_w1_k"][l], p["cmp_w1_v"][l]])),
                cw2=bf(jnp.stack([p["cmp_w2_k"][l], p["cmp_w2_v"][l]])),
                wo_a=bf(p["w_out"][l][:NSA_WIDTH]), wo_b=bf(p["w_out"][l][NSA_WIDTH:]),
                gpost=row(p["norm_mix_post"]), fpre=row(p["norm_ffn_pre"]), fpost=row(p["norm_ffn_post"]),
                fwg=bf(p["w_gate_up"][l][:, :dff]), fwu=bf(p["w_gate_up"][l][:, dff:]), fwd=bf(p["w_down"][l]))


def _trunk(p, tm):
    x = p["x"]
    b, s, d = x.shape
    h = x.reshape(b * s, d)
    tiles = _bias_tiles(p["rel_bias"])
    for l in range(p["w_in"].shape[0]):
        lw = _layer_weights(l, p)
        h = _mixer_ffn(h, b, s, lw, tiles, tm)
    return h.reshape(b, s, d)


def kernel(x, rel_bias, norm_mix_pre, norm_mix_post, norm_ffn_pre, norm_ffn_post, w_in, cmp_pos_k, cmp_w1_k, cmp_w2_k, cmp_pos_v, cmp_w1_v, cmp_w2_v, gmlp_ln_g, gmlp_ln_b, gmlp_w_s, gmlp_b_s, w_out, w_gate_up, w_down):
    p = dict(x=x, rel_bias=rel_bias, norm_mix_pre=norm_mix_pre, norm_mix_post=norm_mix_post,
             norm_ffn_pre=norm_ffn_pre, norm_ffn_post=norm_ffn_post, w_in=w_in,
             cmp_pos_k=cmp_pos_k, cmp_w1_k=cmp_w1_k, cmp_w2_k=cmp_w2_k,
             cmp_pos_v=cmp_pos_v, cmp_w1_v=cmp_w1_v, cmp_w2_v=cmp_w2_v,
             gmlp_ln_g=gmlp_ln_g, gmlp_ln_b=gmlp_ln_b, gmlp_w_s=gmlp_w_s, gmlp_b_s=gmlp_b_s,
             w_out=w_out, w_gate_up=w_gate_up, w_down=w_down)
    return _trunk(p, tm=512)
```

```python
import functools
import math

import jax
import jax.numpy as jnp
from jax import lax
from jax.experimental import pallas as pl
from jax.experimental.pallas import tpu as pltpu

F32 = jnp.float32
BF16 = jnp.bfloat16

N_NSA_HEADS = 8
N_KV_HEADS = 2
GQA_GROUP = N_NSA_HEADS // N_KV_HEADS
HEAD_DIM = 64
NSA_WIDTH = N_NSA_HEADS * HEAD_DIM
KV_WIDTH = N_KV_HEADS * HEAD_DIM
N_BRANCHES = 3
N_GATES = N_BRANCHES * N_NSA_HEADS
CMP_BLOCK = 32
CMP_STRIDE = 16
SLC_BLOCK = 64
SLC_TOPK = 16
N_LOCAL_BLOCKS = 2
WINDOW = 512
Q_BLOCK = 128
N_GMLP_GROUPS = 8
GMLP_CHUNK = 128
N_BUCKETS = 32
REL_MAX_DISTANCE = 128
RMS_EPS = 1e-6
LN_EPS = 1e-5

ROWS = GQA_GROUP * Q_BLOCK
KEY_CHUNK = 128
CMP_FRONT_PAD = 24
CMP_WINDOW = 32
NEG = -1e30
LANE = 128
BF16_ROWS = 16
VMEM_LIMIT = 48 * 1024 * 1024
LOG2E = math.log2(math.e)
Q_SCALE = HEAD_DIM ** -0.5 * LOG2E


def _dot(a, b):
    return jnp.dot(a, b, preferred_element_type=F32)


def _gelu(x):
    c = math.sqrt(2.0 / math.pi)
    return 0.5 * x * (1.0 + jnp.tanh(c * (x + 0.044715 * (x * x * x))))


def _sigmoid(x):
    return 1.0 / (1.0 + jnp.exp(-x))


def _rms(x, g):
    ms = jnp.mean(x * x, axis=-1, keepdims=True)
    return (x * lax.rsqrt(ms + RMS_EPS)) * g


def _resident(a):
    return pl.BlockSpec(a.shape, lambda i: (0,) * a.ndim, pipeline_mode=pl.Buffered(1))


def _params(n_axes):
    return pltpu.CompilerParams(dimension_semantics=("arbitrary",) * n_axes,
                                vmem_limit_bytes=VMEM_LIMIT)


def _dot_nt(a, b):
    return lax.dot_general(a, b, (((1,), (1,)), ((), ())), preferred_element_type=F32)


def _dot_tn(a, b):
    return lax.dot_general(a, b, (((0,), (0,)), ((), ())), preferred_element_type=F32)


def _inproj_body(x_ref, gpre_ref, wqt_ref, wnat_ref, wvt_ref, wgt_ref, wu_ref, wv_ref, lng_ref, lnb_ref,
                 ws_ref, bs_ref, aug_ref, qt_ref, cv_ref, ksa_ref, kw_ref, vst_ref, vwt_ref, gate_ref, gm_ref, cv_sc):
    tm = x_ref.shape[0]
    xb = _rms(x_ref[...], gpre_ref[...]).astype(BF16)
    qt_ref[...] = (_dot_nt(wqt_ref[...], xb) * Q_SCALE).astype(BF16)
    nat = _dot(xb, wnat_ref[...])
    chunks = tm // CMP_STRIDE
    left = lax.broadcasted_iota(jnp.int32, (chunks, LANE), 1) < HEAD_DIM
    for kv in range(2):
        cv_sc[kv] = nat[:, kv * KV_WIDTH:(kv + 1) * KV_WIDTH]
    for kv in range(2):
        for pair in range(CMP_STRIDE // 2):
            lanes = slice(pair * LANE, (pair + 1) * LANE)
            a = cv_sc[kv, pl.ds(2 * pair, chunks, stride=CMP_STRIDE), :]
            b = cv_sc[kv, pl.ds(2 * pair + 1, chunks, stride=CMP_STRIDE), :]
            cv_ref[kv, 0, :, lanes] = jnp.where(left, a, pltpu.roll(b, HEAD_DIM, 1))
            cv_ref[kv, 1, :, lanes] = jnp.where(left, pltpu.roll(a, HEAD_DIM, 1), b)
    k_slc = nat[:, 2 * KV_WIDTH:3 * KV_WIDTH].astype(BF16)
    ksa_ref[...] = jnp.concatenate(
        [piece for g in range(N_KV_HEADS) for piece in (k_slc[:, g * HEAD_DIM:(g + 1) * HEAD_DIM], aug_ref[...])], axis=1)
    kw_ref[...] = nat[:, 3 * KV_WIDTH:].astype(BF16)
    vt = _dot_nt(wvt_ref[...], xb).astype(BF16)
    denom = jnp.concatenate([jnp.ones((1, tm), BF16), jnp.zeros((BF16_ROWS - 1, tm), BF16)], axis=0)
    vst_ref[...] = jnp.concatenate(
        [piece for g in range(N_KV_HEADS) for piece in (vt[g * HEAD_DIM:(g + 1) * HEAD_DIM], denom)], axis=0)
    vwt_ref[...] = vt[KV_WIDTH:]
    gate_ref[...] = _sigmoid(_dot_nt(wgt_ref[...], xb))
    zu = _gelu(_dot(xb, wu_ref[...]))
    zv = _gelu(_dot(xb, wv_ref[...]))
    mu = jnp.mean(zv, axis=-1, keepdims=True)
    zc = zv - mu
    var = jnp.mean(zc * zc, axis=-1, keepdims=True)
    zv = ((zc * lax.rsqrt(var + LN_EPS)) * lng_ref[...] + lnb_ref[...]).astype(BF16)
    gdim = zv.shape[1] // N_GMLP_GROUPS
    left = lax.broadcasted_iota(jnp.int32, (GMLP_CHUNK, LANE), 1) < gdim
    for c in range(tm // GMLP_CHUNK):
        rows = slice(c * GMLP_CHUNK, (c + 1) * GMLP_CHUNK)
        for j in range(zv.shape[1] // LANE):
            cols = slice(j * LANE, (j + 1) * LANE)
            z = zv[rows, cols]
            sv = jnp.where(left, _dot(ws_ref[2 * j], z), _dot(ws_ref[2 * j + 1], z)) + bs_ref[:, cols]
            gm_ref[rows, cols] = (zu[rows, cols] * sv).astype(BF16)


IN_TILE = 1024
VT_ROWS = HEAD_DIM + BF16_ROWS
GATE_ROWS = 16


def _inproj(x, b, weights, tm):
    n, d = x.shape
    s = n // b
    per_b = s // tm
    row = lambda w: pl.BlockSpec((tm, w), lambda i: (i, 0))
    col = lambda h: pl.BlockSpec((None, h, tm), lambda i: (i // per_b, 0, i % per_b))
    gw = weights[5].shape[1]
    flat = CMP_STRIDE * HEAD_DIM
    assert KV_WIDTH == LANE and N_KV_HEADS == 2
    return pl.pallas_call(
        _inproj_body,
        grid=(n // tm,),
        in_specs=[row(d)] + [_resident(a) for a in weights],
        out_specs=[col(NSA_WIDTH),
                   pl.BlockSpec((None, 2, N_KV_HEADS, tm // CMP_STRIDE, flat),
                                lambda i: (i // per_b, 0, 0, i % per_b, 0)),
                   row(K_AUG_WIDTH), row(KV_WIDTH),
                   col(N_KV_HEADS * VT_ROWS), col(KV_WIDTH), col(N_KV_HEADS * GATE_ROWS), row(gw)],
        out_shape=[jax.ShapeDtypeStruct((b, NSA_WIDTH, s), BF16),
                   jax.ShapeDtypeStruct((b, 2, N_KV_HEADS, s // CMP_STRIDE, flat), F32),
                   jax.ShapeDtypeStruct((n, K_AUG_WIDTH), BF16),
                   jax.ShapeDtypeStruct((n, KV_WIDTH), BF16),
                   jax.ShapeDtypeStruct((b, N_KV_HEADS * VT_ROWS, s), BF16),
                   jax.ShapeDtypeStruct((b, KV_WIDTH, s), BF16),
                   jax.ShapeDtypeStruct((b, N_KV_HEADS * GATE_ROWS, s), F32),
                   jax.ShapeDtypeStruct((n, gw), BF16)],
        scratch_shapes=[pltpu.VMEM((2, tm, KV_WIDTH), F32)],
        compiler_params=_params(1),
        name="inproj_gmlp",
    )(x, *weights)


def _compress_body(x_ref, pos_ref, w1_ref, w2_ref, o_ref):
    x = x_ref[...]
    half = x.shape[1]
    a = _dot((x + pos_ref[0:1, :]).astype(BF16), w1_ref[:half, :])
    b = _dot((x + pos_ref[1:2, :]).astype(BF16), w1_ref[half:, :])
    pre = a + pltpu.roll(b, x.shape[0] - 1, 0)
    o_ref[...] = _dot(_gelu(pre).astype(BF16), w2_ref[...])


def _compress(xc, pos, w1, w2):
    b, two, g, nch, width = xc.shape
    hid = w1.shape[2]
    dh = w2.shape[2]
    return pl.pallas_call(
        _compress_body,
        grid=(b, two, g),
        in_specs=[pl.BlockSpec((None, None, None, nch, width), lambda i, t, j: (i, t, j, 0, 0)),
                  pl.BlockSpec((None, 2, width), lambda i, t, j: (t, 0, 0)),
                  pl.BlockSpec((None, 2 * width, hid), lambda i, t, j: (t, 0, 0)),
                  pl.BlockSpec((None, hid, dh), lambda i, t, j: (t, 0, 0))],
        out_specs=pl.BlockSpec((None, None, None, nch, dh), lambda i, t, j: (i, t, j, 0, 0)),
        out_shape=jax.ShapeDtypeStruct((b, two, g, nch, dh), F32),
        compiler_params=_params(3),
        name="compress",
    )(xc, pos, w1, w2)


CMP_CHUNK = 256
CMP_AUG_WIDTH = 2 * HEAD_DIM


def _heads_to_lanes(blk):
    n = blk.shape[1] // Q_BLOCK
    return jnp.concatenate([blk[r * HEAD_DIM:(r + 1) * HEAD_DIM, h * Q_BLOCK:(h + 1) * Q_BLOCK]
                            for h in range(n) for r in range(GQA_GROUP)], axis=1)


def _group_slots(qt, g):
    return jnp.concatenate([jnp.where(g == j, qt, jnp.zeros_like(qt)) for j in range(N_KV_HEADS)], axis=0)


def _q_spec(n_blocks):
    return pl.BlockSpec((None, GQA_GROUP * HEAD_DIM, n_blocks * Q_BLOCK), lambda i, j, q: (i, j, q))


CMP_QBLOCKS = 4
SUBLANES = 8
SORT_KEEP = 16


def _bitonic_merge(xs):
    if len(xs) == 1:
        return xs
    half = len(xs) // 2
    hi = [jnp.maximum(xs[i], xs[i + half]) for i in range(half)]
    lo = [jnp.minimum(xs[i], xs[i + half]) for i in range(half)]
    return _bitonic_merge(hi) + _bitonic_merge(lo)


def _bitonic_sort(xs):
    if len(xs) == 1:
        return xs
    half = len(xs) // 2
    return _bitonic_merge(_bitonic_sort(xs[:half]) + _bitonic_sort(xs[half:])[::-1])


def _top_of_two(a, b):
    n = len(a)
    return _bitonic_merge([jnp.maximum(a[i], b[n - 1 - i]) for i in range(n)])


def _kth_largest(x, k):
    assert k <= SORT_KEEP and x.shape[0] % SUBLANES == 0
    tiles = [x[i * SUBLANES:(i + 1) * SUBLANES] for i in range(x.shape[0] // SUBLANES)]
    tiles += [jnp.full_like(tiles[0], -2.0)] * (-len(tiles) % SORT_KEEP)
    top = _bitonic_sort(tiles[:SORT_KEEP])
    for j in range(SORT_KEEP, len(tiles), SORT_KEEP):
        top = _top_of_two(top, _bitonic_sort(tiles[j:j + SORT_KEEP]))
    shift = SUBLANES // 2
    while shift:
        top = _top_of_two(top, [pltpu.roll(v, shift, 0) for v in top])
        shift //= 2
    return top[k - 1][0:1]


def _cmp_select_body(q_ref, k_ref, vt_ref, crow_ref, cb_ref, oc_ref, sel_ref, sc_ref, *, n_other):
    qis = [pl.program_id(2) * CMP_QBLOCKS + h for h in range(CMP_QBLOCKS)]
    n_slc = sel_ref.shape[1]
    qt = _heads_to_lanes(q_ref[...])
    dh, lanes = qt.shape
    w0s = [pl.multiple_of(qi * (Q_BLOCK // CMP_STRIDE), 8) for qi in qis]
    n_chunks = (w0s[-1] + CMP_WINDOW + CMP_CHUNK - 1) // CMP_CHUNK
    row_iota = lax.broadcasted_iota(jnp.int32, (CMP_CHUNK, lanes), 0)
    lane_blk = lax.broadcasted_iota(jnp.int32, (1, lanes), 1) // ROWS
    w0_lane = w0s[0]
    for h in range(1, CMP_QBLOCKS):
        w0_lane = jnp.where(lane_blk >= h, w0s[h], w0_lane)

    @pl.when(qis[0] == 0)
    def _():
        sc_ref[...] = jnp.zeros_like(sc_ref)

    c = jnp.concatenate([crow_ref[...]] * CMP_QBLOCKS, axis=1)
    c_hi = c.astype(BF16).astype(F32)
    neg_row = jnp.full((1, lanes), NEG, F32)
    zeros = lambda n, dt: jnp.zeros((n, lanes), dt)
    tail = zeros(k_ref.shape[1] - dh - BF16_ROWS, BF16)
    qa = jnp.concatenate([qt, jnp.concatenate([c_hi, c - c_hi, neg_row, zeros(BF16_ROWS - 3, F32)]).astype(BF16), tail])
    qw = jnp.concatenate([qt, jnp.concatenate([zeros(2, F32), neg_row, zeros(BF16_ROWS - 3, F32)]).astype(BF16), tail])

    def rows_of(ch):
        return pl.ds(pl.multiple_of(ch * CMP_CHUNK, CMP_CHUNK), CMP_CHUNK)

    n_slabs = lanes // Q_BLOCK
    slab = lambda i: slice(i * Q_BLOCK, (i + 1) * Q_BLOCK)

    def put(rows, val, first=0):
        for i in range(val.shape[1] // Q_BLOCK):
            sc_ref[first + i, rows, :] = val[:, slab(i)]

    def get(rows):
        return jnp.concatenate([sc_ref[i, rows, :] for i in range(n_slabs)], axis=1)

    def logits(ch, m):
        s = _dot(k_ref[rows_of(ch), :].astype(BF16), qa)
        s = jnp.where(row_iota + ch * CMP_CHUNK < w0_lane, s, NEG)
        put(rows_of(ch), s)
        return jnp.maximum(m, jnp.max(s, axis=0, keepdims=True))

    m = lax.fori_loop(0, n_chunks, logits, jnp.full((1, lanes), NEG, F32))
    win_max = []
    for h, w0 in enumerate(w0s):
        win = pl.ds(w0, CMP_WINDOW)
        s_win = _dot(k_ref[win, :].astype(BF16), qw[:, h * ROWS:(h + 1) * ROWS]) + cb_ref[...]
        put(win, s_win, first=h * GQA_GROUP)
        win_max.append(jnp.max(s_win, axis=0, keepdims=True))
    m = jnp.maximum(m, jnp.concatenate(win_max, axis=1))

    def weigh(ch, l):
        p = jnp.exp2(get(rows_of(ch)) - m)
        put(rows_of(ch), p)
        oc_ref[...] += _dot(vt_ref[:, rows_of(ch)], p.astype(BF16))
        return l + jnp.sum(p, axis=0, keepdims=True)

    oc_ref[...] = jnp.zeros_like(oc_ref)
    l = lax.fori_loop(0, n_chunks, weigh, jnp.zeros((1, lanes), F32))
    scale = jnp.where(m > 0.5 * NEG, 1.0 / jnp.maximum(l, 1e-30), 0.0)
    oc_ref[...] = oc_ref[...] * scale

    per = SLC_BLOCK // CMP_STRIDE
    blk = lax.broadcasted_iota(jnp.int32, (n_slc, Q_BLOCK), 0)
    blk_f = blk.astype(F32)
    lane_pos = lax.broadcasted_iota(jnp.int32, (n_slc, Q_BLOCK), 1)
    def candidates(h):
        imp = jnp.zeros((n_slc, Q_BLOCK), F32)
        for r in range(GQA_GROUP):
            i = h * GQA_GROUP + r
            part = lambda off: sc_ref[i, pl.ds(CMP_FRONT_PAD + off, n_slc, stride=per), :]
            tot = part(0)
            for k in range(1, per - 1):
                tot = tot + part(k)
            imp = imp + (tot + 0.5 * (part(per - 1) + part(-1))) * scale[:, slab(i)]
        jq = (qis[h] * Q_BLOCK + lane_pos) // SLC_BLOCK
        valid = blk <= jq
        forced = (blk == 0) | (valid & (blk > jq - N_LOCAL_BLOCKS))
        free = valid & jnp.logical_not(forced)
        return jnp.where(free, imp, -1.0), forced, free

    tied = []
    for h in range(CMP_QBLOCKS):
        work, forced, free = candidates(h)
        t = _kth_largest(work, n_other)
        picked = (work > t) | (free & (work == t))
        sel_ref[h] = jnp.where(forced | picked, 1.0, 0.0)
        tied.append(jnp.max(jnp.sum(jnp.where(picked, 1.0, 0.0), axis=0, keepdims=True)) > n_other)

    for h in range(CMP_QBLOCKS):
        @pl.when(tied[h])
        def _(h=h):
            work, forced, free = candidates(h)
            for _ in range(n_other):
                mx = jnp.max(work, axis=0, keepdims=True)
                first = jnp.min(jnp.where(work == mx, blk_f, float(n_slc)), axis=0, keepdims=True)
                work = jnp.where((blk_f == first) & (mx >= 0.0), -1.0, work)
            sel_ref[h] = jnp.where(forced | (free & (work < 0.0)), 1.0, 0.0)


def _cmp_select(qt, kcp, vct, crow, cb, n_slc, n_other):
    b, g, dh = qt.shape[0], N_KV_HEADS, HEAD_DIM
    nqb = qt.shape[2] // Q_BLOCK
    total = nqb * ROWS
    ncp = kcp.shape[2]
    blk_q = pl.BlockSpec((None, None, dh, CMP_QBLOCKS * ROWS), lambda i, j, q: (i, j, 0, q))
    return pl.pallas_call(
        functools.partial(_cmp_select_body, n_other=n_other),
        grid=(b, g, nqb // CMP_QBLOCKS),
        in_specs=[_q_spec(CMP_QBLOCKS),
                  pl.BlockSpec((None, None, ncp, kcp.shape[3]), lambda i, j, q: (i, j, 0, 0)),
                  pl.BlockSpec((None, None, dh, ncp), lambda i, j, q: (i, j, 0, 0)),
                  pl.BlockSpec((None, 1, ROWS), lambda i, j, q: (j, 0, 0)),
                  pl.BlockSpec((None, CMP_WINDOW, ROWS), lambda i, j, q: (j, 0, 0))],
        out_specs=[blk_q,
                   pl.BlockSpec((None, None, CMP_QBLOCKS, n_slc, Q_BLOCK), lambda i, j, q: (i, j, q, 0, 0))],
        out_shape=[jax.ShapeDtypeStruct((b, g, dh, total), F32),
                   jax.ShapeDtypeStruct((b, g, nqb, n_slc, Q_BLOCK), F32)],
        scratch_shapes=[pltpu.VMEM((CMP_QBLOCKS * GQA_GROUP, ncp, Q_BLOCK), F32)],
        compiler_params=_params(3),
        name="cmp_select",
    )(qt, kcp, vct, crow, cb)


SLC_QBLOCKS = 2
FAR_KEYS = 256
FAR_BLOCKS = FAR_KEYS // SLC_BLOCK
FAR_CHUNKS = FAR_KEYS // KEY_CHUNK
FAR_BUFFERS = 4
FAR_AHEAD = 2
PEN_BLOCKS = 8
AUG_CONST = 2
AUG_ROWS = 16
K_SLAB = LANE
K_AUG_WIDTH = N_KV_HEADS * K_SLAB


def _key_augmentation(tm):
    assert tm % (PEN_BLOCKS * SLC_BLOCK) == 0
    pos = jnp.arange(tm, dtype=jnp.int32)
    onehot = ((pos[:, None] // SLC_BLOCK) % PEN_BLOCKS == jnp.arange(PEN_BLOCKS, dtype=jnp.int32)[None, :])
    return jnp.concatenate([jnp.ones((tm, AUG_CONST), BF16),
                            jnp.zeros((tm, AUG_ROWS - PEN_BLOCKS - AUG_CONST), BF16),
                            onehot.astype(BF16),
                            jnp.zeros((tm, K_SLAB - HEAD_DIM - AUG_ROWS), BF16)], axis=1)


def _slc_body(q_ref, k_ref, vt_ref, sel_ref, sb_ref, crow_ref, os_ref, sbuf, acc_ref):
    qis = [pl.program_id(2) * SLC_QBLOCKS + h for h in range(SLC_QBLOCKS)]
    qt = _heads_to_lanes(q_ref[...])
    dh, lanes = qt.shape
    per = KEY_CHUNK // SLC_BLOCK
    far_limits = [jnp.maximum(qi - 1, 0) * per for qi in qis]
    n_steps = (jnp.maximum(qis[-1] - 1, 0) + FAR_CHUNKS - 1) // FAR_CHUNKS
    last_step = k_ref.shape[0] // FAR_KEYS - 1
    steps_per_group = PEN_BLOCKS // FAR_BLOCKS

    c = jnp.concatenate([crow_ref[...]] * SLC_QBLOCKS, axis=1)
    c_hi = c.astype(BF16).astype(F32)
    const_rows = jnp.concatenate([c_hi, c - c_hi, jnp.zeros((AUG_ROWS - PEN_BLOCKS - AUG_CONST, lanes), F32)], axis=0)
    pad_rows = jnp.zeros((k_ref.shape[1] - dh - AUG_ROWS, lanes), BF16)
    blk_iota = lax.broadcasted_iota(jnp.int32, (PEN_BLOCKS, Q_BLOCK), 0)

    def far_logits(u):
        ua = jnp.minimum(u, last_step)
        grp0 = pl.multiple_of((ua // steps_per_group) * PEN_BLOCKS, PEN_BLOCKS)
        blk = blk_iota + (u // steps_per_group) * PEN_BLOCKS
        pens = []
        for h in range(SLC_QBLOCKS):
            pen = jnp.where((sel_ref[h, pl.ds(grp0, PEN_BLOCKS), :] > 0.5) & (blk < far_limits[h]), 0.0, NEG)
            pens += [pen] * GQA_GROUP
        qa = jnp.concatenate([qt, jnp.concatenate([const_rows, jnp.concatenate(pens, axis=1)], axis=0).astype(BF16),
                              pad_rows], axis=0)
        return _dot(k_ref[pl.ds(pl.multiple_of(ua * FAR_KEYS, FAR_KEYS), FAR_KEYS), :], qa)

    def stage(slot, u):
        s_new = far_logits(u).astype(BF16)
        sbuf[slot] = s_new
        groups = s_new.reshape(FAR_KEYS // BF16_ROWS, BF16_ROWS, lanes)
        return jnp.max(jnp.max(groups, axis=0).astype(F32), axis=0, keepdims=True)

    def far_update(m, s_ref, mx, key0):
        m_new = jnp.maximum(m, mx)
        p = jnp.exp2(s_ref[...] - m_new.astype(BF16))
        cols = pl.ds(pl.multiple_of(key0, FAR_KEYS), FAR_KEYS)
        acc_ref[...] = jnp.exp2(m - m_new) * acc_ref[...] + _dot(vt_ref[:, cols], p)
        return m_new

    def far_round(v, carry, staged_slots=FAR_BUFFERS):
        m = carry[0]
        ahead = list(carry[1:])
        for slot in range(FAR_BUFFERS):
            u = FAR_BUFFERS * v + slot
            if slot < staged_slots:
                ahead.append(stage((slot + FAR_AHEAD) % FAR_BUFFERS, u + FAR_AHEAD))
            ua = jnp.minimum(u, last_step)
            m = far_update(m, sbuf.at[slot], ahead.pop(0), ua * FAR_KEYS)
        return (m, *ahead)

    ahead = [stage(u, u) for u in range(FAR_AHEAD)]

    diag = []
    for h, qi in enumerate(qis):
        qd = jnp.concatenate([qt[:, h * ROWS:(h + 1) * ROWS], jnp.zeros((k_ref.shape[1] - dh, ROWS), BF16)], axis=0)
        prev = jnp.maximum(qi - 1, 0)
        tiles, vts = [], []
        for chunk, kind in ((prev, jnp.where(qi >= 1, 1, 3)), (qi, 0)):
            rows = pl.ds(pl.multiple_of(chunk * KEY_CHUNK, KEY_CHUNK), KEY_CHUNK)
            s = _dot(k_ref[rows, :], qd) + sb_ref[kind]
            for j in range(per):
                srow = sel_ref[h, pl.ds(chunk * per + j, 1), :]
                srow = jnp.concatenate([srow] * GQA_GROUP, axis=1)
                tiles.append(jnp.where(srow > 0.5, s[j * SLC_BLOCK:(j + 1) * SLC_BLOCK, :], NEG))
            vts.append(vt_ref[:, rows])
        diag.append((jnp.concatenate(tiles, axis=0), jnp.concatenate(vts, axis=1)))
    m_parts = []
    for h, (s, vt) in enumerate(diag):
        m_h = jnp.max(s, axis=0, keepdims=True)
        acc_ref[:, h * ROWS:(h + 1) * ROWS] = _dot(vt, jnp.exp2(s - m_h).astype(BF16))
        m_parts.append(m_h)

    init = (jnp.concatenate(m_parts, axis=1), *ahead)
    rounds = (n_steps + FAR_BUFFERS - 1) // FAR_BUFFERS
    carry = lax.fori_loop(0, jnp.maximum(rounds - 1, 0), far_round, init)
    lax.cond(rounds > 0,
             lambda c: far_round(rounds - 1, c, staged_slots=FAR_BUFFERS - FAR_AHEAD)[0],
             lambda c: c[0], carry)
    acc = acc_ref[...]
    os_ref[...] = acc[:dh] / jnp.maximum(acc[dh:dh + 1], 1e-30)


def _slc(qt, ks, vst, sel, sb, crow):
    b, g, dh = qt.shape[0], N_KV_HEADS, HEAD_DIM
    s_len = ks.shape[1]
    nqb = s_len // Q_BLOCK
    total = nqb * ROWS
    n_slc = sel.shape[3]
    blk_q = pl.BlockSpec((None, None, dh, SLC_QBLOCKS * ROWS), lambda i, j, q: (i, j, 0, q))
    return pl.pallas_call(
        _slc_body,
        grid=(b, g, nqb // SLC_QBLOCKS),
        in_specs=[_q_spec(SLC_QBLOCKS),
                  pl.BlockSpec((None, s_len, K_SLAB), lambda i, j, q: (i, 0, j)),
                  pl.BlockSpec((None, VT_ROWS, s_len), lambda i, j, q: (i, j, 0)),
                  pl.BlockSpec((None, None, SLC_QBLOCKS, n_slc, Q_BLOCK), lambda i, j, q: (i, j, q, 0, 0)),
                  pl.BlockSpec((None, 4, KEY_CHUNK, ROWS), lambda i, j, q: (j, 0, 0, 0)),
                  pl.BlockSpec((None, 1, ROWS), lambda i, j, q: (j, 0, 0))],
        out_specs=blk_q,
        out_shape=jax.ShapeDtypeStruct((b, g, dh, total), F32),
        scratch_shapes=[pltpu.VMEM((FAR_BUFFERS, FAR_KEYS, SLC_QBLOCKS * ROWS), BF16),
                        pltpu.VMEM((VT_ROWS, SLC_QBLOCKS * ROWS), F32)],
        compiler_params=_params(3),
        name="slc_attention",
    )(qt, ks, vst, sel, sb, crow)


WIN_CHUNKS = WINDOW // KEY_CHUNK + 1
WIN_QBLOCKS = 8


def _win_body(q_ref, k_ref, vt_ref, wb_ref, oc_ref, os_ref, gate_ref, o_ref):
    q_all = _group_slots(_heads_to_lanes(q_ref[...]), pl.program_id(1))
    logits = []
    for h in range(WIN_QBLOCKS):
        qi = pl.program_id(2) * WIN_QBLOCKS + h
        qt = q_all[:, h * ROWS:(h + 1) * ROWS]
        tiles, vts = [], []
        for delta in range(WIN_CHUNKS - 1, -1, -1):
            c = jnp.maximum(qi - delta, 0)
            rows = pl.ds(pl.multiple_of(c * KEY_CHUNK, KEY_CHUNK), KEY_CHUNK)
            kind = jnp.where(qi >= delta, delta, WIN_CHUNKS)
            tiles.append(_dot(k_ref[rows, :], qt) + wb_ref[kind])
            vts.append(vt_ref[:, rows])
        logits.append((jnp.concatenate(tiles, axis=0), vts))
    for h in range(WIN_QBLOCKS):
        lanes = slice(h * ROWS, (h + 1) * ROWS)
        tokens = slice(h * Q_BLOCK, (h + 1) * Q_BLOCK)
        s, vts = logits[h]
        m = jnp.max(s, axis=0, keepdims=True)
        p = jnp.exp2(s - m)
        l = jnp.sum(p, axis=0, keepdims=True)
        o_w = _dot(jnp.concatenate(vts, axis=1), p.astype(BF16)) / jnp.maximum(l, 1e-30)
        gate = lambda br: jnp.concatenate([gate_ref[br * GQA_GROUP + r:br * GQA_GROUP + r + 1, tokens]
                                           for r in range(GQA_GROUP)], axis=1)
        o = gate(0) * oc_ref[:, lanes] + gate(1) * os_ref[:, lanes] + gate(2) * o_w
        for r in range(GQA_GROUP):
            o_ref[r * HEAD_DIM:(r + 1) * HEAD_DIM, tokens] = o[:, r * Q_BLOCK:(r + 1) * Q_BLOCK].astype(o_ref.dtype)


def _win_mix(qt, kw, vwt, wb, oc, osl, gates):
    b, g, dh = qt.shape[0], N_KV_HEADS, HEAD_DIM
    s_len = kw.shape[1]
    blk_q = pl.BlockSpec((None, None, dh, WIN_QBLOCKS * ROWS), lambda i, j, q: (i, j, 0, q))
    return pl.pallas_call(
        _win_body,
        grid=(b, g, s_len // (WIN_QBLOCKS * Q_BLOCK)),
        in_specs=[_q_spec(WIN_QBLOCKS),
                  pl.BlockSpec((None, s_len, kw.shape[2]), lambda i, j, q: (i, 0, 0)),
                  pl.BlockSpec((None, dh, s_len), lambda i, j, q: (i, j, 0)),
                  pl.BlockSpec((None, WIN_CHUNKS + 1, KEY_CHUNK, ROWS), lambda i, j, q: (j, 0, 0, 0)),
                  blk_q, blk_q,
                  pl.BlockSpec((None, GATE_ROWS, WIN_QBLOCKS * Q_BLOCK), lambda i, j, q: (i, j, q))],
        out_specs=_q_spec(WIN_QBLOCKS),
        out_shape=jax.ShapeDtypeStruct(qt.shape, BF16),
        compiler_params=_params(3),
        name="window_mix",
    )(qt, kw, vwt, wb, oc, osl, gates)


FFN_TILE = 256


def _outproj_ffn_body(a_ref, gm_ref, h_ref, wa_ref, wb_ref, gmix_ref, gpre_ref, wg_ref, wu_ref, wd_ref, gpost_ref,
                      o_ref):
    y = _dot_tn(a_ref[...], wa_ref[...]) + _dot(gm_ref[...], wb_ref[...])
    h = h_ref[...] + _rms(y, gmix_ref[...])
    xb = _rms(h, gpre_ref[...]).astype(BF16)
    acc = jnp.zeros(h.shape, F32)
    for j in range(wg_ref.shape[1] // FFN_TILE):
        cols = slice(j * FFN_TILE, (j + 1) * FFN_TILE)
        gate = _dot(xb, wg_ref[:, cols])
        up = _dot(xb, wu_ref[:, cols])
        act = (gate * _sigmoid(gate) * up).astype(BF16)
        acc = acc + _dot(act, wd_ref[cols, :])
    o_ref[...] = h + _rms(acc, gpost_ref[...])


def _outproj_ffn(a, gm, h, weights, tm):
    n, d = h.shape
    per_b = a.shape[2] // tm
    row = lambda w: pl.BlockSpec((tm, w), lambda i: (i, 0))
    return pl.pallas_call(
        _outproj_ffn_body,
        grid=(n // tm,),
        in_specs=[pl.BlockSpec((None, a.shape[1], tm), lambda i: (i // per_b, 0, i % per_b)),
                  row(gm.shape[1]), row(d)] + [_resident(w) for w in weights],
        out_specs=row(d),
        out_shape=jax.ShapeDtypeStruct((n, d), F32),
        compiler_params=_params(1),
        name="outproj_ffn",
    )(a, gm, h, *weights)


def _t5_bucket(dist):
    n = jnp.maximum(dist, 0)
    max_exact = N_BUCKETS // 2
    nf = jnp.maximum(n, max_exact).astype(F32)
    large = max_exact + (jnp.log(nf / max_exact) / math.log(REL_MAX_DISTANCE / max_exact)
                         * (N_BUCKETS - max_exact)).astype(jnp.int32)
    return jnp.where(n < max_exact, n, jnp.minimum(large, N_BUCKETS - 1))


def _bias_tile(table, dist, mask):
    onehot = (_t5_bucket(dist)[..., None] == jnp.arange(N_BUCKETS, dtype=jnp.int32)).astype(F32)
    b = jnp.einsum("kqn,nh->kqh", onehot, table.astype(F32), precision=lax.Precision.HIGHEST)
    b = jnp.where(mask[..., None], b * LOG2E, NEG)
    k = dist.shape[0]
    return b.reshape(k, Q_BLOCK, N_KV_HEADS, GQA_GROUP).transpose(2, 0, 3, 1).reshape(N_KV_HEADS, k, ROWS)


def _bias_tiles(rel_bias):
    ql = jnp.arange(Q_BLOCK, dtype=jnp.int32)[None, :]
    kl = jnp.arange(KEY_CHUNK, dtype=jnp.int32)[:, None]
    chunk_dist = lambda delta: delta * KEY_CHUNK + ql - kl
    everything = jnp.ones((KEY_CHUNK, Q_BLOCK), bool)
    nothing = jnp.zeros((KEY_CHUNK, Q_BLOCK), bool)
    far = jnp.full((KEY_CHUNK, Q_BLOCK), REL_MAX_DISTANCE, jnp.int32)
    d0 = chunk_dist(0)
    sb = jnp.stack([_bias_tile(rel_bias, d0, d0 >= 0),
                    _bias_tile(rel_bias, chunk_dist(1), everything),
                    _bias_tile(rel_bias, far, everything),
                    _bias_tile(rel_bias, far, nothing)], axis=1)
    wtiles = []
    for delta in range(WIN_CHUNKS):
        d = chunk_dist(delta)
        wtiles.append(_bias_tile(rel_bias, d, (d >= 0) & (d < WINDOW)))
    wtiles.append(_bias_tile(rel_bias, far, nothing))
    wb = jnp.stack(wtiles, axis=1)
    rel = jnp.arange(CMP_WINDOW, dtype=jnp.int32)[:, None] - CMP_FRONT_PAD
    dc = ql - CMP_STRIDE * rel - (CMP_BLOCK - 1)
    cb = _bias_tile(rel_bias, dc, dc >= 0)
    crow = _bias_tile(rel_bias, far[:1], everything[:1])
    return sb, wb, cb, crow


def _mixer_ffn(h, b, s, lw, tiles, tm):
    sb, wb, cb, crow = tiles
    qt, cv, ksa, kw, vst, vwt, gates, gm = _inproj(
        h, b, (lw["gpre"], lw["wqt"], lw["wnat"], lw["wvt"], lw["wgt"], lw["wu"], lw["wv"],
               lw["lng"], lw["lnb"], lw["ws"], lw["bs"], _key_augmentation(IN_TILE)), IN_TILE)
    g, dh = N_KV_HEADS, HEAD_DIM
    n_chunks = s // CMP_STRIDE
    comp = _compress(cv, lw["cpos"], lw["cw1"], lw["cw2"])
    ncp = -(-(n_chunks + CMP_WINDOW) // CMP_CHUNK) * CMP_CHUNK
    back = ncp - CMP_FRONT_PAD - n_chunks
    comp = jnp.pad(comp, ((0, 0), (0, 0), (0, 0), (CMP_FRONT_PAD, back), (0, 0)))
    front = (jnp.arange(ncp) < CMP_FRONT_PAD).astype(F32)[:, None]
    aug = jnp.concatenate([jnp.ones((ncp, 2), F32), front, jnp.zeros((ncp, CMP_AUG_WIDTH - dh - 3), F32)], axis=1)
    kcp = jnp.concatenate([comp[:, 0], jnp.broadcast_to(aug, (b, g) + aug.shape)], axis=-1)
    vct = comp[:, 1].transpose(0, 1, 3, 2).astype(BF16)

    n_slc = s // SLC_BLOCK
    n_other = min(SLC_TOPK, n_slc) - (N_LOCAL_BLOCKS + 1)
    oc, sel = _cmp_select(qt, kcp, vct, crow, cb, n_slc, n_other)
    osl = _slc(qt, ksa.reshape(b, s, ksa.shape[1]), vst, sel, sb, crow)
    ot = _win_mix(qt, kw.reshape(b, s, kw.shape[1]), vwt, wb, oc, osl, gates)
    return _outproj_ffn(ot, gm, h, (lw["wo_a"], lw["wo_b"], lw["gpost"], lw["fpre"], lw["fwg"], lw["fwu"], lw["fwd"],
                                    lw["fpost"]), tm)


def _layer_weights(l, p):
    d = p["w_in"].shape[1]
    w_in = p["w_in"][l]
    o = NSA_WIDTH
    wqt = w_in[:, :o].T
    k_c, v_c, k_s, v_s, k_w, v_w = [w_in[:, o + i * KV_WIDTH:o + (i + 1) * KV_WIDTH] for i in range(6)]
    wnat = jnp.concatenate([k_c, v_c, k_s, k_w], axis=1)
    wvt = jnp.concatenate([v_s, v_w], axis=1).T
    o += 6 * KV_WIDTH
    wg = w_in[:, o:o + N_GATES].reshape(d, N_KV_HEADS, GQA_GROUP, N_BRANCHES).transpose(1, 3, 2, 0)
    wg = wg.reshape(N_KV_HEADS, N_BRANCHES * GQA_GROUP, d)
    wgt = jnp.pad(wg, ((0, 0), (0, GATE_ROWS - N_BRANCHES * GQA_GROUP), (0, 0))).reshape(N_KV_HEADS * GATE_ROWS, d)
    o += N_GATES
    gw = (w_in.shape[1] - o) // 2
    wu, wv = w_in[:, o:o + gw], w_in[:, o + gw:]
    causal = jnp.tril(jnp.ones((GMLP_CHUNK, GMLP_CHUNK), bool))
    ws = jnp.where(causal, p["gmlp_w_s"][l], 0.0)
    bs = jnp.repeat(p["gmlp_b_s"][l].T, gw // N_GMLP_GROUPS, axis=1)
    half = CMP_STRIDE * HEAD_DIM
    cpos = jnp.stack([p["cmp_pos_k"][l].reshape(2, half), p["cmp_pos_v"][l].reshape(2, half)])
    dff = p["w_down"].shape[1]
    row = lambda v: v[l].reshape(1, -1)
    bf = lambda w: w.astype(BF16)
    return dict(gpre=row(p["norm_mix_pre"]), wqt=bf(wqt), wnat=bf(wnat), wvt=bf(wvt), wgt=bf(wgt), wu=bf(wu), wv=bf(wv),
                lng=row(p["gmlp_ln_g"]), lnb=row(p["gmlp_ln_b"]), ws=bf(ws), bs=bs,
                cpos=cpos, cw1=bf(jnp.stack([p["cmp_w1_k"][l], p["cmp_w1_v"][l]])),
                cw2=bf(jnp.stack([p["cmp_w2_k"][l], p["cmp_w2_v"][l]])),
                wo_a=bf(p["w_out"][l][:NSA_WIDTH]), wo_b=bf(p["w_out"][l][NSA_WIDTH:]),
                gpost=row(p["norm_mix_post"]), fpre=row(p["norm_ffn_pre"]), fpost=row(p["norm_ffn_post"]),
                fwg=bf(p["w_gate_up"][l][:, :dff]), fwu=bf(p["w_gate_up"][l][:, dff:]), fwd=bf(p["w_down"][l]))


def _trunk(p, tm):
    x = p["x"]
    b, s, d = x.shape
    h = x.reshape(b * s, d)
    tiles = _bias_tiles(p["rel_bias"])
    for l in range(p["w_in"].shape[0]):
        lw = _layer_weights(l, p)
        h = _mixer_ffn(h, b, s, lw, tiles, tm)
    return h.reshape(b, s, d)


def kernel(x, rel_bias, norm_mix_pre, norm_mix_post, norm_ffn_pre, norm_ffn_post, w_in, cmp_pos_k, cmp_w1_k, cmp_w2_k, cmp_pos_v, cmp_w1_v, cmp_w2_v, gmlp_ln_g, gmlp_ln_b, gmlp_w_s, gmlp_b_s, w_out, w_gate_up, w_down):
    p = dict(x=x, rel_bias=rel_bias, norm_mix_pre=norm_mix_pre, norm_mix_post=norm_mix_post,
             norm_ffn_pre=norm_ffn_pre, norm_ffn_post=norm_ffn_post, w_in=w_in,
             cmp_pos_k=cmp_pos_k, cmp_w1_k=cmp_w1_k, cmp_w2_k=cmp_w2_k,
             cmp_pos_v=cmp_pos_v, cmp_w1_v=cmp_w1_v, cmp_w2_v=cmp_w2_v,
             gmlp_ln_g=gmlp_ln_g, gmlp_ln_b=gmlp_ln_b, gmlp_w_s=gmlp_w_s, gmlp_b_s=gmlp_b_s,
             w_out=w_out, w_gate_up=w_gate_up, w_down=w_down)
    return _trunk(p, tm=512)
```

```python
import functools
import math

import jax
import jax.numpy as jnp
from jax import lax
from jax.experimental import pallas as pl
from jax.experimental.pallas import tpu as pltpu

F32 = jnp.float32
BF16 = jnp.bfloat16

N_NSA_HEADS = 8
N_KV_HEADS = 2
GQA_GROUP = N_NSA_HEADS // N_KV_HEADS
HEAD_DIM = 64
NSA_WIDTH = N_NSA_HEADS * HEAD_DIM
KV_WIDTH = N_KV_HEADS * HEAD_DIM
N_BRANCHES = 3
N_GATES = N_BRANCHES * N_NSA_HEADS
CMP_BLOCK = 32
CMP_STRIDE = 16
SLC_BLOCK = 64
SLC_TOPK = 16
N_LOCAL_BLOCKS = 2
WINDOW = 512
Q_BLOCK = 128
N_GMLP_GROUPS = 8
GMLP_CHUNK = 128
N_BUCKETS = 32
REL_MAX_DISTANCE = 128
RMS_EPS = 1e-6
LN_EPS = 1e-5

ROWS = GQA_GROUP * Q_BLOCK
KEY_CHUNK = 128
CMP_FRONT_PAD = 24
CMP_WINDOW = 32
NEG = -1e30
LANE = 128
BF16_ROWS = 16
VMEM_LIMIT = 48 * 1024 * 1024
LOG2E = math.log2(math.e)
Q_SCALE = HEAD_DIM ** -0.5 * LOG2E


def _dot(a, b):
    return jnp.dot(a, b, preferred_element_type=F32)


def _gelu(x):
    c = math.sqrt(2.0 / math.pi)
    return 0.5 * x * (1.0 + jnp.tanh(c * (x + 0.044715 * (x * x * x))))


def _sigmoid(x):
    return 1.0 / (1.0 + jnp.exp(-x))


def _rms(x, g):
    ms = jnp.mean(x * x, axis=-1, keepdims=True)
    return (x * lax.rsqrt(ms + RMS_EPS)) * g


def _resident(a):
    return pl.BlockSpec(a.shape, lambda i: (0,) * a.ndim, pipeline_mode=pl.Buffered(1))


def _params(n_axes):
    return pltpu.CompilerParams(dimension_semantics=("arbitrary",) * n_axes,
                                vmem_limit_bytes=VMEM_LIMIT)


def _dot_nt(a, b):
    return lax.dot_general(a, b, (((1,), (1,)), ((), ())), preferred_element_type=F32)


def _dot_tn(a, b):
    return lax.dot_general(a, b, (((0,), (0,)), ((), ())), preferred_element_type=F32)


def _inproj_body(x_ref, gpre_ref, wqt_ref, wnat_ref, wvt_ref, wgt_ref, wu_ref, wv_ref, lng_ref, lnb_ref,
                 ws_ref, bs_ref, aug_ref, qt_ref, cv_ref, ksa_ref, kw_ref, vst_ref, vwt_ref, gate_ref, gm_ref, cv_sc):
    tm = x_ref.shape[0]
    xb = _rms(x_ref[...], gpre_ref[...]).astype(BF16)
    qt_ref[...] = (_dot_nt(wqt_ref[...], xb) * Q_SCALE).astype(BF16)
    nat = _dot(xb, wnat_ref[...])
    chunks = tm // CMP_STRIDE
    left = lax.broadcasted_iota(jnp.int32, (chunks, LANE), 1) < HEAD_DIM
    for kv in range(2):
        cv_sc[kv] = nat[:, kv * KV_WIDTH:(kv + 1) * KV_WIDTH]
    for kv in range(2):
        for pair in range(CMP_STRIDE // 2):
            lanes = slice(pair * LANE, (pair + 1) * LANE)
            a = cv_sc[kv, pl.ds(2 * pair, chunks, stride=CMP_STRIDE), :]
            b = cv_sc[kv, pl.ds(2 * pair + 1, chunks, stride=CMP_STRIDE), :]
            cv_ref[kv, 0, :, lanes] = jnp.where(left, a, pltpu.roll(b, HEAD_DIM, 1))
            cv_ref[kv, 1, :, lanes] = jnp.where(left, pltpu.roll(a, HEAD_DIM, 1), b)
    k_slc = nat[:, 2 * KV_WIDTH:3 * KV_WIDTH].astype(BF16)
    ksa_ref[...] = jnp.concatenate(
        [piece for g in range(N_KV_HEADS) for piece in (k_slc[:, g * HEAD_DIM:(g + 1) * HEAD_DIM], aug_ref[...])], axis=1)
    kw_ref[...] = nat[:, 3 * KV_WIDTH:].astype(BF16)
    vt = _dot_nt(wvt_ref[...], xb).astype(BF16)
    denom = jnp.concatenate([jnp.ones((1, tm), BF16), jnp.zeros((BF16_ROWS - 1, tm), BF16)], axis=0)
    vst_ref[...] = jnp.concatenate(
        [piece for g in range(N_KV_HEADS) for piece in (vt[g * HEAD_DIM:(g + 1) * HEAD_DIM], denom)], axis=0)
    vwt_ref[...] = vt[KV_WIDTH:]
    gate_ref[...] = _sigmoid(_dot_nt(wgt_ref[...], xb))
    zu = _gelu(_dot(xb, wu_ref[...]))
    zv = _gelu(_dot(xb, wv_ref[...]))
    mu = jnp.mean(zv, axis=-1, keepdims=True)
    zc = zv - mu
    var = jnp.mean(zc * zc, axis=-1, keepdims=True)
    zv = ((zc * lax.rsqrt(var + LN_EPS)) * lng_ref[...] + lnb_ref[...]).astype(BF16)
    gdim = zv.shape[1] // N_GMLP_GROUPS
    left = lax.broadcasted_iota(jnp.int32, (GMLP_CHUNK, LANE), 1) < gdim
    for c in range(tm // GMLP_CHUNK):
        rows = slice(c * GMLP_CHUNK, (c + 1) * GMLP_CHUNK)
        for j in range(zv.shape[1] // LANE):
            cols = slice(j * LANE, (j + 1) * LANE)
            z = zv[rows, cols]
            sv = jnp.where(left, _dot(ws_ref[2 * j], z), _dot(ws_ref[2 * j + 1], z)) + bs_ref[:, cols]
            gm_ref[rows, cols] = (zu[rows, cols] * sv).astype(BF16)


IN_TILE = 1024
VT_ROWS = HEAD_DIM + BF16_ROWS
GATE_ROWS = 16


def _inproj(x, b, weights, tm):
    n, d = x.shape
    s = n // b
    per_b = s // tm
    row = lambda w: pl.BlockSpec((tm, w), lambda i: (i, 0))
    col = lambda h: pl.BlockSpec((None, h, tm), lambda i: (i // per_b, 0, i % per_b))
    gw = weights[5].shape[1]
    flat = CMP_STRIDE * HEAD_DIM
    assert KV_WIDTH == LANE and N_KV_HEADS == 2
    return pl.pallas_call(
        _inproj_body,
        grid=(n // tm,),
        in_specs=[row(d)] + [_resident(a) for a in weights],
        out_specs=[col(NSA_WIDTH),
                   pl.BlockSpec((None, 2, N_KV_HEADS, tm // CMP_STRIDE, flat),
                                lambda i: (i // per_b, 0, 0, i % per_b, 0)),
                   row(K_AUG_WIDTH), row(KV_WIDTH),
                   col(N_KV_HEADS * VT_ROWS), col(KV_WIDTH), col(N_KV_HEADS * GATE_ROWS), row(gw)],
        out_shape=[jax.ShapeDtypeStruct((b, NSA_WIDTH, s), BF16),
                   jax.ShapeDtypeStruct((b, 2, N_KV_HEADS, s // CMP_STRIDE, flat), F32),
                   jax.ShapeDtypeStruct((n, K_AUG_WIDTH), BF16),
                   jax.ShapeDtypeStruct((n, KV_WIDTH), BF16),
                   jax.ShapeDtypeStruct((b, N_KV_HEADS * VT_ROWS, s), BF16),
                   jax.ShapeDtypeStruct((b, KV_WIDTH, s), BF16),
                   jax.ShapeDtypeStruct((b, N_KV_HEADS * GATE_ROWS, s), F32),
                   jax.ShapeDtypeStruct((n, gw), BF16)],
        scratch_shapes=[pltpu.VMEM((2, tm, KV_WIDTH), F32)],
        compiler_params=_params(1),
        name="inproj_gmlp",
    )(x, *weights)


def _compress_body(x_ref, pos_ref, w1_ref, w2_ref, o_ref):
    x = x_ref[...]
    half = x.shape[1]
    a = _dot((x + pos_ref[0:1, :]).astype(BF16), w1_ref[:half, :])
    b = _dot((x + pos_ref[1:2, :]).astype(BF16), w1_ref[half:, :])
    pre = a + pltpu.roll(b, x.shape[0] - 1, 0)
    o_ref[...] = _dot(_gelu(pre).astype(BF16), w2_ref[...])


def _compress(xc, pos, w1, w2):
    b, two, g, nch, width = xc.shape
    hid = w1.shape[2]
    dh = w2.shape[2]
    return pl.pallas_call(
        _compress_body,
        grid=(b, two, g),
        in_specs=[pl.BlockSpec((None, None, None, nch, width), lambda i, t, j: (i, t, j, 0, 0)),
                  pl.BlockSpec((None, 2, width), lambda i, t, j: (t, 0, 0)),
                  pl.BlockSpec((None, 2 * width, hid), lambda i, t, j: (t, 0, 0)),
                  pl.BlockSpec((None, hid, dh), lambda i, t, j: (t, 0, 0))],
        out_specs=pl.BlockSpec((None, None, None, nch, dh), lambda i, t, j: (i, t, j, 0, 0)),
        out_shape=jax.ShapeDtypeStruct((b, two, g, nch, dh), F32),
        compiler_params=_params(3),
        name="compress",
    )(xc, pos, w1, w2)


CMP_CHUNK = 256
CMP_AUG_WIDTH = 2 * HEAD_DIM


def _heads_to_lanes(blk):
    n = blk.shape[1] // Q_BLOCK
    return jnp.concatenate([blk[r * HEAD_DIM:(r + 1) * HEAD_DIM, h * Q_BLOCK:(h + 1) * Q_BLOCK]
                            for h in range(n) for r in range(GQA_GROUP)], axis=1)


def _group_slots(qt, g):
    return jnp.concatenate([jnp.where(g == j, qt, jnp.zeros_like(qt)) for j in range(N_KV_HEADS)], axis=0)


def _q_spec(n_blocks):
    return pl.BlockSpec((None, GQA_GROUP * HEAD_DIM, n_blocks * Q_BLOCK), lambda i, j, q: (i, j, q))


CMP_QBLOCKS = 4
SUBLANES = 8
SORT_KEEP = 16


def _bitonic_merge(xs):
    if len(xs) == 1:
        return xs
    half = len(xs) // 2
    hi = [jnp.maximum(xs[i], xs[i + half]) for i in range(half)]
    lo = [jnp.minimum(xs[i], xs[i + half]) for i in range(half)]
    return _bitonic_merge(hi) + _bitonic_merge(lo)


def _bitonic_sort(xs):
    if len(xs) == 1:
        return xs
    half = len(xs) // 2
    return _bitonic_merge(_bitonic_sort(xs[:half]) + _bitonic_sort(xs[half:])[::-1])


def _top_of_two(a, b):
    n = len(a)
    return _bitonic_merge([jnp.maximum(a[i], b[n - 1 - i]) for i in range(n)])


def _kth_largest(x, k):
    assert k <= SORT_KEEP and x.shape[0] % SUBLANES == 0
    tiles = [x[i * SUBLANES:(i + 1) * SUBLANES] for i in range(x.shape[0] // SUBLANES)]
    tiles += [jnp.full_like(tiles[0], -2.0)] * (-len(tiles) % SORT_KEEP)
    top = _bitonic_sort(tiles[:SORT_KEEP])
    for j in range(SORT_KEEP, len(tiles), SORT_KEEP):
        top = _top_of_two(top, _bitonic_sort(tiles[j:j + SORT_KEEP]))
    shift = SUBLANES // 2
    while shift:
        top = _top_of_two(top, [pltpu.roll(v, shift, 0) for v in top])
        shift //= 2
    return top[k - 1][0:1]


def _cmp_select_body(q_ref, k_ref, vt_ref, crow_ref, cb_ref, oc_ref, sel_ref, sc_ref, *, n_other):
    qis = [pl.program_id(2) * CMP_QBLOCKS + h for h in range(CMP_QBLOCKS)]
    n_slc = sel_ref.shape[1]
    qt = _heads_to_lanes(q_ref[...])
    dh, lanes = qt.shape
    w0s = [pl.multiple_of(qi * (Q_BLOCK // CMP_STRIDE), 8) for qi in qis]
    n_chunks = (w0s[-1] + CMP_WINDOW + CMP_CHUNK - 1) // CMP_CHUNK
    row_iota = lax.broadcasted_iota(jnp.int32, (CMP_CHUNK, lanes), 0)
    lane_blk = lax.broadcasted_iota(jnp.int32, (1, lanes), 1) // ROWS
    w0_lane = w0s[0]
    for h in range(1, CMP_QBLOCKS):
        w0_lane = jnp.where(lane_blk >= h, w0s[h], w0_lane)

    @pl.when(qis[0] == 0)
    def _():
        sc_ref[...] = jnp.zeros_like(sc_ref)

    c = jnp.concatenate([crow_ref[...]] * CMP_QBLOCKS, axis=1)
    c_hi = c.astype(BF16).astype(F32)
    neg_row = jnp.full((1, lanes), NEG, F32)
    zeros = lambda n, dt: jnp.zeros((n, lanes), dt)
    tail = zeros(k_ref.shape[1] - dh - BF16_ROWS, BF16)
    qa = jnp.concatenate([qt, jnp.concatenate([c_hi, c - c_hi, neg_row, zeros(BF16_ROWS - 3, F32)]).astype(BF16), tail])
    qw = jnp.concatenate([qt, jnp.concatenate([zeros(2, F32), neg_row, zeros(BF16_ROWS - 3, F32)]).astype(BF16), tail])

    def rows_of(ch):
        return pl.ds(pl.multiple_of(ch * CMP_CHUNK, CMP_CHUNK), CMP_CHUNK)

    n_slabs = lanes // Q_BLOCK
    slab = lambda i: slice(i * Q_BLOCK, (i + 1) * Q_BLOCK)

    def put(rows, val, first=0):
        for i in range(val.shape[1] // Q_BLOCK):
            sc_ref[first + i, rows, :] = val[:, slab(i)]

    def get(rows):
        return jnp.concatenate([sc_ref[i, rows, :] for i in range(n_slabs)], axis=1)

    def logits(ch, m):
        s = _dot(k_ref[rows_of(ch), :].astype(BF16), qa)
        s = jnp.where(row_iota + ch * CMP_CHUNK < w0_lane, s, NEG)
        put(rows_of(ch), s)
        return jnp.maximum(m, jnp.max(s, axis=0, keepdims=True))

    m = lax.fori_loop(0, n_chunks, logits, jnp.full((1, lanes), NEG, F32))
    win_max = []
    for h, w0 in enumerate(w0s):
        win = pl.ds(w0, CMP_WINDOW)
        s_win = _dot(k_ref[win, :].astype(BF16), qw[:, h * ROWS:(h + 1) * ROWS]) + cb_ref[...]
        put(win, s_win, first=h * GQA_GROUP)
        win_max.append(jnp.max(s_win, axis=0, keepdims=True))
    m = jnp.maximum(m, jnp.concatenate(win_max, axis=1))

    def weigh(ch, l):
        p = jnp.exp2(get(rows_of(ch)) - m)
        put(rows_of(ch), p)
        oc_ref[...] += _dot(vt_ref[:, rows_of(ch)], p.astype(BF16))
        return l + jnp.sum(p, axis=0, keepdims=True)

    oc_ref[...] = jnp.zeros_like(oc_ref)
    l = lax.fori_loop(0, n_chunks, weigh, jnp.zeros((1, lanes), F32))
    scale = jnp.where(m > 0.5 * NEG, 1.0 / jnp.maximum(l, 1e-30), 0.0)
    oc_ref[...] = oc_ref[...] * scale

    per = SLC_BLOCK // CMP_STRIDE
    blk = lax.broadcasted_iota(jnp.int32, (n_slc, Q_BLOCK), 0)
    blk_f = blk.astype(F32)
    lane_pos = lax.broadcasted_iota(jnp.int32, (n_slc, Q_BLOCK), 1)
    def candidates(h):
        imp = jnp.zeros((n_slc, Q_BLOCK), F32)
        for r in range(GQA_GROUP):
            i = h * GQA_GROUP + r
            part = lambda off: sc_ref[i, pl.ds(CMP_FRONT_PAD + off, n_slc, stride=per), :]
            tot = part(0)
            for k in range(1, per - 1):
                tot = tot + part(k)
            imp = imp + (tot + 0.5 * (part(per - 1) + part(-1))) * scale[:, slab(i)]
        jq = (qis[h] * Q_BLOCK + lane_pos) // SLC_BLOCK
        valid = blk <= jq
        forced = (blk == 0) | (valid & (blk > jq - N_LOCAL_BLOCKS))
        free = valid & jnp.logical_not(forced)
        return jnp.where(free, imp, -1.0), forced, free

    tied = []
    for h in range(CMP_QBLOCKS):
        work, forced, free = candidates(h)
        t = _kth_largest(work, n_other)
        picked = (work > t) | (free & (work == t))
        sel_ref[h] = jnp.where(forced | picked, 1.0, 0.0)
        tied.append(jnp.max(jnp.sum(jnp.where(picked, 1.0, 0.0), axis=0, keepdims=True)) > n_other)

    for h in range(CMP_QBLOCKS):
        @pl.when(tied[h])
        def _(h=h):
            work, forced, free = candidates(h)
            for _ in range(n_other):
                mx = jnp.max(work, axis=0, keepdims=True)
                first = jnp.min(jnp.where(work == mx, blk_f, float(n_slc)), axis=0, keepdims=True)
                work = jnp.where((blk_f == first) & (mx >= 0.0), -1.0, work)
            sel_ref[h] = jnp.where(forced | (free & (work < 0.0)), 1.0, 0.0)


def _cmp_select(qt, kcp, vct, crow, cb, n_slc, n_other):
    b, g, dh = qt.shape[0], N_KV_HEADS, HEAD_DIM
    nqb = qt.shape[2] // Q_BLOCK
    total = nqb * ROWS
    ncp = kcp.shape[2]
    blk_q = pl.BlockSpec((None, None, dh, CMP_QBLOCKS * ROWS), lambda i, j, q: (i, j, 0, q))
    return pl.pallas_call(
        functools.partial(_cmp_select_body, n_other=n_other),
        grid=(b, g, nqb // CMP_QBLOCKS),
        in_specs=[_q_spec(CMP_QBLOCKS),
                  pl.BlockSpec((None, None, ncp, kcp.shape[3]), lambda i, j, q: (i, j, 0, 0)),
                  pl.BlockSpec((None, None, dh, ncp), lambda i, j, q: (i, j, 0, 0)),
                  pl.BlockSpec((None, 1, ROWS), lambda i, j, q: (j, 0, 0)),
                  pl.BlockSpec((None, CMP_WINDOW, ROWS), lambda i, j, q: (j, 0, 0))],
        out_specs=[blk_q,
                   pl.BlockSpec((None, None, CMP_QBLOCKS, n_slc, Q_BLOCK), lambda i, j, q: (i, j, q, 0, 0))],
        out_shape=[jax.ShapeDtypeStruct((b, g, dh, total), F32),
                   jax.ShapeDtypeStruct((b, g, nqb, n_slc, Q_BLOCK), F32)],
        scratch_shapes=[pltpu.VMEM((CMP_QBLOCKS * GQA_GROUP, ncp, Q_BLOCK), F32)],
        compiler_params=_params(3),
        name="cmp_select",
    )(qt, kcp, vct, crow, cb)


SLC_QBLOCKS = 2
FAR_KEYS = 256
FAR_BLOCKS = FAR_KEYS // SLC_BLOCK
FAR_CHUNKS = FAR_KEYS // KEY_CHUNK
FAR_BUFFERS = 4
FAR_AHEAD = 2
PEN_BLOCKS = 8
AUG_CONST = 2
AUG_ROWS = 16
K_SLAB = LANE
K_AUG_WIDTH = N_KV_HEADS * K_SLAB


def _key_augmentation(tm):
    assert tm % (PEN_BLOCKS * SLC_BLOCK) == 0
    pos = jnp.arange(tm, dtype=jnp.int32)
    onehot = ((pos[:, None] // SLC_BLOCK) % PEN_BLOCKS == jnp.arange(PEN_BLOCKS, dtype=jnp.int32)[None, :])
    return jnp.concatenate([jnp.ones((tm, AUG_CONST), BF16),
                            jnp.zeros((tm, AUG_ROWS - PEN_BLOCKS - AUG_CONST), BF16),
                            onehot.astype(BF16),
                            jnp.zeros((tm, K_SLAB - HEAD_DIM - AUG_ROWS), BF16)], axis=1)


def _slc_body(q_ref, k_ref, vt_ref, sel_ref, sb_ref, crow_ref, os_ref, sbuf, acc_ref):
    qis = [pl.program_id(2) * SLC_QBLOCKS + h for h in range(SLC_QBLOCKS)]
    qt = _heads_to_lanes(q_ref[...])
    dh, lanes = qt.shape
    per = KEY_CHUNK // SLC_BLOCK
    far_limits = [jnp.maximum(qi - 1, 0) * per for qi in qis]
    n_steps = (jnp.maximum(qis[-1] - 1, 0) + FAR_CHUNKS - 1) // FAR_CHUNKS
    last_step = k_ref.shape[0] // FAR_KEYS - 1
    steps_per_group = PEN_BLOCKS // FAR_BLOCKS

    c = jnp.concatenate([crow_ref[...]] * SLC_QBLOCKS, axis=1)
    c_hi = c.astype(BF16).astype(F32)
    const_rows = jnp.concatenate([c_hi, c - c_hi, jnp.zeros((AUG_ROWS - PEN_BLOCKS - AUG_CONST, lanes), F32)], axis=0)
    pad_rows = jnp.zeros((k_ref.shape[1] - dh - AUG_ROWS, lanes), BF16)
    blk_iota = lax.broadcasted_iota(jnp.int32, (PEN_BLOCKS, Q_BLOCK), 0)

    def far_logits(u):
        ua = jnp.minimum(u, last_step)
        grp0 = pl.multiple_of((ua // steps_per_group) * PEN_BLOCKS, PEN_BLOCKS)
        blk = blk_iota + (u // steps_per_group) * PEN_BLOCKS
        pens = []
        for h in range(SLC_QBLOCKS):
            pen = jnp.where((sel_ref[h, pl.ds(grp0, PEN_BLOCKS), :] > 0.5) & (blk < far_limits[h]), 0.0, NEG)
            pens += [pen] * GQA_GROUP
        qa = jnp.concatenate([qt, jnp.concatenate([const_rows, jnp.concatenate(pens, axis=1)], axis=0).astype(BF16),
                              pad_rows], axis=0)
        return _dot(k_ref[pl.ds(pl.multiple_of(ua * FAR_KEYS, FAR_KEYS), FAR_KEYS), :], qa)

    def stage(slot, u):
        s_new = far_logits(u).astype(BF16)
        sbuf[slot] = s_new
        groups = s_new.reshape(FAR_KEYS // BF16_ROWS, BF16_ROWS, lanes)
        return jnp.max(jnp.max(groups, axis=0).astype(F32), axis=0, keepdims=True)

    def far_update(m, s_ref, mx, key0):
        m_new = jnp.maximum(m, mx)
        p = jnp.exp2(s_ref[...] - m_new.astype(BF16))
        cols = pl.ds(pl.multiple_of(key0, FAR_KEYS), FAR_KEYS)
        acc_ref[...] = jnp.exp2(m - m_new) * acc_ref[...] + _dot(vt_ref[:, cols], p)
        return m_new

    def far_round(v, carry, staged_slots=FAR_BUFFERS):
        m = carry[0]
        ahead = list(carry[1:])
        for slot in range(FAR_BUFFERS):
            u = FAR_BUFFERS * v + slot
            if slot < staged_slots:
                ahead.append(stage((slot + FAR_AHEAD) % FAR_BUFFERS, u + FAR_AHEAD))
            ua = jnp.minimum(u, last_step)
            m = far_update(m, sbuf.at[slot], ahead.pop(0), ua * FAR_KEYS)
        return (m, *ahead)

    diag = []
    for h, qi in enumerate(qis):
        qd = jnp.concatenate([qt[:, h * ROWS:(h + 1) * ROWS], jnp.zeros((k_ref.shape[1] - dh, ROWS), BF16)], axis=0)
        prev = jnp.maximum(qi - 1, 0)
        tiles, vts = [], []
        for chunk, kind in ((prev, jnp.where(qi >= 1, 1, 3)), (qi, 0)):
            rows = pl.ds(pl.multiple_of(chunk * KEY_CHUNK, KEY_CHUNK), KEY_CHUNK)
            s = _dot(k_ref[rows, :], qd) + sb_ref[kind]
            for j in range(per):
                srow = sel_ref[h, pl.ds(chunk * per + j, 1), :]
                srow = jnp.concatenate([srow] * GQA_GROUP, axis=1)
                tiles.append(jnp.where(srow > 0.5, s[j * SLC_BLOCK:(j + 1) * SLC_BLOCK, :], NEG))
            vts.append(vt_ref[:, rows])
        diag.append((jnp.concatenate(tiles, axis=0), jnp.concatenate(vts, axis=1)))
    ahead = [stage(u, u) for u in range(FAR_AHEAD)]
    m_parts = []
    for h, (s, vt) in enumerate(diag):
        m_h = jnp.max(s, axis=0, keepdims=True)
        acc_ref[:, h * ROWS:(h + 1) * ROWS] = _dot(vt, jnp.exp2(s - m_h).astype(BF16))
        m_parts.append(m_h)

    init = (jnp.concatenate(m_parts, axis=1), *ahead)
    rounds = (n_steps + FAR_BUFFERS - 1) // FAR_BUFFERS
    carry = lax.fori_loop(0, jnp.maximum(rounds - 1, 0), far_round, init)
    lax.cond(rounds > 0,
             lambda c: far_round(rounds - 1, c, staged_slots=FAR_BUFFERS - FAR_AHEAD)[0],
             lambda c: c[0], carry)
    acc = acc_ref[...]
    os_ref[...] = acc[:dh] / jnp.maximum(acc[dh:dh + 1], 1e-30)


def _slc(qt, ks, vst, sel, sb, crow):
    b, g, dh = qt.shape[0], N_KV_HEADS, HEAD_DIM
    s_len = ks.shape[1]
    nqb = s_len // Q_BLOCK
    total = nqb * ROWS
    n_slc = sel.shape[3]
    blk_q = pl.BlockSpec((None, None, dh, SLC_QBLOCKS * ROWS), lambda i, j, q: (i, j, 0, q))
    return pl.pallas_call(
        _slc_body,
        grid=(b, g, nqb // SLC_QBLOCKS),
        in_specs=[_q_spec(SLC_QBLOCKS),
                  pl.BlockSpec((None, s_len, K_SLAB), lambda i, j, q: (i, 0, j)),
                  pl.BlockSpec((None, VT_ROWS, s_len), lambda i, j, q: (i, j, 0)),
                  pl.BlockSpec((None, None, SLC_QBLOCKS, n_slc, Q_BLOCK), lambda i, j, q: (i, j, q, 0, 0)),
                  pl.BlockSpec((None, 4, KEY_CHUNK, ROWS), lambda i, j, q: (j, 0, 0, 0)),
                  pl.BlockSpec((None, 1, ROWS), lambda i, j, q: (j, 0, 0))],
        out_specs=blk_q,
        out_shape=jax.ShapeDtypeStruct((b, g, dh, total), F32),
        scratch_shapes=[pltpu.VMEM((FAR_BUFFERS, FAR_KEYS, SLC_QBLOCKS * ROWS), BF16),
                        pltpu.VMEM((VT_ROWS, SLC_QBLOCKS * ROWS), F32)],
        compiler_params=_params(3),
        name="slc_attention",
    )(qt, ks, vst, sel, sb, crow)


WIN_CHUNKS = WINDOW // KEY_CHUNK + 1
WIN_QBLOCKS = 8


def _win_body(q_ref, k_ref, vt_ref, wb_ref, oc_ref, os_ref, gate_ref, o_ref):
    q_all = _group_slots(_heads_to_lanes(q_ref[...]), pl.program_id(1))
    logits = []
    for h in range(WIN_QBLOCKS):
        qi = pl.program_id(2) * WIN_QBLOCKS + h
        qt = q_all[:, h * ROWS:(h + 1) * ROWS]
        tiles, vts = [], []
        for delta in range(WIN_CHUNKS - 1, -1, -1):
            c = jnp.maximum(qi - delta, 0)
            rows = pl.ds(pl.multiple_of(c * KEY_CHUNK, KEY_CHUNK), KEY_CHUNK)
            kind = jnp.where(qi >= delta, delta, WIN_CHUNKS)
            tiles.append(_dot(k_ref[rows, :], qt) + wb_ref[kind])
            vts.append(vt_ref[:, rows])
        logits.append((jnp.concatenate(tiles, axis=0), vts))
    for h in range(WIN_QBLOCKS):
        lanes = slice(h * ROWS, (h + 1) * ROWS)
        tokens = slice(h * Q_BLOCK, (h + 1) * Q_BLOCK)
        s, vts = logits[h]
        m = jnp.max(s, axis=0, keepdims=True)
        p = jnp.exp2(s - m)
        l = jnp.sum(p, axis=0, keepdims=True)
        o_w = _dot(jnp.concatenate(vts, axis=1), p.astype(BF16)) / jnp.maximum(l, 1e-30)
        gate = lambda br: jnp.concatenate([gate_ref[br * GQA_GROUP + r:br * GQA_GROUP + r + 1, tokens]
                                           for r in range(GQA_GROUP)], axis=1)
        o = gate(0) * oc_ref[:, lanes] + gate(1) * os_ref[:, lanes] + gate(2) * o_w
        for r in range(GQA_GROUP):
            o_ref[r * HEAD_DIM:(r + 1) * HEAD_DIM, tokens] = o[:, r * Q_BLOCK:(r + 1) * Q_BLOCK].astype(o_ref.dtype)


def _win_mix(qt, kw, vwt, wb, oc, osl, gates):
    b, g, dh = qt.shape[0], N_KV_HEADS, HEAD_DIM
    s_len = kw.shape[1]
    blk_q = pl.BlockSpec((None, None, dh, WIN_QBLOCKS * ROWS), lambda i, j, q: (i, j, 0, q))
    return pl.pallas_call(
        _win_body,
        grid=(b, g, s_len // (WIN_QBLOCKS * Q_BLOCK)),
        in_specs=[_q_spec(WIN_QBLOCKS),
                  pl.BlockSpec((None, s_len, kw.shape[2]), lambda i, j, q: (i, 0, 0)),
                  pl.BlockSpec((None, dh, s_len), lambda i, j, q: (i, j, 0)),
                  pl.BlockSpec((None, WIN_CHUNKS + 1, KEY_CHUNK, ROWS), lambda i, j, q: (j, 0, 0, 0)),
                  blk_q, blk_q,
                  pl.BlockSpec((None, GATE_ROWS, WIN_QBLOCKS * Q_BLOCK), lambda i, j, q: (i, j, q))],
        out_specs=_q_spec(WIN_QBLOCKS),
        out_shape=jax.ShapeDtypeStruct(qt.shape, BF16),
        compiler_params=_params(3),
        name="window_mix",
    )(qt, kw, vwt, wb, oc, osl, gates)


FFN_TILE = 256


def _outproj_ffn_body(a_ref, gm_ref, h_ref, wa_ref, wb_ref, gmix_ref, gpre_ref, wg_ref, wu_ref, wd_ref, gpost_ref,
                      o_ref):
    y = _dot_tn(a_ref[...], wa_ref[...]) + _dot(gm_ref[...], wb_ref[...])
    h = h_ref[...] + _rms(y, gmix_ref[...])
    xb = _rms(h, gpre_ref[...]).astype(BF16)
    acc = jnp.zeros(h.shape, F32)
    for j in range(wg_ref.shape[1] // FFN_TILE):
        cols = slice(j * FFN_TILE, (j + 1) * FFN_TILE)
        gate = _dot(xb, wg_ref[:, cols])
        up = _dot(xb, wu_ref[:, cols])
        act = (gate * _sigmoid(gate) * up).astype(BF16)
        acc = acc + _dot(act, wd_ref[cols, :])
    o_ref[...] = h + _rms(acc, gpost_ref[...])


def _outproj_ffn(a, gm, h, weights, tm):
    n, d = h.shape
    per_b = a.shape[2] // tm
    row = lambda w: pl.BlockSpec((tm, w), lambda i: (i, 0))
    return pl.pallas_call(
        _outproj_ffn_body,
        grid=(n // tm,),
        in_specs=[pl.BlockSpec((None, a.shape[1], tm), lambda i: (i // per_b, 0, i % per_b)),
                  row(gm.shape[1]), row(d)] + [_resident(w) for w in weights],
        out_specs=row(d),
        out_shape=jax.ShapeDtypeStruct((n, d), F32),
        compiler_params=_params(1),
        name="outproj_ffn",
    )(a, gm, h, *weights)


def _t5_bucket(dist):
    n = jnp.maximum(dist, 0)
    max_exact = N_BUCKETS // 2
    nf = jnp.maximum(n, max_exact).astype(F32)
    large = max_exact + (jnp.log(nf / max_exact) / math.log(REL_MAX_DISTANCE / max_exact)
                         * (N_BUCKETS - max_exact)).astype(jnp.int32)
    return jnp.where(n < max_exact, n, jnp.minimum(large, N_BUCKETS - 1))


def _bias_tile(table, dist, mask):
    onehot = (_t5_bucket(dist)[..., None] == jnp.arange(N_BUCKETS, dtype=jnp.int32)).astype(F32)
    b = jnp.einsum("kqn,nh->kqh", onehot, table.astype(F32), precision=lax.Precision.HIGHEST)
    b = jnp.where(mask[..., None], b * LOG2E, NEG)
    k = dist.shape[0]
    return b.reshape(k, Q_BLOCK, N_KV_HEADS, GQA_GROUP).transpose(2, 0, 3, 1).reshape(N_KV_HEADS, k, ROWS)


def _bias_tiles(rel_bias):
    ql = jnp.arange(Q_BLOCK, dtype=jnp.int32)[None, :]
    kl = jnp.arange(KEY_CHUNK, dtype=jnp.int32)[:, None]
    chunk_dist = lambda delta: delta * KEY_CHUNK + ql - kl
    everything = jnp.ones((KEY_CHUNK, Q_BLOCK), bool)
    nothing = jnp.zeros((KEY_CHUNK, Q_BLOCK), bool)
    far = jnp.full((KEY_CHUNK, Q_BLOCK), REL_MAX_DISTANCE, jnp.int32)
    d0 = chunk_dist(0)
    sb = jnp.stack([_bias_tile(rel_bias, d0, d0 >= 0),
                    _bias_tile(rel_bias, chunk_dist(1), everything),
                    _bias_tile(rel_bias, far, everything),
                    _bias_tile(rel_bias, far, nothing)], axis=1)
    wtiles = []
    for delta in range(WIN_CHUNKS):
        d = chunk_dist(delta)
        wtiles.append(_bias_tile(rel_bias, d, (d >= 0) & (d < WINDOW)))
    wtiles.append(_bias_tile(rel_bias, far, nothing))
    wb = jnp.stack(wtiles, axis=1)
    rel = jnp.arange(CMP_WINDOW, dtype=jnp.int32)[:, None] - CMP_FRONT_PAD
    dc = ql - CMP_STRIDE * rel - (CMP_BLOCK - 1)
    cb = _bias_tile(rel_bias, dc, dc >= 0)
    crow = _bias_tile(rel_bias, far[:1], everything[:1])
    return sb, wb, cb, crow


def _mixer_ffn(h, b, s, lw, tiles, tm):
    sb, wb, cb, crow = tiles
    qt, cv, ksa, kw, vst, vwt, gates, gm = _inproj(
        h, b, (lw["gpre"], lw["wqt"], lw["wnat"], lw["wvt"], lw["wgt"], lw["wu"], lw["wv"],
               lw["lng"], lw["lnb"], lw["ws"], lw["bs"], _key_augmentation(IN_TILE)), IN_TILE)
    g, dh = N_KV_HEADS, HEAD_DIM
    n_chunks = s // CMP_STRIDE
    comp = _compress(cv, lw["cpos"], lw["cw1"], lw["cw2"])
    ncp = -(-(n_chunks + CMP_WINDOW) // CMP_CHUNK) * CMP_CHUNK
    back = ncp - CMP_FRONT_PAD - n_chunks
    comp = jnp.pad(comp, ((0, 0), (0, 0), (0, 0), (CMP_FRONT_PAD, back), (0, 0)))
    front = (jnp.arange(ncp) < CMP_FRONT_PAD).astype(F32)[:, None]
    aug = jnp.concatenate([jnp.ones((ncp, 2), F32), front, jnp.zeros((ncp, CMP_AUG_WIDTH - dh - 3), F32)], axis=1)
    kcp = jnp.concatenate([comp[:, 0], jnp.broadcast_to(aug, (b, g) + aug.shape)], axis=-1)
    vct = comp[:, 1].transpose(0, 1, 3, 2).astype(BF16)

    n_slc = s // SLC_BLOCK
    n_other = min(SLC_TOPK, n_slc) - (N_LOCAL_BLOCKS + 1)
    oc, sel = _cmp_select(qt, kcp, vct, crow, cb, n_slc, n_other)
    osl = _slc(qt, ksa.reshape(b, s, ksa.shape[1]), vst, sel, sb, crow)
    ot = _win_mix(qt, kw.reshape(b, s, kw.shape[1]), vwt, wb, oc, osl, gates)
    return _outproj_ffn(ot, gm, h, (lw["wo_a"], lw["wo_b"], lw["gpost"], lw["fpre"], lw["fwg"], lw["fwu"], lw["fwd"],
                                    lw["fpost"]), tm)


def _layer_weights(l, p):
    d = p["w_in"].shape[1]
    w_in = p["w_in"][l]
    o = NSA_WIDTH
    wqt = w_in[:, :o].T
    k_c, v_c, k_s, v_s, k_w, v_w = [w_in[:, o + i * KV_WIDTH:o + (i + 1) * KV_WIDTH] for i in range(6)]
    wnat = jnp.concatenate([k_c, v_c, k_s, k_w], axis=1)
    wvt = jnp.concatenate([v_s, v_w], axis=1).T
    o += 6 * KV_WIDTH
    wg = w_in[:, o:o + N_GATES].reshape(d, N_KV_HEADS, GQA_GROUP, N_BRANCHES).transpose(1, 3, 2, 0)
    wg = wg.reshape(N_KV_HEADS, N_BRANCHES * GQA_GROUP, d)
    wgt = jnp.pad(wg, ((0, 0), (0, GATE_ROWS - N_BRANCHES * GQA_GROUP), (0, 0))).reshape(N_KV_HEADS * GATE_ROWS, d)
    o += N_GATES
    gw = (w_in.shape[1] - o) // 2
    wu, wv = w_in[:, o:o + gw], w_in[:, o + gw:]
    causal = jnp.tril(jnp.ones((GMLP_CHUNK, GMLP_CHUNK), bool))
    ws = jnp.where(causal, p["gmlp_w_s"][l], 0.0)
    bs = jnp.repeat(p["gmlp_b_s"][l].T, gw // N_GMLP_GROUPS, axis=1)
    half = CMP_STRIDE * HEAD_DIM
    cpos = jnp.stack([p["cmp_pos_k"][l].reshape(2, half), p["cmp_pos_v"][l].reshape(2, half)])
    dff = p["w_down"].shape[1]
    row = lambda v: v[l].reshape(1, -1)
    bf = lambda w: w.astype(BF16)
    return dict(gpre=row(p["norm_mix_pre"]), wqt=bf(wqt), wnat=bf(wnat), wvt=bf(wvt), wgt=bf(wgt), wu=bf(wu), wv=bf(wv),
                lng=row(p["gmlp_ln_g"]), lnb=row(p["gmlp_ln_b"]), ws=bf(ws), bs=bs,
                cpos=cpos, cw1=bf(jnp.stack([p["cmp_w1_k"][l], p["cmp_w1_v"][l]])),
                cw2=bf(jnp.stack([p["cmp_w2_k"][l], p["cmp_w2_v"][l]])),
                wo_a=bf(p["w_out"][l][:NSA_WIDTH]), wo_b=bf(p["w_out"][l][NSA_WIDTH:]),
                gpost=row(p["norm_mix_post"]), fpre=row(p["norm_ffn_pre"]), fpost=row(p["norm_ffn_post"]),
                fwg=bf(p["w_gate_up"][l][:, :dff]), fwu=bf(p["w_gate_up"][l][:, dff:]), fwd=bf(p["w_down"][l]))


def _trunk(p, tm):
    x = p["x"]
    b, s, d = x.shape
    h = x.reshape(b * s, d)
    tiles = _bias_tiles(p["rel_bias"])
    for l in range(p["w_in"].shape[0]):
        lw = _layer_weights(l, p)
        h = _mixer_ffn(h, b, s, lw, tiles, tm)
    return h.reshape(b, s, d)


def kernel(x, rel_bias, norm_mix_pre, norm_mix_post, norm_ffn_pre, norm_ffn_post, w_in, cmp_pos_k, cmp_w1_k, cmp_w2_k, cmp_pos_v, cmp_w1_v, cmp_w2_v, gmlp_ln_g, gmlp_ln_b, gmlp_w_s, gmlp_b_s, w_out, w_gate_up, w_down):
    p = dict(x=x, rel_bias=rel_bias, norm_mix_pre=norm_mix_pre, norm_mix_post=norm_mix_post,
             norm_ffn_pre=norm_ffn_pre, norm_ffn_post=norm_ffn_post, w_in=w_in,
             cmp_pos_k=cmp_pos_k, cmp_w1_k=cmp_w1_k, cmp_w2_k=cmp_w2_k,
             cmp_pos_v=cmp_pos_v, cmp_w1_v=cmp_w1_v, cmp_w2_v=cmp_w2_v,
             gmlp_ln_g=gmlp_ln_g, gmlp_ln_b=gmlp_ln_b, gmlp_w_s=gmlp_w_s, gmlp_b_s=gmlp_b_s,
             w_out=w_out, w_gate_up=w_gate_up, w_down=w_down)
    return _trunk(p, tm=512)
```

```python
import functools
import math

import jax
import jax.numpy as jnp
from jax import lax
from jax.experimental import pallas as pl
from jax.experimental.pallas import tpu as pltpu

F32 = jnp.float32
BF16 = jnp.bfloat16

N_NSA_HEADS = 8
N_KV_HEADS = 2
GQA_GROUP = N_NSA_HEADS // N_KV_HEADS
HEAD_DIM = 64
NSA_WIDTH = N_NSA_HEADS * HEAD_DIM
KV_WIDTH = N_KV_HEADS * HEAD_DIM
N_BRANCHES = 3
N_GATES = N_BRANCHES * N_NSA_HEADS
CMP_BLOCK = 32
CMP_STRIDE = 16
SLC_BLOCK = 64
SLC_TOPK = 16
N_LOCAL_BLOCKS = 2
WINDOW = 512
Q_BLOCK = 128
N_GMLP_GROUPS = 8
GMLP_CHUNK = 128
N_BUCKETS = 32
REL_MAX_DISTANCE = 128
RMS_EPS = 1e-6
LN_EPS = 1e-5

ROWS = GQA_GROUP * Q_BLOCK
KEY_CHUNK = 128
CMP_FRONT_PAD = 24
CMP_WINDOW = 32
NEG = -1e30
LANE = 128
BF16_ROWS = 16
VMEM_LIMIT = 48 * 1024 * 1024
LOG2E = math.log2(math.e)
Q_SCALE = HEAD_DIM ** -0.5 * LOG2E


def _dot(a, b):
    return jnp.dot(a, b, preferred_element_type=F32)


def _gelu(x):
    c = math.sqrt(2.0 / math.pi)
    return 0.5 * x * (1.0 + jnp.tanh(c * (x + 0.044715 * (x * x * x))))


def _sigmoid(x):
    return 1.0 / (1.0 + jnp.exp(-x))


def _rms(x, g):
    ms = jnp.mean(x * x, axis=-1, keepdims=True)
    return (x * lax.rsqrt(ms + RMS_EPS)) * g


def _resident(a):
    return pl.BlockSpec(a.shape, lambda i: (0,) * a.ndim, pipeline_mode=pl.Buffered(1))


def _params(n_axes):
    return pltpu.CompilerParams(dimension_semantics=("arbitrary",) * n_axes,
                                vmem_limit_bytes=VMEM_LIMIT)


def _dot_nt(a, b):
    return lax.dot_general(a, b, (((1,), (1,)), ((), ())), preferred_element_type=F32)


def _dot_tn(a, b):
    return lax.dot_general(a, b, (((0,), (0,)), ((), ())), preferred_element_type=F32)


def _inproj_body(x_ref, gpre_ref, wqt_ref, wnat_ref, wvt_ref, wgt_ref, wu_ref, wv_ref, lng_ref, lnb_ref,
                 ws_ref, bs_ref, aug_ref, qt_ref, cv_ref, ksa_ref, kw_ref, vst_ref, vwt_ref, gate_ref, gm_ref, cv_sc):
    tm = x_ref.shape[0]
    xb = _rms(x_ref[...], gpre_ref[...]).astype(BF16)
    zu = _gelu(_dot(xb, wu_ref[...]))
    zv = _gelu(_dot(xb, wv_ref[...]))
    mu = jnp.mean(zv, axis=-1, keepdims=True)
    zc = zv - mu
    var = jnp.mean(zc * zc, axis=-1, keepdims=True)
    zv = ((zc * lax.rsqrt(var + LN_EPS)) * lng_ref[...] + lnb_ref[...]).astype(BF16)
    qt_ref[...] = (_dot_nt(wqt_ref[...], xb) * Q_SCALE).astype(BF16)
    nat = _dot(xb, wnat_ref[...])
    chunks = tm // CMP_STRIDE
    left = lax.broadcasted_iota(jnp.int32, (chunks, LANE), 1) < HEAD_DIM
    for kv in range(2):
        cv_sc[kv] = nat[:, kv * KV_WIDTH:(kv + 1) * KV_WIDTH]
    for kv in range(2):
        for pair in range(CMP_STRIDE // 2):
            lanes = slice(pair * LANE, (pair + 1) * LANE)
            a = cv_sc[kv, pl.ds(2 * pair, chunks, stride=CMP_STRIDE), :]
            b = cv_sc[kv, pl.ds(2 * pair + 1, chunks, stride=CMP_STRIDE), :]
            cv_ref[kv, 0, :, lanes] = jnp.where(left, a, pltpu.roll(b, HEAD_DIM, 1))
            cv_ref[kv, 1, :, lanes] = jnp.where(left, pltpu.roll(a, HEAD_DIM, 1), b)
    k_slc = nat[:, 2 * KV_WIDTH:3 * KV_WIDTH].astype(BF16)
    ksa_ref[...] = jnp.concatenate(
        [piece for g in range(N_KV_HEADS) for piece in (k_slc[:, g * HEAD_DIM:(g + 1) * HEAD_DIM], aug_ref[...])], axis=1)
    kw_ref[...] = nat[:, 3 * KV_WIDTH:].astype(BF16)
    vt = _dot_nt(wvt_ref[...], xb).astype(BF16)
    denom = jnp.concatenate([jnp.ones((1, tm), BF16), jnp.zeros((BF16_ROWS - 1, tm), BF16)], axis=0)
    vst_ref[...] = jnp.concatenate(
        [piece for g in range(N_KV_HEADS) for piece in (vt[g * HEAD_DIM:(g + 1) * HEAD_DIM], denom)], axis=0)
    vwt_ref[...] = vt[KV_WIDTH:]
    gate_ref[...] = _sigmoid(_dot_nt(wgt_ref[...], xb))
    gdim = zv.shape[1] // N_GMLP_GROUPS
    left = lax.broadcasted_iota(jnp.int32, (GMLP_CHUNK, LANE), 1) < gdim
    for c in range(tm // GMLP_CHUNK):
        rows = slice(c * GMLP_CHUNK, (c + 1) * GMLP_CHUNK)
        for j in range(zv.shape[1] // LANE):
            cols = slice(j * LANE, (j + 1) * LANE)
            z = zv[rows, cols]
            sv = jnp.where(left, _dot(ws_ref[2 * j], z), _dot(ws_ref[2 * j + 1], z)) + bs_ref[:, cols]
            gm_ref[rows, cols] = (zu[rows, cols] * sv).astype(BF16)


IN_TILE = 1024
VT_ROWS = HEAD_DIM + BF16_ROWS
GATE_ROWS = 16


def _inproj(x, b, weights, tm):
    n, d = x.shape
    s = n // b
    per_b = s // tm
    row = lambda w: pl.BlockSpec((tm, w), lambda i: (i, 0))
    col = lambda h: pl.BlockSpec((None, h, tm), lambda i: (i // per_b, 0, i % per_b))
    gw = weights[5].shape[1]
    flat = CMP_STRIDE * HEAD_DIM
    assert KV_WIDTH == LANE and N_KV_HEADS == 2
    return pl.pallas_call(
        _inproj_body,
        grid=(n // tm,),
        in_specs=[row(d)] + [_resident(a) for a in weights],
        out_specs=[col(NSA_WIDTH),
                   pl.BlockSpec((None, 2, N_KV_HEADS, tm // CMP_STRIDE, flat),
                                lambda i: (i // per_b, 0, 0, i % per_b, 0)),
                   row(K_AUG_WIDTH), row(KV_WIDTH),
                   col(N_KV_HEADS * VT_ROWS), col(KV_WIDTH), col(N_KV_HEADS * GATE_ROWS), row(gw)],
        out_shape=[jax.ShapeDtypeStruct((b, NSA_WIDTH, s), BF16),
                   jax.ShapeDtypeStruct((b, 2, N_KV_HEADS, s // CMP_STRIDE, flat), F32),
                   jax.ShapeDtypeStruct((n, K_AUG_WIDTH), BF16),
                   jax.ShapeDtypeStruct((n, KV_WIDTH), BF16),
                   jax.ShapeDtypeStruct((b, N_KV_HEADS * VT_ROWS, s), BF16),
                   jax.ShapeDtypeStruct((b, KV_WIDTH, s), BF16),
                   jax.ShapeDtypeStruct((b, N_KV_HEADS * GATE_ROWS, s), F32),
                   jax.ShapeDtypeStruct((n, gw), BF16)],
        scratch_shapes=[pltpu.VMEM((2, tm, KV_WIDTH), F32)],
        compiler_params=_params(1),
        name="inproj_gmlp",
    )(x, *weights)


def _compress_body(x_ref, pos_ref, w1_ref, w2_ref, o_ref):
    x = x_ref[...]
    half = x.shape[1]
    a = _dot((x + pos_ref[0:1, :]).astype(BF16), w1_ref[:half, :])
    b = _dot((x + pos_ref[1:2, :]).astype(BF16), w1_ref[half:, :])
    pre = a + pltpu.roll(b, x.shape[0] - 1, 0)
    o_ref[...] = _dot(_gelu(pre).astype(BF16), w2_ref[...])


def _compress(xc, pos, w1, w2):
    b, two, g, nch, width = xc.shape
    hid = w1.shape[2]
    dh = w2.shape[2]
    return pl.pallas_call(
        _compress_body,
        grid=(b, two, g),
        in_specs=[pl.BlockSpec((None, None, None, nch, width), lambda i, t, j: (i, t, j, 0, 0)),
                  pl.BlockSpec((None, 2, width), lambda i, t, j: (t, 0, 0)),
                  pl.BlockSpec((None, 2 * width, hid), lambda i, t, j: (t, 0, 0)),
                  pl.BlockSpec((None, hid, dh), lambda i, t, j: (t, 0, 0))],
        out_specs=pl.BlockSpec((None, None, None, nch, dh), lambda i, t, j: (i, t, j, 0, 0)),
        out_shape=jax.ShapeDtypeStruct((b, two, g, nch, dh), F32),
        compiler_params=_params(3),
        name="compress",
    )(xc, pos, w1, w2)


CMP_CHUNK = 256
CMP_AUG_WIDTH = 2 * HEAD_DIM


def _heads_to_lanes(blk):
    n = blk.shape[1] // Q_BLOCK
    return jnp.concatenate([blk[r * HEAD_DIM:(r + 1) * HEAD_DIM, h * Q_BLOCK:(h + 1) * Q_BLOCK]
                            for h in range(n) for r in range(GQA_GROUP)], axis=1)


def _group_slots(qt, g):
    return jnp.concatenate([jnp.where(g == j, qt, jnp.zeros_like(qt)) for j in range(N_KV_HEADS)], axis=0)


def _q_spec(n_blocks):
    return pl.BlockSpec((None, GQA_GROUP * HEAD_DIM, n_blocks * Q_BLOCK), lambda i, j, q: (i, j, q))


CMP_QBLOCKS = 4
SUBLANES = 8
SORT_KEEP = 16


def _bitonic_merge(xs):
    if len(xs) == 1:
        return xs
    half = len(xs) // 2
    hi = [jnp.maximum(xs[i], xs[i + half]) for i in range(half)]
    lo = [jnp.minimum(xs[i], xs[i + half]) for i in range(half)]
    return _bitonic_merge(hi) + _bitonic_merge(lo)


def _bitonic_sort(xs):
    if len(xs) == 1:
        return xs
    half = len(xs) // 2
    return _bitonic_merge(_bitonic_sort(xs[:half]) + _bitonic_sort(xs[half:])[::-1])


def _top_of_two(a, b):
    n = len(a)
    return _bitonic_merge([jnp.maximum(a[i], b[n - 1 - i]) for i in range(n)])


def _kth_largest(x, k):
    assert k <= SORT_KEEP and x.shape[0] % SUBLANES == 0
    tiles = [x[i * SUBLANES:(i + 1) * SUBLANES] for i in range(x.shape[0] // SUBLANES)]
    tiles += [jnp.full_like(tiles[0], -2.0)] * (-len(tiles) % SORT_KEEP)
    top = _bitonic_sort(tiles[:SORT_KEEP])
    for j in range(SORT_KEEP, len(tiles), SORT_KEEP):
        top = _top_of_two(top, _bitonic_sort(tiles[j:j + SORT_KEEP]))
    shift = SUBLANES // 2
    while shift:
        top = _top_of_two(top, [pltpu.roll(v, shift, 0) for v in top])
        shift //= 2
    return top[k - 1][0:1]


def _cmp_select_body(q_ref, k_ref, vt_ref, crow_ref, cb_ref, oc_ref, sel_ref, sc_ref, *, n_other):
    qis = [pl.program_id(2) * CMP_QBLOCKS + h for h in range(CMP_QBLOCKS)]
    n_slc = sel_ref.shape[1]
    qt = _heads_to_lanes(q_ref[...])
    dh, lanes = qt.shape
    w0s = [pl.multiple_of(qi * (Q_BLOCK // CMP_STRIDE), 8) for qi in qis]
    n_chunks = (w0s[-1] + CMP_WINDOW + CMP_CHUNK - 1) // CMP_CHUNK
    row_iota = lax.broadcasted_iota(jnp.int32, (CMP_CHUNK, lanes), 0)
    lane_blk = lax.broadcasted_iota(jnp.int32, (1, lanes), 1) // ROWS
    w0_lane = w0s[0]
    for h in range(1, CMP_QBLOCKS):
        w0_lane = jnp.where(lane_blk >= h, w0s[h], w0_lane)

    @pl.when(qis[0] == 0)
    def _():
        sc_ref[...] = jnp.zeros_like(sc_ref)

    c = jnp.concatenate([crow_ref[...]] * CMP_QBLOCKS, axis=1)
    c_hi = c.astype(BF16).astype(F32)
    neg_row = jnp.full((1, lanes), NEG, F32)
    zeros = lambda n, dt: jnp.zeros((n, lanes), dt)
    tail = zeros(k_ref.shape[1] - dh - BF16_ROWS, BF16)
    qa = jnp.concatenate([qt, jnp.concatenate([c_hi, c - c_hi, neg_row, zeros(BF16_ROWS - 3, F32)]).astype(BF16), tail])
    qw = jnp.concatenate([qt, jnp.concatenate([zeros(2, F32), neg_row, zeros(BF16_ROWS - 3, F32)]).astype(BF16), tail])

    def rows_of(ch):
        return pl.ds(pl.multiple_of(ch * CMP_CHUNK, CMP_CHUNK), CMP_CHUNK)

    n_slabs = lanes // Q_BLOCK
    slab = lambda i: slice(i * Q_BLOCK, (i + 1) * Q_BLOCK)

    def put(rows, val, first=0):
        for i in range(val.shape[1] // Q_BLOCK):
            sc_ref[first + i, rows, :] = val[:, slab(i)]

    def get(rows):
        return jnp.concatenate([sc_ref[i, rows, :] for i in range(n_slabs)], axis=1)

    def logits(ch, m):
        s = _dot(k_ref[rows_of(ch), :].astype(BF16), qa)
        s = jnp.where(row_iota + ch * CMP_CHUNK < w0_lane, s, NEG)
        put(rows_of(ch), s)
        return jnp.maximum(m, jnp.max(s, axis=0, keepdims=True))

    m = lax.fori_loop(0, n_chunks, logits, jnp.full((1, lanes), NEG, F32))
    win_max = []
    for h, w0 in enumerate(w0s):
        win = pl.ds(w0, CMP_WINDOW)
        s_win = _dot(k_ref[win, :].astype(BF16), qw[:, h * ROWS:(h + 1) * ROWS]) + cb_ref[...]
        put(win, s_win, first=h * GQA_GROUP)
        win_max.append(jnp.max(s_win, axis=0, keepdims=True))
    m = jnp.maximum(m, jnp.concatenate(win_max, axis=1))

    def weigh(ch, l):
        p = jnp.exp2(get(rows_of(ch)) - m)
        put(rows_of(ch), p)
        oc_ref[...] += _dot(vt_ref[:, rows_of(ch)], p.astype(BF16))
        return l + jnp.sum(p, axis=0, keepdims=True)

    oc_ref[...] = jnp.zeros_like(oc_ref)
    l = lax.fori_loop(0, n_chunks, weigh, jnp.zeros((1, lanes), F32))
    scale = jnp.where(m > 0.5 * NEG, 1.0 / jnp.maximum(l, 1e-30), 0.0)
    oc_ref[...] = oc_ref[...] * scale

    per = SLC_BLOCK // CMP_STRIDE
    blk = lax.broadcasted_iota(jnp.int32, (n_slc, Q_BLOCK), 0)
    blk_f = blk.astype(F32)
    lane_pos = lax.broadcasted_iota(jnp.int32, (n_slc, Q_BLOCK), 1)
    def candidates(h):
        imp = jnp.zeros((n_slc, Q_BLOCK), F32)
        for r in range(GQA_GROUP):
            i = h * GQA_GROUP + r
            part = lambda off: sc_ref[i, pl.ds(CMP_FRONT_PAD + off, n_slc, stride=per), :]
            tot = part(0)
            for k in range(1, per - 1):
                tot = tot + part(k)
            imp = imp + (tot + 0.5 * (part(per - 1) + part(-1))) * scale[:, slab(i)]
        jq = (qis[h] * Q_BLOCK + lane_pos) // SLC_BLOCK
        valid = blk <= jq
        forced = (blk == 0) | (valid & (blk > jq - N_LOCAL_BLOCKS))
        free = valid & jnp.logical_not(forced)
        return jnp.where(free, imp, -1.0), forced, free

    tied = []
    for h in range(CMP_QBLOCKS):
        work, forced, free = candidates(h)
        t = _kth_largest(work, n_other)
        picked = (work > t) | (free & (work == t))
        sel_ref[h] = jnp.where(forced | picked, 1.0, 0.0)
        tied.append(jnp.max(jnp.sum(jnp.where(picked, 1.0, 0.0), axis=0, keepdims=True)) > n_other)

    for h in range(CMP_QBLOCKS):
        @pl.when(tied[h])
        def _(h=h):
            work, forced, free = candidates(h)
            for _ in range(n_other):
                mx = jnp.max(work, axis=0, keepdims=True)
                first = jnp.min(jnp.where(work == mx, blk_f, float(n_slc)), axis=0, keepdims=True)
                work = jnp.where((blk_f == first) & (mx >= 0.0), -1.0, work)
            sel_ref[h] = jnp.where(forced | (free & (work < 0.0)), 1.0, 0.0)


def _cmp_select(qt, kcp, vct, crow, cb, n_slc, n_other):
    b, g, dh = qt.shape[0], N_KV_HEADS, HEAD_DIM
    nqb = qt.shape[2] // Q_BLOCK
    total = nqb * ROWS
    ncp = kcp.shape[2]
    blk_q = pl.BlockSpec((None, None, dh, CMP_QBLOCKS * ROWS), lambda i, j, q: (i, j, 0, q))
    return pl.pallas_call(
        functools.partial(_cmp_select_body, n_other=n_other),
        grid=(b, g, nqb // CMP_QBLOCKS),
        in_specs=[_q_spec(CMP_QBLOCKS),
                  pl.BlockSpec((None, None, ncp, kcp.shape[3]), lambda i, j, q: (i, j, 0, 0)),
                  pl.BlockSpec((None, None, dh, ncp), lambda i, j, q: (i, j, 0, 0)),
                  pl.BlockSpec((None, 1, ROWS), lambda i, j, q: (j, 0, 0)),
                  pl.BlockSpec((None, CMP_WINDOW, ROWS), lambda i, j, q: (j, 0, 0))],
        out_specs=[blk_q,
                   pl.BlockSpec((None, None, CMP_QBLOCKS, n_slc, Q_BLOCK), lambda i, j, q: (i, j, q, 0, 0))],
        out_shape=[jax.ShapeDtypeStruct((b, g, dh, total), F32),
                   jax.ShapeDtypeStruct((b, g, nqb, n_slc, Q_BLOCK), F32)],
        scratch_shapes=[pltpu.VMEM((CMP_QBLOCKS * GQA_GROUP, ncp, Q_BLOCK), F32)],
        compiler_params=_params(3),
        name="cmp_select",
    )(qt, kcp, vct, crow, cb)


SLC_QBLOCKS = 2
FAR_KEYS = 256
FAR_BLOCKS = FAR_KEYS // SLC_BLOCK
FAR_CHUNKS = FAR_KEYS // KEY_CHUNK
FAR_BUFFERS = 4
FAR_AHEAD = 2
PEN_BLOCKS = 8
AUG_CONST = 2
AUG_ROWS = 16
K_SLAB = LANE
K_AUG_WIDTH = N_KV_HEADS * K_SLAB


def _key_augmentation(tm):
    assert tm % (PEN_BLOCKS * SLC_BLOCK) == 0
    pos = jnp.arange(tm, dtype=jnp.int32)
    onehot = ((pos[:, None] // SLC_BLOCK) % PEN_BLOCKS == jnp.arange(PEN_BLOCKS, dtype=jnp.int32)[None, :])
    return jnp.concatenate([jnp.ones((tm, AUG_CONST), BF16),
                            jnp.zeros((tm, AUG_ROWS - PEN_BLOCKS - AUG_CONST), BF16),
                            onehot.astype(BF16),
                            jnp.zeros((tm, K_SLAB - HEAD_DIM - AUG_ROWS), BF16)], axis=1)


def _slc_body(q_ref, k_ref, vt_ref, sel_ref, sb_ref, crow_ref, os_ref, sbuf, acc_ref):
    qis = [pl.program_id(2) * SLC_QBLOCKS + h for h in range(SLC_QBLOCKS)]
    qt = _heads_to_lanes(q_ref[...])
    dh, lanes = qt.shape
    per = KEY_CHUNK // SLC_BLOCK
    far_limits = [jnp.maximum(qi - 1, 0) * per for qi in qis]
    n_steps = (jnp.maximum(qis[-1] - 1, 0) + FAR_CHUNKS - 1) // FAR_CHUNKS
    last_step = k_ref.shape[0] // FAR_KEYS - 1
    steps_per_group = PEN_BLOCKS // FAR_BLOCKS

    c = jnp.concatenate([crow_ref[...]] * SLC_QBLOCKS, axis=1)
    c_hi = c.astype(BF16).astype(F32)
    const_rows = jnp.concatenate([c_hi, c - c_hi, jnp.zeros((AUG_ROWS - PEN_BLOCKS - AUG_CONST, lanes), F32)], axis=0)
    pad_rows = jnp.zeros((k_ref.shape[1] - dh - AUG_ROWS, lanes), BF16)
    blk_iota = lax.broadcasted_iota(jnp.int32, (PEN_BLOCKS, Q_BLOCK), 0)

    def far_logits(u):
        ua = jnp.minimum(u, last_step)
        grp0 = pl.multiple_of((ua // steps_per_group) * PEN_BLOCKS, PEN_BLOCKS)
        blk = blk_iota + (u // steps_per_group) * PEN_BLOCKS
        pens = []
        for h in range(SLC_QBLOCKS):
            pen = jnp.where((sel_ref[h, pl.ds(grp0, PEN_BLOCKS), :] > 0.5) & (blk < far_limits[h]), 0.0, NEG)
            pens += [pen] * GQA_GROUP
        qa = jnp.concatenate([qt, jnp.concatenate([const_rows, jnp.concatenate(pens, axis=1)], axis=0).astype(BF16),
                              pad_rows], axis=0)
        return _dot(k_ref[pl.ds(pl.multiple_of(ua * FAR_KEYS, FAR_KEYS), FAR_KEYS), :], qa)

    def stage(slot, u):
        s_new = far_logits(u).astype(BF16)
        sbuf[slot] = s_new
        groups = s_new.reshape(FAR_KEYS // BF16_ROWS, BF16_ROWS, lanes)
        return jnp.max(jnp.max(groups, axis=0).astype(F32), axis=0, keepdims=True)

    def far_update(m, s_ref, mx, key0):
        m_new = jnp.maximum(m, mx)
        p = jnp.exp2(s_ref[...] - m_new.astype(BF16))
        cols = pl.ds(pl.multiple_of(key0, FAR_KEYS), FAR_KEYS)
        acc_ref[...] = jnp.exp2(m - m_new) * acc_ref[...] + _dot(vt_ref[:, cols], p)
        return m_new

    def far_round(v, carry, staged_slots=FAR_BUFFERS):
        m = carry[0]
        ahead = list(carry[1:])
        for slot in range(FAR_BUFFERS):
            u = FAR_BUFFERS * v + slot
            if slot < staged_slots:
                ahead.append(stage((slot + FAR_AHEAD) % FAR_BUFFERS, u + FAR_AHEAD))
            ua = jnp.minimum(u, last_step)
            m = far_update(m, sbuf.at[slot], ahead.pop(0), ua * FAR_KEYS)
        return (m, *ahead)

    diag = []
    for h, qi in enumerate(qis):
        qd = jnp.concatenate([qt[:, h * ROWS:(h + 1) * ROWS], jnp.zeros((k_ref.shape[1] - dh, ROWS), BF16)], axis=0)
        prev = jnp.maximum(qi - 1, 0)
        tiles, vts = [], []
        for chunk, kind in ((prev, jnp.where(qi >= 1, 1, 3)), (qi, 0)):
            rows = pl.ds(pl.multiple_of(chunk * KEY_CHUNK, KEY_CHUNK), KEY_CHUNK)
            s = _dot(k_ref[rows, :], qd) + sb_ref[kind]
            for j in range(per):
                srow = sel_ref[h, pl.ds(chunk * per + j, 1), :]
                srow = jnp.concatenate([srow] * GQA_GROUP, axis=1)
                tiles.append(jnp.where(srow > 0.5, s[j * SLC_BLOCK:(j + 1) * SLC_BLOCK, :], NEG))
            vts.append(vt_ref[:, rows])
        diag.append((jnp.concatenate(tiles, axis=0), jnp.concatenate(vts, axis=1)))
    ahead = [stage(u, u) for u in range(FAR_AHEAD)]
    m_parts = []
    for h, (s, vt) in enumerate(diag):
        m_h = jnp.max(s, axis=0, keepdims=True)
        acc_ref[:, h * ROWS:(h + 1) * ROWS] = _dot(vt, jnp.exp2(s - m_h).astype(BF16))
        m_parts.append(m_h)

    init = (jnp.concatenate(m_parts, axis=1), *ahead)
    rounds = (n_steps + FAR_BUFFERS - 1) // FAR_BUFFERS
    carry = lax.fori_loop(0, jnp.maximum(rounds - 1, 0), far_round, init)
    lax.cond(rounds > 0,
             lambda c: far_round(rounds - 1, c, staged_slots=FAR_BUFFERS - FAR_AHEAD)[0],
             lambda c: c[0], carry)
    acc = acc_ref[...]
    os_ref[...] = acc[:dh] / jnp.maximum(acc[dh:dh + 1], 1e-30)


def _slc(qt, ks, vst, sel, sb, crow):
    b, g, dh = qt.shape[0], N_KV_HEADS, HEAD_DIM
    s_len = ks.shape[1]
    nqb = s_len // Q_BLOCK
    total = nqb * ROWS
    n_slc = sel.shape[3]
    blk_q = pl.BlockSpec((None, None, dh, SLC_QBLOCKS * ROWS), lambda i, j, q: (i, j, 0, q))
    return pl.pallas_call(
        _slc_body,
        grid=(b, g, nqb // SLC_QBLOCKS),
        in_specs=[_q_spec(SLC_QBLOCKS),
                  pl.BlockSpec((None, s_len, K_SLAB), lambda i, j, q: (i, 0, j)),
                  pl.BlockSpec((None, VT_ROWS, s_len), lambda i, j, q: (i, j, 0)),
                  pl.BlockSpec((None, None, SLC_QBLOCKS, n_slc, Q_BLOCK), lambda i, j, q: (i, j, q, 0, 0)),
                  pl.BlockSpec((None, 4, KEY_CHUNK, ROWS), lambda i, j, q: (j, 0, 0, 0)),
                  pl.BlockSpec((None, 1, ROWS), lambda i, j, q: (j, 0, 0))],
        out_specs=blk_q,
        out_shape=jax.ShapeDtypeStruct((b, g, dh, total), F32),
        scratch_shapes=[pltpu.VMEM((FAR_BUFFERS, FAR_KEYS, SLC_QBLOCKS * ROWS), BF16),
                        pltpu.VMEM((VT_ROWS, SLC_QBLOCKS * ROWS), F32)],
        compiler_params=_params(3),
        name="slc_attention",
    )(qt, ks, vst, sel, sb, crow)


WIN_CHUNKS = WINDOW // KEY_CHUNK + 1
WIN_QBLOCKS = 8


def _win_body(q_ref, k_ref, vt_ref, wb_ref, oc_ref, os_ref, gate_ref, o_ref):
    q_all = _group_slots(_heads_to_lanes(q_ref[...]), pl.program_id(1))
    logits = []
    for h in range(WIN_QBLOCKS):
        qi = pl.program_id(2) * WIN_QBLOCKS + h
        qt = q_all[:, h * ROWS:(h + 1) * ROWS]
        tiles, vts = [], []
        for delta in range(WIN_CHUNKS - 1, -1, -1):
            c = jnp.maximum(qi - delta, 0)
            rows = pl.ds(pl.multiple_of(c * KEY_CHUNK, KEY_CHUNK), KEY_CHUNK)
            kind = jnp.where(qi >= delta, delta, WIN_CHUNKS)
            tiles.append(_dot(k_ref[rows, :], qt) + wb_ref[kind])
            vts.append(vt_ref[:, rows])
        logits.append((jnp.concatenate(tiles, axis=0), vts))
    for h in range(WIN_QBLOCKS):
        lanes = slice(h * ROWS, (h + 1) * ROWS)
        tokens = slice(h * Q_BLOCK, (h + 1) * Q_BLOCK)
        s, vts = logits[h]
        m = jnp.max(s, axis=0, keepdims=True)
        p = jnp.exp2(s - m)
        l = jnp.sum(p, axis=0, keepdims=True)
        o_w = _dot(jnp.concatenate(vts, axis=1), p.astype(BF16)) / jnp.maximum(l, 1e-30)
        gate = lambda br: jnp.concatenate([gate_ref[br * GQA_GROUP + r:br * GQA_GROUP + r + 1, tokens]
                                           for r in range(GQA_GROUP)], axis=1)
        o = gate(0) * oc_ref[:, lanes] + gate(1) * os_ref[:, lanes] + gate(2) * o_w
        for r in range(GQA_GROUP):
            o_ref[r * HEAD_DIM:(r + 1) * HEAD_DIM, tokens] = o[:, r * Q_BLOCK:(r + 1) * Q_BLOCK].astype(o_ref.dtype)


def _win_mix(qt, kw, vwt, wb, oc, osl, gates):
    b, g, dh = qt.shape[0], N_KV_HEADS, HEAD_DIM
    s_len = kw.shape[1]
    blk_q = pl.BlockSpec((None, None, dh, WIN_QBLOCKS * ROWS), lambda i, j, q: (i, j, 0, q))
    return pl.pallas_call(
        _win_body,
        grid=(b, g, s_len // (WIN_QBLOCKS * Q_BLOCK)),
        in_specs=[_q_spec(WIN_QBLOCKS),
                  pl.BlockSpec((None, s_len, kw.shape[2]), lambda i, j, q: (i, 0, 0)),
                  pl.BlockSpec((None, dh, s_len), lambda i, j, q: (i, j, 0)),
                  pl.BlockSpec((None, WIN_CHUNKS + 1, KEY_CHUNK, ROWS), lambda i, j, q: (j, 0, 0, 0)),
                  blk_q, blk_q,
                  pl.BlockSpec((None, GATE_ROWS, WIN_QBLOCKS * Q_BLOCK), lambda i, j, q: (i, j, q))],
        out_specs=_q_spec(WIN_QBLOCKS),
        out_shape=jax.ShapeDtypeStruct(qt.shape, BF16),
        compiler_params=_params(3),
        name="window_mix",
    )(qt, kw, vwt, wb, oc, osl, gates)


FFN_TILE = 256


def _outproj_ffn_body(a_ref, gm_ref, h_ref, wa_ref, wb_ref, gmix_ref, gpre_ref, wg_ref, wu_ref, wd_ref, gpost_ref,
                      o_ref):
    y = _dot_tn(a_ref[...], wa_ref[...]) + _dot(gm_ref[...], wb_ref[...])
    h = h_ref[...] + _rms(y, gmix_ref[...])
    xb = _rms(h, gpre_ref[...]).astype(BF16)
    acc = jnp.zeros(h.shape, F32)
    for j in range(wg_ref.shape[1] // FFN_TILE):
        cols = slice(j * FFN_TILE, (j + 1) * FFN_TILE)
        gate = _dot(xb, wg_ref[:, cols])
        up = _dot(xb, wu_ref[:, cols])
        act = (gate * _sigmoid(gate) * up).astype(BF16)
        acc = acc + _dot(act, wd_ref[cols, :])
    o_ref[...] = h + _rms(acc, gpost_ref[...])


def _outproj_ffn(a, gm, h, weights, tm):
    n, d = h.shape
    per_b = a.shape[2] // tm
    row = lambda w: pl.BlockSpec((tm, w), lambda i: (i, 0))
    return pl.pallas_call(
        _outproj_ffn_body,
        grid=(n // tm,),
        in_specs=[pl.BlockSpec((None, a.shape[1], tm), lambda i: (i // per_b, 0, i % per_b)),
                  row(gm.shape[1]), row(d)] + [_resident(w) for w in weights],
        out_specs=row(d),
        out_shape=jax.ShapeDtypeStruct((n, d), F32),
        compiler_params=_params(1),
        name="outproj_ffn",
    )(a, gm, h, *weights)


def _t5_bucket(dist):
    n = jnp.maximum(dist, 0)
    max_exact = N_BUCKETS // 2
    nf = jnp.maximum(n, max_exact).astype(F32)
    large = max_exact + (jnp.log(nf / max_exact) / math.log(REL_MAX_DISTANCE / max_exact)
                         * (N_BUCKETS - max_exact)).astype(jnp.int32)
    return jnp.where(n < max_exact, n, jnp.minimum(large, N_BUCKETS - 1))


def _bias_tile(table, dist, mask):
    onehot = (_t5_bucket(dist)[..., None] == jnp.arange(N_BUCKETS, dtype=jnp.int32)).astype(F32)
    b = jnp.einsum("kqn,nh->kqh", onehot, table.astype(F32), precision=lax.Precision.HIGHEST)
    b = jnp.where(mask[..., None], b * LOG2E, NEG)
    k = dist.shape[0]
    return b.reshape(k, Q_BLOCK, N_KV_HEADS, GQA_GROUP).transpose(2, 0, 3, 1).reshape(N_KV_HEADS, k, ROWS)


def _bias_tiles(rel_bias):
    ql = jnp.arange(Q_BLOCK, dtype=jnp.int32)[None, :]
    kl = jnp.arange(KEY_CHUNK, dtype=jnp.int32)[:, None]
    chunk_dist = lambda delta: delta * KEY_CHUNK + ql - kl
    everything = jnp.ones((KEY_CHUNK, Q_BLOCK), bool)
    nothing = jnp.zeros((KEY_CHUNK, Q_BLOCK), bool)
    far = jnp.full((KEY_CHUNK, Q_BLOCK), REL_MAX_DISTANCE, jnp.int32)
    d0 = chunk_dist(0)
    sb = jnp.stack([_bias_tile(rel_bias, d0, d0 >= 0),
                    _bias_tile(rel_bias, chunk_dist(1), everything),
                    _bias_tile(rel_bias, far, everything),
                    _bias_tile(rel_bias, far, nothing)], axis=1)
    wtiles = []
    for delta in range(WIN_CHUNKS):
        d = chunk_dist(delta)
        wtiles.append(_bias_tile(rel_bias, d, (d >= 0) & (d < WINDOW)))
    wtiles.append(_bias_tile(rel_bias, far, nothing))
    wb = jnp.stack(wtiles, axis=1)
    rel = jnp.arange(CMP_WINDOW, dtype=jnp.int32)[:, None] - CMP_FRONT_PAD
    dc = ql - CMP_STRIDE * rel - (CMP_BLOCK - 1)
    cb = _bias_tile(rel_bias, dc, dc >= 0)
    crow = _bias_tile(rel_bias, far[:1], everything[:1])
    return sb, wb, cb, crow


def _mixer_ffn(h, b, s, lw, tiles, tm):
    sb, wb, cb, crow = tiles
    qt, cv, ksa, kw, vst, vwt, gates, gm = _inproj(
        h, b, (lw["gpre"], lw["wqt"], lw["wnat"], lw["wvt"], lw["wgt"], lw["wu"], lw["wv"],
               lw["lng"], lw["lnb"], lw["ws"], lw["bs"], _key_augmentation(IN_TILE)), IN_TILE)
    g, dh = N_KV_HEADS, HEAD_DIM
    n_chunks = s // CMP_STRIDE
    comp = _compress(cv, lw["cpos"], lw["cw1"], lw["cw2"])
    ncp = -(-(n_chunks + CMP_WINDOW) // CMP_CHUNK) * CMP_CHUNK
    back = ncp - CMP_FRONT_PAD - n_chunks
    comp = jnp.pad(comp, ((0, 0), (0, 0), (0, 0), (CMP_FRONT_PAD, back), (0, 0)))
    front = (jnp.arange(ncp) < CMP_FRONT_PAD).astype(F32)[:, None]
    aug = jnp.concatenate([jnp.ones((ncp, 2), F32), front, jnp.zeros((ncp, CMP_AUG_WIDTH - dh - 3), F32)], axis=1)
    kcp = jnp.concatenate([comp[:, 0], jnp.broadcast_to(aug, (b, g) + aug.shape)], axis=-1)
    vct = comp[:, 1].transpose(0, 1, 3, 2).astype(BF16)

    n_slc = s // SLC_BLOCK
    n_other = min(SLC_TOPK, n_slc) - (N_LOCAL_BLOCKS + 1)
    oc, sel = _cmp_select(qt, kcp, vct, crow, cb, n_slc, n_other)
    osl = _slc(qt, ksa.reshape(b, s, ksa.shape[1]), vst, sel, sb, crow)
    ot = _win_mix(qt, kw.reshape(b, s, kw.shape[1]), vwt, wb, oc, osl, gates)
    return _outproj_ffn(ot, gm, h, (lw["wo_a"], lw["wo_b"], lw["gpost"], lw["fpre"], lw["fwg"], lw["fwu"], lw["fwd"],
                                    lw["fpost"]), tm)


def _layer_weights(l, p):
    d = p["w_in"].shape[1]
    w_in = p["w_in"][l]
    o = NSA_WIDTH
    wqt = w_in[:, :o].T
    k_c, v_c, k_s, v_s, k_w, v_w = [w_in[:, o + i * KV_WIDTH:o + (i + 1) * KV_WIDTH] for i in range(6)]
    wnat = jnp.concatenate([k_c, v_c, k_s, k_w], axis=1)
    wvt = jnp.concatenate([v_s, v_w], axis=1).T
    o += 6 * KV_WIDTH
    wg = w_in[:, o:o + N_GATES].reshape(d, N_KV_HEADS, GQA_GROUP, N_BRANCHES).transpose(1, 3, 2, 0)
    wg = wg.reshape(N_KV_HEADS, N_BRANCHES * GQA_GROUP, d)
    wgt = jnp.pad(wg, ((0, 0), (0, GATE_ROWS - N_BRANCHES * GQA_GROUP), (0, 0))).reshape(N_KV_HEADS * GATE_ROWS, d)
    o += N_GATES
    gw = (w_in.shape[1] - o) // 2
    wu, wv = w_in[:, o:o + gw], w_in[:, o + gw:]
    causal = jnp.tril(jnp.ones((GMLP_CHUNK, GMLP_CHUNK), bool))
    ws = jnp.where(causal, p["gmlp_w_s"][l], 0.0)
    bs = jnp.repeat(p["gmlp_b_s"][l].T, gw // N_GMLP_GROUPS, axis=1)
    half = CMP_STRIDE * HEAD_DIM
    cpos = jnp.stack([p["cmp_pos_k"][l].reshape(2, half), p["cmp_pos_v"][l].reshape(2, half)])
    dff = p["w_down"].shape[1]
    row = lambda v: v[l].reshape(1, -1)
    bf = lambda w: w.astype(BF16)
    return dict(gpre=row(p["norm_mix_pre"]), wqt=bf(wqt), wnat=bf(wnat), wvt=bf(wvt), wgt=bf(wgt), wu=bf(wu), wv=bf(wv),
                lng=row(p["gmlp_ln_g"]), lnb=row(p["gmlp_ln_b"]), ws=bf(ws), bs=bs,
                cpos=cpos, cw1=bf(jnp.stack([p["cmp_w1_k"][l], p["cmp_w1_v"][l]])),
                cw2=bf(jnp.stack([p["cmp_w2_k"][l], p["cmp_w2_v"][l]])),
                wo_a=bf(p["w_out"][l][:NSA_WIDTH]), wo_b=bf(p["w_out"][l][NSA_WIDTH:]),
                gpost=row(p["norm_mix_post"]), fpre=row(p["norm_ffn_pre"]), fpost=row(p["norm_ffn_post"]),
                fwg=bf(p["w_gate_up"][l][:, :dff]), fwu=bf(p["w_gate_up"][l][:, dff:]), fwd=bf(p["w_down"][l]))


def _trunk(p, tm):
    x = p["x"]
    b, s, d = x.shape
    h = x.reshape(b * s, d)
    tiles = _bias_tiles(p["rel_bias"])
    for l in range(p["w_in"].shape[0]):
        lw = _layer_weights(l, p)
        h = _mixer_ffn(h, b, s, lw, tiles, tm)
    return h.reshape(b, s, d)


def kernel(x, rel_bias, norm_mix_pre, norm_mix_post, norm_ffn_pre, norm_ffn_post, w_in, cmp_pos_k, cmp_w1_k, cmp_w2_k, cmp_pos_v, cmp_w1_v, cmp_w2_v, gmlp_ln_g, gmlp_ln_b, gmlp_w_s, gmlp_b_s, w_out, w_gate_up, w_down):
    p = dict(x=x, rel_bias=rel_bias, norm_mix_pre=norm_mix_pre, norm_mix_post=norm_mix_post,
             norm_ffn_pre=norm_ffn_pre, norm_ffn_post=norm_ffn_post, w_in=w_in,
             cmp_pos_k=cmp_pos_k, cmp_w1_k=cmp_w1_k, cmp_w2_k=cmp_w2_k,
             cmp_pos_v=cmp_pos_v, cmp_w1_v=cmp_w1_v, cmp_w2_v=cmp_w2_v,
             gmlp_ln_g=gmlp_ln_g, gmlp_ln_b=gmlp_ln_b, gmlp_w_s=gmlp_w_s, gmlp_b_s=gmlp_b_s,
             w_out=w_out, w_gate_up=w_gate_up, w_down=w_down)
    return _trunk(p, tm=512)
```
